```python
import math
import jax, jax.numpy as jnp
from jax import lax
import numpy as np

D_MODEL = 1024
BATCH = 4
SEQ = 4096
DEPTH = 1
DEC_BATCH = 32
DEC_SEQ = 1
PAST_LEN = 8192
PAGE_SIZE = 128

N_ATT_HEADS = 8
ATT_HEAD_DIM = 64
ATT_WIDTH = N_ATT_HEADS * ATT_HEAD_DIM
ROPE_DIMS = ATT_HEAD_DIM // 4
ROPE_THETA = 500000.0
MOBA_BLOCK = 256
MOBA_TOPK = 3
MOBA_QUERY_BLOCK = 32
N_ML_HEADS = 4
ML_QK_DIM = 128
ML_V_DIM = 256
ML_QK_WIDTH = N_ML_HEADS * ML_QK_DIM
ML_V_WIDTH = N_ML_HEADS * ML_V_DIM
ML_CHUNK = 64
D_FF = ((8 * D_MODEL + 3 * 256 - 1) // (3 * 256)) * 256
PLE_DIM = 256
NORM_EPS = 1e-6
NEG_INF = -1e30
IN_SPLITS = (ATT_WIDTH, ATT_WIDTH, ATT_WIDTH, ML_QK_WIDTH, ML_QK_WIDTH, ML_V_WIDTH,
             N_ML_HEADS, N_ML_HEADS, ML_V_WIDTH, D_MODEL, D_MODEL)
D_IN = 3 * ATT_WIDTH + 2 * ML_QK_WIDTH + 2 * ML_V_WIDTH + 2 * N_ML_HEADS + 2 * D_MODEL

kernel_name = "hybrid_moba_mlstm_gated_decoder_step"


def rms_norm(x, g):
    xf = x.astype(jnp.float32)
    y = xf * lax.rsqrt(jnp.mean(xf * xf, axis=-1, keepdims=True) + NORM_EPS)
    return (y * g.astype(jnp.float32)).astype(x.dtype)


def split_columns(z):
    outs, start = [], 0
    for w in IN_SPLITS:
        outs.append(z[..., start:start + w])
        start += w
    return outs


def rope_partial(x, pos):
    half = ROPE_DIMS // 2
    freqs = ROPE_THETA ** (-jnp.arange(0, ROPE_DIMS, 2, dtype=jnp.float32) / ROPE_DIMS)
    ang = pos.astype(jnp.float32)[:, None] * freqs[None, :]
    cos = jnp.cos(ang)[None, :, None, :]
    sin = jnp.sin(ang)[None, :, None, :]
    xr = x[..., :ROPE_DIMS].astype(jnp.float32)
    x1, x2 = xr[..., :half], xr[..., half:]
    rot = jnp.concatenate([x1 * cos - x2 * sin, x2 * cos + x1 * sin], axis=-1).astype(x.dtype)
    return jnp.concatenate([rot, x[..., ROPE_DIMS:]], axis=-1)


def moba_attention(q, k, v, q_start, q_block):
    B, Lq, H, hd = q.shape
    Lk = k.shape[1]
    nb = -(-Lk // MOBA_BLOCK)
    pad = nb * MOBA_BLOCK - Lk
    kb = jnp.pad(k, ((0, 0), (0, pad), (0, 0), (0, 0))).reshape(B, nb, MOBA_BLOCK, H, hd).transpose(0, 3, 1, 2, 4)
    vb = jnp.pad(v, ((0, 0), (0, pad), (0, 0), (0, 0))).reshape(B, nb, MOBA_BLOCK, H, hd).transpose(0, 3, 1, 2, 4)
    k_mean = kb.astype(jnp.float32).mean(axis=3)
    n_gate = max(nb, MOBA_TOPK)
    if n_gate > nb:
        k_mean = jnp.pad(k_mean, ((0, 0), (0, 0), (0, n_gate - nb), (0, 0)))
    blk_ids = jnp.arange(n_gate, dtype=jnp.int32)
    offs = jnp.arange(MOBA_BLOCK, dtype=jnp.int32)
    scale = hd ** -0.5
    nq = Lq // q_block
    qs = q.reshape(B, nq, q_block, H, hd).transpose(1, 0, 3, 2, 4)
    pos = (q_start + jnp.arange(Lq, dtype=jnp.int32)).reshape(nq, q_block)
    b_ix = jnp.arange(B)[:, None, None, None]
    h_ix = jnp.arange(H)[None, :, None, None]

    def one_query_block(args):
        qc, pc = args
        cur = pc // MOBA_BLOCK
        gate = jnp.einsum('bhqd,bhnd->bhqn', qc.astype(jnp.float32), k_mean)
        gate = jnp.where(blk_ids[None, :] < cur[:, None], gate, NEG_INF)
        _, sel = lax.top_k(gate, MOBA_TOPK)
        sel_ok = sel < cur[:, None]
        idx = jnp.concatenate([jnp.minimum(sel, nb - 1),
                               jnp.broadcast_to(cur[:, None], sel.shape[:-1] + (1,))], axis=-1)
        ok = jnp.concatenate([sel_ok, jnp.ones(sel.shape[:-1] + (1,), dtype=bool)], axis=-1)
        kg = kb[b_ix, h_ix, idx]
        vg = vb[b_ix, h_ix, idx]
        s = jnp.einsum('bhqd,bhqnkd->bhqnk', qc, kg).astype(jnp.float32) * scale
        kpos = idx[..., None] * MOBA_BLOCK + offs
        mask = ok[..., None] & (kpos <= pc[:, None, None])
        s = jnp.where(mask, s, NEG_INF)
        p = jax.nn.softmax(s.reshape(s.shape[:3] + (-1,)), axis=-1).reshape(s.shape).astype(vg.dtype)
        return jnp.einsum('bhqnk,bhqnkd->bqhd', p, vg)

    out = lax.map(one_query_block, (qs, pos))
    return out.transpose(1, 0, 2, 3, 4).reshape(B, Lq, H * hd)


def mlstm_chunkwise(q, k, v, i_pre, logf, C0, n0, m0, chunk):
    B, L, H, _ = q.shape
    nc = L // chunk

    def to_chunks(t):
        t = t.reshape((B, nc, chunk, H) + t.shape[3:])
        return jnp.moveaxis(t, (1, 3), (0, 2))

    causal = jnp.tril(jnp.ones((chunk, chunk), dtype=bool))

    def step(carry, xs):
        C, n, m = carry
        qc, kc, vc, ic, fc = xs
        b = jnp.cumsum(fc, axis=-1)
        dmat = jnp.where(causal, b[..., :, None] - b[..., None, :] + ic[..., None, :], -jnp.inf)
        inter = b + m[..., None]
        m_t = jnp.maximum(inter, dmat.max(axis=-1))
        w_inter = jnp.exp(inter - m_t)
        s = jnp.einsum('bhtd,bhsd->bhts', qc, kc) * jnp.exp(dmat - m_t[..., None])
        num = w_inter[..., None] * jnp.einsum('bhvd,bhtd->bhtv', C, qc) + jnp.einsum('bhts,bhsv->bhtv', s, vc)
        den = w_inter * jnp.einsum('bhd,bhtd->bht', n, qc) + s.sum(axis=-1)
        h = num / jnp.maximum(jnp.abs(den), jnp.exp(-m_t))[..., None]
        b_end = b[..., -1]
        g = b_end[..., None] - b + ic
        m_new = jnp.maximum(b_end + m, g.max(axis=-1))
        a = jnp.exp(b_end + m - m_new)
        w = jnp.exp(g - m_new[..., None])
        C_new = a[..., None, None] * C + jnp.einsum('bhs,bhsv,bhsd->bhvd', w, vc, kc)
        n_new = a[..., None] * n + jnp.einsum('bhs,bhsd->bhd', w, kc)
        return (C_new, n_new, m_new), h

    (C, n, m), h = lax.scan(step, (C0, n0, m0),
                            (to_chunks(q), to_chunks(k), to_chunks(v), to_chunks(i_pre), to_chunks(logf)))
    h = h.transpose(1, 0, 3, 2, 4).reshape(B, L, H, v.shape[-1])
    return h, C, n, m


def mlstm_branch(q, k, v, i_pre, f_pre, o_pre, b_igate, b_fgate, g_out, C0, n0, m0, chunk):
    B, L, _ = q.shape
    f32 = jnp.float32
    q = q.reshape(B, L, N_ML_HEADS, ML_QK_DIM).astype(f32)
    k = k.reshape(B, L, N_ML_HEADS, ML_QK_DIM).astype(f32) * (ML_QK_DIM ** -0.5)
    v = v.reshape(B, L, N_ML_HEADS, ML_V_DIM).astype(f32)
    i = (i_pre + b_igate).astype(f32)
    logf = jax.nn.log_sigmoid((f_pre + b_fgate).astype(f32))
    h, C, n, m = mlstm_chunkwise(q, k, v, i, logf, C0.astype(f32), n0.astype(f32), m0.astype(f32), chunk)
    h = rms_norm(h, g_out)
    out = h.reshape(B, L, ML_V_WIDTH).astype(o_pre.dtype) * jax.nn.sigmoid(o_pre)
    return out, C, n, m


def decoder_layer(x, pe, past_k, past_v, C0, n0, m0, chunk, q_block,
                  w_in, b_igate, b_fgate, g_mix_norm, g_q_norm, g_k_norm, g_ml_out_norm,
                  w_att_branch, w_ml_branch, w_out, g_ffn_norm, w_ffn_gate, w_ffn_up, w_ffn_down,
                  w_ple, g_ple_norm, w_ple_gate):
    B, L, _ = x.shape
    start = 0 if past_k is None else past_k.shape[1]
    pos = start + jnp.arange(L, dtype=jnp.int32)
    xn = rms_norm(x, g_mix_norm)
    z = xn @ w_in
    aq, ak, av, mq, mk, mv, mi, mf, mo, ga, gm = split_columns(z)
    q = rope_partial(rms_norm(aq.reshape(B, L, N_ATT_HEADS, ATT_HEAD_DIM), g_q_norm), pos)
    k = rope_partial(rms_norm(ak.reshape(B, L, N_ATT_HEADS, ATT_HEAD_DIM), g_k_norm), pos)
    v = av.reshape(B, L, N_ATT_HEADS, ATT_HEAD_DIM)
    if past_k is None:
        k_all, v_all = k, v
    else:
        k_all = jnp.concatenate([past_k.astype(k.dtype), k], axis=1)
        v_all = jnp.concatenate([past_v.astype(v.dtype), v], axis=1)
    att = moba_attention(q, k_all, v_all, start, q_block)
    ml, C, n, m = mlstm_branch(mq, mk, mv, mi, mf, mo, b_igate, b_fgate, g_ml_out_norm, C0, n0, m0, chunk)
    mix = jax.nn.sigmoid(ga) * (att @ w_att_branch) + jax.nn.sigmoid(gm) * (ml @ w_ml_branch)
    h = x + mix @ w_out
    hn = rms_norm(h, g_ffn_norm)
    h = h + (jax.nn.silu(hn @ w_ffn_gate) * (hn @ w_ffn_up)) @ w_ffn_down
    gate = jax.nn.sigmoid(rms_norm(h, g_ple_norm) @ w_ple_gate)
    y = h + gate * (pe @ w_ple)
    return y, k, v, C, n, m


def setup_inputs(seed: int = 0) -> dict:
    key = jax.random.key(seed)
    ks = jax.random.split(key, 32)
    f32 = jnp.float32
    n_pages = PAST_LEN // PAGE_SIZE
    n_pool = (DEC_BATCH * n_pages * 5) // 4

    def nrm(k, shape, scale=1.0):
        return jax.random.normal(k, shape, f32) * scale

    def gain(k, shape):
        return 1.0 + 0.1 * nrm(k, shape)

    page_table = jax.random.permutation(ks[6], n_pool)[:DEC_BATCH * n_pages].reshape(DEC_BATCH, n_pages).astype(jnp.int32)
    return {
        "x_prompt": nrm(ks[0], (BATCH, SEQ, D_MODEL)),
        "x_sample": nrm(ks[1], (DEC_BATCH, DEC_SEQ, D_MODEL)),
        "cache_k": nrm(ks[2], (DEPTH, n_pool, PAGE_SIZE, N_ATT_HEADS, ATT_HEAD_DIM)),
        "cache_v": nrm(ks[3], (DEPTH, n_pool, PAGE_SIZE, N_ATT_HEADS, ATT_HEAD_DIM)),
        "state_mlstm_C": nrm(ks[4], (DEPTH, DEC_BATCH, N_ML_HEADS, ML_V_DIM, ML_QK_DIM), 0.5),
        "state_mlstm_n": nrm(ks[5], (DEPTH, DEC_BATCH, N_ML_HEADS, ML_QK_DIM), 0.5),
        "state_mlstm_m": nrm(ks[7], (DEPTH, DEC_BATCH, N_ML_HEADS)),
        "page_table": page_table,
        "p_prompt": nrm(ks[8], (DEPTH, BATCH, SEQ, PLE_DIM)),
        "p_sample": nrm(ks[9], (DEPTH, DEC_BATCH, DEC_SEQ, PLE_DIM)),
        "w_in": nrm(ks[10], (DEPTH, D_MODEL, D_IN), D_MODEL ** -0.5),
        "b_igate": nrm(ks[11], (DEPTH, N_ML_HEADS), 0.1),
        "b_fgate": 3.0 + nrm(ks[12], (DEPTH, N_ML_HEADS), 0.5),
        "g_mix_norm": gain(ks[13], (DEPTH, D_MODEL)),
        "g_q_norm": gain(ks[14], (DEPTH, ATT_HEAD_DIM)),
        "g_k_norm": gain(ks[15], (DEPTH, ATT_HEAD_DIM)),
        "g_ml_out_norm": gain(ks[16], (DEPTH, N_ML_HEADS, ML_V_DIM)),
        "w_att_branch": nrm(ks[17], (DEPTH, ATT_WIDTH, D_MODEL), ATT_WIDTH ** -0.5),
        "w_ml_branch": nrm(ks[18], (DEPTH, ML_V_WIDTH, D_MODEL), ML_V_WIDTH ** -0.5),
        "w_out": nrm(ks[19], (DEPTH, D_MODEL, D_MODEL), D_MODEL ** -0.5),
        "g_ffn_norm": gain(ks[20], (DEPTH, D_MODEL)),
        "w_ffn_gate": nrm(ks[21], (DEPTH, D_MODEL, D_FF), D_MODEL ** -0.5),
        "w_ffn_up": nrm(ks[22], (DEPTH, D_MODEL, D_FF), D_MODEL ** -0.5),
        "w_ffn_down": nrm(ks[23], (DEPTH, D_FF, D_MODEL), D_FF ** -0.5),
        "w_ple": nrm(ks[24], (DEPTH, PLE_DIM, D_MODEL), PLE_DIM ** -0.5),
        "g_ple_norm": gain(ks[25], (DEPTH, D_MODEL)),
        "w_ple_gate": nrm(ks[26], (DEPTH, D_MODEL, D_MODEL), D_MODEL ** -0.5),
    }


def reference(x_prompt, x_sample, cache_k, cache_v, state_mlstm_C, state_mlstm_n, state_mlstm_m, page_table,
              p_prompt, p_sample, w_in, b_igate, b_fgate, g_mix_norm, g_q_norm, g_k_norm, g_ml_out_norm,
              w_att_branch, w_ml_branch, w_out, g_ffn_norm, w_ffn_gate, w_ffn_up, w_ffn_down,
              w_ple, g_ple_norm, w_ple_gate):
    Bp, Lp, _ = x_prompt.shape
    Bs, Ls, _ = x_sample.shape
    xp, xs = x_prompt, x_sample
    kp_l, vp_l, ks_l, vs_l = [], [], [], []
    Cp_l, np_l, mp_l, Cs_l, ns_l, ms_l = [], [], [], [], [], []
    for l in range(DEPTH):
        weights = dict(w_in=w_in[l], b_igate=b_igate[l], b_fgate=b_fgate[l], g_mix_norm=g_mix_norm[l],
                       g_q_norm=g_q_norm[l], g_k_norm=g_k_norm[l], g_ml_out_norm=g_ml_out_norm[l],
                       w_att_branch=w_att_branch[l], w_ml_branch=w_ml_branch[l], w_out=w_out[l],
                       g_ffn_norm=g_ffn_norm[l], w_ffn_gate=w_ffn_gate[l], w_ffn_up=w_ffn_up[l],
                       w_ffn_down=w_ffn_down[l], w_ple=w_ple[l], g_ple_norm=g_ple_norm[l],
                       w_ple_gate=w_ple_gate[l])
        C0 = jnp.zeros((Bp, N_ML_HEADS, ML_V_DIM, ML_QK_DIM), jnp.float32)
        n0 = jnp.zeros((Bp, N_ML_HEADS, ML_QK_DIM), jnp.float32)
        m0 = jnp.zeros((Bp, N_ML_HEADS), jnp.float32)
        xp, kp, vp, Cp, n_p, mp = decoder_layer(xp, p_prompt[l], None, None, C0, n0, m0, ML_CHUNK,
                                                min(MOBA_QUERY_BLOCK, Lp), **weights)
        past_k = cache_k[l][page_table].reshape(Bs, -1, N_ATT_HEADS, ATT_HEAD_DIM)
        past_v = cache_v[l][page_table].reshape(Bs, -1, N_ATT_HEADS, ATT_HEAD_DIM)
        xs, ks_, vs_, Cs, ns, ms = decoder_layer(xs, p_sample[l], past_k, past_v, state_mlstm_C[l],
                                                 state_mlstm_n[l], state_mlstm_m[l], Ls, Ls, **weights)
        kp_l.append(kp); vp_l.append(vp); ks_l.append(ks_); vs_l.append(vs_)
        Cp_l.append(Cp); np_l.append(n_p); mp_l.append(mp)
        Cs_l.append(Cs); ns_l.append(ns); ms_l.append(ms)
    return (xp, xs, jnp.stack(kp_l), jnp.stack(vp_l), jnp.stack(ks_l), jnp.stack(vs_l),
            jnp.stack(Cp_l), jnp.stack(np_l), jnp.stack(mp_l), jnp.stack(Cs_l), jnp.stack(ns_l), jnp.stack(ms_l))
```

```python
import functools

import jax
import jax.numpy as jnp
from jax import lax
from jax.experimental import pallas as pl
from jax.experimental.pallas import tpu as pltpu

F32 = jnp.float32
BF16 = jnp.bfloat16

D_MODEL = 1024
N_ATT_HEADS = 8
ATT_HEAD_DIM = 64
ATT_WIDTH = N_ATT_HEADS * ATT_HEAD_DIM
ROPE_DIMS = ATT_HEAD_DIM // 4
ROPE_THETA = 500000.0
MOBA_BLOCK = 256
MOBA_TOPK = 3
PAGE_SIZE = 128
N_ML_HEADS = 4
ML_QK_DIM = 128
ML_V_DIM = 256
ML_QK_WIDTH = N_ML_HEADS * ML_QK_DIM
ML_V_WIDTH = N_ML_HEADS * ML_V_DIM
NORM_EPS = 1e-6
NEG_INF = -1e30

LANES = 128
HEAD_PAIR = LANES // ATT_HEAD_DIM
ML_CHUNK = 256
VMEM_LIMIT = 56 * 1024 * 1024

_C_Q, _C_K, _C_V = 0, ATT_WIDTH, 2 * ATT_WIDTH
_C_MQ = 3 * ATT_WIDTH
_C_MK = _C_MQ + ML_QK_WIDTH
_C_MV = _C_MK + ML_QK_WIDTH
_C_GATES = _C_MV + ML_V_WIDTH
_C_MO = _C_GATES + 2 * N_ML_HEADS
_C_GA = _C_MO + ML_V_WIDTH
_C_GM = _C_GA + D_MODEL
_C_END = _C_GM + D_MODEL


def _dot(a, b):
    return jnp.dot(a, b, preferred_element_type=F32)


def _dot_nt(a, b, precision=None):
    return lax.dot_general(a, b, (((1,), (1,)), ((), ())), precision=precision,
                           preferred_element_type=F32)


def _rms(x, g):
    return x * lax.rsqrt(jnp.mean(x * x, axis=-1, keepdims=True) + NORM_EPS) * g


def _log_sigmoid(x):
    return jnp.minimum(x, 0.0) - jnp.log1p(jnp.exp(-jnp.abs(x)))


def _proj_shared(xb, wm_ref, wg_ref, wgt_ref, wb_ref,
                 mq_ref, mk_ref, mv_ref, gcol_ref, grow_ref, mo_ref, ga_ref, gm_ref):
    mq_ref[0] = _dot(xb, wm_ref[:, 0:ML_QK_WIDTH]).astype(BF16)
    mk_ref[0] = (_dot(xb, wm_ref[:, ML_QK_WIDTH:2 * ML_QK_WIDTH]) * (ML_QK_DIM ** -0.5)).astype(BF16)
    mv_ref[0] = _dot(xb, wm_ref[:, 2 * ML_QK_WIDTH:2 * ML_QK_WIDTH + ML_V_WIDTH]).astype(BF16)
    gcol_ref[0] = _dot(xb, wg_ref[...])
    grow_ref[0] = _dot_nt(wgt_ref[...], xb)
    mo_ref[0] = _dot(xb, wb_ref[:, 0:ML_V_WIDTH])
    ga_ref[0] = _dot(xb, wb_ref[:, ML_V_WIDTH:ML_V_WIDTH + D_MODEL])
    gm_ref[0] = _dot(xb, wb_ref[:, ML_V_WIDTH + D_MODEL:ML_V_WIDTH + 2 * D_MODEL])


def _proj_kernel_rows(x_ref, cos_ref, sa_ref, sb_ref, gmix_ref, gq_ref, gk_ref, e_ref, wqkv_ref,
                      wm_ref, wg_ref, wgt_ref, wb_ref, q_ref, k_ref, v_ref, *rest_refs):
    xb = _rms(x_ref[0], gmix_ref[...]).astype(BF16)
    cos, sa, sb = cos_ref[...], sa_ref[...], sb_ref[...]
    e = e_ref[...]

    def head_norm_rope(z, g):
        zz = z * z
        hi = zz.astype(BF16)
        lo = (zz - hi.astype(F32)).astype(BF16)
        ss = _dot(hi, e) + _dot(lo, e)
        zn = z * lax.rsqrt(ss * (1.0 / ATT_HEAD_DIM) + NORM_EPS) * g
        outs = []
        for grp in range(ATT_WIDTH // LANES):
            zg = zn[:, grp * LANES:(grp + 1) * LANES]
            outs.append(zg * cos + pltpu.roll(zg, LANES - ROPE_DIMS // 2, 1) * sa
                        + pltpu.roll(zg, ROPE_DIMS // 2, 1) * sb)
        return jnp.concatenate(outs, axis=-1)

    q_ref[0] = head_norm_rope(_dot(xb, wqkv_ref[:, 0:ATT_WIDTH]), gq_ref[...])
    k_ref[0] = head_norm_rope(_dot(xb, wqkv_ref[:, ATT_WIDTH:2 * ATT_WIDTH]), gk_ref[...])
    v_ref[0] = _dot(xb, wqkv_ref[:, 2 * ATT_WIDTH:3 * ATT_WIDTH])
    _proj_shared(xb, wm_ref, wg_ref, wgt_ref, wb_ref, *rest_refs)


def _proj_kernel_cols(x_ref, cos_ref, sin_ref, gmix_ref, gq_ref, gk_ref, wqkvt_ref,
                      wm_ref, wg_ref, wgt_ref, wb_ref, qt_ref, kt_ref, vt_ref, *rest_refs):
    xb = _rms(x_ref[0], gmix_ref[...]).astype(BF16)
    tm = xb.shape[0]
    zt = _dot_nt(wqkvt_ref[...], xb)
    cos, sin = cos_ref[...][None], sin_ref[...][None]
    half = ROPE_DIMS // 2

    def head_norm_rope(z, g):
        z3 = z.reshape(N_ATT_HEADS, ATT_HEAD_DIM, tm)
        ss = jnp.sum(z3 * z3, axis=1, keepdims=True)
        zn = z3 * lax.rsqrt(ss * (1.0 / ATT_HEAD_DIM) + NORM_EPS) * g
        x1, x2 = zn[:, 0:half, :], zn[:, half:ROPE_DIMS, :]
        out = jnp.concatenate([x1 * cos - x2 * sin, x2 * cos + x1 * sin, zn[:, ROPE_DIMS:, :]], axis=1)
        return out.reshape(ATT_WIDTH, tm)

    qt_ref[0] = head_norm_rope(zt[0:ATT_WIDTH], gq_ref[...])
    kt_ref[0] = head_norm_rope(zt[ATT_WIDTH:2 * ATT_WIDTH], gk_ref[...])
    vt_ref[0] = zt[2 * ATT_WIDTH:3 * ATT_WIDTH]
    _proj_shared(xb, wm_ref, wg_ref, wgt_ref, wb_ref, *rest_refs)


def _project(x, pos, pw, tm, feature_major):
    B, L, _ = x.shape
    tok = lambda w: pl.BlockSpec((1, tm, w), lambda b, i: (b, i, 0))
    full = lambda a: pl.BlockSpec(a.shape, lambda b, i: (0,) * a.ndim)
    shared = (pw["w_m"], pw["w_g"], pw["w_gt"], pw["w_b"])
    rest_shape = (
        jax.ShapeDtypeStruct((B, L, ML_QK_WIDTH), BF16),
        jax.ShapeDtypeStruct((B, L, ML_QK_WIDTH), BF16),
        jax.ShapeDtypeStruct((B, L, ML_V_WIDTH), BF16),
        jax.ShapeDtypeStruct((B, L, LANES), F32),
        jax.ShapeDtypeStruct((B, 16, L), F32),
        jax.ShapeDtypeStruct((B, L, ML_V_WIDTH), F32),
        jax.ShapeDtypeStruct((B, L, D_MODEL), F32),
        jax.ShapeDtypeStruct((B, L, D_MODEL), F32),
    )
    rest_specs = (tok(ML_QK_WIDTH), tok(ML_QK_WIDTH), tok(ML_V_WIDTH), tok(LANES),
                  pl.BlockSpec((1, 16, tm), lambda b, i: (b, 0, i)),
                  tok(ML_V_WIDTH), tok(D_MODEL), tok(D_MODEL))
    if feature_major:
        cos, sin = _rope_tables_cols(pos)
        tab = pl.BlockSpec((ROPE_DIMS // 2, tm), lambda b, i: (0, i))
        consts = (pw["g_mix"], pw["g_q_col"], pw["g_k_col"], pw["w_qkv_t"]) + shared
        body, tables, tab_specs = _proj_kernel_cols, (cos, sin), [tab, tab]
        qkv_shape = (jax.ShapeDtypeStruct((B, ATT_WIDTH, L), F32),) * 3
        qkv_specs = (pl.BlockSpec((1, ATT_WIDTH, tm), lambda b, i: (b, 0, i)),) * 3
    else:
        tables = _rope_tables_rows(pos)
        tab = pl.BlockSpec((tm, LANES), lambda b, i: (i, 0))
        consts = (pw["g_mix"], pw["g_q"], pw["g_k"], pw["e_head"], pw["w_qkv"]) + shared
        body, tab_specs = _proj_kernel_rows, [tab, tab, tab]
        qkv_shape = (jax.ShapeDtypeStruct((B, L, ATT_WIDTH), F32),) * 3
        qkv_specs = (tok(ATT_WIDTH),) * 3
    return pl.pallas_call(
        body,
        grid=(B, L // tm),
        in_specs=[tok(D_MODEL)] + tab_specs + [full(a) for a in consts],
        out_specs=qkv_specs + rest_specs,
        out_shape=qkv_shape + rest_shape,
        compiler_params=pltpu.CompilerParams(dimension_semantics=("parallel", "parallel"),
                                             vmem_limit_bytes=VMEM_LIMIT),
        name="in_proj_cols" if feature_major else "in_proj_rows",
    )(x, *tables, *consts)


def _attn_kernel(q_ref, k_ref, v_ref, o_ref, kb_ref, vt_ref, kmean_ref, bias_ref, acc_ref, *, nblk):
    blk = MOBA_BLOCK
    c = pl.program_id(2)

    @pl.when(c == 0)
    def _():
        for n in range(nblk):
            kf = k_ref[0, :, n * blk:(n + 1) * blk].T
            kb_ref[n * blk:(n + 1) * blk, :] = kf.astype(BF16)
            kmean_ref[n:n + 1, :] = jnp.sum(kf, axis=0, keepdims=True) * (1.0 / blk)
            vt_ref[n] = v_ref[0, :, n * blk:(n + 1) * blk].astype(BF16)

    q2 = q_ref[0]
    feat = lax.broadcasted_iota(jnp.int32, (LANES, blk), 0)
    blk_id = lax.broadcasted_iota(jnp.int32, (nblk, blk), 0)
    kmean = kmean_ref[...]

    qbs = []
    for h in range(HEAD_PAIR):
        qh = jnp.where((feat >= h * ATT_HEAD_DIM) & (feat < (h + 1) * ATT_HEAD_DIM), q2, 0.0)
        gate = jnp.dot(kmean, qh, precision=lax.Precision.HIGHEST,
                       preferred_element_type=F32)
        rank = jnp.zeros((nblk, blk), F32)
        for n2 in range(nblk):
            row = gate[n2:n2 + 1, :]
            beats = (row > gate) | ((row == gate) & (blk_id > n2))
            rank = rank + jnp.where(beats, 1.0, 0.0) * (n2 < c).astype(F32)
        sel = (blk_id < c) & (rank < MOBA_TOPK)
        bias_ref[h] = jnp.where(sel, 0.0, NEG_INF)
        qbs.append((qh * (ATT_HEAD_DIM ** -0.5)).astype(BF16))

    key_i = lax.broadcasted_iota(jnp.int32, (blk, blk), 0)
    qry_j = lax.broadcasted_iota(jnp.int32, (blk, blk), 1)
    r_own = pl.multiple_of(c * blk, blk)
    k_own = kb_ref[pl.ds(r_own, blk), :]
    vt_own = vt_ref[c]
    carry = []
    for h in range(HEAD_PAIR):
        st = jnp.where(key_i <= qry_j, _dot(k_own, qbs[h]), NEG_INF)
        m = jnp.max(st, axis=0, keepdims=True)
        p = jnp.exp(st - m)
        carry += [m, jnp.sum(p, axis=0, keepdims=True)]
        acc_ref[h] = _dot(vt_own, p.astype(BF16))

    def sweep(n, carry):
        r = pl.multiple_of(n * blk, blk)
        kn = kb_ref[pl.ds(r, blk), :]
        vtn = vt_ref[n]
        new = []
        for h in range(HEAD_PAIR):
            m, l = carry[2 * h], carry[2 * h + 1]
            st = _dot(kn, qbs[h]) + bias_ref[h, pl.ds(n, 1), :]
            m_new = jnp.maximum(m, jnp.max(st, axis=0, keepdims=True))
            alpha = jnp.exp(m - m_new)
            p = jnp.exp(st - m_new)
            acc_ref[h] = alpha * acc_ref[h] + _dot(vtn, p.astype(BF16))
            new += [m_new, alpha * l + jnp.sum(p, axis=0, keepdims=True)]
        return tuple(new)

    carry = lax.fori_loop(0, c, sweep, tuple(carry))
    ot = jnp.where(feat < ATT_HEAD_DIM, acc_ref[0] / carry[1], acc_ref[1] / carry[3])
    o_ref[0] = ot.T.astype(o_ref.dtype)


def _attention_prompt(qt, kt, vt):
    B, _, L = qt.shape
    blk = MOBA_BLOCK
    nblk = L // blk
    groups = ATT_WIDTH // LANES
    return pl.pallas_call(
        functools.partial(_attn_kernel, nblk=nblk),
        grid=(B, groups, nblk),
        in_specs=[pl.BlockSpec((1, LANES, blk), lambda b, g, i: (b, g, i)),
                  pl.BlockSpec((1, LANES, L), lambda b, g, i: (b, g, 0)),
                  pl.BlockSpec((1, LANES, L), lambda b, g, i: (b, g, 0))],
        out_specs=pl.BlockSpec((1, blk, LANES), lambda b, g, i: (b, i, g)),
        out_shape=jax.ShapeDtypeStruct((B, L, ATT_WIDTH), BF16),
        scratch_shapes=[pltpu.VMEM((L, LANES), BF16),
                        pltpu.VMEM((nblk, LANES, blk), BF16),
                        pltpu.VMEM((nblk, LANES), F32),
                        pltpu.VMEM((HEAD_PAIR, nblk, blk), F32),
                        pltpu.VMEM((HEAD_PAIR, LANES, blk), F32)],
        compiler_params=pltpu.CompilerParams(dimension_semantics=("parallel", "parallel", "arbitrary"),
                                             vmem_limit_bytes=VMEM_LIMIT),
        name="moba_prompt",
    )(qt, kt, vt)


def _mlstm_kernel(bi_ref, bf_ref, mq_ref, mk_ref, mv_ref, gcol_ref, grow_ref, mo_ref, gout_ref,
                  ml_ref, c_out_ref, n_out_ref, m_out_ref, ct_ref, n_ref, m_ref, *, chunk):
    j = pl.program_id(1)

    @pl.when(j == 0)
    def _():
        ct_ref[...] = jnp.zeros_like(ct_ref)
        n_ref[...] = jnp.zeros_like(n_ref)
        m_ref[...] = jnp.zeros_like(m_ref)

    t_i = lax.broadcasted_iota(jnp.int32, (chunk, chunk), 0)
    s_i = lax.broadcasted_iota(jnp.int32, (chunk, chunk), 1)
    tril = s_i <= t_i
    gcol = gcol_ref[0]
    grow = grow_ref[0]
    for h in range(N_ML_HEADS):
        i_col = gcol[:, h:h + 1] + bi_ref[h]
        i_row = grow[h:h + 1, :] + bi_ref[h]
        f_col = _log_sigmoid(gcol[:, N_ML_HEADS + h:N_ML_HEADS + h + 1] + bf_ref[h])
        f_row = _log_sigmoid(grow[N_ML_HEADS + h:N_ML_HEADS + h + 1, :] + bf_ref[h])
        b_row = jnp.sum(jnp.where(t_i <= s_i, f_col, 0.0), axis=0, keepdims=True)
        b_col = jnp.sum(jnp.where(tril, f_row, 0.0), axis=1, keepdims=True)
        dmat = jnp.where(tril, b_col - b_row + i_row, -jnp.inf)
        m_old = m_ref[h][:, 0:1]
        inter = b_col + m_old
        m_t = jnp.maximum(inter, jnp.max(dmat, axis=1, keepdims=True))
        w_inter = jnp.exp(inter - m_t)

        q = mq_ref[0, :, h * ML_QK_DIM:(h + 1) * ML_QK_DIM]
        k = mk_ref[0, :, h * ML_QK_DIM:(h + 1) * ML_QK_DIM]
        v = mv_ref[0, :, h * ML_V_DIM:(h + 1) * ML_V_DIM]
        qf = q.astype(F32)
        kf = k.astype(F32)
        ct = ct_ref[h]
        n_old = n_ref[h]
        s = _dot_nt(q, k) * jnp.exp(dmat - m_t)
        num = w_inter * _dot(q, ct.astype(BF16)) + _dot(s.astype(BF16), v)
        den = w_inter * jnp.sum(qf * n_old, axis=1, keepdims=True) + jnp.sum(s, axis=1, keepdims=True)
        hid = num / jnp.maximum(jnp.abs(den), jnp.exp(-m_t))

        b_end = b_col[chunk - 1:chunk, :]
        g_row = b_end - b_row + i_row
        g_col = b_end - b_col + i_col
        m_new = jnp.maximum(b_end + m_old, jnp.max(g_row, axis=1, keepdims=True))
        a = jnp.exp(b_end + m_old - m_new)
        w_col = jnp.exp(g_col - m_new)
        vw = (v.astype(F32) * w_col).astype(BF16)
        ct_ref[h] = a * ct + _dot(kf.T.astype(BF16), vw)
        n_ref[h] = a * n_old + jnp.sum(kf * w_col, axis=0, keepdims=True)
        m_ref[h] = jnp.broadcast_to(m_new, (1, LANES))

        hn = _rms(hid, gout_ref[h:h + 1, :])
        mo = mo_ref[0, :, h * ML_V_DIM:(h + 1) * ML_V_DIM]
        ml_ref[0, :, h * ML_V_DIM:(h + 1) * ML_V_DIM] = (hn * jax.nn.sigmoid(mo)).astype(BF16)

    @pl.when(j == pl.num_programs(1) - 1)
    def _():
        for h in range(N_ML_HEADS):
            c_out_ref[0, h] = ct_ref[h].T
            n_out_ref[0, h:h + 1, :] = n_ref[h]
            m_out_ref[0, h:h + 1, :] = m_ref[h]


def _mlstm_prompt(mq, mk, mv, gcol, grow, mo, pw):
    B, L, _ = mq.shape
    chunk = ML_CHUNK
    tok = lambda w: pl.BlockSpec((1, chunk, w), lambda b, j, *_: (b, j, 0))
    per_b = lambda shp: pl.BlockSpec((1,) + shp, lambda b, j, *_: (b,) + (0,) * len(shp))
    grid_spec = pltpu.PrefetchScalarGridSpec(
        num_scalar_prefetch=2,
        grid=(B, L // chunk),
        in_specs=[tok(ML_QK_WIDTH), tok(ML_QK_WIDTH), tok(ML_V_WIDTH), tok(LANES),
                  pl.BlockSpec((1, 16, chunk), lambda b, j, *_: (b, 0, j)), tok(ML_V_WIDTH),
                  pl.BlockSpec((N_ML_HEADS, ML_V_DIM), lambda b, j, *_: (0, 0))],
        out_specs=(tok(ML_V_WIDTH), per_b((N_ML_HEADS, ML_V_DIM, ML_QK_DIM)),
                   per_b((N_ML_HEADS, ML_QK_DIM)), per_b((N_ML_HEADS, LANES))),
        scratch_shapes=[pltpu.VMEM((N_ML_HEADS, ML_QK_DIM, ML_V_DIM), F32),
                        pltpu.VMEM((N_ML_HEADS, 1, ML_QK_DIM), F32),
                        pltpu.VMEM((N_ML_HEADS, 1, LANES), F32)],
    )
    return pl.pallas_call(
        functools.partial(_mlstm_kernel, chunk=chunk),
        grid_spec=grid_spec,
        out_shape=(jax.ShapeDtypeStruct((B, L, ML_V_WIDTH), BF16),
                   jax.ShapeDtypeStruct((B, N_ML_HEADS, ML_V_DIM, ML_QK_DIM), F32),
                   jax.ShapeDtypeStruct((B, N_ML_HEADS, ML_QK_DIM), F32),
                   jax.ShapeDtypeStruct((B, N_ML_HEADS, LANES), F32)),
        compiler_params=pltpu.CompilerParams(dimension_semantics=("parallel", "arbitrary"),
                                             vmem_limit_bytes=VMEM_LIMIT),
        name="mlstm_prompt",
    )(pw["b_i"], pw["b_f"], mq, mk, mv, gcol, grow, mo, pw["g_ml_out"])


def _post_kernel(x_ref, att_ref, ml_ref, ga_ref, gm_ref, pe_ref,
                 watt_ref, wml_ref, wout_ref, gffn_ref, wgate_ref, wup_ref, wdown_ref,
                 wple_ref, gple_ref, wpg_ref, y_ref):
    mix = (jax.nn.sigmoid(ga_ref[0]) * _dot(att_ref[0].astype(BF16), watt_ref[...])
           + jax.nn.sigmoid(gm_ref[0]) * _dot(ml_ref[0], wml_ref[...]))
    h = x_ref[0] + _dot(mix.astype(BF16), wout_ref[...])
    hb = _rms(h, gffn_ref[...]).astype(BF16)
    act = jax.nn.silu(_dot(hb, wgate_ref[...])) * _dot(hb, wup_ref[...])
    h = h + _dot(act.astype(BF16), wdown_ref[...])
    gate = jax.nn.sigmoid(_dot(_rms(h, gple_ref[...]).astype(BF16), wpg_ref[...]))
    y_ref[0] = h + gate * _dot(pe_ref[0].astype(BF16), wple_ref[...])


def _post(x, att, ml, ga, gm, pe, pw, tm):
    B, L, _ = x.shape
    tok = lambda w: pl.BlockSpec((1, tm, w), lambda b, i: (b, i, 0))
    full = lambda a: pl.BlockSpec(a.shape, lambda b, i: (0,) * a.ndim, pipeline_mode=pl.Buffered(1))
    consts = (pw["w_att"], pw["w_ml"], pw["w_out"], pw["g_ffn"], pw["w_ffn_gate"], pw["w_ffn_up"],
              pw["w_ffn_down"], pw["w_ple"], pw["g_ple"], pw["w_ple_gate"])
    return pl.pallas_call(
        _post_kernel,
        grid=(B, L // tm),
        in_specs=[tok(D_MODEL), tok(ATT_WIDTH), tok(ML_V_WIDTH), tok(D_MODEL), tok(D_MODEL),
                  tok(pe.shape[-1])] + [full(a) for a in consts],
        out_specs=tok(D_MODEL),
        out_shape=jax.ShapeDtypeStruct((B, L, D_MODEL), F32),
        compiler_params=pltpu.CompilerParams(dimension_semantics=("parallel", "parallel"),
                                             vmem_limit_bytes=VMEM_LIMIT),
        name="merge_ffn",
    )(x, att, ml, ga, gm, pe, *consts)


def _sample_select_kernel(pt_ref, ka_ref, kb_ref, q_ref, sel_ref, qbd_ref, gate_ref, *, nblk):
    j = pl.program_id(1)
    lane = lax.broadcasted_iota(jnp.int32, (N_ATT_HEADS, LANES), 1)

    @pl.when(j == 0)
    def _():
        head = lax.broadcasted_iota(jnp.int32, (N_ATT_HEADS, ATT_WIDTH), 0)
        col = lax.broadcasted_iota(jnp.int32, (N_ATT_HEADS, ATT_WIDTH), 1)
        in_head = (col >= head * ATT_HEAD_DIM) & (col < (head + 1) * ATT_HEAD_DIM)
        qbd = jnp.where(in_head, q_ref[0], 0.0)
        hi = qbd.astype(BF16).astype(F32)
        qbd_ref[...] = jnp.concatenate([hi, qbd - hi], axis=0).astype(BF16)
        gate_ref[...] = jnp.zeros_like(gate_ref)

    qbd = qbd_ref[...]
    s = _dot(qbd, ka_ref[0].astype(BF16)) + _dot(qbd, kb_ref[0].astype(BF16))
    g = jnp.sum(s[0:N_ATT_HEADS] + s[N_ATT_HEADS:], axis=1, keepdims=True) * (1.0 / MOBA_BLOCK)
    gate_ref[...] = jnp.where(lane == j, g, gate_ref[...])

    @pl.when(j == nblk - 1)
    def _():
        gate = gate_ref[...]
        rank = jnp.zeros((N_ATT_HEADS, LANES), F32)
        for n2 in range(nblk):
            col = gate[:, n2:n2 + 1]
            beats = (col > gate) | ((col == gate) & (lane > n2))
            rank = rank + jnp.where(beats, 1.0, 0.0)
        out = jnp.zeros((N_ATT_HEADS, LANES), jnp.int32)
        lane_f = lane.astype(F32)
        for r in range(MOBA_TOPK):
            pick = (rank == float(r)) & (lane < nblk)
            idx = jnp.sum(jnp.where(pick, lane_f, 0.0), axis=1, keepdims=True)
            out = jnp.where(lane == r, idx.astype(jnp.int32), out)
        sel_ref[0] = out


def _sample_select(page_table, cache_kt, q_s):
    Bs, n_pages = page_table.shape
    nblk = n_pages * PAGE_SIZE // MOBA_BLOCK
    assert MOBA_BLOCK == 2 * PAGE_SIZE and nblk <= LANES
    grid_spec = pltpu.PrefetchScalarGridSpec(
        num_scalar_prefetch=1,
        grid=(Bs, nblk),
        in_specs=[pl.BlockSpec((1, ATT_WIDTH, PAGE_SIZE), lambda b, j, pt: (pt[b, 2 * j], 0, 0)),
                  pl.BlockSpec((1, ATT_WIDTH, PAGE_SIZE), lambda b, j, pt: (pt[b, 2 * j + 1], 0, 0)),
                  pl.BlockSpec((1, 1, ATT_WIDTH), lambda b, j, pt: (b, 0, 0))],
        out_specs=pl.BlockSpec((1, N_ATT_HEADS, LANES), lambda b, j, pt: (b, 0, 0)),
        scratch_shapes=[pltpu.VMEM((2 * N_ATT_HEADS, ATT_WIDTH), BF16),
                        pltpu.VMEM((N_ATT_HEADS, LANES), F32)],
    )
    return pl.pallas_call(
        functools.partial(_sample_select_kernel, nblk=nblk),
        grid_spec=grid_spec,
        out_shape=jax.ShapeDtypeStruct((Bs, N_ATT_HEADS, LANES), jnp.int32),
        compiler_params=pltpu.CompilerParams(dimension_semantics=("parallel", "arbitrary"),
                                             vmem_limit_bytes=VMEM_LIMIT),
        name="moba_sample_select",
    )(page_table, cache_kt, cache_kt, q_s)


def _sample_attn_kernel(pt_ref, sel_ref, kp_ref, vp_ref, q_ref, kn_ref, vn_ref, o_ref,
                        m_ref, l_ref, acc_ref, out_ref, *, n_steps):
    h = pl.program_id(1)
    j = pl.program_id(2)
    lane = lax.broadcasted_iota(jnp.int32, (1, LANES), 1)
    lo = (h % HEAD_PAIR) * ATT_HEAD_DIM
    in_head = (lane >= lo) & (lane < lo + ATT_HEAD_DIM)
    qh = jnp.where(in_head, q_ref[0], 0.0) * (ATT_HEAD_DIM ** -0.5)

    @pl.when(j == 0)
    def _():
        s_own = jnp.sum(qh * kn_ref[0], axis=1, keepdims=True)
        m_ref[...] = jnp.broadcast_to(s_own, (1, LANES))
        l_ref[...] = jnp.ones((1, LANES), F32)
        acc_ref[...] = vn_ref[0]

    rows = 16
    qb = jnp.broadcast_to(qh, (rows, LANES)).astype(BF16)
    s = _dot(qb, kp_ref[0].astype(BF16))[0:1]
    m_old = m_ref[...][:, 0:1]
    m_new = jnp.maximum(m_old, jnp.max(s, axis=1, keepdims=True))
    alpha = jnp.exp(m_old - m_new)
    p = jnp.exp(s - m_new)
    l_new = alpha * l_ref[...][:, 0:1] + jnp.sum(p, axis=1, keepdims=True)
    pv = _dot_nt(jnp.broadcast_to(p, (rows, PAGE_SIZE)).astype(BF16), vp_ref[0].astype(BF16))[0:1]
    acc_new = alpha * acc_ref[...] + pv
    m_ref[...] = jnp.broadcast_to(m_new, (1, LANES))
    l_ref[...] = jnp.broadcast_to(l_new, (1, LANES))
    acc_ref[...] = acc_new

    @pl.when(j == n_steps - 1)
    def _():
        res = acc_new / l_new

        @pl.when(h % HEAD_PAIR == 0)
        def _():
            out_ref[...] = res

        @pl.when(h % HEAD_PAIR == HEAD_PAIR - 1)
        def _():
            o_ref[0] = jnp.where(in_head, res, out_ref[...])


def _sample_attention(page_table, sel, cache_kt, cache_vt, q_s, k_s, v_s):
    Bs = page_table.shape[0]
    ppb = MOBA_BLOCK // PAGE_SIZE
    n_steps = MOBA_TOPK * ppb

    def page_map(b, h, j, pt, sl):
        return (pt[b, ppb * sl[b, h * MOBA_TOPK + j // ppb] + j % ppb], h // HEAD_PAIR, 0)

    row = pl.BlockSpec((1, 1, LANES), lambda b, h, j, pt, sl: (b, 0, h // HEAD_PAIR))
    grid_spec = pltpu.PrefetchScalarGridSpec(
        num_scalar_prefetch=2,
        grid=(Bs, N_ATT_HEADS, n_steps),
        in_specs=[pl.BlockSpec((1, LANES, PAGE_SIZE), page_map),
                  pl.BlockSpec((1, LANES, PAGE_SIZE), page_map), row, row, row],
        out_specs=row,
        scratch_shapes=[pltpu.VMEM((1, LANES), F32)] * 4,
    )
    return pl.pallas_call(
        functools.partial(_sample_attn_kernel, n_steps=n_steps),
        grid_spec=grid_spec,
        out_shape=jax.ShapeDtypeStruct((Bs, 1, ATT_WIDTH), F32),
        compiler_params=pltpu.CompilerParams(dimension_semantics=("parallel", "arbitrary", "arbitrary"),
                                             vmem_limit_bytes=VMEM_LIMIT),
        name="moba_sample_attn",
    )(page_table, sel, cache_kt, cache_vt, q_s, k_s, v_s)


def _mlstm_step_kernel(bi_ref, bf_ref, m0_ref, mq_ref, mk_ref, mv_ref, g_ref, mo_ref, gout_ref,
                       c0_ref, n0_ref, ml_ref, c_out_ref, n_out_ref, m_out_ref):
    b = pl.program_id(0)
    dv = ML_V_DIM
    r_i = lax.broadcasted_iota(jnp.int32, (dv, dv), 0)
    c_i = lax.broadcasted_iota(jnp.int32, (dv, dv), 1)
    eye = r_i == c_i
    g = g_ref[0]
    for h in range(N_ML_HEADS):
        i_g = g[:, h:h + 1] + bi_ref[h]
        logf = _log_sigmoid(g[:, N_ML_HEADS + h:N_ML_HEADS + h + 1] + bf_ref[h])
        m_old = m0_ref[b, h]
        inter = logf + m_old
        m_t = jnp.maximum(inter, i_g)
        w_inter = jnp.exp(inter - m_t)
        w_in = jnp.exp(i_g - m_t)
        q = mq_ref[0, :, h * ML_QK_DIM:(h + 1) * ML_QK_DIM].astype(F32)
        k = mk_ref[0, :, h * ML_QK_DIM:(h + 1) * ML_QK_DIM].astype(F32)
        v = mv_ref[0, :, h * dv:(h + 1) * dv].astype(F32)
        c_old = c0_ref[0, 0, h]
        n_old = n0_ref[0, 0, h:h + 1, :]
        s = jnp.sum(q * k, axis=1, keepdims=True) * w_in
        v_col = jnp.sum(jnp.where(eye, v, 0.0), axis=1, keepdims=True)
        num = w_inter * jnp.sum(c_old * q, axis=1, keepdims=True) + s * v_col
        den = w_inter * jnp.sum(n_old * q, axis=1, keepdims=True) + s
        hid_col = num / jnp.maximum(jnp.abs(den), jnp.exp(-m_t))
        hid = jnp.sum(jnp.where(eye, hid_col, 0.0), axis=0, keepdims=True)
        c_out_ref[0, 0, h] = w_inter * c_old + (w_in * v_col) * k
        n_out_ref[0, 0, h:h + 1, :] = w_inter * n_old + w_in * k
        m_out_ref[0, h:h + 1, :] = jnp.broadcast_to(m_t, (1, LANES))
        hn = _rms(hid, gout_ref[h:h + 1, :])
        mo = mo_ref[0, :, h * dv:(h + 1) * dv]
        ml_ref[0, :, h * dv:(h + 1) * dv] = (hn * jax.nn.sigmoid(mo)).astype(BF16)


def _mlstm_sample(mq, mk, mv, gcol, mo, state_c, state_n, state_m, pw):
    Bs = mq.shape[0]
    row = lambda w: pl.BlockSpec((1, 1, w), lambda b, *_: (b, 0, 0))
    grid_spec = pltpu.PrefetchScalarGridSpec(
        num_scalar_prefetch=3,
        grid=(Bs,),
        in_specs=[row(ML_QK_WIDTH), row(ML_QK_WIDTH), row(ML_V_WIDTH), row(LANES), row(ML_V_WIDTH),
                  pl.BlockSpec((N_ML_HEADS, ML_V_DIM), lambda b, *_: (0, 0)),
                  pl.BlockSpec((1, 1, N_ML_HEADS, ML_V_DIM, ML_QK_DIM), lambda b, *_: (0, b, 0, 0, 0)),
                  pl.BlockSpec((1, 1, N_ML_HEADS, ML_QK_DIM), lambda b, *_: (0, b, 0, 0))],
        out_specs=(row(ML_V_WIDTH),
                   pl.BlockSpec((1, 1, N_ML_HEADS, ML_V_DIM, ML_QK_DIM), lambda b, *_: (0, b, 0, 0, 0)),
                   pl.BlockSpec((1, 1, N_ML_HEADS, ML_QK_DIM), lambda b, *_: (0, b, 0, 0)),
                   pl.BlockSpec((1, N_ML_HEADS, LANES), lambda b, *_: (b, 0, 0))),
    )
    return pl.pallas_call(
        _mlstm_step_kernel,
        grid_spec=grid_spec,
        out_shape=(jax.ShapeDtypeStruct((Bs, 1, ML_V_WIDTH), BF16),
                   jax.ShapeDtypeStruct(state_c.shape, F32),
                   jax.ShapeDtypeStruct(state_n.shape, F32),
                   jax.ShapeDtypeStruct((Bs, N_ML_HEADS, LANES), F32)),
        compiler_params=pltpu.CompilerParams(dimension_semantics=("arbitrary",),
                                             vmem_limit_bytes=VMEM_LIMIT),
        name="mlstm_sample",
    )(pw["b_i"], pw["b_f"], state_m[0], mq, mk, mv, gcol, mo, pw["g_ml_out"], state_c, state_n)


def _rope_angles(pos):
    freqs = ROPE_THETA ** (-jnp.arange(0, ROPE_DIMS, 2, dtype=F32) / ROPE_DIMS)
    return pos.astype(F32)[:, None] * freqs[None, :]


def _rope_tables_cols(pos):
    ang = _rope_angles(pos).T
    return jnp.cos(ang), jnp.sin(ang)


def _rope_tables_rows(pos):
    half = ROPE_DIMS // 2
    n = pos.shape[0]
    ang = _rope_angles(pos)
    cos, sin = jnp.cos(ang), jnp.sin(ang)
    rest = ATT_HEAD_DIM - ROPE_DIMS
    zh = jnp.zeros((n, half), F32)
    cos_h = jnp.concatenate([cos, cos, jnp.ones((n, rest), F32)], axis=-1)
    sa_h = jnp.concatenate([-sin, zh, jnp.zeros((n, rest), F32)], axis=-1)
    sb_h = jnp.concatenate([zh, sin, jnp.zeros((n, rest), F32)], axis=-1)
    tile = lambda t: jnp.tile(t, (1, HEAD_PAIR))
    return tile(cos_h), tile(sa_h), tile(sb_h)


def _prepare_weights(w_in, b_igate, b_fgate, g_mix_norm, g_q_norm, g_k_norm, g_ml_out_norm,
                     w_att_branch, w_ml_branch, w_out, g_ffn_norm, w_ffn_gate, w_ffn_up, w_ffn_down,
                     w_ple, g_ple_norm, w_ple_gate):
    w_gates = w_in[:, _C_GATES:_C_MO]
    head_of = jnp.arange(ATT_WIDTH, dtype=jnp.int32) // ATT_HEAD_DIM
    w_qkv = w_in[:, _C_Q:_C_MQ].astype(BF16)
    return {
        "w_qkv": w_qkv,
        "w_qkv_t": w_qkv.T,
        "w_m": w_in[:, _C_MQ:_C_GATES].astype(BF16),
        "g_q_col": g_q_norm.reshape(1, ATT_HEAD_DIM, 1),
        "g_k_col": g_k_norm.reshape(1, ATT_HEAD_DIM, 1),
        "w_g": jnp.pad(w_gates, ((0, 0), (0, LANES - 2 * N_ML_HEADS))).astype(BF16),
        "w_gt": jnp.pad(w_gates.T, ((0, 16 - 2 * N_ML_HEADS), (0, 0))).astype(BF16),
        "w_b": w_in[:, _C_MO:_C_END].astype(BF16),
        "e_head": (head_of[:, None] == head_of[None, :]).astype(BF16),
        "g_mix": g_mix_norm.reshape(1, D_MODEL),
        "g_q": jnp.tile(g_q_norm, N_ATT_HEADS).reshape(1, ATT_WIDTH),
        "g_k": jnp.tile(g_k_norm, N_ATT_HEADS).reshape(1, ATT_WIDTH),
        "g_ml_out": g_ml_out_norm,
        "b_i": b_igate, "b_f": b_fgate,
        "w_att": w_att_branch.astype(BF16), "w_ml": w_ml_branch.astype(BF16), "w_out": w_out.astype(BF16),
        "g_ffn": g_ffn_norm.reshape(1, D_MODEL),
        "w_ffn_gate": w_ffn_gate.astype(BF16), "w_ffn_up": w_ffn_up.astype(BF16),
        "w_ffn_down": w_ffn_down.astype(BF16),
        "w_ple": w_ple.astype(BF16), "g_ple": g_ple_norm.reshape(1, D_MODEL),
        "w_ple_gate": w_ple_gate.astype(BF16),
    }


def _prompt_layer(x, pe, pw):
    B, L, _ = x.shape
    pos = jnp.arange(L, dtype=jnp.int32)
    qt, kt, vt, mq, mk, mv, gcol, grow, mo, ga, gm = _project(x, pos, pw, tm=256, feature_major=True)
    att = _attention_prompt(qt, kt, vt)
    ml, c_fin, n_fin, m_pad = _mlstm_prompt(mq, mk, mv, gcol, grow, mo, pw)
    y = _post(x, att, ml, ga, gm, pe, pw, tm=256)
    tokens_major = lambda t: jnp.transpose(t.reshape(B, N_ATT_HEADS, ATT_HEAD_DIM, L), (0, 3, 1, 2))
    return y, tokens_major(kt), tokens_major(vt), c_fin, n_fin, m_pad[:, :, 0]


def _sample_layer(x, pe, cache_k, cache_v, page_table, state_c, state_n, state_m, pw):
    Bs, Ls, _ = x.shape
    assert Ls == 1
    past_len = page_table.shape[1] * PAGE_SIZE
    n_pool = cache_k.shape[0]
    pos = jnp.full((Bs,), past_len, dtype=jnp.int32)
    xt = x.reshape(1, Bs, D_MODEL)
    q, k, v, mq, mk, mv, gcol, _, mo, ga, gm = _project(xt, pos, pw, tm=Bs, feature_major=False)
    per_seq = lambda t: t.reshape(Bs, 1, t.shape[-1])
    pages_t = lambda c: jnp.transpose(c, (0, 2, 3, 1)).reshape(n_pool, ATT_WIDTH, PAGE_SIZE)
    cache_kt, cache_vt = pages_t(cache_k), pages_t(cache_v)
    sel = _sample_select(page_table, cache_kt, per_seq(q))
    sel2 = sel[:, :, :MOBA_TOPK].reshape(Bs, N_ATT_HEADS * MOBA_TOPK)
    att = _sample_attention(page_table, sel2, cache_kt, cache_vt, per_seq(q), per_seq(k), per_seq(v))
    ml, c_new, n_new, m_pad = _mlstm_sample(per_seq(mq), per_seq(mk), per_seq(mv), per_seq(gcol),
                                            per_seq(mo), state_c, state_n, state_m, pw)
    y = _post(xt, att.reshape(1, Bs, ATT_WIDTH), ml.reshape(1, Bs, ML_V_WIDTH), ga, gm,
              pe.reshape(1, Bs, pe.shape[-1]), pw, tm=Bs)
    return y.reshape(Bs, 1, D_MODEL), k, v, c_new, n_new, m_pad[:, :, 0]


def kernel(x_prompt, x_sample, cache_k, cache_v, state_mlstm_C, state_mlstm_n, state_mlstm_m, page_table,
           p_prompt, p_sample, w_in, b_igate, b_fgate, g_mix_norm, g_q_norm, g_k_norm, g_ml_out_norm,
           w_att_branch, w_ml_branch, w_out, g_ffn_norm, w_ffn_gate, w_ffn_up, w_ffn_down,
           w_ple, g_ple_norm, w_ple_gate):
    depth = w_in.shape[0]
    assert depth == 1
    Bp, Lp, _ = x_prompt.shape
    Bs, Ls, _ = x_sample.shape
    pw = _prepare_weights(w_in[0], b_igate[0], b_fgate[0], g_mix_norm[0], g_q_norm[0], g_k_norm[0],
                          g_ml_out_norm[0], w_att_branch[0], w_ml_branch[0], w_out[0], g_ffn_norm[0],
                          w_ffn_gate[0], w_ffn_up[0], w_ffn_down[0], w_ple[0], g_ple_norm[0], w_ple_gate[0])
    yp, kp, vp, cp, n_p, mp = _prompt_layer(x_prompt, p_prompt[0], pw)
    ys, ks, vs, cs, ns, ms = _sample_layer(x_sample, p_sample[0], cache_k[0], cache_v[0], page_table,
                                           state_mlstm_C, state_mlstm_n, state_mlstm_m, pw)
    heads = lambda t: t.reshape(1, Bs, Ls, N_ATT_HEADS, ATT_HEAD_DIM)
    return (yp, ys, kp[None], vp[None], heads(ks), heads(vs),
            cp[None], n_p[None], mp[None], cs, ns, ms[None])
```

```python
import functools

import jax
import jax.numpy as jnp
from jax import lax
from jax.experimental import pallas as pl
from jax.experimental.pallas import tpu as pltpu

F32 = jnp.float32
BF16 = jnp.bfloat16

D_MODEL = 1024
N_ATT_HEADS = 8
ATT_HEAD_DIM = 64
ATT_WIDTH = N_ATT_HEADS * ATT_HEAD_DIM
ROPE_DIMS = ATT_HEAD_DIM // 4
ROPE_THETA = 500000.0
MOBA_BLOCK = 256
MOBA_TOPK = 3
PAGE_SIZE = 128
N_ML_HEADS = 4
ML_QK_DIM = 128
ML_V_DIM = 256
ML_QK_WIDTH = N_ML_HEADS * ML_QK_DIM
ML_V_WIDTH = N_ML_HEADS * ML_V_DIM
NORM_EPS = 1e-6
NEG_INF = -1e30
LOG2_E = 1.4426950408889634

LANES = 128
HEAD_PAIR = LANES // ATT_HEAD_DIM
ML_CHUNK = 256
SELECT_BLOCKS_PER_STEP = 8
VMEM_LIMIT = 56 * 1024 * 1024

_C_Q, _C_K, _C_V = 0, ATT_WIDTH, 2 * ATT_WIDTH
_C_MQ = 3 * ATT_WIDTH
_C_MK = _C_MQ + ML_QK_WIDTH
_C_MV = _C_MK + ML_QK_WIDTH
_C_GATES = _C_MV + ML_V_WIDTH
_C_MO = _C_GATES + 2 * N_ML_HEADS
_C_GA = _C_MO + ML_V_WIDTH
_C_GM = _C_GA + D_MODEL
_C_END = _C_GM + D_MODEL


def _dot(a, b):
    return jnp.dot(a, b, preferred_element_type=F32)


def _dot_nt(a, b, precision=None):
    return lax.dot_general(a, b, (((1,), (1,)), ((), ())), precision=precision,
                           preferred_element_type=F32)


def _rms(x, g):
    return x * lax.rsqrt(jnp.mean(x * x, axis=-1, keepdims=True) + NORM_EPS) * g


def _log_sigmoid(x):
    return jnp.minimum(x, 0.0) - jnp.log1p(jnp.exp(-jnp.abs(x)))


def _proj_shared(xb, wm_ref, wg_ref, wgt_ref, wb_ref,
                 mq_ref, mk_ref, mv_ref, gcol_ref, grow_ref, mo_ref, ga_ref, gm_ref):
    mq_ref[0] = _dot(xb, wm_ref[:, 0:ML_QK_WIDTH]).astype(BF16)
    mk_ref[0] = (_dot(xb, wm_ref[:, ML_QK_WIDTH:2 * ML_QK_WIDTH]) * (ML_QK_DIM ** -0.5)).astype(BF16)
    mv_ref[0] = _dot(xb, wm_ref[:, 2 * ML_QK_WIDTH:2 * ML_QK_WIDTH + ML_V_WIDTH]).astype(BF16)
    gcol_ref[0] = _dot(xb, wg_ref[...])
    grow_ref[0] = _dot_nt(wgt_ref[...], xb)
    mo_ref[0] = _dot(xb, wb_ref[:, 0:ML_V_WIDTH])
    ga_ref[0] = _dot(xb, wb_ref[:, ML_V_WIDTH:ML_V_WIDTH + D_MODEL])
    gm_ref[0] = _dot(xb, wb_ref[:, ML_V_WIDTH + D_MODEL:ML_V_WIDTH + 2 * D_MODEL])


def _proj_kernel_rows(x_ref, cos_ref, sa_ref, sb_ref, gmix_ref, gq_ref, gk_ref, e_ref, wqkv_ref,
                      wm_ref, wg_ref, wgt_ref, wb_ref, q_ref, k_ref, v_ref, *rest_refs):
    xb = _rms(x_ref[0], gmix_ref[...]).astype(BF16)
    cos, sa, sb = cos_ref[...], sa_ref[...], sb_ref[...]
    e = e_ref[...]

    def head_norm_rope(z, g):
        zz = z * z
        hi = zz.astype(BF16)
        lo = (zz - hi.astype(F32)).astype(BF16)
        ss = _dot(hi, e) + _dot(lo, e)
        zn = z * lax.rsqrt(ss * (1.0 / ATT_HEAD_DIM) + NORM_EPS) * g
        outs = []
        for grp in range(ATT_WIDTH // LANES):
            zg = zn[:, grp * LANES:(grp + 1) * LANES]
            outs.append(zg * cos + pltpu.roll(zg, LANES - ROPE_DIMS // 2, 1) * sa
                        + pltpu.roll(zg, ROPE_DIMS // 2, 1) * sb)
        return jnp.concatenate(outs, axis=-1)

    q_ref[0] = head_norm_rope(_dot(xb, wqkv_ref[:, 0:ATT_WIDTH]), gq_ref[...])
    k_ref[0] = head_norm_rope(_dot(xb, wqkv_ref[:, ATT_WIDTH:2 * ATT_WIDTH]), gk_ref[...])
    v_ref[0] = _dot(xb, wqkv_ref[:, 2 * ATT_WIDTH:3 * ATT_WIDTH])
    _proj_shared(xb, wm_ref, wg_ref, wgt_ref, wb_ref, *rest_refs)


def _proj_kernel_cols(x_ref, cos_ref, sin_ref, gmix_ref, gq_ref, gk_ref, wqkvt_ref,
                      wm_ref, wg_ref, wgt_ref, wb_ref, qt_ref, kt_ref, vt_ref, *rest_refs):
    xb = _rms(x_ref[0], gmix_ref[...]).astype(BF16)
    tm = xb.shape[0]
    zt = _dot_nt(wqkvt_ref[...], xb)
    cos, sin = cos_ref[...][None], sin_ref[...][None]
    half = ROPE_DIMS // 2

    def head_norm_rope(z, g):
        z3 = z.reshape(N_ATT_HEADS, ATT_HEAD_DIM, tm)
        ss = jnp.sum(z3 * z3, axis=1, keepdims=True)
        zn = z3 * lax.rsqrt(ss * (1.0 / ATT_HEAD_DIM) + NORM_EPS) * g
        x1, x2 = zn[:, 0:half, :], zn[:, half:ROPE_DIMS, :]
        out = jnp.concatenate([x1 * cos - x2 * sin, x2 * cos + x1 * sin, zn[:, ROPE_DIMS:, :]], axis=1)
        return out.reshape(ATT_WIDTH, tm)

    qt_ref[0] = head_norm_rope(zt[0:ATT_WIDTH], gq_ref[...])
    kt_ref[0] = head_norm_rope(zt[ATT_WIDTH:2 * ATT_WIDTH], gk_ref[...])
    vt_ref[0] = zt[2 * ATT_WIDTH:3 * ATT_WIDTH]
    _proj_shared(xb, wm_ref, wg_ref, wgt_ref, wb_ref, *rest_refs)


def _project(x, pos, pw, tm, feature_major):
    B, L, _ = x.shape
    tok = lambda w: pl.BlockSpec((1, tm, w), lambda b, i: (b, i, 0))
    full = lambda a: pl.BlockSpec(a.shape, lambda b, i: (0,) * a.ndim)
    shared = (pw["w_m"], pw["w_g"], pw["w_gt"], pw["w_b"])
    rest_shape = (
        jax.ShapeDtypeStruct((B, L, ML_QK_WIDTH), BF16),
        jax.ShapeDtypeStruct((B, L, ML_QK_WIDTH), BF16),
        jax.ShapeDtypeStruct((B, L, ML_V_WIDTH), BF16),
        jax.ShapeDtypeStruct((B, L, LANES), F32),
        jax.ShapeDtypeStruct((B, 16, L), F32),
        jax.ShapeDtypeStruct((B, L, ML_V_WIDTH), F32),
        jax.ShapeDtypeStruct((B, L, D_MODEL), F32),
        jax.ShapeDtypeStruct((B, L, D_MODEL), F32),
    )
    rest_specs = (tok(ML_QK_WIDTH), tok(ML_QK_WIDTH), tok(ML_V_WIDTH), tok(LANES),
                  pl.BlockSpec((1, 16, tm), lambda b, i: (b, 0, i)),
                  tok(ML_V_WIDTH), tok(D_MODEL), tok(D_MODEL))
    if feature_major:
        cos, sin = _rope_tables_cols(pos)
        tab = pl.BlockSpec((ROPE_DIMS // 2, tm), lambda b, i: (0, i))
        consts = (pw["g_mix"], pw["g_q_col"], pw["g_k_col"], pw["w_qkv_t"]) + shared
        body, tables, tab_specs = _proj_kernel_cols, (cos, sin), [tab, tab]
        qkv_shape = (jax.ShapeDtypeStruct((B, ATT_WIDTH, L), F32),) * 3
        qkv_specs = (pl.BlockSpec((1, ATT_WIDTH, tm), lambda b, i: (b, 0, i)),) * 3
    else:
        tables = _rope_tables_rows(pos)
        tab = pl.BlockSpec((tm, LANES), lambda b, i: (i, 0))
        consts = (pw["g_mix"], pw["g_q"], pw["g_k"], pw["e_head"], pw["w_qkv"]) + shared
        body, tab_specs = _proj_kernel_rows, [tab, tab, tab]
        qkv_shape = (jax.ShapeDtypeStruct((B, L, ATT_WIDTH), F32),) * 3
        qkv_specs = (tok(ATT_WIDTH),) * 3
    return pl.pallas_call(
        body,
        grid=(B, L // tm),
        in_specs=[tok(D_MODEL)] + tab_specs + [full(a) for a in consts],
        out_specs=qkv_specs + rest_specs,
        out_shape=qkv_shape + rest_shape,
        compiler_params=pltpu.CompilerParams(dimension_semantics=("parallel", "parallel"),
                                             vmem_limit_bytes=VMEM_LIMIT),
        name="in_proj_cols" if feature_major else "in_proj_rows",
    )(x, *tables, *consts)


def _attn_kernel(q_ref, k_ref, v_ref, o_ref, kb_ref, vt_ref, kmean_ref, bias_ref, acc_ref, *, nblk):
    blk = MOBA_BLOCK
    tq = 2 * blk
    t = pl.program_id(2)
    c0 = 2 * t

    @pl.when(t == 0)
    def _():
        for n in range(nblk):
            kf = k_ref[0, :, n * blk:(n + 1) * blk].T
            kb_ref[n * blk:(n + 1) * blk, :] = kf.astype(BF16)
            kmean_ref[n:n + 1, :] = jnp.sum(kf, axis=0, keepdims=True) * (1.0 / blk)
            vt_ref[n] = v_ref[0, :, n * blk:(n + 1) * blk].astype(BF16)

    q2 = q_ref[0]
    feat = lax.broadcasted_iota(jnp.int32, (LANES, tq), 0)
    blk_id = lax.broadcasted_iota(jnp.int32, (nblk, tq), 0)
    col = lax.broadcasted_iota(jnp.int32, (nblk, tq), 1)
    cur = c0 + (col >= blk).astype(jnp.int32)
    kmean = kmean_ref[...]

    qbs = []
    for h in range(HEAD_PAIR):
        qh = jnp.where((feat >= h * ATT_HEAD_DIM) & (feat < (h + 1) * ATT_HEAD_DIM), q2, 0.0)
        gate = jnp.dot(kmean, qh, precision=lax.Precision.HIGHEST,
                       preferred_element_type=F32)
        rank = jnp.zeros((nblk, tq), F32)
        for n2 in range(nblk):
            row = gate[n2:n2 + 1, :]
            beats = ((row > gate) | ((row == gate) & (blk_id > n2))) & (cur > n2)
            rank = rank + jnp.where(beats, 1.0, 0.0)
        sel = (blk_id < cur) & (rank < MOBA_TOPK)
        bias_ref[h] = jnp.where(sel, 0.0, NEG_INF)
        qbs.append((qh * (ATT_HEAD_DIM ** -0.5 * LOG2_E)).astype(BF16))

    def block_rows(n):
        return kb_ref[pl.ds(pl.multiple_of(n * blk, blk), blk), :]

    def softmax_terms(sts, m):
        ps = [jnp.exp2(st - m) for st in sts]
        l = jnp.sum(ps[0], axis=0, keepdims=True) + jnp.sum(ps[1], axis=0, keepdims=True)
        return ps, l

    key_i = lax.broadcasted_iota(jnp.int32, (blk, tq), 0)
    qry_j = lax.broadcasted_iota(jnp.int32, (blk, tq), 1)
    first_half = lax.broadcasted_iota(jnp.int32, (1, tq), 1) < blk
    k_own = [block_rows(c0), block_rows(c0 + 1)]
    vt_own = [vt_ref[c0], vt_ref[c0 + 1]]
    carry = []
    for h in range(HEAD_PAIR):
        past_bias = jnp.where(first_half, 0.0, bias_ref[h, pl.ds(c0, 1), :])
        sts = [jnp.where(key_i <= qry_j, _dot(k_own[0], qbs[h]), NEG_INF) + past_bias,
               jnp.where(key_i <= qry_j - blk, _dot(k_own[1], qbs[h]), NEG_INF)]
        m = jnp.maximum(jnp.max(sts[0], axis=0, keepdims=True), jnp.max(sts[1], axis=0, keepdims=True))
        ps, l = softmax_terms(sts, m)
        acc_ref[h] = _dot(vt_own[0], ps[0].astype(BF16)) + _dot(vt_own[1], ps[1].astype(BF16))
        carry += [m, l]

    def sweep(j, carry):
        ks = [block_rows(2 * j), block_rows(2 * j + 1)]
        vts = [vt_ref[2 * j], vt_ref[2 * j + 1]]
        new = []
        for h in range(HEAD_PAIR):
            m, l = carry[2 * h], carry[2 * h + 1]
            sts = [_dot(ks[i], qbs[h]) + bias_ref[h, pl.ds(2 * j + i, 1), :] for i in range(2)]
            m_new = jnp.maximum(m, jnp.maximum(jnp.max(sts[0], axis=0, keepdims=True),
                                               jnp.max(sts[1], axis=0, keepdims=True)))
            alpha = jnp.exp2(m - m_new)
            ps, l_add = softmax_terms(sts, m_new)
            pv = _dot(vts[0], ps[0].astype(BF16)) + _dot(vts[1], ps[1].astype(BF16))
            acc_ref[h] = alpha * acc_ref[h] + pv
            new += [m_new, alpha * l + l_add]
        return tuple(new)

    carry = lax.fori_loop(0, t, sweep, tuple(carry))
    ot = jnp.where(feat < ATT_HEAD_DIM, acc_ref[0] / carry[1], acc_ref[1] / carry[3])
    o_ref[0] = ot.T.astype(o_ref.dtype)


def _attention_prompt(qt, kt, vt):
    B, _, L = qt.shape
    blk = MOBA_BLOCK
    tq = 2 * blk
    assert L % tq == 0
    nblk = L // blk
    groups = ATT_WIDTH // LANES
    return pl.pallas_call(
        functools.partial(_attn_kernel, nblk=nblk),
        grid=(B, groups, L // tq),
        in_specs=[pl.BlockSpec((1, LANES, tq), lambda b, g, i: (b, g, i)),
                  pl.BlockSpec((1, LANES, L), lambda b, g, i: (b, g, 0)),
                  pl.BlockSpec((1, LANES, L), lambda b, g, i: (b, g, 0))],
        out_specs=pl.BlockSpec((1, tq, LANES), lambda b, g, i: (b, i, g)),
        out_shape=jax.ShapeDtypeStruct((B, L, ATT_WIDTH), BF16),
        scratch_shapes=[pltpu.VMEM((L, LANES), BF16),
                        pltpu.VMEM((nblk, LANES, blk), BF16),
                        pltpu.VMEM((nblk, LANES), F32),
                        pltpu.VMEM((HEAD_PAIR, nblk, tq), F32),
                        pltpu.VMEM((HEAD_PAIR, LANES, tq), F32)],
        compiler_params=pltpu.CompilerParams(dimension_semantics=("parallel", "parallel", "arbitrary"),
                                             vmem_limit_bytes=VMEM_LIMIT),
        name="moba_prompt",
    )(qt, kt, vt)


def _mlstm_kernel(bi_ref, bf_ref, mq_ref, mk_ref, mv_ref, gcol_ref, grow_ref, mo_ref, gout_ref,
                  ml_ref, c_out_ref, n_out_ref, m_out_ref, ct_ref, n_ref, m_ref, *, chunk):
    j = pl.program_id(1)

    @pl.when(j == 0)
    def _():
        ct_ref[...] = jnp.zeros_like(ct_ref)
        n_ref[...] = jnp.zeros_like(n_ref)
        m_ref[...] = jnp.zeros_like(m_ref)

    t_i = lax.broadcasted_iota(jnp.int32, (chunk, chunk), 0)
    s_i = lax.broadcasted_iota(jnp.int32, (chunk, chunk), 1)
    tril = s_i <= t_i
    gcol = gcol_ref[0]
    grow = grow_ref[0]
    for h in range(N_ML_HEADS):
        i_col = gcol[:, h:h + 1] + bi_ref[h]
        i_row = grow[h:h + 1, :] + bi_ref[h]
        f_col = _log_sigmoid(gcol[:, N_ML_HEADS + h:N_ML_HEADS + h + 1] + bf_ref[h])
        f_row = _log_sigmoid(grow[N_ML_HEADS + h:N_ML_HEADS + h + 1, :] + bf_ref[h])
        b_row = jnp.sum(jnp.where(t_i <= s_i, f_col, 0.0), axis=0, keepdims=True)
        b_col = jnp.sum(jnp.where(tril, f_row, 0.0), axis=1, keepdims=True)
        dmat = jnp.where(tril, b_col - b_row + i_row, -jnp.inf)
        m_old = m_ref[h][:, 0:1]
        inter = b_col + m_old
        m_t = jnp.maximum(inter, jnp.max(dmat, axis=1, keepdims=True))
        w_inter = jnp.exp(inter - m_t)

        q = mq_ref[0, :, h * ML_QK_DIM:(h + 1) * ML_QK_DIM]
        k = mk_ref[0, :, h * ML_QK_DIM:(h + 1) * ML_QK_DIM]
        v = mv_ref[0, :, h * ML_V_DIM:(h + 1) * ML_V_DIM]
        qf = q.astype(F32)
        kf = k.astype(F32)
        ct = ct_ref[h]
        n_old = n_ref[h]
        s = _dot_nt(q, k) * jnp.exp(dmat - m_t)
        num = w_inter * _dot(q, ct.astype(BF16)) + _dot(s.astype(BF16), v)
        den = w_inter * jnp.sum(qf * n_old, axis=1, keepdims=True) + jnp.sum(s, axis=1, keepdims=True)
        hid = num / jnp.maximum(jnp.abs(den), jnp.exp(-m_t))

        b_end = b_col[chunk - 1:chunk, :]
        g_row = b_end - b_row + i_row
        g_col = b_end - b_col + i_col
        m_new = jnp.maximum(b_end + m_old, jnp.max(g_row, axis=1, keepdims=True))
        a = jnp.exp(b_end + m_old - m_new)
        w_col = jnp.exp(g_col - m_new)
        vw = (v.astype(F32) * w_col).astype(BF16)
        ct_ref[h] = a * ct + _dot(kf.T.astype(BF16), vw)
        n_ref[h] = a * n_old + jnp.sum(kf * w_col, axis=0, keepdims=True)
        m_ref[h] = jnp.broadcast_to(m_new, (1, LANES))

        hn = _rms(hid, gout_ref[h:h + 1, :])
        mo = mo_ref[0, :, h * ML_V_DIM:(h + 1) * ML_V_DIM]
        ml_ref[0, :, h * ML_V_DIM:(h + 1) * ML_V_DIM] = (hn * jax.nn.sigmoid(mo)).astype(BF16)

    @pl.when(j == pl.num_programs(1) - 1)
    def _():
        for h in range(N_ML_HEADS):
            c_out_ref[0, h] = ct_ref[h].T
            n_out_ref[0, h:h + 1, :] = n_ref[h]
            m_out_ref[0, h:h + 1, :] = m_ref[h]


def _mlstm_prompt(mq, mk, mv, gcol, grow, mo, pw):
    B, L, _ = mq.shape
    chunk = ML_CHUNK
    tok = lambda w: pl.BlockSpec((1, chunk, w), lambda b, j, *_: (b, j, 0))
    per_b = lambda shp: pl.BlockSpec((1,) + shp, lambda b, j, *_: (b,) + (0,) * len(shp))
    grid_spec = pltpu.PrefetchScalarGridSpec(
        num_scalar_prefetch=2,
        grid=(B, L // chunk),
        in_specs=[tok(ML_QK_WIDTH), tok(ML_QK_WIDTH), tok(ML_V_WIDTH), tok(LANES),
                  pl.BlockSpec((1, 16, chunk), lambda b, j, *_: (b, 0, j)), tok(ML_V_WIDTH),
                  pl.BlockSpec((N_ML_HEADS, ML_V_DIM), lambda b, j, *_: (0, 0))],
        out_specs=(tok(ML_V_WIDTH), per_b((N_ML_HEADS, ML_V_DIM, ML_QK_DIM)),
                   per_b((N_ML_HEADS, ML_QK_DIM)), per_b((N_ML_HEADS, LANES))),
        scratch_shapes=[pltpu.VMEM((N_ML_HEADS, ML_QK_DIM, ML_V_DIM), F32),
                        pltpu.VMEM((N_ML_HEADS, 1, ML_QK_DIM), F32),
                        pltpu.VMEM((N_ML_HEADS, 1, LANES), F32)],
    )
    return pl.pallas_call(
        functools.partial(_mlstm_kernel, chunk=chunk),
        grid_spec=grid_spec,
        out_shape=(jax.ShapeDtypeStruct((B, L, ML_V_WIDTH), BF16),
                   jax.ShapeDtypeStruct((B, N_ML_HEADS, ML_V_DIM, ML_QK_DIM), F32),
                   jax.ShapeDtypeStruct((B, N_ML_HEADS, ML_QK_DIM), F32),
                   jax.ShapeDtypeStruct((B, N_ML_HEADS, LANES), F32)),
        compiler_params=pltpu.CompilerParams(dimension_semantics=("parallel", "arbitrary"),
                                             vmem_limit_bytes=VMEM_LIMIT),
        name="mlstm_prompt",
    )(pw["b_i"], pw["b_f"], mq, mk, mv, gcol, grow, mo, pw["g_ml_out"])


def _post_kernel(x_ref, att_ref, ml_ref, ga_ref, gm_ref, pe_ref,
                 watt_ref, wml_ref, wout_ref, gffn_ref, wgate_ref, wup_ref, wdown_ref,
                 wple_ref, gple_ref, wpg_ref, y_ref):
    mix = (jax.nn.sigmoid(ga_ref[0]) * _dot(att_ref[0].astype(BF16), watt_ref[...])
           + jax.nn.sigmoid(gm_ref[0]) * _dot(ml_ref[0], wml_ref[...]))
    h = x_ref[0] + _dot(mix.astype(BF16), wout_ref[...])
    hb = _rms(h, gffn_ref[...]).astype(BF16)
    act = jax.nn.silu(_dot(hb, wgate_ref[...])) * _dot(hb, wup_ref[...])
    h = h + _dot(act.astype(BF16), wdown_ref[...])
    gate = jax.nn.sigmoid(_dot(_rms(h, gple_ref[...]).astype(BF16), wpg_ref[...]))
    y_ref[0] = h + gate * _dot(pe_ref[0].astype(BF16), wple_ref[...])


def _post(x, att, ml, ga, gm, pe, pw, tm):
    B, L, _ = x.shape
    tok = lambda w: pl.BlockSpec((1, tm, w), lambda b, i: (b, i, 0))
    full = lambda a: pl.BlockSpec(a.shape, lambda b, i: (0,) * a.ndim, pipeline_mode=pl.Buffered(1))
    consts = (pw["w_att"], pw["w_ml"], pw["w_out"], pw["g_ffn"], pw["w_ffn_gate"], pw["w_ffn_up"],
              pw["w_ffn_down"], pw["w_ple"], pw["g_ple"], pw["w_ple_gate"])
    return pl.pallas_call(
        _post_kernel,
        grid=(B, L // tm),
        in_specs=[tok(D_MODEL), tok(ATT_WIDTH), tok(ML_V_WIDTH), tok(D_MODEL), tok(D_MODEL),
                  tok(pe.shape[-1])] + [full(a) for a in consts],
        out_specs=tok(D_MODEL),
        out_shape=jax.ShapeDtypeStruct((B, L, D_MODEL), F32),
        compiler_params=pltpu.CompilerParams(dimension_semantics=("parallel", "parallel"),
                                             vmem_limit_bytes=VMEM_LIMIT),
        name="merge_ffn",
    )(x, att, ml, ga, gm, pe, *consts)


def _sample_select_kernel(pt_ref, *refs, nblk, blocks_per_step):
    page_refs = refs[:2 * blocks_per_step]
    q_ref, sel_ref, qbd_ref, gate_ref = refs[2 * blocks_per_step:]
    j = pl.program_id(1)
    lane = lax.broadcasted_iota(jnp.int32, (N_ATT_HEADS, LANES), 1)

    @pl.when(j == 0)
    def _():
        head = lax.broadcasted_iota(jnp.int32, (N_ATT_HEADS, ATT_WIDTH), 0)
        col = lax.broadcasted_iota(jnp.int32, (N_ATT_HEADS, ATT_WIDTH), 1)
        in_head = (col >= head * ATT_HEAD_DIM) & (col < (head + 1) * ATT_HEAD_DIM)
        qbd = jnp.where(in_head, q_ref[0], 0.0)
        hi = qbd.astype(BF16).astype(F32)
        qbd_ref[...] = jnp.concatenate([hi, qbd - hi], axis=0).astype(BF16)
        gate_ref[...] = jnp.zeros_like(gate_ref)

    qbd = qbd_ref[...]
    gate = gate_ref[...]
    for i in range(blocks_per_step):
        ka, kb = page_refs[2 * i], page_refs[2 * i + 1]
        s = _dot(qbd, ka[0].astype(BF16)) + _dot(qbd, kb[0].astype(BF16))
        g = jnp.sum(s[0:N_ATT_HEADS] + s[N_ATT_HEADS:], axis=1, keepdims=True) * (1.0 / MOBA_BLOCK)
        gate = jnp.where(lane == j * blocks_per_step + i, g, gate)
    gate_ref[...] = gate

    @pl.when(j == pl.num_programs(1) - 1)
    def _():
        gate = gate_ref[...]
        rank = jnp.zeros((N_ATT_HEADS, LANES), F32)
        for n2 in range(nblk):
            col = gate[:, n2:n2 + 1]
            beats = (col > gate) | ((col == gate) & (lane > n2))
            rank = rank + jnp.where(beats, 1.0, 0.0)
        out = jnp.zeros((N_ATT_HEADS, LANES), jnp.int32)
        lane_f = lane.astype(F32)
        for r in range(MOBA_TOPK):
            pick = (rank == float(r)) & (lane < nblk)
            idx = jnp.sum(jnp.where(pick, lane_f, 0.0), axis=1, keepdims=True)
            out = jnp.where(lane == r, idx.astype(jnp.int32), out)
        sel_ref[0] = out


def _sample_select(page_table, cache_kt, q_s):
    Bs, n_pages = page_table.shape
    nblk = n_pages * PAGE_SIZE // MOBA_BLOCK
    assert MOBA_BLOCK == 2 * PAGE_SIZE and nblk <= LANES
    bps = SELECT_BLOCKS_PER_STEP
    assert nblk % bps == 0
    pages_per_step = 2 * bps

    def page_spec(p):
        return pl.BlockSpec((1, ATT_WIDTH, PAGE_SIZE), lambda b, j, pt: (pt[b, j * pages_per_step + p], 0, 0))

    grid_spec = pltpu.PrefetchScalarGridSpec(
        num_scalar_prefetch=1,
        grid=(Bs, nblk // bps),
        in_specs=[page_spec(p) for p in range(pages_per_step)]
                 + [pl.BlockSpec((1, 1, ATT_WIDTH), lambda b, j, pt: (b, 0, 0))],
        out_specs=pl.BlockSpec((1, N_ATT_HEADS, LANES), lambda b, j, pt: (b, 0, 0)),
        scratch_shapes=[pltpu.VMEM((2 * N_ATT_HEADS, ATT_WIDTH), BF16),
                        pltpu.VMEM((N_ATT_HEADS, LANES), F32)],
    )
    return pl.pallas_call(
        functools.partial(_sample_select_kernel, nblk=nblk, blocks_per_step=bps),
        grid_spec=grid_spec,
        out_shape=jax.ShapeDtypeStruct((Bs, N_ATT_HEADS, LANES), jnp.int32),
        compiler_params=pltpu.CompilerParams(dimension_semantics=("parallel", "arbitrary"),
                                             vmem_limit_bytes=VMEM_LIMIT),
        name="moba_sample_select",
    )(page_table, *([cache_kt] * pages_per_step), q_s)


def _sample_attn_kernel(pt_ref, sel_ref, *refs, n_pages):
    k_refs, v_refs = refs[:n_pages], refs[n_pages:2 * n_pages]
    q_ref, kn_ref, vn_ref, o_ref, out_ref = refs[2 * n_pages:]
    h = pl.program_id(1)
    lane = lax.broadcasted_iota(jnp.int32, (1, LANES), 1)
    lo = (h % HEAD_PAIR) * ATT_HEAD_DIM
    in_head = (lane >= lo) & (lane < lo + ATT_HEAD_DIM)
    qh = jnp.where(in_head, q_ref[0], 0.0) * (ATT_HEAD_DIM ** -0.5)
    rows = 16
    qb = jnp.broadcast_to(qh, (rows, LANES)).astype(BF16)

    s_own = jnp.sum(qh * kn_ref[0], axis=1, keepdims=True)
    scores = [_dot(qb, k_refs[j][0].astype(BF16))[0:1] for j in range(n_pages)]
    m = s_own
    for s in scores:
        m = jnp.maximum(m, jnp.max(s, axis=1, keepdims=True))
    p_own = jnp.exp(s_own - m)
    l = p_own
    acc = p_own * vn_ref[0]
    for j, s in enumerate(scores):
        p = jnp.exp(s - m)
        l = l + jnp.sum(p, axis=1, keepdims=True)
        acc = acc + _dot_nt(jnp.broadcast_to(p, (rows, PAGE_SIZE)).astype(BF16), v_refs[j][0].astype(BF16))[0:1]
    res = acc / l

    @pl.when(h % HEAD_PAIR == 0)
    def _():
        out_ref[...] = res

    @pl.when(h % HEAD_PAIR == HEAD_PAIR - 1)
    def _():
        o_ref[0] = jnp.where(in_head, res, out_ref[...])


def _sample_attention(page_table, sel, cache_kt, cache_vt, q_s, k_s, v_s):
    Bs = page_table.shape[0]
    ppb = MOBA_BLOCK // PAGE_SIZE
    n_pages = MOBA_TOPK * ppb

    def tile_spec(j):
        def index(b, h, pt, sl):
            return (pt[b, ppb * sl[b, h * MOBA_TOPK + j // ppb] + j % ppb], h // HEAD_PAIR, 0)
        return pl.BlockSpec((1, LANES, PAGE_SIZE), index)

    tiles = [tile_spec(j) for j in range(n_pages)]
    row = pl.BlockSpec((1, 1, LANES), lambda b, h, pt, sl: (b, 0, h // HEAD_PAIR))
    grid_spec = pltpu.PrefetchScalarGridSpec(
        num_scalar_prefetch=2,
        grid=(Bs, N_ATT_HEADS),
        in_specs=tiles + tiles + [row, row, row],
        out_specs=row,
        scratch_shapes=[pltpu.VMEM((1, LANES), F32)],
    )
    return pl.pallas_call(
        functools.partial(_sample_attn_kernel, n_pages=n_pages),
        grid_spec=grid_spec,
        out_shape=jax.ShapeDtypeStruct((Bs, 1, ATT_WIDTH), F32),
        compiler_params=pltpu.CompilerParams(dimension_semantics=("parallel", "arbitrary"),
                                             vmem_limit_bytes=VMEM_LIMIT),
        name="moba_sample_attn",
    )(page_table, sel, *([cache_kt] * n_pages), *([cache_vt] * n_pages), q_s, k_s, v_s)


def _mlstm_step_kernel(bi_ref, bf_ref, m0_ref, mq_ref, mk_ref, mv_ref, g_ref, mo_ref, gout_ref,
                       c0_ref, n0_ref, ml_ref, c_out_ref, n_out_ref, m_out_ref):
    b = pl.program_id(0)
    dv = ML_V_DIM
    r_i = lax.broadcasted_iota(jnp.int32, (dv, dv), 0)
    c_i = lax.broadcasted_iota(jnp.int32, (dv, dv), 1)
    eye = r_i == c_i
    g = g_ref[0]
    for h in range(N_ML_HEADS):
        i_g = g[:, h:h + 1] + bi_ref[h]
        logf = _log_sigmoid(g[:, N_ML_HEADS + h:N_ML_HEADS + h + 1] + bf_ref[h])
        m_old = m0_ref[b, h]
        inter = logf + m_old
        m_t = jnp.maximum(inter, i_g)
        w_inter = jnp.exp(inter - m_t)
        w_in = jnp.exp(i_g - m_t)
        q = mq_ref[0, :, h * ML_QK_DIM:(h + 1) * ML_QK_DIM].astype(F32)
        k = mk_ref[0, :, h * ML_QK_DIM:(h + 1) * ML_QK_DIM].astype(F32)
        v = mv_ref[0, :, h * dv:(h + 1) * dv].astype(F32)
        c_old = c0_ref[0, 0, h]
        n_old = n0_ref[0, 0, h:h + 1, :]
        s = jnp.sum(q * k, axis=1, keepdims=True) * w_in
        v_col = jnp.sum(jnp.where(eye, v, 0.0), axis=1, keepdims=True)
        num = w_inter * jnp.sum(c_old * q, axis=1, keepdims=True) + s * v_col
        den = w_inter * jnp.sum(n_old * q, axis=1, keepdims=True) + s
        hid_col = num / jnp.maximum(jnp.abs(den), jnp.exp(-m_t))
        hid = jnp.sum(jnp.where(eye, hid_col, 0.0), axis=0, keepdims=True)
        c_out_ref[0, 0, h] = w_inter * c_old + (w_in * v_col) * k
        n_out_ref[0, 0, h:h + 1, :] = w_inter * n_old + w_in * k
        m_out_ref[0, h:h + 1, :] = jnp.broadcast_to(m_t, (1, LANES))
        hn = _rms(hid, gout_ref[h:h + 1, :])
        mo = mo_ref[0, :, h * dv:(h + 1) * dv]
        ml_ref[0, :, h * dv:(h + 1) * dv] = (hn * jax.nn.sigmoid(mo)).astype(BF16)


def _mlstm_sample(mq, mk, mv, gcol, mo, state_c, state_n, state_m, pw):
    Bs = mq.shape[0]
    row = lambda w: pl.BlockSpec((1, 1, w), lambda b, *_: (b, 0, 0))
    grid_spec = pltpu.PrefetchScalarGridSpec(
        num_scalar_prefetch=3,
        grid=(Bs,),
        in_specs=[row(ML_QK_WIDTH), row(ML_QK_WIDTH), row(ML_V_WIDTH), row(LANES), row(ML_V_WIDTH),
                  pl.BlockSpec((N_ML_HEADS, ML_V_DIM), lambda b, *_: (0, 0)),
                  pl.BlockSpec((1, 1, N_ML_HEADS, ML_V_DIM, ML_QK_DIM), lambda b, *_: (0, b, 0, 0, 0)),
                  pl.BlockSpec((1, 1, N_ML_HEADS, ML_QK_DIM), lambda b, *_: (0, b, 0, 0))],
        out_specs=(row(ML_V_WIDTH),
                   pl.BlockSpec((1, 1, N_ML_HEADS, ML_V_DIM, ML_QK_DIM), lambda b, *_: (0, b, 0, 0, 0)),
                   pl.BlockSpec((1, 1, N_ML_HEADS, ML_QK_DIM), lambda b, *_: (0, b, 0, 0)),
                   pl.BlockSpec((1, N_ML_HEADS, LANES), lambda b, *_: (b, 0, 0))),
    )
    return pl.pallas_call(
        _mlstm_step_kernel,
        grid_spec=grid_spec,
        out_shape=(jax.ShapeDtypeStruct((Bs, 1, ML_V_WIDTH), BF16),
                   jax.ShapeDtypeStruct(state_c.shape, F32),
                   jax.ShapeDtypeStruct(state_n.shape, F32),
                   jax.ShapeDtypeStruct((Bs, N_ML_HEADS, LANES), F32)),
        compiler_params=pltpu.CompilerParams(dimension_semantics=("arbitrary",),
                                             vmem_limit_bytes=VMEM_LIMIT),
        name="mlstm_sample",
    )(pw["b_i"], pw["b_f"], state_m[0], mq, mk, mv, gcol, mo, pw["g_ml_out"], state_c, state_n)


def _rope_angles(pos):
    freqs = ROPE_THETA ** (-jnp.arange(0, ROPE_DIMS, 2, dtype=F32) / ROPE_DIMS)
    return pos.astype(F32)[:, None] * freqs[None, :]


def _rope_tables_cols(pos):
    ang = _rope_angles(pos).T
    return jnp.cos(ang), jnp.sin(ang)


def _rope_tables_rows(pos):
    half = ROPE_DIMS // 2
    n = pos.shape[0]
    ang = _rope_angles(pos)
    cos, sin = jnp.cos(ang), jnp.sin(ang)
    rest = ATT_HEAD_DIM - ROPE_DIMS
    zh = jnp.zeros((n, half), F32)
    cos_h = jnp.concatenate([cos, cos, jnp.ones((n, rest), F32)], axis=-1)
    sa_h = jnp.concatenate([-sin, zh, jnp.zeros((n, rest), F32)], axis=-1)
    sb_h = jnp.concatenate([zh, sin, jnp.zeros((n, rest), F32)], axis=-1)
    tile = lambda t: jnp.tile(t, (1, HEAD_PAIR))
    return tile(cos_h), tile(sa_h), tile(sb_h)


def _prepare_weights(w_in, b_igate, b_fgate, g_mix_norm, g_q_norm, g_k_norm, g_ml_out_norm,
                     w_att_branch, w_ml_branch, w_out, g_ffn_norm, w_ffn_gate, w_ffn_up, w_ffn_down,
                     w_ple, g_ple_norm, w_ple_gate):
    w_gates = w_in[:, _C_GATES:_C_MO]
    head_of = jnp.arange(ATT_WIDTH, dtype=jnp.int32) // ATT_HEAD_DIM
    w_qkv = w_in[:, _C_Q:_C_MQ].astype(BF16)
    return {
        "w_qkv": w_qkv,
        "w_qkv_t": w_qkv.T,
        "w_m": w_in[:, _C_MQ:_C_GATES].astype(BF16),
        "g_q_col": g_q_norm.reshape(1, ATT_HEAD_DIM, 1),
        "g_k_col": g_k_norm.reshape(1, ATT_HEAD_DIM, 1),
        "w_g": jnp.pad(w_gates, ((0, 0), (0, LANES - 2 * N_ML_HEADS))).astype(BF16),
        "w_gt": jnp.pad(w_gates.T, ((0, 16 - 2 * N_ML_HEADS), (0, 0))).astype(BF16),
        "w_b": w_in[:, _C_MO:_C_END].astype(BF16),
        "e_head": (head_of[:, None] == head_of[None, :]).astype(BF16),
        "g_mix": g_mix_norm.reshape(1, D_MODEL),
        "g_q": jnp.tile(g_q_norm, N_ATT_HEADS).reshape(1, ATT_WIDTH),
        "g_k": jnp.tile(g_k_norm, N_ATT_HEADS).reshape(1, ATT_WIDTH),
        "g_ml_out": g_ml_out_norm,
        "b_i": b_igate, "b_f": b_fgate,
        "w_att": w_att_branch.astype(BF16), "w_ml": w_ml_branch.astype(BF16), "w_out": w_out.astype(BF16),
        "g_ffn": g_ffn_norm.reshape(1, D_MODEL),
        "w_ffn_gate": w_ffn_gate.astype(BF16), "w_ffn_up": w_ffn_up.astype(BF16),
        "w_ffn_down": w_ffn_down.astype(BF16),
        "w_ple": w_ple.astype(BF16), "g_ple": g_ple_norm.reshape(1, D_MODEL),
        "w_ple_gate": w_ple_gate.astype(BF16),
    }


def _prompt_layer(x, pe, pw):
    B, L, _ = x.shape
    pos = jnp.arange(L, dtype=jnp.int32)
    qt, kt, vt, mq, mk, mv, gcol, grow, mo, ga, gm = _project(x, pos, pw, tm=256, feature_major=True)
    att = _attention_prompt(qt, kt, vt)
    ml, c_fin, n_fin, m_pad = _mlstm_prompt(mq, mk, mv, gcol, grow, mo, pw)
    y = _post(x, att, ml, ga, gm, pe, pw, tm=256)
    tokens_major = lambda t: jnp.transpose(t.reshape(B, N_ATT_HEADS, ATT_HEAD_DIM, L), (0, 3, 1, 2))
    return y, tokens_major(kt), tokens_major(vt), c_fin, n_fin, m_pad[:, :, 0]


def _sample_layer(x, pe, cache_k, cache_v, page_table, state_c, state_n, state_m, pw):
    Bs, Ls, _ = x.shape
    assert Ls == 1
    past_len = page_table.shape[1] * PAGE_SIZE
    n_pool = cache_k.shape[0]
    pos = jnp.full((Bs,), past_len, dtype=jnp.int32)
    xt = x.reshape(1, Bs, D_MODEL)
    q, k, v, mq, mk, mv, gcol, _, mo, ga, gm = _project(xt, pos, pw, tm=Bs, feature_major=False)
    per_seq = lambda t: t.reshape(Bs, 1, t.shape[-1])
    pages_t = lambda c: jnp.transpose(c, (0, 2, 3, 1)).reshape(n_pool, ATT_WIDTH, PAGE_SIZE)
    cache_kt, cache_vt = pages_t(cache_k), pages_t(cache_v)
    sel = _sample_select(page_table, cache_kt, per_seq(q))
    sel2 = sel[:, :, :MOBA_TOPK].reshape(Bs, N_ATT_HEADS * MOBA_TOPK)
    att = _sample_attention(page_table, sel2, cache_kt, cache_vt, per_seq(q), per_seq(k), per_seq(v))
    ml, c_new, n_new, m_pad = _mlstm_sample(per_seq(mq), per_seq(mk), per_seq(mv), per_seq(gcol),
                                            per_seq(mo), state_c, state_n, state_m, pw)
    y = _post(xt, att.reshape(1, Bs, ATT_WIDTH), ml.reshape(1, Bs, ML_V_WIDTH), ga, gm,
              pe.reshape(1, Bs, pe.shape[-1]), pw, tm=Bs)
    return y.reshape(Bs, 1, D_MODEL), k, v, c_new, n_new, m_pad[:, :, 0]


def kernel(x_prompt, x_sample, cache_k, cache_v, state_mlstm_C, state_mlstm_n, state_mlstm_m, page_table,
           p_prompt, p_sample, w_in, b_igate, b_fgate, g_mix_norm, g_q_norm, g_k_norm, g_ml_out_norm,
           w_att_branch, w_ml_branch, w_out, g_ffn_norm, w_ffn_gate, w_ffn_up, w_ffn_down,
           w_ple, g_ple_norm, w_ple_gate):
    depth = w_in.shape[0]
    assert depth == 1
    Bp, Lp, _ = x_prompt.shape
    Bs, Ls, _ = x_sample.shape
    pw = _prepare_weights(w_in[0], b_igate[0], b_fgate[0], g_mix_norm[0], g_q_norm[0], g_k_norm[0],
                          g_ml_out_norm[0], w_att_branch[0], w_ml_branch[0], w_out[0], g_ffn_norm[0],
                          w_ffn_gate[0], w_ffn_up[0], w_ffn_down[0], w_ple[0], g_ple_norm[0], w_ple_gate[0])
    yp, kp, vp, cp, n_p, mp = _prompt_layer(x_prompt, p_prompt[0], pw)
    ys, ks, vs, cs, ns, ms = _sample_layer(x_sample, p_sample[0], cache_k[0], cache_v[0], page_table,
                                           state_mlstm_C, state_mlstm_n, state_mlstm_m, pw)
    heads = lambda t: t.reshape(1, Bs, Ls, N_ATT_HEADS, ATT_HEAD_DIM)
    return (yp, ys, kp[None], vp[None], heads(ks), heads(vs),
            cp[None], n_p[None], mp[None], cs, ns, ms[None])
```

```python
import functools

import jax
import jax.numpy as jnp
from jax import lax
from jax.experimental import pallas as pl
from jax.experimental.pallas import tpu as pltpu

F32 = jnp.float32
BF16 = jnp.bfloat16

D_MODEL = 1024
N_ATT_HEADS = 8
ATT_HEAD_DIM = 64
ATT_WIDTH = N_ATT_HEADS * ATT_HEAD_DIM
ROPE_DIMS = ATT_HEAD_DIM // 4
ROPE_THETA = 500000.0
MOBA_BLOCK = 256
MOBA_TOPK = 3
PAGE_SIZE = 128
N_ML_HEADS = 4
ML_QK_DIM = 128
ML_V_DIM = 256
ML_QK_WIDTH = N_ML_HEADS * ML_QK_DIM
ML_V_WIDTH = N_ML_HEADS * ML_V_DIM
NORM_EPS = 1e-6
NEG_INF = -1e30
LOG2_E = 1.4426950408889634

LANES = 128
HEAD_PAIR = LANES // ATT_HEAD_DIM
ML_CHUNK = 256
SELECT_BLOCKS_PER_STEP = 8
VMEM_LIMIT = 56 * 1024 * 1024

_C_Q, _C_K, _C_V = 0, ATT_WIDTH, 2 * ATT_WIDTH
_C_MQ = 3 * ATT_WIDTH
_C_MK = _C_MQ + ML_QK_WIDTH
_C_MV = _C_MK + ML_QK_WIDTH
_C_GATES = _C_MV + ML_V_WIDTH
_C_MO = _C_GATES + 2 * N_ML_HEADS
_C_GA = _C_MO + ML_V_WIDTH
_C_GM = _C_GA + D_MODEL
_C_END = _C_GM + D_MODEL


def _dot(a, b):
    return jnp.dot(a, b, preferred_element_type=F32)


def _dot_nt(a, b, precision=None):
    return lax.dot_general(a, b, (((1,), (1,)), ((), ())), precision=precision,
                           preferred_element_type=F32)


def _rms(x, g):
    return x * lax.rsqrt(jnp.mean(x * x, axis=-1, keepdims=True) + NORM_EPS) * g


def _log_sigmoid(x):
    return jnp.minimum(x, 0.0) - jnp.log1p(jnp.exp(-jnp.abs(x)))


def _proj_shared(xb, wm_ref, wg_ref, wgt_ref, wb_ref,
                 mq_ref, mk_ref, mv_ref, gcol_ref, grow_ref, mo_ref, ga_ref, gm_ref):
    mq_ref[0] = _dot(xb, wm_ref[:, 0:ML_QK_WIDTH]).astype(BF16)
    mk_ref[0] = (_dot(xb, wm_ref[:, ML_QK_WIDTH:2 * ML_QK_WIDTH]) * (ML_QK_DIM ** -0.5)).astype(BF16)
    mv_ref[0] = _dot(xb, wm_ref[:, 2 * ML_QK_WIDTH:2 * ML_QK_WIDTH + ML_V_WIDTH]).astype(BF16)
    gcol_ref[0] = _dot(xb, wg_ref[...])
    grow_ref[0] = _dot_nt(wgt_ref[...], xb)
    mo_ref[0] = _dot(xb, wb_ref[:, 0:ML_V_WIDTH])
    ga_ref[0] = _dot(xb, wb_ref[:, ML_V_WIDTH:ML_V_WIDTH + D_MODEL])
    gm_ref[0] = _dot(xb, wb_ref[:, ML_V_WIDTH + D_MODEL:ML_V_WIDTH + 2 * D_MODEL])


def _proj_kernel_rows(x_ref, cos_ref, sa_ref, sb_ref, gmix_ref, gq_ref, gk_ref, e_ref, wqkv_ref,
                      wm_ref, wg_ref, wgt_ref, wb_ref, q_ref, k_ref, v_ref, *rest_refs):
    xb = _rms(x_ref[0], gmix_ref[...]).astype(BF16)
    cos, sa, sb = cos_ref[...], sa_ref[...], sb_ref[...]
    e = e_ref[...]

    def head_norm_rope(z, g):
        zz = z * z
        hi = zz.astype(BF16)
        lo = (zz - hi.astype(F32)).astype(BF16)
        ss = _dot(hi, e) + _dot(lo, e)
        zn = z * lax.rsqrt(ss * (1.0 / ATT_HEAD_DIM) + NORM_EPS) * g
        outs = []
        for grp in range(ATT_WIDTH // LANES):
            zg = zn[:, grp * LANES:(grp + 1) * LANES]
            outs.append(zg * cos + pltpu.roll(zg, LANES - ROPE_DIMS // 2, 1) * sa
                        + pltpu.roll(zg, ROPE_DIMS // 2, 1) * sb)
        return jnp.concatenate(outs, axis=-1)

    q_ref[0] = head_norm_rope(_dot(xb, wqkv_ref[:, 0:ATT_WIDTH]), gq_ref[...])
    k_ref[0] = head_norm_rope(_dot(xb, wqkv_ref[:, ATT_WIDTH:2 * ATT_WIDTH]), gk_ref[...])
    v_ref[0] = _dot(xb, wqkv_ref[:, 2 * ATT_WIDTH:3 * ATT_WIDTH])
    _proj_shared(xb, wm_ref, wg_ref, wgt_ref, wb_ref, *rest_refs)


def _proj_kernel_cols(x_ref, cos_ref, sin_ref, gmix_ref, gq_ref, gk_ref, wqkvt_ref,
                      wm_ref, wg_ref, wgt_ref, wb_ref, qt_ref, kt_ref, vt_ref, *rest_refs):
    xb = _rms(x_ref[0], gmix_ref[...]).astype(BF16)
    tm = xb.shape[0]
    zt = _dot_nt(wqkvt_ref[...], xb)
    cos, sin = cos_ref[...][None], sin_ref[...][None]
    half = ROPE_DIMS // 2

    def head_norm_rope(z, g):
        z3 = z.reshape(N_ATT_HEADS, ATT_HEAD_DIM, tm)
        ss = jnp.sum(z3 * z3, axis=1, keepdims=True)
        zn = z3 * lax.rsqrt(ss * (1.0 / ATT_HEAD_DIM) + NORM_EPS) * g
        x1, x2 = zn[:, 0:half, :], zn[:, half:ROPE_DIMS, :]
        out = jnp.concatenate([x1 * cos - x2 * sin, x2 * cos + x1 * sin, zn[:, ROPE_DIMS:, :]], axis=1)
        return out.reshape(ATT_WIDTH, tm)

    qt_ref[0] = head_norm_rope(zt[0:ATT_WIDTH], gq_ref[...])
    kt_ref[0] = head_norm_rope(zt[ATT_WIDTH:2 * ATT_WIDTH], gk_ref[...])
    vt_ref[0] = zt[2 * ATT_WIDTH:3 * ATT_WIDTH]
    _proj_shared(xb, wm_ref, wg_ref, wgt_ref, wb_ref, *rest_refs)


def _project(x, pos, pw, tm, feature_major):
    B, L, _ = x.shape
    tok = lambda w: pl.BlockSpec((1, tm, w), lambda b, i: (b, i, 0))
    full = lambda a: pl.BlockSpec(a.shape, lambda b, i: (0,) * a.ndim)
    shared = (pw["w_m"], pw["w_g"], pw["w_gt"], pw["w_b"])
    rest_shape = (
        jax.ShapeDtypeStruct((B, L, ML_QK_WIDTH), BF16),
        jax.ShapeDtypeStruct((B, L, ML_QK_WIDTH), BF16),
        jax.ShapeDtypeStruct((B, L, ML_V_WIDTH), BF16),
        jax.ShapeDtypeStruct((B, L, LANES), F32),
        jax.ShapeDtypeStruct((B, 16, L), F32),
        jax.ShapeDtypeStruct((B, L, ML_V_WIDTH), F32),
        jax.ShapeDtypeStruct((B, L, D_MODEL), F32),
        jax.ShapeDtypeStruct((B, L, D_MODEL), F32),
    )
    rest_specs = (tok(ML_QK_WIDTH), tok(ML_QK_WIDTH), tok(ML_V_WIDTH), tok(LANES),
                  pl.BlockSpec((1, 16, tm), lambda b, i: (b, 0, i)),
                  tok(ML_V_WIDTH), tok(D_MODEL), tok(D_MODEL))
    if feature_major:
        cos, sin = _rope_tables_cols(pos)
        tab = pl.BlockSpec((ROPE_DIMS // 2, tm), lambda b, i: (0, i))
        consts = (pw["g_mix"], pw["g_q_col"], pw["g_k_col"], pw["w_qkv_t"]) + shared
        body, tables, tab_specs = _proj_kernel_cols, (cos, sin), [tab, tab]
        qkv_shape = (jax.ShapeDtypeStruct((B, ATT_WIDTH, L), F32),) * 3
        qkv_specs = (pl.BlockSpec((1, ATT_WIDTH, tm), lambda b, i: (b, 0, i)),) * 3
    else:
        tables = _rope_tables_rows(pos)
        tab = pl.BlockSpec((tm, LANES), lambda b, i: (i, 0))
        consts = (pw["g_mix"], pw["g_q"], pw["g_k"], pw["e_head"], pw["w_qkv"]) + shared
        body, tab_specs = _proj_kernel_rows, [tab, tab, tab]
        qkv_shape = (jax.ShapeDtypeStruct((B, L, ATT_WIDTH), F32),) * 3
        qkv_specs = (tok(ATT_WIDTH),) * 3
    return pl.pallas_call(
        body,
        grid=(B, L // tm),
        in_specs=[tok(D_MODEL)] + tab_specs + [full(a) for a in consts],
        out_specs=qkv_specs + rest_specs,
        out_shape=qkv_shape + rest_shape,
        compiler_params=pltpu.CompilerParams(dimension_semantics=("parallel", "parallel"),
                                             vmem_limit_bytes=VMEM_LIMIT),
        name="in_proj_cols" if feature_major else "in_proj_rows",
    )(x, *tables, *consts)


def _attn_kernel(q_ref, k_ref, v_ref, o_ref, kb_ref, vt_ref, kmean_ref, acc_ref, *, nblk):
    blk = MOBA_BLOCK
    tq = 2 * blk
    ones_rows = vt_ref.shape[1] - LANES
    t = pl.program_id(2)
    c0 = 2 * t

    nb = kmean_ref.shape[0]

    @pl.when(t == 0)
    def _():
        lane = lax.broadcasted_iota(jnp.int32, (blk, LANES), 1)
        kmean_ref[...] = jnp.zeros_like(kmean_ref)
        for n in range(nblk):
            rows = slice(n * blk, (n + 1) * blk)
            kf = k_ref[0, :, rows].T
            kb_ref[rows, 0:LANES] = kf.astype(BF16)
            kb_ref[rows, LANES:2 * LANES] = jnp.where(lane == n, 1.0, 0.0).astype(BF16)
            kmean_ref[n:n + 1, :] = jnp.sum(kf, axis=0, keepdims=True) * (1.0 / blk)
            vt_ref[n] = jnp.concatenate([v_ref[0, :, rows], jnp.ones((ones_rows, blk), F32)],
                                        axis=0).astype(BF16)

    q2 = q_ref[0]
    feat = lax.broadcasted_iota(jnp.int32, (LANES, tq), 0)
    blk_id = lax.broadcasted_iota(jnp.int32, (nb, tq), 0)
    col = lax.broadcasted_iota(jnp.int32, (nb, tq), 1)
    cur = c0 + (col >= blk).astype(jnp.int32)
    kmean = kmean_ref[...]
    bias_pad = jnp.zeros((LANES - nb, tq), F32)

    qas = []
    for h in range(HEAD_PAIR):
        qh = jnp.where((feat >= h * ATT_HEAD_DIM) & (feat < (h + 1) * ATT_HEAD_DIM), q2, 0.0)
        gate = jnp.dot(kmean, qh, precision=lax.Precision.HIGHEST,
                       preferred_element_type=F32)
        rank = jnp.zeros((nb, tq), F32)
        for n2 in range(nblk):
            row = gate[n2:n2 + 1, :]
            beats = ((row > gate) | ((row == gate) & (blk_id > n2))) & (cur > n2)
            rank = rank + jnp.where(beats, 1.0, 0.0)
        open_blk = ((blk_id < cur) & (rank < MOBA_TOPK)) | (blk_id == cur)
        bias = jnp.where(open_blk, 0.0, NEG_INF)
        qas.append(jnp.concatenate([qh * (ATT_HEAD_DIM ** -0.5 * LOG2_E), bias, bias_pad],
                                   axis=0).astype(BF16))

    def block_rows(n):
        return kb_ref[pl.ds(pl.multiple_of(n * blk, blk), blk), :]

    def pv(vts, sts, m):
        return (_dot(vts[0], jnp.exp2(sts[0] - m).astype(BF16))
                + _dot(vts[1], jnp.exp2(sts[1] - m).astype(BF16)))

    def col_max(sts):
        return jnp.maximum(jnp.max(sts[0], axis=0, keepdims=True), jnp.max(sts[1], axis=0, keepdims=True))

    key_i = lax.broadcasted_iota(jnp.int32, (blk, tq), 0)
    qry_j = lax.broadcasted_iota(jnp.int32, (blk, tq), 1)
    k_own = [block_rows(c0), block_rows(c0 + 1)]
    vt_own = [vt_ref[c0], vt_ref[c0 + 1]]
    ms = []
    for h in range(HEAD_PAIR):
        sts = [jnp.where(key_i <= qry_j, _dot(k_own[0], qas[h]), NEG_INF),
               jnp.where(key_i <= qry_j - blk, _dot(k_own[1], qas[h]), NEG_INF)]
        m = col_max(sts)
        acc_ref[h] = pv(vt_own, sts, m)
        ms.append(m)

    def sweep(j, ms):
        ks = [block_rows(2 * j), block_rows(2 * j + 1)]
        vts = [vt_ref[2 * j], vt_ref[2 * j + 1]]
        new = []
        for h in range(HEAD_PAIR):
            sts = [_dot(ks[0], qas[h]), _dot(ks[1], qas[h])]
            m_new = jnp.maximum(ms[h], col_max(sts))
            acc_ref[h] = jnp.exp2(ms[h] - m_new) * acc_ref[h] + pv(vts, sts, m_new)
            new.append(m_new)
        return tuple(new)

    lax.fori_loop(0, t, sweep, tuple(ms))
    outs = [acc_ref[h, 0:LANES, :] / acc_ref[h, LANES:LANES + 1, :] for h in range(HEAD_PAIR)]
    ot = jnp.where(feat < ATT_HEAD_DIM, outs[0], outs[1])
    o_ref[0] = ot.T.astype(o_ref.dtype)


def _attention_prompt(qt, kt, vt):
    B, _, L = qt.shape
    blk = MOBA_BLOCK
    tq = 2 * blk
    assert L % tq == 0
    nblk = L // blk
    assert nblk <= LANES
    nb = -(-nblk // 8) * 8
    ones_rows = 16
    groups = ATT_WIDTH // LANES
    return pl.pallas_call(
        functools.partial(_attn_kernel, nblk=nblk),
        grid=(B, groups, L // tq),
        in_specs=[pl.BlockSpec((1, LANES, tq), lambda b, g, i: (b, g, i)),
                  pl.BlockSpec((1, LANES, L), lambda b, g, i: (b, g, 0)),
                  pl.BlockSpec((1, LANES, L), lambda b, g, i: (b, g, 0))],
        out_specs=pl.BlockSpec((1, tq, LANES), lambda b, g, i: (b, i, g)),
        out_shape=jax.ShapeDtypeStruct((B, L, ATT_WIDTH), BF16),
        scratch_shapes=[pltpu.VMEM((L, 2 * LANES), BF16),
                        pltpu.VMEM((nblk, LANES + ones_rows, blk), BF16),
                        pltpu.VMEM((nb, LANES), F32),
                        pltpu.VMEM((HEAD_PAIR, LANES + ones_rows, tq), F32)],
        compiler_params=pltpu.CompilerParams(dimension_semantics=("parallel", "parallel", "arbitrary"),
                                             vmem_limit_bytes=VMEM_LIMIT),
        name="moba_prompt",
    )(qt, kt, vt)


def _mlstm_kernel(bi_ref, bf_ref, mq_ref, mk_ref, mv_ref, grow_ref, mo_ref, gout_ref,
                  ml_ref, c_out_ref, n_out_ref, m_out_ref, cn_ref, m_ref, *, chunk):
    j = pl.program_id(1)
    dk, dv = ML_QK_DIM, ML_V_DIM

    @pl.when(j == 0)
    def _():
        cn_ref[...] = jnp.zeros_like(cn_ref)
        m_ref[...] = jnp.zeros_like(m_ref)

    t_i = lax.broadcasted_iota(jnp.int32, (chunk, chunk), 0)
    s_i = lax.broadcasted_iota(jnp.int32, (chunk, chunk), 1)
    tril = s_i <= t_i
    tril_b = jnp.where(tril, 1.0, 0.0).astype(BF16)
    triu_b = jnp.where(t_i <= s_i, 1.0, 0.0).astype(BF16)
    ones_cols = jnp.ones((chunk, LANES), BF16)
    pad_rows = jnp.zeros((13, chunk), F32)
    grow = grow_ref[0]
    for h in range(N_ML_HEADS):
        i_row = grow[h:h + 1, :] + bi_ref[h]
        f_row = _log_sigmoid(grow[N_ML_HEADS + h:N_ML_HEADS + h + 1, :] + bf_ref[h])
        hi = f_row.astype(BF16).astype(F32)
        mid = (f_row - hi).astype(BF16).astype(F32)
        pieces = jnp.concatenate([hi, mid, f_row - hi - mid, pad_rows], axis=0).astype(BF16)
        rows = _dot(pieces, triu_b)
        cols = _dot_nt(tril_b, pieces)
        b_row = rows[0:1] + rows[1:2] + rows[2:3]
        b_col = cols[:, 0:1] + cols[:, 1:2] + cols[:, 2:3]
        a_row = (i_row - b_row) * LOG2_E
        m_old = m_ref[h][:, 0:1]
        a_low = jnp.where(tril, a_row, NEG_INF)
        mm_col = jnp.maximum(jnp.max(a_low, axis=1, keepdims=True), m_old * LOG2_E)
        w_inter = jnp.exp2(m_old * LOG2_E - mm_col)
        floor_col = jnp.exp2(-(b_col * LOG2_E + mm_col))
        decay = jnp.exp2(a_low - mm_col)

        q = mq_ref[0, :, h * dk:(h + 1) * dk]
        k = mk_ref[0, :, h * dk:(h + 1) * dk]
        v_ones = jnp.concatenate([mv_ref[0, :, h * dv:(h + 1) * dv], ones_cols], axis=1)
        cn = cn_ref[h]
        s = _dot_nt(q, k) * decay
        inter = _dot(q, cn.astype(BF16))
        intra = _dot(s.astype(BF16), v_ones)
        num = w_inter * inter[:, 0:dv] + intra[:, 0:dv]
        den = w_inter * inter[:, dv:dv + 1] + intra[:, dv:dv + 1]
        hid = num / jnp.maximum(jnp.abs(den), floor_col)

        mm_end = mm_col[chunk - 1:chunk, :]
        w_row = jnp.exp2(a_row - mm_end)
        kw = (k.astype(F32).T * w_row).astype(BF16)
        cn_ref[h] = jnp.exp2(m_old * LOG2_E - mm_end) * cn + _dot(kw, v_ones)
        m_ref[h] = jnp.broadcast_to(b_row[:, chunk - 1:chunk] + mm_end * (1.0 / LOG2_E), (1, LANES))

        hn = _rms(hid, gout_ref[h:h + 1, :])
        mo = mo_ref[0, :, h * dv:(h + 1) * dv]
        ml_ref[0, :, h * dv:(h + 1) * dv] = (hn * jax.nn.sigmoid(mo)).astype(BF16)

    @pl.when(j == pl.num_programs(1) - 1)
    def _():
        for h in range(N_ML_HEADS):
            c_out_ref[0, h] = cn_ref[h, :, 0:dv].T
            n_out_ref[0, h:h + 1, :] = cn_ref[h, :, dv:dv + LANES].T[0:1, :]
            m_out_ref[0, h:h + 1, :] = m_ref[h]


def _mlstm_prompt(mq, mk, mv, grow, mo, pw):
    B, L, _ = mq.shape
    chunk = ML_CHUNK
    tok = lambda w: pl.BlockSpec((1, chunk, w), lambda b, j, *_: (b, j, 0))
    per_b = lambda shp: pl.BlockSpec((1,) + shp, lambda b, j, *_: (b,) + (0,) * len(shp))
    grid_spec = pltpu.PrefetchScalarGridSpec(
        num_scalar_prefetch=2,
        grid=(B, L // chunk),
        in_specs=[tok(ML_QK_WIDTH), tok(ML_QK_WIDTH), tok(ML_V_WIDTH),
                  pl.BlockSpec((1, 16, chunk), lambda b, j, *_: (b, 0, j)), tok(ML_V_WIDTH),
                  pl.BlockSpec((N_ML_HEADS, ML_V_DIM), lambda b, j, *_: (0, 0))],
        out_specs=(tok(ML_V_WIDTH), per_b((N_ML_HEADS, ML_V_DIM, ML_QK_DIM)),
                   per_b((N_ML_HEADS, ML_QK_DIM)), per_b((N_ML_HEADS, LANES))),
        scratch_shapes=[pltpu.VMEM((N_ML_HEADS, ML_QK_DIM, ML_V_DIM + LANES), F32),
                        pltpu.VMEM((N_ML_HEADS, 1, LANES), F32)],
    )
    return pl.pallas_call(
        functools.partial(_mlstm_kernel, chunk=chunk),
        grid_spec=grid_spec,
        out_shape=(jax.ShapeDtypeStruct((B, L, ML_V_WIDTH), BF16),
                   jax.ShapeDtypeStruct((B, N_ML_HEADS, ML_V_DIM, ML_QK_DIM), F32),
                   jax.ShapeDtypeStruct((B, N_ML_HEADS, ML_QK_DIM), F32),
                   jax.ShapeDtypeStruct((B, N_ML_HEADS, LANES), F32)),
        compiler_params=pltpu.CompilerParams(dimension_semantics=("parallel", "arbitrary"),
                                             vmem_limit_bytes=VMEM_LIMIT),
        name="mlstm_prompt",
    )(pw["b_i"], pw["b_f"], mq, mk, mv, grow, mo, pw["g_ml_out"])


def _post_kernel(x_ref, att_ref, ml_ref, ga_ref, gm_ref, pe_ref,
                 watt_ref, wml_ref, wout_ref, gffn_ref, wgate_ref, wup_ref, wdown_ref,
                 wple_ref, gple_ref, wpg_ref, y_ref):
    mix = (jax.nn.sigmoid(ga_ref[0]) * _dot(att_ref[0].astype(BF16), watt_ref[...])
           + jax.nn.sigmoid(gm_ref[0]) * _dot(ml_ref[0], wml_ref[...]))
    h = x_ref[0] + _dot(mix.astype(BF16), wout_ref[...])
    hb = _rms(h, gffn_ref[...]).astype(BF16)
    act = jax.nn.silu(_dot(hb, wgate_ref[...])) * _dot(hb, wup_ref[...])
    h = h + _dot(act.astype(BF16), wdown_ref[...])
    gate = jax.nn.sigmoid(_dot(_rms(h, gple_ref[...]).astype(BF16), wpg_ref[...]))
    y_ref[0] = h + gate * _dot(pe_ref[0].astype(BF16), wple_ref[...])


def _post(x, att, ml, ga, gm, pe, pw, tm):
    B, L, _ = x.shape
    tok = lambda w: pl.BlockSpec((1, tm, w), lambda b, i: (b, i, 0))
    full = lambda a: pl.BlockSpec(a.shape, lambda b, i: (0,) * a.ndim, pipeline_mode=pl.Buffered(1))
    consts = (pw["w_att"], pw["w_ml"], pw["w_out"], pw["g_ffn"], pw["w_ffn_gate"], pw["w_ffn_up"],
              pw["w_ffn_down"], pw["w_ple"], pw["g_ple"], pw["w_ple_gate"])
    return pl.pallas_call(
        _post_kernel,
        grid=(B, L // tm),
        in_specs=[tok(D_MODEL), tok(ATT_WIDTH), tok(ML_V_WIDTH), tok(D_MODEL), tok(D_MODEL),
                  tok(pe.shape[-1])] + [full(a) for a in consts],
        out_specs=tok(D_MODEL),
        out_shape=jax.ShapeDtypeStruct((B, L, D_MODEL), F32),
        compiler_params=pltpu.CompilerParams(dimension_semantics=("parallel", "parallel"),
                                             vmem_limit_bytes=VMEM_LIMIT),
        name="merge_ffn",
    )(x, att, ml, ga, gm, pe, *consts)


def _sample_select_kernel(pt_ref, *refs, nblk, blocks_per_step):
    page_refs = refs[:2 * blocks_per_step]
    q_ref, sel_ref, qbd_ref, gate_ref = refs[2 * blocks_per_step:]
    j = pl.program_id(1)
    lane = lax.broadcasted_iota(jnp.int32, (N_ATT_HEADS, LANES), 1)

    @pl.when(j == 0)
    def _():
        head = lax.broadcasted_iota(jnp.int32, (N_ATT_HEADS, ATT_WIDTH), 0)
        col = lax.broadcasted_iota(jnp.int32, (N_ATT_HEADS, ATT_WIDTH), 1)
        in_head = (col >= head * ATT_HEAD_DIM) & (col < (head + 1) * ATT_HEAD_DIM)
        qbd = jnp.where(in_head, q_ref[0], 0.0)
        hi = qbd.astype(BF16).astype(F32)
        qbd_ref[...] = jnp.concatenate([hi, qbd - hi], axis=0).astype(BF16)
        gate_ref[...] = jnp.zeros_like(gate_ref)

    qbd = qbd_ref[...]
    gate = gate_ref[...]
    for i in range(blocks_per_step):
        ka, kb = page_refs[2 * i], page_refs[2 * i + 1]
        s = _dot(qbd, ka[0].astype(BF16)) + _dot(qbd, kb[0].astype(BF16))
        g = jnp.sum(s[0:N_ATT_HEADS] + s[N_ATT_HEADS:], axis=1, keepdims=True) * (1.0 / MOBA_BLOCK)
        gate = jnp.where(lane == j * blocks_per_step + i, g, gate)
    gate_ref[...] = gate

    @pl.when(j == pl.num_programs(1) - 1)
    def _():
        gate = gate_ref[...]
        rank = jnp.zeros((N_ATT_HEADS, LANES), F32)
        for n2 in range(nblk):
            col = gate[:, n2:n2 + 1]
            beats = (col > gate) | ((col == gate) & (lane > n2))
            rank = rank + jnp.where(beats, 1.0, 0.0)
        out = jnp.zeros((N_ATT_HEADS, LANES), jnp.int32)
        lane_f = lane.astype(F32)
        for r in range(MOBA_TOPK):
            pick = (rank == float(r)) & (lane < nblk)
            idx = jnp.sum(jnp.where(pick, lane_f, 0.0), axis=1, keepdims=True)
            out = jnp.where(lane == r, idx.astype(jnp.int32), out)
        sel_ref[0] = out


def _sample_select(page_table, cache_kt, q_s):
    Bs, n_pages = page_table.shape
    nblk = n_pages * PAGE_SIZE // MOBA_BLOCK
    assert MOBA_BLOCK == 2 * PAGE_SIZE and nblk <= LANES
    bps = SELECT_BLOCKS_PER_STEP
    assert nblk % bps == 0
    pages_per_step = 2 * bps

    def page_spec(p):
        return pl.BlockSpec((1, ATT_WIDTH, PAGE_SIZE), lambda b, j, pt: (pt[b, j * pages_per_step + p], 0, 0))

    grid_spec = pltpu.PrefetchScalarGridSpec(
        num_scalar_prefetch=1,
        grid=(Bs, nblk // bps),
        in_specs=[page_spec(p) for p in range(pages_per_step)]
                 + [pl.BlockSpec((1, 1, ATT_WIDTH), lambda b, j, pt: (b, 0, 0))],
        out_specs=pl.BlockSpec((1, N_ATT_HEADS, LANES), lambda b, j, pt: (b, 0, 0)),
        scratch_shapes=[pltpu.VMEM((2 * N_ATT_HEADS, ATT_WIDTH), BF16),
                        pltpu.VMEM((N_ATT_HEADS, LANES), F32)],
    )
    return pl.pallas_call(
        functools.partial(_sample_select_kernel, nblk=nblk, blocks_per_step=bps),
        grid_spec=grid_spec,
        out_shape=jax.ShapeDtypeStruct((Bs, N_ATT_HEADS, LANES), jnp.int32),
        compiler_params=pltpu.CompilerParams(dimension_semantics=("parallel", "arbitrary"),
                                             vmem_limit_bytes=VMEM_LIMIT),
        name="moba_sample_select",
    )(page_table, *([cache_kt] * pages_per_step), q_s)


def _sample_attn_kernel(pt_ref, sel_ref, *refs, n_pages):
    k_refs, v_refs = refs[:n_pages], refs[n_pages:2 * n_pages]
    q_ref, kn_ref, vn_ref, o_ref, out_ref = refs[2 * n_pages:]
    h = pl.program_id(1)
    lane = lax.broadcasted_iota(jnp.int32, (1, LANES), 1)
    lo = (h % HEAD_PAIR) * ATT_HEAD_DIM
    in_head = (lane >= lo) & (lane < lo + ATT_HEAD_DIM)
    qh = jnp.where(in_head, q_ref[0], 0.0) * (ATT_HEAD_DIM ** -0.5)
    rows = 16
    qb = jnp.broadcast_to(qh, (rows, LANES)).astype(BF16)

    s_own = jnp.sum(qh * kn_ref[0], axis=1, keepdims=True)
    scores = [_dot(qb, k_refs[j][0].astype(BF16))[0:1] for j in range(n_pages)]
    m = s_own
    for s in scores:
        m = jnp.maximum(m, jnp.max(s, axis=1, keepdims=True))
    p_own = jnp.exp(s_own - m)
    l = p_own
    acc = p_own * vn_ref[0]
    for j, s in enumerate(scores):
        p = jnp.exp(s - m)
        l = l + jnp.sum(p, axis=1, keepdims=True)
        acc = acc + _dot_nt(jnp.broadcast_to(p, (rows, PAGE_SIZE)).astype(BF16), v_refs[j][0].astype(BF16))[0:1]
    res = acc / l

    @pl.when(h % HEAD_PAIR == 0)
    def _():
        out_ref[...] = res

    @pl.when(h % HEAD_PAIR == HEAD_PAIR - 1)
    def _():
        o_ref[0] = jnp.where(in_head, res, out_ref[...])


def _sample_attention(page_table, sel, cache_kt, cache_vt, q_s, k_s, v_s):
    Bs = page_table.shape[0]
    ppb = MOBA_BLOCK // PAGE_SIZE
    n_pages = MOBA_TOPK * ppb

    def tile_spec(j):
        def index(b, h, pt, sl):
            return (pt[b, ppb * sl[b, h * MOBA_TOPK + j // ppb] + j % ppb], h // HEAD_PAIR, 0)
        return pl.BlockSpec((1, LANES, PAGE_SIZE), index)

    tiles = [tile_spec(j) for j in range(n_pages)]
    row = pl.BlockSpec((1, 1, LANES), lambda b, h, pt, sl: (b, 0, h // HEAD_PAIR))
    grid_spec = pltpu.PrefetchScalarGridSpec(
        num_scalar_prefetch=2,
        grid=(Bs, N_ATT_HEADS),
        in_specs=tiles + tiles + [row, row, row],
        out_specs=row,
        scratch_shapes=[pltpu.VMEM((1, LANES), F32)],
    )
    return pl.pallas_call(
        functools.partial(_sample_attn_kernel, n_pages=n_pages),
        grid_spec=grid_spec,
        out_shape=jax.ShapeDtypeStruct((Bs, 1, ATT_WIDTH), F32),
        compiler_params=pltpu.CompilerParams(dimension_semantics=("parallel", "arbitrary"),
                                             vmem_limit_bytes=VMEM_LIMIT),
        name="moba_sample_attn",
    )(page_table, sel, *([cache_kt] * n_pages), *([cache_vt] * n_pages), q_s, k_s, v_s)


def _mlstm_step_kernel(bi_ref, bf_ref, m0_ref, mq_ref, mk_ref, mv_ref, g_ref, mo_ref, gout_ref,
                       c0_ref, n0_ref, ml_ref, c_out_ref, n_out_ref, m_out_ref):
    b = pl.program_id(0)
    dv = ML_V_DIM
    r_i = lax.broadcasted_iota(jnp.int32, (dv, dv), 0)
    c_i = lax.broadcasted_iota(jnp.int32, (dv, dv), 1)
    eye = r_i == c_i
    g = g_ref[0]
    for h in range(N_ML_HEADS):
        i_g = g[:, h:h + 1] + bi_ref[h]
        logf = _log_sigmoid(g[:, N_ML_HEADS + h:N_ML_HEADS + h + 1] + bf_ref[h])
        m_old = m0_ref[b, h]
        inter = logf + m_old
        m_t = jnp.maximum(inter, i_g)
        w_inter = jnp.exp(inter - m_t)
        w_in = jnp.exp(i_g - m_t)
        q = mq_ref[0, :, h * ML_QK_DIM:(h + 1) * ML_QK_DIM].astype(F32)
        k = mk_ref[0, :, h * ML_QK_DIM:(h + 1) * ML_QK_DIM].astype(F32)
        v = mv_ref[0, :, h * dv:(h + 1) * dv].astype(F32)
        c_old = c0_ref[0, 0, h]
        n_old = n0_ref[0, 0, h:h + 1, :]
        s = jnp.sum(q * k, axis=1, keepdims=True) * w_in
        v_col = jnp.sum(jnp.where(eye, v, 0.0), axis=1, keepdims=True)
        num = w_inter * jnp.sum(c_old * q, axis=1, keepdims=True) + s * v_col
        den = w_inter * jnp.sum(n_old * q, axis=1, keepdims=True) + s
        hid_col = num / jnp.maximum(jnp.abs(den), jnp.exp(-m_t))
        hid = jnp.sum(jnp.where(eye, hid_col, 0.0), axis=0, keepdims=True)
        c_out_ref[0, 0, h] = w_inter * c_old + (w_in * v_col) * k
        n_out_ref[0, 0, h:h + 1, :] = w_inter * n_old + w_in * k
        m_out_ref[0, h:h + 1, :] = jnp.broadcast_to(m_t, (1, LANES))
        hn = _rms(hid, gout_ref[h:h + 1, :])
        mo = mo_ref[0, :, h * dv:(h + 1) * dv]
        ml_ref[0, :, h * dv:(h + 1) * dv] = (hn * jax.nn.sigmoid(mo)).astype(BF16)


def _mlstm_sample(mq, mk, mv, gcol, mo, state_c, state_n, state_m, pw):
    Bs = mq.shape[0]
    row = lambda w: pl.BlockSpec((1, 1, w), lambda b, *_: (b, 0, 0))
    grid_spec = pltpu.PrefetchScalarGridSpec(
        num_scalar_prefetch=3,
        grid=(Bs,),
        in_specs=[row(ML_QK_WIDTH), row(ML_QK_WIDTH), row(ML_V_WIDTH), row(LANES), row(ML_V_WIDTH),
                  pl.BlockSpec((N_ML_HEADS, ML_V_DIM), lambda b, *_: (0, 0)),
                  pl.BlockSpec((1, 1, N_ML_HEADS, ML_V_DIM, ML_QK_DIM), lambda b, *_: (0, b, 0, 0, 0)),
                  pl.BlockSpec((1, 1, N_ML_HEADS, ML_QK_DIM), lambda b, *_: (0, b, 0, 0))],
        out_specs=(row(ML_V_WIDTH),
                   pl.BlockSpec((1, 1, N_ML_HEADS, ML_V_DIM, ML_QK_DIM), lambda b, *_: (0, b, 0, 0, 0)),
                   pl.BlockSpec((1, 1, N_ML_HEADS, ML_QK_DIM), lambda b, *_: (0, b, 0, 0)),
                   pl.BlockSpec((1, N_ML_HEADS, LANES), lambda b, *_: (b, 0, 0))),
    )
    return pl.pallas_call(
        _mlstm_step_kernel,
        grid_spec=grid_spec,
        out_shape=(jax.ShapeDtypeStruct((Bs, 1, ML_V_WIDTH), BF16),
                   jax.ShapeDtypeStruct(state_c.shape, F32),
                   jax.ShapeDtypeStruct(state_n.shape, F32),
                   jax.ShapeDtypeStruct((Bs, N_ML_HEADS, LANES), F32)),
        compiler_params=pltpu.CompilerParams(dimension_semantics=("arbitrary",),
                                             vmem_limit_bytes=VMEM_LIMIT),
        name="mlstm_sample",
    )(pw["b_i"], pw["b_f"], state_m[0], mq, mk, mv, gcol, mo, pw["g_ml_out"], state_c, state_n)


def _rope_angles(pos):
    freqs = ROPE_THETA ** (-jnp.arange(0, ROPE_DIMS, 2, dtype=F32) / ROPE_DIMS)
    return pos.astype(F32)[:, None] * freqs[None, :]


def _rope_tables_cols(pos):
    ang = _rope_angles(pos).T
    return jnp.cos(ang), jnp.sin(ang)


def _rope_tables_rows(pos):
    half = ROPE_DIMS // 2
    n = pos.shape[0]
    ang = _rope_angles(pos)
    cos, sin = jnp.cos(ang), jnp.sin(ang)
    rest = ATT_HEAD_DIM - ROPE_DIMS
    zh = jnp.zeros((n, half), F32)
    cos_h = jnp.concatenate([cos, cos, jnp.ones((n, rest), F32)], axis=-1)
    sa_h = jnp.concatenate([-sin, zh, jnp.zeros((n, rest), F32)], axis=-1)
    sb_h = jnp.concatenate([zh, sin, jnp.zeros((n, rest), F32)], axis=-1)
    tile = lambda t: jnp.tile(t, (1, HEAD_PAIR))
    return tile(cos_h), tile(sa_h), tile(sb_h)


def _prepare_weights(w_in, b_igate, b_fgate, g_mix_norm, g_q_norm, g_k_norm, g_ml_out_norm,
                     w_att_branch, w_ml_branch, w_out, g_ffn_norm, w_ffn_gate, w_ffn_up, w_ffn_down,
                     w_ple, g_ple_norm, w_ple_gate):
    w_gates = w_in[:, _C_GATES:_C_MO]
    head_of = jnp.arange(ATT_WIDTH, dtype=jnp.int32) // ATT_HEAD_DIM
    w_qkv = w_in[:, _C_Q:_C_MQ].astype(BF16)
    return {
        "w_qkv": w_qkv,
        "w_qkv_t": w_qkv.T,
        "w_m": w_in[:, _C_MQ:_C_GATES].astype(BF16),
        "g_q_col": g_q_norm.reshape(1, ATT_HEAD_DIM, 1),
        "g_k_col": g_k_norm.reshape(1, ATT_HEAD_DIM, 1),
        "w_g": jnp.pad(w_gates, ((0, 0), (0, LANES - 2 * N_ML_HEADS))).astype(BF16),
        "w_gt": jnp.pad(w_gates.T, ((0, 16 - 2 * N_ML_HEADS), (0, 0))).astype(BF16),
        "w_b": w_in[:, _C_MO:_C_END].astype(BF16),
        "e_head": (head_of[:, None] == head_of[None, :]).astype(BF16),
        "g_mix": g_mix_norm.reshape(1, D_MODEL),
        "g_q": jnp.tile(g_q_norm, N_ATT_HEADS).reshape(1, ATT_WIDTH),
        "g_k": jnp.tile(g_k_norm, N_ATT_HEADS).reshape(1, ATT_WIDTH),
        "g_ml_out": g_ml_out_norm,
        "b_i": b_igate, "b_f": b_fgate,
        "w_att": w_att_branch.astype(BF16), "w_ml": w_ml_branch.astype(BF16), "w_out": w_out.astype(BF16),
        "g_ffn": g_ffn_norm.reshape(1, D_MODEL),
        "w_ffn_gate": w_ffn_gate.astype(BF16), "w_ffn_up": w_ffn_up.astype(BF16),
        "w_ffn_down": w_ffn_down.astype(BF16),
        "w_ple": w_ple.astype(BF16), "g_ple": g_ple_norm.reshape(1, D_MODEL),
        "w_ple_gate": w_ple_gate.astype(BF16),
    }


def _prompt_layer(x, pe, pw):
    B, L, _ = x.shape
    pos = jnp.arange(L, dtype=jnp.int32)
    qt, kt, vt, mq, mk, mv, gcol, grow, mo, ga, gm = _project(x, pos, pw, tm=256, feature_major=True)
    att = _attention_prompt(qt, kt, vt)
    ml, c_fin, n_fin, m_pad = _mlstm_prompt(mq, mk, mv, grow, mo, pw)
    y = _post(x, att, ml, ga, gm, pe, pw, tm=256)
    tokens_major = lambda t: jnp.transpose(t.reshape(B, N_ATT_HEADS, ATT_HEAD_DIM, L), (0, 3, 1, 2))
    return y, tokens_major(kt), tokens_major(vt), c_fin, n_fin, m_pad[:, :, 0]


def _sample_layer(x, pe, cache_k, cache_v, page_table, state_c, state_n, state_m, pw):
    Bs, Ls, _ = x.shape
    assert Ls == 1
    past_len = page_table.shape[1] * PAGE_SIZE
    n_pool = cache_k.shape[0]
    pos = jnp.full((Bs,), past_len, dtype=jnp.int32)
    xt = x.reshape(1, Bs, D_MODEL)
    q, k, v, mq, mk, mv, gcol, _, mo, ga, gm = _project(xt, pos, pw, tm=Bs, feature_major=False)
    per_seq = lambda t: t.reshape(Bs, 1, t.shape[-1])
    pages_t = lambda c: jnp.transpose(c, (0, 2, 3, 1)).reshape(n_pool, ATT_WIDTH, PAGE_SIZE)
    cache_kt, cache_vt = pages_t(cache_k), pages_t(cache_v)
    sel = _sample_select(page_table, cache_kt, per_seq(q))
    sel2 = sel[:, :, :MOBA_TOPK].reshape(Bs, N_ATT_HEADS * MOBA_TOPK)
    att = _sample_attention(page_table, sel2, cache_kt, cache_vt, per_seq(q), per_seq(k), per_seq(v))
    ml, c_new, n_new, m_pad = _mlstm_sample(per_seq(mq), per_seq(mk), per_seq(mv), per_seq(gcol),
                                            per_seq(mo), state_c, state_n, state_m, pw)
    y = _post(xt, att.reshape(1, Bs, ATT_WIDTH), ml.reshape(1, Bs, ML_V_WIDTH), ga, gm,
              pe.reshape(1, Bs, pe.shape[-1]), pw, tm=Bs)
    return y.reshape(Bs, 1, D_MODEL), k, v, c_new, n_new, m_pad[:, :, 0]


def kernel(x_prompt, x_sample, cache_k, cache_v, state_mlstm_C, state_mlstm_n, state_mlstm_m, page_table,
           p_prompt, p_sample, w_in, b_igate, b_fgate, g_mix_norm, g_q_norm, g_k_norm, g_ml_out_norm,
           w_att_branch, w_ml_branch, w_out, g_ffn_norm, w_ffn_gate, w_ffn_up, w_ffn_down,
           w_ple, g_ple_norm, w_ple_gate):
    depth = w_in.shape[0]
    assert depth == 1
    Bp, Lp, _ = x_prompt.shape
    Bs, Ls, _ = x_sample.shape
    pw = _prepare_weights(w_in[0], b_igate[0], b_fgate[0], g_mix_norm[0], g_q_norm[0], g_k_norm[0],
                          g_ml_out_norm[0], w_att_branch[0], w_ml_branch[0], w_out[0], g_ffn_norm[0],
                          w_ffn_gate[0], w_ffn_up[0], w_ffn_down[0], w_ple[0], g_ple_norm[0], w_ple_gate[0])
    yp, kp, vp, cp, n_p, mp = _prompt_layer(x_prompt, p_prompt[0], pw)
    ys, ks, vs, cs, ns, ms = _sample_layer(x_sample, p_sample[0], cache_k[0], cache_v[0], page_table,
                                           state_mlstm_C, state_mlstm_n, state_mlstm_m, pw)
    heads = lambda t: t.reshape(1, Bs, Ls, N_ATT_HEADS, ATT_HEAD_DIM)
    return (yp, ys, kp[None], vp[None], heads(ks), heads(vs),
            cp[None], n_p[None], mp[None], cs, ns, ms[None])
```

```python
import functools

import jax
import jax.numpy as jnp
from jax import lax
from jax.experimental import pallas as pl
from jax.experimental.pallas import tpu as pltpu

F32 = jnp.float32
BF16 = jnp.bfloat16

D_MODEL = 1024
N_ATT_HEADS = 8
ATT_HEAD_DIM = 64
ATT_WIDTH = N_ATT_HEADS * ATT_HEAD_DIM
ROPE_DIMS = ATT_HEAD_DIM // 4
ROPE_THETA = 500000.0
MOBA_BLOCK = 256
MOBA_TOPK = 3
PAGE_SIZE = 128
N_ML_HEADS = 4
ML_QK_DIM = 128
ML_V_DIM = 256
ML_QK_WIDTH = N_ML_HEADS * ML_QK_DIM
ML_V_WIDTH = N_ML_HEADS * ML_V_DIM
NORM_EPS = 1e-6
NEG_INF = -1e30
LOG2_E = 1.4426950408889634

LANES = 128
HEAD_PAIR = LANES // ATT_HEAD_DIM
ML_CHUNK = 256
SELECT_BLOCKS_PER_STEP = 8
VMEM_LIMIT = 56 * 1024 * 1024

_C_Q, _C_K, _C_V = 0, ATT_WIDTH, 2 * ATT_WIDTH
_C_MQ = 3 * ATT_WIDTH
_C_MK = _C_MQ + ML_QK_WIDTH
_C_MV = _C_MK + ML_QK_WIDTH
_C_GATES = _C_MV + ML_V_WIDTH
_C_MO = _C_GATES + 2 * N_ML_HEADS
_C_GA = _C_MO + ML_V_WIDTH
_C_GM = _C_GA + D_MODEL
_C_END = _C_GM + D_MODEL


def _dot(a, b):
    return jnp.dot(a, b, preferred_element_type=F32)


def _dot_nt(a, b, precision=None):
    return lax.dot_general(a, b, (((1,), (1,)), ((), ())), precision=precision,
                           preferred_element_type=F32)


def _rms(x, g):
    return x * lax.rsqrt(jnp.mean(x * x, axis=-1, keepdims=True) + NORM_EPS) * g


def _log_sigmoid(x):
    return jnp.minimum(x, 0.0) - jnp.log1p(jnp.exp(-jnp.abs(x)))


def _proj_shared(xb, wm_ref, wg_ref, wgt_ref, wb_ref,
                 mq_ref, mk_ref, mv_ref, gcol_ref, grow_ref, mo_ref, ga_ref, gm_ref):
    mq_ref[0] = _dot(xb, wm_ref[:, 0:ML_QK_WIDTH]).astype(BF16)
    mk_ref[0] = (_dot(xb, wm_ref[:, ML_QK_WIDTH:2 * ML_QK_WIDTH]) * (ML_QK_DIM ** -0.5)).astype(BF16)
    mv_ref[0] = _dot(xb, wm_ref[:, 2 * ML_QK_WIDTH:2 * ML_QK_WIDTH + ML_V_WIDTH]).astype(BF16)
    gcol_ref[0] = _dot(xb, wg_ref[...])
    grow_ref[0] = _dot_nt(wgt_ref[...], xb)
    mo_ref[0] = _dot(xb, wb_ref[:, 0:ML_V_WIDTH])
    ga_ref[0] = _dot(xb, wb_ref[:, ML_V_WIDTH:ML_V_WIDTH + D_MODEL])
    gm_ref[0] = _dot(xb, wb_ref[:, ML_V_WIDTH + D_MODEL:ML_V_WIDTH + 2 * D_MODEL])


def _proj_kernel_rows(x_ref, cos_ref, sa_ref, sb_ref, gmix_ref, gq_ref, gk_ref, e_ref, wqkv_ref,
                      wm_ref, wg_ref, wgt_ref, wb_ref, q_ref, k_ref, v_ref, *rest_refs):
    xb = _rms(x_ref[0], gmix_ref[...]).astype(BF16)
    cos, sa, sb = cos_ref[...], sa_ref[...], sb_ref[...]
    e = e_ref[...]

    def head_norm_rope(z, g):
        zz = z * z
        hi = zz.astype(BF16)
        lo = (zz - hi.astype(F32)).astype(BF16)
        ss = _dot(hi, e) + _dot(lo, e)
        zn = z * lax.rsqrt(ss * (1.0 / ATT_HEAD_DIM) + NORM_EPS) * g
        outs = []
        for grp in range(ATT_WIDTH // LANES):
            zg = zn[:, grp * LANES:(grp + 1) * LANES]
            outs.append(zg * cos + pltpu.roll(zg, LANES - ROPE_DIMS // 2, 1) * sa
                        + pltpu.roll(zg, ROPE_DIMS // 2, 1) * sb)
        return jnp.concatenate(outs, axis=-1)

    q_ref[0] = head_norm_rope(_dot(xb, wqkv_ref[:, 0:ATT_WIDTH]), gq_ref[...])
    k_ref[0] = head_norm_rope(_dot(xb, wqkv_ref[:, ATT_WIDTH:2 * ATT_WIDTH]), gk_ref[...])
    v_ref[0] = _dot(xb, wqkv_ref[:, 2 * ATT_WIDTH:3 * ATT_WIDTH])
    _proj_shared(xb, wm_ref, wg_ref, wgt_ref, wb_ref, *rest_refs)


def _proj_kernel_cols(x_ref, cos_ref, sin_ref, gmix_ref, gq_ref, gk_ref, wqkvt_ref,
                      wm_ref, wg_ref, wgt_ref, wb_ref, qt_ref, kt_ref, vt_ref, *rest_refs):
    xb = _rms(x_ref[0], gmix_ref[...]).astype(BF16)
    tm = xb.shape[0]
    zt = _dot_nt(wqkvt_ref[...], xb)
    cos, sin = cos_ref[...][None], sin_ref[...][None]
    half = ROPE_DIMS // 2

    def head_norm_rope(z, g):
        z3 = z.reshape(N_ATT_HEADS, ATT_HEAD_DIM, tm)
        ss = jnp.sum(z3 * z3, axis=1, keepdims=True)
        zn = z3 * lax.rsqrt(ss * (1.0 / ATT_HEAD_DIM) + NORM_EPS) * g
        x1, x2 = zn[:, 0:half, :], zn[:, half:ROPE_DIMS, :]
        out = jnp.concatenate([x1 * cos - x2 * sin, x2 * cos + x1 * sin, zn[:, ROPE_DIMS:, :]], axis=1)
        return out.reshape(ATT_WIDTH, tm)

    qt_ref[0] = head_norm_rope(zt[0:ATT_WIDTH], gq_ref[...])
    kt_ref[0] = head_norm_rope(zt[ATT_WIDTH:2 * ATT_WIDTH], gk_ref[...])
    vt_ref[0] = zt[2 * ATT_WIDTH:3 * ATT_WIDTH]
    _proj_shared(xb, wm_ref, wg_ref, wgt_ref, wb_ref, *rest_refs)


def _project(x, pos, pw, tm, feature_major):
    B, L, _ = x.shape
    tok = lambda w: pl.BlockSpec((1, tm, w), lambda b, i: (b, i, 0))
    full = lambda a: pl.BlockSpec(a.shape, lambda b, i: (0,) * a.ndim)
    shared = (pw["w_m"], pw["w_g"], pw["w_gt"], pw["w_b"])
    rest_shape = (
        jax.ShapeDtypeStruct((B, L, ML_QK_WIDTH), BF16),
        jax.ShapeDtypeStruct((B, L, ML_QK_WIDTH), BF16),
        jax.ShapeDtypeStruct((B, L, ML_V_WIDTH), BF16),
        jax.ShapeDtypeStruct((B, L, LANES), F32),
        jax.ShapeDtypeStruct((B, 16, L), F32),
        jax.ShapeDtypeStruct((B, L, ML_V_WIDTH), F32),
        jax.ShapeDtypeStruct((B, L, D_MODEL), F32),
        jax.ShapeDtypeStruct((B, L, D_MODEL), F32),
    )
    rest_specs = (tok(ML_QK_WIDTH), tok(ML_QK_WIDTH), tok(ML_V_WIDTH), tok(LANES),
                  pl.BlockSpec((1, 16, tm), lambda b, i: (b, 0, i)),
                  tok(ML_V_WIDTH), tok(D_MODEL), tok(D_MODEL))
    if feature_major:
        cos, sin = _rope_tables_cols(pos)
        tab = pl.BlockSpec((ROPE_DIMS // 2, tm), lambda b, i: (0, i))
        consts = (pw["g_mix"], pw["g_q_col"], pw["g_k_col"], pw["w_qkv_t"]) + shared
        body, tables, tab_specs = _proj_kernel_cols, (cos, sin), [tab, tab]
        qkv_shape = (jax.ShapeDtypeStruct((B, ATT_WIDTH, L), F32),) * 3
        qkv_specs = (pl.BlockSpec((1, ATT_WIDTH, tm), lambda b, i: (b, 0, i)),) * 3
    else:
        tables = _rope_tables_rows(pos)
        tab = pl.BlockSpec((tm, LANES), lambda b, i: (i, 0))
        consts = (pw["g_mix"], pw["g_q"], pw["g_k"], pw["e_head"], pw["w_qkv"]) + shared
        body, tab_specs = _proj_kernel_rows, [tab, tab, tab]
        qkv_shape = (jax.ShapeDtypeStruct((B, L, ATT_WIDTH), F32),) * 3
        qkv_specs = (tok(ATT_WIDTH),) * 3
    return pl.pallas_call(
        body,
        grid=(B, L // tm),
        in_specs=[tok(D_MODEL)] + tab_specs + [full(a) for a in consts],
        out_specs=qkv_specs + rest_specs,
        out_shape=qkv_shape + rest_shape,
        compiler_params=pltpu.CompilerParams(dimension_semantics=("parallel", "parallel"),
                                             vmem_limit_bytes=VMEM_LIMIT),
        name="in_proj_cols" if feature_major else "in_proj_rows",
    )(x, *tables, *consts)


def _attn_kernel(q_ref, k_ref, v_ref, o_ref, kb_ref, vt_ref, kmean_ref, acc_ref, s0_ref, s1_ref, *, nblk):
    blk = MOBA_BLOCK
    tq = 2 * blk
    ones_rows = vt_ref.shape[1] - LANES
    t = pl.program_id(2)
    c0 = 2 * t

    nb = kmean_ref.shape[0]

    @pl.when(t == 0)
    def _():
        lane = lax.broadcasted_iota(jnp.int32, (blk, LANES), 1)
        kmean_ref[...] = jnp.zeros_like(kmean_ref)
        for n in range(nblk):
            rows = slice(n * blk, (n + 1) * blk)
            kf = k_ref[0, :, rows].T
            kb_ref[rows, 0:LANES] = kf.astype(BF16)
            kb_ref[rows, LANES:2 * LANES] = jnp.where(lane == n, 1.0, 0.0).astype(BF16)
            kmean_ref[n:n + 1, :] = jnp.sum(kf, axis=0, keepdims=True) * (1.0 / blk)
            vt_ref[n] = jnp.concatenate([v_ref[0, :, rows], jnp.ones((ones_rows, blk), F32)],
                                        axis=0).astype(BF16)

    q2 = q_ref[0]
    feat = lax.broadcasted_iota(jnp.int32, (LANES, tq), 0)
    blk_id = lax.broadcasted_iota(jnp.int32, (nb, tq), 0)
    col = lax.broadcasted_iota(jnp.int32, (nb, tq), 1)
    cur = c0 + (col >= blk).astype(jnp.int32)
    kmean = kmean_ref[...]
    bias_pad = jnp.zeros((LANES - nb, tq), F32)

    qas = []
    for h in range(HEAD_PAIR):
        qh = jnp.where((feat >= h * ATT_HEAD_DIM) & (feat < (h + 1) * ATT_HEAD_DIM), q2, 0.0)
        gate = jnp.dot(kmean, qh, precision=lax.Precision.HIGHEST,
                       preferred_element_type=F32)
        rank = jnp.zeros((nb, tq), F32)
        for n2 in range(nblk):
            row = gate[n2:n2 + 1, :]
            beats = ((row > gate) | ((row == gate) & (blk_id > n2))) & (cur > n2)
            rank = rank + jnp.where(beats, 1.0, 0.0)
        open_blk = ((blk_id < cur) & (rank < MOBA_TOPK)) | (blk_id == cur)
        bias = jnp.where(open_blk, 0.0, NEG_INF)
        qas.append(jnp.concatenate([qh * (ATT_HEAD_DIM ** -0.5 * LOG2_E), bias, bias_pad],
                                   axis=0).astype(BF16))

    def block_rows(n):
        return kb_ref[pl.ds(pl.multiple_of(n * blk, blk), blk), :]

    key_i = lax.broadcasted_iota(jnp.int32, (blk, tq), 0)
    qry_j = lax.broadcasted_iota(jnp.int32, (blk, tq), 1)
    s_refs = (s0_ref, s1_ref)

    def stage_a(first, slot, own):
        ks = [block_rows(first), block_rows(first + 1)]
        maxima = []
        for h in range(HEAD_PAIR):
            sts = [_dot(ks[0], qas[h]), _dot(ks[1], qas[h])]
            if own:
                sts = [jnp.where(key_i <= qry_j, sts[0], NEG_INF),
                       jnp.where(key_i <= qry_j - blk, sts[1], NEG_INF)]
            s_refs[slot][h, 0] = sts[0]
            s_refs[slot][h, 1] = sts[1]
            maxima.append(jnp.maximum(jnp.max(sts[0], axis=0, keepdims=True),
                                      jnp.max(sts[1], axis=0, keepdims=True)))
        return maxima

    def stage_b(first, slot, m_run, maxima):
        vts = [vt_ref[first], vt_ref[first + 1]]
        m_out = []
        for h in range(HEAD_PAIR):
            m_new = jnp.maximum(m_run[h], maxima[h])
            pv = (_dot(vts[0], jnp.exp2(s_refs[slot][h, 0] - m_new).astype(BF16))
                  + _dot(vts[1], jnp.exp2(s_refs[slot][h, 1] - m_new).astype(BF16)))
            acc_ref[h] = jnp.exp2(m_run[h] - m_new) * acc_ref[h] + pv
            m_out.append(m_new)
        return m_out

    def pending_first(j):
        return jnp.where(j == 0, c0, 2 * j - 2)

    acc_ref[...] = jnp.zeros_like(acc_ref)
    m_start = [jnp.full((1, tq), NEG_INF, F32)] * HEAD_PAIR
    carry = tuple(m_start) + tuple(stage_a(c0, 0, own=True))

    def trip(j, slot, carry):
        m_run, maxima = carry[:HEAD_PAIR], carry[HEAD_PAIR:]
        new_maxima = stage_a(2 * j, 1 - slot, own=False)
        m_new = stage_b(pending_first(j), slot, m_run, maxima)
        return tuple(m_new) + tuple(new_maxima)

    carry = lax.fori_loop(0, lax.shift_right_logical(t, 1),
                          lambda i, cr: trip(2 * i + 1, 1, trip(2 * i, 0, cr)), carry)

    def odd_tail(carry):
        carry = trip(t - 1, 0, carry)
        stage_b(pending_first(t), 1, carry[:HEAD_PAIR], carry[HEAD_PAIR:])
        return 0

    def even_tail(carry):
        stage_b(pending_first(t), 0, carry[:HEAD_PAIR], carry[HEAD_PAIR:])
        return 0

    lax.cond((t & 1) == 1, odd_tail, even_tail, carry)
    outs = [acc_ref[h, 0:LANES, :] / acc_ref[h, LANES:LANES + 1, :] for h in range(HEAD_PAIR)]
    ot = jnp.where(feat < ATT_HEAD_DIM, outs[0], outs[1])
    o_ref[0] = ot.T.astype(o_ref.dtype)


def _attention_prompt(qt, kt, vt):
    B, _, L = qt.shape
    blk = MOBA_BLOCK
    tq = 2 * blk
    assert L % tq == 0
    nblk = L // blk
    assert nblk <= LANES
    nb = -(-nblk // 8) * 8
    ones_rows = 16
    groups = ATT_WIDTH // LANES
    return pl.pallas_call(
        functools.partial(_attn_kernel, nblk=nblk),
        grid=(B, groups, L // tq),
        in_specs=[pl.BlockSpec((1, LANES, tq), lambda b, g, i: (b, g, i)),
                  pl.BlockSpec((1, LANES, L), lambda b, g, i: (b, g, 0)),
                  pl.BlockSpec((1, LANES, L), lambda b, g, i: (b, g, 0))],
        out_specs=pl.BlockSpec((1, tq, LANES), lambda b, g, i: (b, i, g)),
        out_shape=jax.ShapeDtypeStruct((B, L, ATT_WIDTH), BF16),
        scratch_shapes=[pltpu.VMEM((L, 2 * LANES), BF16),
                        pltpu.VMEM((nblk, LANES + ones_rows, blk), BF16),
                        pltpu.VMEM((nb, LANES), F32),
                        pltpu.VMEM((HEAD_PAIR, LANES + ones_rows, tq), F32),
                        pltpu.VMEM((HEAD_PAIR, 2, blk, tq), F32),
                        pltpu.VMEM((HEAD_PAIR, 2, blk, tq), F32)],
        compiler_params=pltpu.CompilerParams(dimension_semantics=("parallel", "parallel", "arbitrary"),
                                             vmem_limit_bytes=VMEM_LIMIT),
        name="moba_prompt",
    )(qt, kt, vt)


def _mlstm_kernel(bi_ref, bf_ref, mq_ref, mk_ref, mv_ref, grow_ref, mo_ref, gout_ref,
                  ml_ref, c_out_ref, n_out_ref, m_out_ref, cn_ref, m_ref, *, chunk):
    j = pl.program_id(1)
    dk, dv = ML_QK_DIM, ML_V_DIM

    @pl.when(j == 0)
    def _():
        cn_ref[...] = jnp.zeros_like(cn_ref)
        m_ref[...] = jnp.zeros_like(m_ref)

    t_i = lax.broadcasted_iota(jnp.int32, (chunk, chunk), 0)
    s_i = lax.broadcasted_iota(jnp.int32, (chunk, chunk), 1)
    tril = s_i <= t_i
    tril_b = jnp.where(tril, 1.0, 0.0).astype(BF16)
    triu_b = jnp.where(t_i <= s_i, 1.0, 0.0).astype(BF16)
    ones_cols = jnp.ones((chunk, LANES), BF16)
    pad_rows = jnp.zeros((13, chunk), F32)
    grow = grow_ref[0]
    for h in range(N_ML_HEADS):
        i_row = grow[h:h + 1, :] + bi_ref[h]
        f_row = _log_sigmoid(grow[N_ML_HEADS + h:N_ML_HEADS + h + 1, :] + bf_ref[h])
        hi = f_row.astype(BF16).astype(F32)
        mid = (f_row - hi).astype(BF16).astype(F32)
        pieces = jnp.concatenate([hi, mid, f_row - hi - mid, pad_rows], axis=0).astype(BF16)
        rows = _dot(pieces, triu_b)
        cols = _dot_nt(tril_b, pieces)
        b_row = rows[0:1] + rows[1:2] + rows[2:3]
        b_col = cols[:, 0:1] + cols[:, 1:2] + cols[:, 2:3]
        a_row = (i_row - b_row) * LOG2_E
        m_old = m_ref[h][:, 0:1]
        a_low = jnp.where(tril, a_row, NEG_INF)
        mm_col = jnp.maximum(jnp.max(a_low, axis=1, keepdims=True), m_old * LOG2_E)
        w_inter = jnp.exp2(m_old * LOG2_E - mm_col)
        floor_col = jnp.exp2(-(b_col * LOG2_E + mm_col))
        decay = jnp.exp2(a_low - mm_col)

        q = mq_ref[0, :, h * dk:(h + 1) * dk]
        k = mk_ref[0, :, h * dk:(h + 1) * dk]
        v_ones = jnp.concatenate([mv_ref[0, :, h * dv:(h + 1) * dv], ones_cols], axis=1)
        cn = cn_ref[h]
        s = _dot_nt(q, k) * decay
        inter = _dot(q, cn.astype(BF16))
        intra = _dot(s.astype(BF16), v_ones)
        num = w_inter * inter[:, 0:dv] + intra[:, 0:dv]
        den = w_inter * inter[:, dv:dv + 1] + intra[:, dv:dv + 1]
        hid = num / jnp.maximum(jnp.abs(den), floor_col)

        mm_end = mm_col[chunk - 1:chunk, :]
        w_row = jnp.exp2(a_row - mm_end)
        kw = (k.astype(F32).T * w_row).astype(BF16)
        cn_ref[h] = jnp.exp2(m_old * LOG2_E - mm_end) * cn + _dot(kw, v_ones)
        m_ref[h] = jnp.broadcast_to(b_row[:, chunk - 1:chunk] + mm_end * (1.0 / LOG2_E), (1, LANES))

        hn = _rms(hid, gout_ref[h:h + 1, :])
        mo = mo_ref[0, :, h * dv:(h + 1) * dv]
        ml_ref[0, :, h * dv:(h + 1) * dv] = (hn * jax.nn.sigmoid(mo)).astype(BF16)

    @pl.when(j == pl.num_programs(1) - 1)
    def _():
        for h in range(N_ML_HEADS):
            c_out_ref[0, h] = cn_ref[h, :, 0:dv].T
            n_out_ref[0, h:h + 1, :] = cn_ref[h, :, dv:dv + LANES].T[0:1, :]
            m_out_ref[0, h:h + 1, :] = m_ref[h]


def _mlstm_prompt(mq, mk, mv, grow, mo, pw):
    B, L, _ = mq.shape
    chunk = ML_CHUNK
    tok = lambda w: pl.BlockSpec((1, chunk, w), lambda b, j, *_: (b, j, 0))
    per_b = lambda shp: pl.BlockSpec((1,) + shp, lambda b, j, *_: (b,) + (0,) * len(shp))
    grid_spec = pltpu.PrefetchScalarGridSpec(
        num_scalar_prefetch=2,
        grid=(B, L // chunk),
        in_specs=[tok(ML_QK_WIDTH), tok(ML_QK_WIDTH), tok(ML_V_WIDTH),
                  pl.BlockSpec((1, 16, chunk), lambda b, j, *_: (b, 0, j)), tok(ML_V_WIDTH),
                  pl.BlockSpec((N_ML_HEADS, ML_V_DIM), lambda b, j, *_: (0, 0))],
        out_specs=(tok(ML_V_WIDTH), per_b((N_ML_HEADS, ML_V_DIM, ML_QK_DIM)),
                   per_b((N_ML_HEADS, ML_QK_DIM)), per_b((N_ML_HEADS, LANES))),
        scratch_shapes=[pltpu.VMEM((N_ML_HEADS, ML_QK_DIM, ML_V_DIM + LANES), F32),
                        pltpu.VMEM((N_ML_HEADS, 1, LANES), F32)],
    )
    return pl.pallas_call(
        functools.partial(_mlstm_kernel, chunk=chunk),
        grid_spec=grid_spec,
        out_shape=(jax.ShapeDtypeStruct((B, L, ML_V_WIDTH), BF16),
                   jax.ShapeDtypeStruct((B, N_ML_HEADS, ML_V_DIM, ML_QK_DIM), F32),
                   jax.ShapeDtypeStruct((B, N_ML_HEADS, ML_QK_DIM), F32),
                   jax.ShapeDtypeStruct((B, N_ML_HEADS, LANES), F32)),
        compiler_params=pltpu.CompilerParams(dimension_semantics=("parallel", "arbitrary"),
                                             vmem_limit_bytes=VMEM_LIMIT),
        name="mlstm_prompt",
    )(pw["b_i"], pw["b_f"], mq, mk, mv, grow, mo, pw["g_ml_out"])


def _post_kernel(x_ref, att_ref, ml_ref, ga_ref, gm_ref, pe_ref,
                 watt_ref, wml_ref, wout_ref, gffn_ref, wgate_ref, wup_ref, wdown_ref,
                 wple_ref, gple_ref, wpg_ref, y_ref):
    mix = (jax.nn.sigmoid(ga_ref[0]) * _dot(att_ref[0].astype(BF16), watt_ref[...])
           + jax.nn.sigmoid(gm_ref[0]) * _dot(ml_ref[0], wml_ref[...]))
    h = x_ref[0] + _dot(mix.astype(BF16), wout_ref[...])
    hb = _rms(h, gffn_ref[...]).astype(BF16)
    act = jax.nn.silu(_dot(hb, wgate_ref[...])) * _dot(hb, wup_ref[...])
    h = h + _dot(act.astype(BF16), wdown_ref[...])
    gate = jax.nn.sigmoid(_dot(_rms(h, gple_ref[...]).astype(BF16), wpg_ref[...]))
    y_ref[0] = h + gate * _dot(pe_ref[0].astype(BF16), wple_ref[...])


def _post(x, att, ml, ga, gm, pe, pw, tm):
    B, L, _ = x.shape
    tok = lambda w: pl.BlockSpec((1, tm, w), lambda b, i: (b, i, 0))
    full = lambda a: pl.BlockSpec(a.shape, lambda b, i: (0,) * a.ndim, pipeline_mode=pl.Buffered(1))
    consts = (pw["w_att"], pw["w_ml"], pw["w_out"], pw["g_ffn"], pw["w_ffn_gate"], pw["w_ffn_up"],
              pw["w_ffn_down"], pw["w_ple"], pw["g_ple"], pw["w_ple_gate"])
    return pl.pallas_call(
        _post_kernel,
        grid=(B, L // tm),
        in_specs=[tok(D_MODEL), tok(ATT_WIDTH), tok(ML_V_WIDTH), tok(D_MODEL), tok(D_MODEL),
                  tok(pe.shape[-1])] + [full(a) for a in consts],
        out_specs=tok(D_MODEL),
        out_shape=jax.ShapeDtypeStruct((B, L, D_MODEL), F32),
        compiler_params=pltpu.CompilerParams(dimension_semantics=("parallel", "parallel"),
                                             vmem_limit_bytes=VMEM_LIMIT),
        name="merge_ffn",
    )(x, att, ml, ga, gm, pe, *consts)


def _sample_select_kernel(pt_ref, *refs, nblk, blocks_per_step):
    page_refs = refs[:2 * blocks_per_step]
    q_ref, sel_ref, qbd_ref, gate_ref = refs[2 * blocks_per_step:]
    j = pl.program_id(1)
    lane = lax.broadcasted_iota(jnp.int32, (N_ATT_HEADS, LANES), 1)

    @pl.when(j == 0)
    def _():
        head = lax.broadcasted_iota(jnp.int32, (N_ATT_HEADS, ATT_WIDTH), 0)
        col = lax.broadcasted_iota(jnp.int32, (N_ATT_HEADS, ATT_WIDTH), 1)
        in_head = (col >= head * ATT_HEAD_DIM) & (col < (head + 1) * ATT_HEAD_DIM)
        qbd = jnp.where(in_head, q_ref[0], 0.0)
        hi = qbd.astype(BF16).astype(F32)
        qbd_ref[...] = jnp.concatenate([hi, qbd - hi], axis=0).astype(BF16)
        gate_ref[...] = jnp.zeros_like(gate_ref)

    qbd = qbd_ref[...]
    gate = gate_ref[...]
    for i in range(blocks_per_step):
        ka, kb = page_refs[2 * i], page_refs[2 * i + 1]
        s = _dot(qbd, ka[0].astype(BF16)) + _dot(qbd, kb[0].astype(BF16))
        g = jnp.sum(s[0:N_ATT_HEADS] + s[N_ATT_HEADS:], axis=1, keepdims=True) * (1.0 / MOBA_BLOCK)
        gate = jnp.where(lane == j * blocks_per_step + i, g, gate)
    gate_ref[...] = gate

    @pl.when(j == pl.num_programs(1) - 1)
    def _():
        gate = gate_ref[...]
        rank = jnp.zeros((N_ATT_HEADS, LANES), F32)
        for n2 in range(nblk):
            col = gate[:, n2:n2 + 1]
            beats = (col > gate) | ((col == gate) & (lane > n2))
            rank = rank + jnp.where(beats, 1.0, 0.0)
        out = jnp.zeros((N_ATT_HEADS, LANES), jnp.int32)
        lane_f = lane.astype(F32)
        for r in range(MOBA_TOPK):
            pick = (rank == float(r)) & (lane < nblk)
            idx = jnp.sum(jnp.where(pick, lane_f, 0.0), axis=1, keepdims=True)
            out = jnp.where(lane == r, idx.astype(jnp.int32), out)
        sel_ref[0] = out


def _sample_select(page_table, cache_kt, q_s):
    Bs, n_pages = page_table.shape
    nblk = n_pages * PAGE_SIZE // MOBA_BLOCK
    assert MOBA_BLOCK == 2 * PAGE_SIZE and nblk <= LANES
    bps = SELECT_BLOCKS_PER_STEP
    assert nblk % bps == 0
    pages_per_step = 2 * bps

    def page_spec(p):
        return pl.BlockSpec((1, ATT_WIDTH, PAGE_SIZE), lambda b, j, pt: (pt[b, j * pages_per_step + p], 0, 0))

    grid_spec = pltpu.PrefetchScalarGridSpec(
        num_scalar_prefetch=1,
        grid=(Bs, nblk // bps),
        in_specs=[page_spec(p) for p in range(pages_per_step)]
                 + [pl.BlockSpec((1, 1, ATT_WIDTH), lambda b, j, pt: (b, 0, 0))],
        out_specs=pl.BlockSpec((1, N_ATT_HEADS, LANES), lambda b, j, pt: (b, 0, 0)),
        scratch_shapes=[pltpu.VMEM((2 * N_ATT_HEADS, ATT_WIDTH), BF16),
                        pltpu.VMEM((N_ATT_HEADS, LANES), F32)],
    )
    return pl.pallas_call(
        functools.partial(_sample_select_kernel, nblk=nblk, blocks_per_step=bps),
        grid_spec=grid_spec,
        out_shape=jax.ShapeDtypeStruct((Bs, N_ATT_HEADS, LANES), jnp.int32),
        compiler_params=pltpu.CompilerParams(dimension_semantics=("parallel", "arbitrary"),
                                             vmem_limit_bytes=VMEM_LIMIT),
        name="moba_sample_select",
    )(page_table, *([cache_kt] * pages_per_step), q_s)


def _sample_attn_kernel(pt_ref, sel_ref, *refs, n_pages):
    k_refs, v_refs = refs[:n_pages], refs[n_pages:2 * n_pages]
    q_ref, kn_ref, vn_ref, o_ref, out_ref = refs[2 * n_pages:]
    h = pl.program_id(1)
    lane = lax.broadcasted_iota(jnp.int32, (1, LANES), 1)
    lo = (h % HEAD_PAIR) * ATT_HEAD_DIM
    in_head = (lane >= lo) & (lane < lo + ATT_HEAD_DIM)
    qh = jnp.where(in_head, q_ref[0], 0.0) * (ATT_HEAD_DIM ** -0.5)
    rows = 16
    qb = jnp.broadcast_to(qh, (rows, LANES)).astype(BF16)

    s_own = jnp.sum(qh * kn_ref[0], axis=1, keepdims=True)
    scores = [_dot(qb, k_refs[j][0].astype(BF16))[0:1] for j in range(n_pages)]
    m = s_own
    for s in scores:
        m = jnp.maximum(m, jnp.max(s, axis=1, keepdims=True))
    p_own = jnp.exp(s_own - m)
    l = p_own
    acc = p_own * vn_ref[0]
    for j, s in enumerate(scores):
        p = jnp.exp(s - m)
        l = l + jnp.sum(p, axis=1, keepdims=True)
        acc = acc + _dot_nt(jnp.broadcast_to(p, (rows, PAGE_SIZE)).astype(BF16), v_refs[j][0].astype(BF16))[0:1]
    res = acc / l

    @pl.when(h % HEAD_PAIR == 0)
    def _():
        out_ref[...] = res

    @pl.when(h % HEAD_PAIR == HEAD_PAIR - 1)
    def _():
        o_ref[0] = jnp.where(in_head, res, out_ref[...])


def _sample_attention(page_table, sel, cache_kt, cache_vt, q_s, k_s, v_s):
    Bs = page_table.shape[0]
    ppb = MOBA_BLOCK // PAGE_SIZE
    n_pages = MOBA_TOPK * ppb

    def tile_spec(j):
        def index(b, h, pt, sl):
            return (pt[b, ppb * sl[b, h * MOBA_TOPK + j // ppb] + j % ppb], h // HEAD_PAIR, 0)
        return pl.BlockSpec((1, LANES, PAGE_SIZE), index)

    tiles = [tile_spec(j) for j in range(n_pages)]
    row = pl.BlockSpec((1, 1, LANES), lambda b, h, pt, sl: (b, 0, h // HEAD_PAIR))
    grid_spec = pltpu.PrefetchScalarGridSpec(
        num_scalar_prefetch=2,
        grid=(Bs, N_ATT_HEADS),
        in_specs=tiles + tiles + [row, row, row],
        out_specs=row,
        scratch_shapes=[pltpu.VMEM((1, LANES), F32)],
    )
    return pl.pallas_call(
        functools.partial(_sample_attn_kernel, n_pages=n_pages),
        grid_spec=grid_spec,
        out_shape=jax.ShapeDtypeStruct((Bs, 1, ATT_WIDTH), F32),
        compiler_params=pltpu.CompilerParams(dimension_semantics=("parallel", "arbitrary"),
                                             vmem_limit_bytes=VMEM_LIMIT),
        name="moba_sample_attn",
    )(page_table, sel, *([cache_kt] * n_pages), *([cache_vt] * n_pages), q_s, k_s, v_s)


def _mlstm_step_kernel(bi_ref, bf_ref, m0_ref, mq_ref, mk_ref, mv_ref, g_ref, mo_ref, gout_ref,
                       c0_ref, n0_ref, ml_ref, c_out_ref, n_out_ref, m_out_ref):
    b = pl.program_id(0)
    dv = ML_V_DIM
    r_i = lax.broadcasted_iota(jnp.int32, (dv, dv), 0)
    c_i = lax.broadcasted_iota(jnp.int32, (dv, dv), 1)
    eye = r_i == c_i
    g = g_ref[0]
    for h in range(N_ML_HEADS):
        i_g = g[:, h:h + 1] + bi_ref[h]
        logf = _log_sigmoid(g[:, N_ML_HEADS + h:N_ML_HEADS + h + 1] + bf_ref[h])
        m_old = m0_ref[b, h]
        inter = logf + m_old
        m_t = jnp.maximum(inter, i_g)
        w_inter = jnp.exp(inter - m_t)
        w_in = jnp.exp(i_g - m_t)
        q = mq_ref[0, :, h * ML_QK_DIM:(h + 1) * ML_QK_DIM].astype(F32)
        k = mk_ref[0, :, h * ML_QK_DIM:(h + 1) * ML_QK_DIM].astype(F32)
        v = mv_ref[0, :, h * dv:(h + 1) * dv].astype(F32)
        c_old = c0_ref[0, 0, h]
        n_old = n0_ref[0, 0, h:h + 1, :]
        s = jnp.sum(q * k, axis=1, keepdims=True) * w_in
        v_col = jnp.sum(jnp.where(eye, v, 0.0), axis=1, keepdims=True)
        num = w_inter * jnp.sum(c_old * q, axis=1, keepdims=True) + s * v_col
        den = w_inter * jnp.sum(n_old * q, axis=1, keepdims=True) + s
        hid_col = num / jnp.maximum(jnp.abs(den), jnp.exp(-m_t))
        hid = jnp.sum(jnp.where(eye, hid_col, 0.0), axis=0, keepdims=True)
        c_out_ref[0, 0, h] = w_inter * c_old + (w_in * v_col) * k
        n_out_ref[0, 0, h:h + 1, :] = w_inter * n_old + w_in * k
        m_out_ref[0, h:h + 1, :] = jnp.broadcast_to(m_t, (1, LANES))
        hn = _rms(hid, gout_ref[h:h + 1, :])
        mo = mo_ref[0, :, h * dv:(h + 1) * dv]
        ml_ref[0, :, h * dv:(h + 1) * dv] = (hn * jax.nn.sigmoid(mo)).astype(BF16)


def _mlstm_sample(mq, mk, mv, gcol, mo, state_c, state_n, state_m, pw):
    Bs = mq.shape[0]
    row = lambda w: pl.BlockSpec((1, 1, w), lambda b, *_: (b, 0, 0))
    grid_spec = pltpu.PrefetchScalarGridSpec(
        num_scalar_prefetch=3,
        grid=(Bs,),
        in_specs=[row(ML_QK_WIDTH), row(ML_QK_WIDTH), row(ML_V_WIDTH), row(LANES), row(ML_V_WIDTH),
                  pl.BlockSpec((N_ML_HEADS, ML_V_DIM), lambda b, *_: (0, 0)),
                  pl.BlockSpec((1, 1, N_ML_HEADS, ML_V_DIM, ML_QK_DIM), lambda b, *_: (0, b, 0, 0, 0)),
                  pl.BlockSpec((1, 1, N_ML_HEADS, ML_QK_DIM), lambda b, *_: (0, b, 0, 0))],
        out_specs=(row(ML_V_WIDTH),
                   pl.BlockSpec((1, 1, N_ML_HEADS, ML_V_DIM, ML_QK_DIM), lambda b, *_: (0, b, 0, 0, 0)),
                   pl.BlockSpec((1, 1, N_ML_HEADS, ML_QK_DIM), lambda b, *_: (0, b, 0, 0)),
                   pl.BlockSpec((1, N_ML_HEADS, LANES), lambda b, *_: (b, 0, 0))),
    )
    return pl.pallas_call(
        _mlstm_step_kernel,
        grid_spec=grid_spec,
        out_shape=(jax.ShapeDtypeStruct((Bs, 1, ML_V_WIDTH), BF16),
                   jax.ShapeDtypeStruct(state_c.shape, F32),
                   jax.ShapeDtypeStruct(state_n.shape, F32),
                   jax.ShapeDtypeStruct((Bs, N_ML_HEADS, LANES), F32)),
        compiler_params=pltpu.CompilerParams(dimension_semantics=("arbitrary",),
                                             vmem_limit_bytes=VMEM_LIMIT),
        name="mlstm_sample",
    )(pw["b_i"], pw["b_f"], state_m[0], mq, mk, mv, gcol, mo, pw["g_ml_out"], state_c, state_n)


def _rope_angles(pos):
    freqs = ROPE_THETA ** (-jnp.arange(0, ROPE_DIMS, 2, dtype=F32) / ROPE_DIMS)
    return pos.astype(F32)[:, None] * freqs[None, :]


def _rope_tables_cols(pos):
    ang = _rope_angles(pos).T
    return jnp.cos(ang), jnp.sin(ang)


def _rope_tables_rows(pos):
    half = ROPE_DIMS // 2
    n = pos.shape[0]
    ang = _rope_angles(pos)
    cos, sin = jnp.cos(ang), jnp.sin(ang)
    rest = ATT_HEAD_DIM - ROPE_DIMS
    zh = jnp.zeros((n, half), F32)
    cos_h = jnp.concatenate([cos, cos, jnp.ones((n, rest), F32)], axis=-1)
    sa_h = jnp.concatenate([-sin, zh, jnp.zeros((n, rest), F32)], axis=-1)
    sb_h = jnp.concatenate([zh, sin, jnp.zeros((n, rest), F32)], axis=-1)
    tile = lambda t: jnp.tile(t, (1, HEAD_PAIR))
    return tile(cos_h), tile(sa_h), tile(sb_h)


def _prepare_weights(w_in, b_igate, b_fgate, g_mix_norm, g_q_norm, g_k_norm, g_ml_out_norm,
                     w_att_branch, w_ml_branch, w_out, g_ffn_norm, w_ffn_gate, w_ffn_up, w_ffn_down,
                     w_ple, g_ple_norm, w_ple_gate):
    w_gates = w_in[:, _C_GATES:_C_MO]
    head_of = jnp.arange(ATT_WIDTH, dtype=jnp.int32) // ATT_HEAD_DIM
    w_qkv = w_in[:, _C_Q:_C_MQ].astype(BF16)
    return {
        "w_qkv": w_qkv,
        "w_qkv_t": w_qkv.T,
        "w_m": w_in[:, _C_MQ:_C_GATES].astype(BF16),
        "g_q_col": g_q_norm.reshape(1, ATT_HEAD_DIM, 1),
        "g_k_col": g_k_norm.reshape(1, ATT_HEAD_DIM, 1),
        "w_g": jnp.pad(w_gates, ((0, 0), (0, LANES - 2 * N_ML_HEADS))).astype(BF16),
        "w_gt": jnp.pad(w_gates.T, ((0, 16 - 2 * N_ML_HEADS), (0, 0))).astype(BF16),
        "w_b": w_in[:, _C_MO:_C_END].astype(BF16),
        "e_head": (head_of[:, None] == head_of[None, :]).astype(BF16),
        "g_mix": g_mix_norm.reshape(1, D_MODEL),
        "g_q": jnp.tile(g_q_norm, N_ATT_HEADS).reshape(1, ATT_WIDTH),
        "g_k": jnp.tile(g_k_norm, N_ATT_HEADS).reshape(1, ATT_WIDTH),
        "g_ml_out": g_ml_out_norm,
        "b_i": b_igate, "b_f": b_fgate,
        "w_att": w_att_branch.astype(BF16), "w_ml": w_ml_branch.astype(BF16), "w_out": w_out.astype(BF16),
        "g_ffn": g_ffn_norm.reshape(1, D_MODEL),
        "w_ffn_gate": w_ffn_gate.astype(BF16), "w_ffn_up": w_ffn_up.astype(BF16),
        "w_ffn_down": w_ffn_down.astype(BF16),
        "w_ple": w_ple.astype(BF16), "g_ple": g_ple_norm.reshape(1, D_MODEL),
        "w_ple_gate": w_ple_gate.astype(BF16),
    }


def _prompt_layer(x, pe, pw):
    B, L, _ = x.shape
    pos = jnp.arange(L, dtype=jnp.int32)
    qt, kt, vt, mq, mk, mv, gcol, grow, mo, ga, gm = _project(x, pos, pw, tm=256, feature_major=True)
    att = _attention_prompt(qt, kt, vt)
    ml, c_fin, n_fin, m_pad = _mlstm_prompt(mq, mk, mv, grow, mo, pw)
    y = _post(x, att, ml, ga, gm, pe, pw, tm=256)
    tokens_major = lambda t: jnp.transpose(t.reshape(B, N_ATT_HEADS, ATT_HEAD_DIM, L), (0, 3, 1, 2))
    return y, tokens_major(kt), tokens_major(vt), c_fin, n_fin, m_pad[:, :, 0]


def _sample_layer(x, pe, cache_k, cache_v, page_table, state_c, state_n, state_m, pw):
    Bs, Ls, _ = x.shape
    assert Ls == 1
    past_len = page_table.shape[1] * PAGE_SIZE
    n_pool = cache_k.shape[0]
    pos = jnp.full((Bs,), past_len, dtype=jnp.int32)
    xt = x.reshape(1, Bs, D_MODEL)
    q, k, v, mq, mk, mv, gcol, _, mo, ga, gm = _project(xt, pos, pw, tm=Bs, feature_major=False)
    per_seq = lambda t: t.reshape(Bs, 1, t.shape[-1])
    pages_t = lambda c: jnp.transpose(c, (0, 2, 3, 1)).reshape(n_pool, ATT_WIDTH, PAGE_SIZE)
    cache_kt, cache_vt = pages_t(cache_k), pages_t(cache_v)
    sel = _sample_select(page_table, cache_kt, per_seq(q))
    sel2 = sel[:, :, :MOBA_TOPK].reshape(Bs, N_ATT_HEADS * MOBA_TOPK)
    att = _sample_attention(page_table, sel2, cache_kt, cache_vt, per_seq(q), per_seq(k), per_seq(v))
    ml, c_new, n_new, m_pad = _mlstm_sample(per_seq(mq), per_seq(mk), per_seq(mv), per_seq(gcol),
                                            per_seq(mo), state_c, state_n, state_m, pw)
    y = _post(xt, att.reshape(1, Bs, ATT_WIDTH), ml.reshape(1, Bs, ML_V_WIDTH), ga, gm,
              pe.reshape(1, Bs, pe.shape[-1]), pw, tm=Bs)
    return y.reshape(Bs, 1, D_MODEL), k, v, c_new, n_new, m_pad[:, :, 0]


def kernel(x_prompt, x_sample, cache_k, cache_v, state_mlstm_C, state_mlstm_n, state_mlstm_m, page_table,
           p_prompt, p_sample, w_in, b_igate, b_fgate, g_mix_norm, g_q_norm, g_k_norm, g_ml_out_norm,
           w_att_branch, w_ml_branch, w_out, g_ffn_norm, w_ffn_gate, w_ffn_up, w_ffn_down,
           w_ple, g_ple_norm, w_ple_gate):
    depth = w_in.shape[0]
    assert depth == 1
    Bp, Lp, _ = x_prompt.shape
    Bs, Ls, _ = x_sample.shape
    pw = _prepare_weights(w_in[0], b_igate[0], b_fgate[0], g_mix_norm[0], g_q_norm[0], g_k_norm[0],
                          g_ml_out_norm[0], w_att_branch[0], w_ml_branch[0], w_out[0], g_ffn_norm[0],
                          w_ffn_gate[0], w_ffn_up[0], w_ffn_down[0], w_ple[0], g_ple_norm[0], w_ple_gate[0])
    yp, kp, vp, cp, n_p, mp = _prompt_layer(x_prompt, p_prompt[0], pw)
    ys, ks, vs, cs, ns, ms = _sample_layer(x_sample, p_sample[0], cache_k[0], cache_v[0], page_table,
                                           state_mlstm_C, state_mlstm_n, state_mlstm_m, pw)
    heads = lambda t: t.reshape(1, Bs, Ls, N_ATT_HEADS, ATT_HEAD_DIM)
    return (yp, ys, kp[None], vp[None], heads(ks), heads(vs),
            cp[None], n_p[None], mp[None], cs, ns, ms[None])
```

```python
import functools

import jax
import jax.numpy as jnp
from jax import lax
from jax.experimental import pallas as pl
from jax.experimental.pallas import tpu as pltpu

F32 = jnp.float32
BF16 = jnp.bfloat16

D_MODEL = 1024
N_ATT_HEADS = 8
ATT_HEAD_DIM = 64
ATT_WIDTH = N_ATT_HEADS * ATT_HEAD_DIM
ROPE_DIMS = ATT_HEAD_DIM // 4
ROPE_THETA = 500000.0
MOBA_BLOCK = 256
MOBA_TOPK = 3
PAGE_SIZE = 128
N_ML_HEADS = 4
ML_QK_DIM = 128
ML_V_DIM = 256
ML_QK_WIDTH = N_ML_HEADS * ML_QK_DIM
ML_V_WIDTH = N_ML_HEADS * ML_V_DIM
NORM_EPS = 1e-6
NEG_INF = -1e30
LOG2_E = 1.4426950408889634

LANES = 128
HEAD_PAIR = LANES // ATT_HEAD_DIM
ML_CHUNK = 256
SELECT_BLOCKS_PER_STEP = 16
VMEM_LIMIT = 56 * 1024 * 1024

_C_Q, _C_K, _C_V = 0, ATT_WIDTH, 2 * ATT_WIDTH
_C_MQ = 3 * ATT_WIDTH
_C_MK = _C_MQ + ML_QK_WIDTH
_C_MV = _C_MK + ML_QK_WIDTH
_C_GATES = _C_MV + ML_V_WIDTH
_C_MO = _C_GATES + 2 * N_ML_HEADS
_C_GA = _C_MO + ML_V_WIDTH
_C_GM = _C_GA + D_MODEL
_C_END = _C_GM + D_MODEL


def _dot(a, b):
    return jnp.dot(a, b, preferred_element_type=F32)


def _dot_nt(a, b, precision=None):
    return lax.dot_general(a, b, (((1,), (1,)), ((), ())), precision=precision,
                           preferred_element_type=F32)


def _rms(x, g):
    return x * lax.rsqrt(jnp.mean(x * x, axis=-1, keepdims=True) + NORM_EPS) * g


def _log_sigmoid(x):
    return jnp.minimum(x, 0.0) - jnp.log1p(jnp.exp(-jnp.abs(x)))


def _proj_shared(xb, wm_ref, wg_ref, wgt_ref, wb_ref,
                 mq_ref, mk_ref, mv_ref, gcol_ref, grow_ref, mo_ref, ga_ref, gm_ref):
    mq_ref[0] = _dot(xb, wm_ref[:, 0:ML_QK_WIDTH]).astype(BF16)
    mk_ref[0] = (_dot(xb, wm_ref[:, ML_QK_WIDTH:2 * ML_QK_WIDTH]) * (ML_QK_DIM ** -0.5)).astype(BF16)
    mv_ref[0] = _dot(xb, wm_ref[:, 2 * ML_QK_WIDTH:2 * ML_QK_WIDTH + ML_V_WIDTH]).astype(BF16)
    gcol_ref[0] = _dot(xb, wg_ref[...])
    grow_ref[0] = _dot_nt(wgt_ref[...], xb)
    mo_ref[0] = _dot(xb, wb_ref[:, 0:ML_V_WIDTH])
    ga_ref[0] = _dot(xb, wb_ref[:, ML_V_WIDTH:ML_V_WIDTH + D_MODEL])
    gm_ref[0] = _dot(xb, wb_ref[:, ML_V_WIDTH + D_MODEL:ML_V_WIDTH + 2 * D_MODEL])


def _proj_kernel_rows(x_ref, cos_ref, sa_ref, sb_ref, gmix_ref, gq_ref, gk_ref, e_ref, wqkv_ref,
                      wm_ref, wg_ref, wgt_ref, wb_ref, q_ref, k_ref, v_ref, *rest_refs):
    xb = _rms(x_ref[0], gmix_ref[...]).astype(BF16)
    cos, sa, sb = cos_ref[...], sa_ref[...], sb_ref[...]
    e = e_ref[...]

    def head_norm_rope(z, g):
        zz = z * z
        hi = zz.astype(BF16)
        lo = (zz - hi.astype(F32)).astype(BF16)
        ss = _dot(hi, e) + _dot(lo, e)
        zn = z * lax.rsqrt(ss * (1.0 / ATT_HEAD_DIM) + NORM_EPS) * g
        outs = []
        for grp in range(ATT_WIDTH // LANES):
            zg = zn[:, grp * LANES:(grp + 1) * LANES]
            outs.append(zg * cos + pltpu.roll(zg, LANES - ROPE_DIMS // 2, 1) * sa
                        + pltpu.roll(zg, ROPE_DIMS // 2, 1) * sb)
        return jnp.concatenate(outs, axis=-1)

    q_ref[0] = head_norm_rope(_dot(xb, wqkv_ref[:, 0:ATT_WIDTH]), gq_ref[...])
    k_ref[0] = head_norm_rope(_dot(xb, wqkv_ref[:, ATT_WIDTH:2 * ATT_WIDTH]), gk_ref[...])
    v_ref[0] = _dot(xb, wqkv_ref[:, 2 * ATT_WIDTH:3 * ATT_WIDTH])
    _proj_shared(xb, wm_ref, wg_ref, wgt_ref, wb_ref, *rest_refs)


def _proj_kernel_cols(x_ref, cos_ref, sin_ref, gmix_ref, gq_ref, gk_ref, wqkvt_ref,
                      wm_ref, wg_ref, wgt_ref, wb_ref, qt_ref, kt_ref, vt_ref, *rest_refs):
    xb = _rms(x_ref[0], gmix_ref[...]).astype(BF16)
    tm = xb.shape[0]
    zt = _dot_nt(wqkvt_ref[...], xb)
    cos, sin = cos_ref[...][None], sin_ref[...][None]
    half = ROPE_DIMS // 2

    def head_norm_rope(z, g):
        z3 = z.reshape(N_ATT_HEADS, ATT_HEAD_DIM, tm)
        ss = jnp.sum(z3 * z3, axis=1, keepdims=True)
        zn = z3 * lax.rsqrt(ss * (1.0 / ATT_HEAD_DIM) + NORM_EPS) * g
        x1, x2 = zn[:, 0:half, :], zn[:, half:ROPE_DIMS, :]
        out = jnp.concatenate([x1 * cos - x2 * sin, x2 * cos + x1 * sin, zn[:, ROPE_DIMS:, :]], axis=1)
        return out.reshape(ATT_WIDTH, tm)

    qt_ref[0] = head_norm_rope(zt[0:ATT_WIDTH], gq_ref[...])
    kt_ref[0] = head_norm_rope(zt[ATT_WIDTH:2 * ATT_WIDTH], gk_ref[...])
    vt_ref[0] = zt[2 * ATT_WIDTH:3 * ATT_WIDTH]
    _proj_shared(xb, wm_ref, wg_ref, wgt_ref, wb_ref, *rest_refs)


def _project(x, pos, pw, tm, feature_major):
    B, L, _ = x.shape
    tok = lambda w: pl.BlockSpec((1, tm, w), lambda b, i: (b, i, 0))
    full = lambda a: pl.BlockSpec(a.shape, lambda b, i: (0,) * a.ndim)
    shared = (pw["w_m"], pw["w_g"], pw["w_gt"], pw["w_b"])
    rest_shape = (
        jax.ShapeDtypeStruct((B, L, ML_QK_WIDTH), BF16),
        jax.ShapeDtypeStruct((B, L, ML_QK_WIDTH), BF16),
        jax.ShapeDtypeStruct((B, L, ML_V_WIDTH), BF16),
        jax.ShapeDtypeStruct((B, L, LANES), F32),
        jax.ShapeDtypeStruct((B, 16, L), F32),
        jax.ShapeDtypeStruct((B, L, ML_V_WIDTH), F32),
        jax.ShapeDtypeStruct((B, L, D_MODEL), F32),
        jax.ShapeDtypeStruct((B, L, D_MODEL), F32),
    )
    rest_specs = (tok(ML_QK_WIDTH), tok(ML_QK_WIDTH), tok(ML_V_WIDTH), tok(LANES),
                  pl.BlockSpec((1, 16, tm), lambda b, i: (b, 0, i)),
                  tok(ML_V_WIDTH), tok(D_MODEL), tok(D_MODEL))
    if feature_major:
        cos, sin = _rope_tables_cols(pos)
        tab = pl.BlockSpec((ROPE_DIMS // 2, tm), lambda b, i: (0, i))
        consts = (pw["g_mix"], pw["g_q_col"], pw["g_k_col"], pw["w_qkv_t"]) + shared
        body, tables, tab_specs = _proj_kernel_cols, (cos, sin), [tab, tab]
        qkv_shape = (jax.ShapeDtypeStruct((B, ATT_WIDTH, L), F32),) * 3
        qkv_specs = (pl.BlockSpec((1, ATT_WIDTH, tm), lambda b, i: (b, 0, i)),) * 3
    else:
        tables = _rope_tables_rows(pos)
        tab = pl.BlockSpec((tm, LANES), lambda b, i: (i, 0))
        consts = (pw["g_mix"], pw["g_q"], pw["g_k"], pw["e_head"], pw["w_qkv"]) + shared
        body, tab_specs = _proj_kernel_rows, [tab, tab, tab]
        qkv_shape = (jax.ShapeDtypeStruct((B, L, ATT_WIDTH), F32),) * 3
        qkv_specs = (tok(ATT_WIDTH),) * 3
    return pl.pallas_call(
        body,
        grid=(B, L // tm),
        in_specs=[tok(D_MODEL)] + tab_specs + [full(a) for a in consts],
        out_specs=qkv_specs + rest_specs,
        out_shape=qkv_shape + rest_shape,
        compiler_params=pltpu.CompilerParams(dimension_semantics=("parallel", "parallel"),
                                             vmem_limit_bytes=VMEM_LIMIT),
        name="in_proj_cols" if feature_major else "in_proj_rows",
    )(x, *tables, *consts)


def _attn_kernel(q_ref, k_ref, v_ref, o_ref, kb_ref, vt_ref, kmean_ref, acc_ref, s0_ref, s1_ref, *, nblk):
    blk = MOBA_BLOCK
    tq = 2 * blk
    ones_rows = vt_ref.shape[1] - LANES
    t = pl.program_id(2)
    c0 = 2 * t

    nb = kmean_ref.shape[0]

    @pl.when(t == 0)
    def _():
        lane = lax.broadcasted_iota(jnp.int32, (blk, LANES), 1)
        kmean_ref[...] = jnp.zeros_like(kmean_ref)
        for n in range(nblk):
            rows = slice(n * blk, (n + 1) * blk)
            kf = k_ref[0, :, rows].T
            kb_ref[rows, 0:LANES] = kf.astype(BF16)
            kb_ref[rows, LANES:2 * LANES] = jnp.where(lane == n, 1.0, 0.0).astype(BF16)
            kmean_ref[n:n + 1, :] = jnp.sum(kf, axis=0, keepdims=True) * (1.0 / blk)
            vt_ref[n] = jnp.concatenate([v_ref[0, :, rows], jnp.ones((ones_rows, blk), F32)],
                                        axis=0).astype(BF16)

    q2 = q_ref[0]
    feat = lax.broadcasted_iota(jnp.int32, (LANES, tq), 0)
    blk_id = lax.broadcasted_iota(jnp.int32, (nb, tq), 0)
    col = lax.broadcasted_iota(jnp.int32, (nb, tq), 1)
    cur = c0 + (col >= blk).astype(jnp.int32)
    kmean = kmean_ref[...]
    bias_pad = jnp.zeros((LANES - nb, tq), F32)

    qas = []
    for h in range(HEAD_PAIR):
        qh = jnp.where((feat >= h * ATT_HEAD_DIM) & (feat < (h + 1) * ATT_HEAD_DIM), q2, 0.0)
        gate = jnp.dot(kmean, qh, precision=lax.Precision.HIGHEST,
                       preferred_element_type=F32)
        rank = jnp.zeros((nb, tq), F32)
        for n2 in range(nblk):
            row = gate[n2:n2 + 1, :]
            beats = ((row > gate) | ((row == gate) & (blk_id > n2))) & (cur > n2)
            rank = rank + jnp.where(beats, 1.0, 0.0)
        open_blk = ((blk_id < cur) & (rank < MOBA_TOPK)) | (blk_id == cur)
        bias = jnp.where(open_blk, 0.0, NEG_INF)
        qas.append(jnp.concatenate([qh * (ATT_HEAD_DIM ** -0.5 * LOG2_E), bias, bias_pad],
                                   axis=0).astype(BF16))

    def block_rows(n):
        return kb_ref[pl.ds(pl.multiple_of(n * blk, blk), blk), :]

    key_i = lax.broadcasted_iota(jnp.int32, (blk, tq), 0)
    qry_j = lax.broadcasted_iota(jnp.int32, (blk, tq), 1)
    s_refs = (s0_ref, s1_ref)

    def stage_a(first, slot, own):
        ks = [block_rows(first), block_rows(first + 1)]
        maxima = []
        for h in range(HEAD_PAIR):
            sts = [_dot(ks[0], qas[h]), _dot(ks[1], qas[h])]
            if own:
                sts = [jnp.where(key_i <= qry_j, sts[0], NEG_INF),
                       jnp.where(key_i <= qry_j - blk, sts[1], NEG_INF)]
            s_refs[slot][h, 0] = sts[0]
            s_refs[slot][h, 1] = sts[1]
            maxima.append(jnp.maximum(jnp.max(sts[0], axis=0, keepdims=True),
                                      jnp.max(sts[1], axis=0, keepdims=True)))
        return maxima

    def stage_b(first, slot, m_run, maxima):
        vts = [vt_ref[first], vt_ref[first + 1]]
        m_out = []
        for h in range(HEAD_PAIR):
            m_new = jnp.maximum(m_run[h], maxima[h])
            pv = (_dot(vts[0], jnp.exp2(s_refs[slot][h, 0] - m_new).astype(BF16))
                  + _dot(vts[1], jnp.exp2(s_refs[slot][h, 1] - m_new).astype(BF16)))
            acc_ref[h] = jnp.exp2(m_run[h] - m_new) * acc_ref[h] + pv
            m_out.append(m_new)
        return m_out

    def pending_first(j):
        return jnp.where(j == 0, c0, 2 * j - 2)

    acc_ref[...] = jnp.zeros_like(acc_ref)
    m_start = [jnp.full((1, tq), NEG_INF, F32)] * HEAD_PAIR
    carry = tuple(m_start) + tuple(stage_a(c0, 0, own=True))

    def trip(j, slot, carry):
        m_run, maxima = carry[:HEAD_PAIR], carry[HEAD_PAIR:]
        new_maxima = stage_a(2 * j, 1 - slot, own=False)
        m_new = stage_b(pending_first(j), slot, m_run, maxima)
        return tuple(m_new) + tuple(new_maxima)

    carry = lax.fori_loop(0, lax.shift_right_logical(t, 1),
                          lambda i, cr: trip(2 * i + 1, 1, trip(2 * i, 0, cr)), carry)

    def odd_tail(carry):
        carry = trip(t - 1, 0, carry)
        stage_b(pending_first(t), 1, carry[:HEAD_PAIR], carry[HEAD_PAIR:])
        return 0

    def even_tail(carry):
        stage_b(pending_first(t), 0, carry[:HEAD_PAIR], carry[HEAD_PAIR:])
        return 0

    lax.cond((t & 1) == 1, odd_tail, even_tail, carry)
    outs = [acc_ref[h, 0:LANES, :] / acc_ref[h, LANES:LANES + 1, :] for h in range(HEAD_PAIR)]
    ot = jnp.where(feat < ATT_HEAD_DIM, outs[0], outs[1])
    o_ref[0] = ot.T.astype(o_ref.dtype)


def _attention_prompt(qt, kt, vt):
    B, _, L = qt.shape
    blk = MOBA_BLOCK
    tq = 2 * blk
    assert L % tq == 0
    nblk = L // blk
    assert nblk <= LANES
    nb = -(-nblk // 8) * 8
    ones_rows = 16
    groups = ATT_WIDTH // LANES
    return pl.pallas_call(
        functools.partial(_attn_kernel, nblk=nblk),
        grid=(B, groups, L // tq),
        in_specs=[pl.BlockSpec((1, LANES, tq), lambda b, g, i: (b, g, i)),
                  pl.BlockSpec((1, LANES, L), lambda b, g, i: (b, g, 0)),
                  pl.BlockSpec((1, LANES, L), lambda b, g, i: (b, g, 0))],
        out_specs=pl.BlockSpec((1, tq, LANES), lambda b, g, i: (b, i, g)),
        out_shape=jax.ShapeDtypeStruct((B, L, ATT_WIDTH), BF16),
        scratch_shapes=[pltpu.VMEM((L, 2 * LANES), BF16),
                        pltpu.VMEM((nblk, LANES + ones_rows, blk), BF16),
                        pltpu.VMEM((nb, LANES), F32),
                        pltpu.VMEM((HEAD_PAIR, LANES + ones_rows, tq), F32),
                        pltpu.VMEM((HEAD_PAIR, 2, blk, tq), F32),
                        pltpu.VMEM((HEAD_PAIR, 2, blk, tq), F32)],
        compiler_params=pltpu.CompilerParams(dimension_semantics=("parallel", "parallel", "arbitrary"),
                                             vmem_limit_bytes=VMEM_LIMIT),
        name="moba_prompt",
    )(qt, kt, vt)


def _mlstm_kernel(bi_ref, bf_ref, mq_ref, mk_ref, mv_ref, grow_ref, mo_ref, gout_ref,
                  ml_ref, c_out_ref, n_out_ref, m_out_ref, cn_ref, m_ref, *, chunk):
    j = pl.program_id(1)
    dk, dv = ML_QK_DIM, ML_V_DIM

    @pl.when(j == 0)
    def _():
        cn_ref[...] = jnp.zeros_like(cn_ref)
        m_ref[...] = jnp.zeros_like(m_ref)

    t_i = lax.broadcasted_iota(jnp.int32, (chunk, chunk), 0)
    s_i = lax.broadcasted_iota(jnp.int32, (chunk, chunk), 1)
    tril = s_i <= t_i
    tril_b = jnp.where(tril, 1.0, 0.0).astype(BF16)
    triu_b = jnp.where(t_i <= s_i, 1.0, 0.0).astype(BF16)
    ones_cols = jnp.ones((chunk, LANES), BF16)
    pad_rows = jnp.zeros((13, chunk), F32)
    grow = grow_ref[0]
    for h in range(N_ML_HEADS):
        i_row = grow[h:h + 1, :] + bi_ref[h]
        f_row = _log_sigmoid(grow[N_ML_HEADS + h:N_ML_HEADS + h + 1, :] + bf_ref[h])
        hi = f_row.astype(BF16).astype(F32)
        mid = (f_row - hi).astype(BF16).astype(F32)
        pieces = jnp.concatenate([hi, mid, f_row - hi - mid, pad_rows], axis=0).astype(BF16)
        rows = _dot(pieces, triu_b)
        cols = _dot_nt(tril_b, pieces)
        b_row = rows[0:1] + rows[1:2] + rows[2:3]
        b_col = cols[:, 0:1] + cols[:, 1:2] + cols[:, 2:3]
        a_row = (i_row - b_row) * LOG2_E
        m_old = m_ref[h][:, 0:1]
        a_low = jnp.where(tril, a_row, NEG_INF)
        mm_col = jnp.maximum(jnp.max(a_low, axis=1, keepdims=True), m_old * LOG2_E)
        w_inter = jnp.exp2(m_old * LOG2_E - mm_col)
        floor_col = jnp.exp2(-(b_col * LOG2_E + mm_col))
        decay = jnp.exp2(a_low - mm_col)

        q = mq_ref[0, :, h * dk:(h + 1) * dk]
        k = mk_ref[0, :, h * dk:(h + 1) * dk]
        v_ones = jnp.concatenate([mv_ref[0, :, h * dv:(h + 1) * dv], ones_cols], axis=1)
        cn = cn_ref[h]
        s = _dot_nt(q, k) * decay
        inter = _dot(q, cn.astype(BF16))
        intra = _dot(s.astype(BF16), v_ones)
        num = w_inter * inter[:, 0:dv] + intra[:, 0:dv]
        den = w_inter * inter[:, dv:dv + 1] + intra[:, dv:dv + 1]
        hid = num / jnp.maximum(jnp.abs(den), floor_col)

        mm_end = mm_col[chunk - 1:chunk, :]
        w_row = jnp.exp2(a_row - mm_end)
        kw = (k.astype(F32).T * w_row).astype(BF16)
        cn_ref[h] = jnp.exp2(m_old * LOG2_E - mm_end) * cn + _dot(kw, v_ones)
        m_ref[h] = jnp.broadcast_to(b_row[:, chunk - 1:chunk] + mm_end * (1.0 / LOG2_E), (1, LANES))

        hn = _rms(hid, gout_ref[h:h + 1, :])
        mo = mo_ref[0, :, h * dv:(h + 1) * dv]
        ml_ref[0, :, h * dv:(h + 1) * dv] = (hn * jax.nn.sigmoid(mo)).astype(BF16)

    @pl.when(j == pl.num_programs(1) - 1)
    def _():
        for h in range(N_ML_HEADS):
            c_out_ref[0, h] = cn_ref[h, :, 0:dv].T
            n_out_ref[0, h:h + 1, :] = cn_ref[h, :, dv:dv + LANES].T[0:1, :]
            m_out_ref[0, h:h + 1, :] = m_ref[h]


def _mlstm_prompt(mq, mk, mv, grow, mo, pw):
    B, L, _ = mq.shape
    chunk = ML_CHUNK
    tok = lambda w: pl.BlockSpec((1, chunk, w), lambda b, j, *_: (b, j, 0))
    per_b = lambda shp: pl.BlockSpec((1,) + shp, lambda b, j, *_: (b,) + (0,) * len(shp))
    grid_spec = pltpu.PrefetchScalarGridSpec(
        num_scalar_prefetch=2,
        grid=(B, L // chunk),
        in_specs=[tok(ML_QK_WIDTH), tok(ML_QK_WIDTH), tok(ML_V_WIDTH),
                  pl.BlockSpec((1, 16, chunk), lambda b, j, *_: (b, 0, j)), tok(ML_V_WIDTH),
                  pl.BlockSpec((N_ML_HEADS, ML_V_DIM), lambda b, j, *_: (0, 0))],
        out_specs=(tok(ML_V_WIDTH), per_b((N_ML_HEADS, ML_V_DIM, ML_QK_DIM)),
                   per_b((N_ML_HEADS, ML_QK_DIM)), per_b((N_ML_HEADS, LANES))),
        scratch_shapes=[pltpu.VMEM((N_ML_HEADS, ML_QK_DIM, ML_V_DIM + LANES), F32),
                        pltpu.VMEM((N_ML_HEADS, 1, LANES), F32)],
    )
    return pl.pallas_call(
        functools.partial(_mlstm_kernel, chunk=chunk),
        grid_spec=grid_spec,
        out_shape=(jax.ShapeDtypeStruct((B, L, ML_V_WIDTH), BF16),
                   jax.ShapeDtypeStruct((B, N_ML_HEADS, ML_V_DIM, ML_QK_DIM), F32),
                   jax.ShapeDtypeStruct((B, N_ML_HEADS, ML_QK_DIM), F32),
                   jax.ShapeDtypeStruct((B, N_ML_HEADS, LANES), F32)),
        compiler_params=pltpu.CompilerParams(dimension_semantics=("parallel", "arbitrary"),
                                             vmem_limit_bytes=VMEM_LIMIT),
        name="mlstm_prompt",
    )(pw["b_i"], pw["b_f"], mq, mk, mv, grow, mo, pw["g_ml_out"])


def _post_kernel(x_ref, att_ref, ml_ref, ga_ref, gm_ref, pe_ref,
                 watt_ref, wml_ref, wout_ref, gffn_ref, wgate_ref, wup_ref, wdown_ref,
                 wple_ref, gple_ref, wpg_ref, y_ref):
    mix = (jax.nn.sigmoid(ga_ref[0]) * _dot(att_ref[0].astype(BF16), watt_ref[...])
           + jax.nn.sigmoid(gm_ref[0]) * _dot(ml_ref[0], wml_ref[...]))
    h = x_ref[0] + _dot(mix.astype(BF16), wout_ref[...])
    hb = _rms(h, gffn_ref[...]).astype(BF16)
    act = jax.nn.silu(_dot(hb, wgate_ref[...])) * _dot(hb, wup_ref[...])
    h = h + _dot(act.astype(BF16), wdown_ref[...])
    gate = jax.nn.sigmoid(_dot(_rms(h, gple_ref[...]).astype(BF16), wpg_ref[...]))
    y_ref[0] = h + gate * _dot(pe_ref[0].astype(BF16), wple_ref[...])


def _post(x, att, ml, ga, gm, pe, pw, tm):
    B, L, _ = x.shape
    tok = lambda w: pl.BlockSpec((1, tm, w), lambda b, i: (b, i, 0))
    full = lambda a: pl.BlockSpec(a.shape, lambda b, i: (0,) * a.ndim, pipeline_mode=pl.Buffered(1))
    consts = (pw["w_att"], pw["w_ml"], pw["w_out"], pw["g_ffn"], pw["w_ffn_gate"], pw["w_ffn_up"],
              pw["w_ffn_down"], pw["w_ple"], pw["g_ple"], pw["w_ple_gate"])
    return pl.pallas_call(
        _post_kernel,
        grid=(B, L // tm),
        in_specs=[tok(D_MODEL), tok(ATT_WIDTH), tok(ML_V_WIDTH), tok(D_MODEL), tok(D_MODEL),
                  tok(pe.shape[-1])] + [full(a) for a in consts],
        out_specs=tok(D_MODEL),
        out_shape=jax.ShapeDtypeStruct((B, L, D_MODEL), F32),
        compiler_params=pltpu.CompilerParams(dimension_semantics=("parallel", "parallel"),
                                             vmem_limit_bytes=VMEM_LIMIT),
        name="merge_ffn",
    )(x, att, ml, ga, gm, pe, *consts)


def _sample_select_kernel(pt_ref, *refs, nblk, blocks_per_step):
    page_refs = refs[:2 * blocks_per_step]
    q_ref, sel_ref, score_ref, qbd_ref, gate_ref = refs[2 * blocks_per_step:]
    j = pl.program_id(1)
    lane = lax.broadcasted_iota(jnp.int32, (N_ATT_HEADS, LANES), 1)

    @pl.when(j == 0)
    def _():
        head = lax.broadcasted_iota(jnp.int32, (N_ATT_HEADS, ATT_WIDTH), 0)
        col = lax.broadcasted_iota(jnp.int32, (N_ATT_HEADS, ATT_WIDTH), 1)
        in_head = (col >= head * ATT_HEAD_DIM) & (col < (head + 1) * ATT_HEAD_DIM)
        qbd = jnp.where(in_head, q_ref[0], 0.0)
        hi = qbd.astype(BF16).astype(F32)
        qbd_ref[...] = jnp.concatenate([hi, qbd - hi], axis=0).astype(BF16)
        gate_ref[...] = jnp.zeros_like(gate_ref)

    qbd = qbd_ref[...]
    gate = gate_ref[...]
    for i in range(blocks_per_step):
        block = j * blocks_per_step + i
        s_pages = []
        for half in range(2):
            s = _dot(qbd, page_refs[2 * i + half][0].astype(BF16))
            s = s[0:N_ATT_HEADS] + s[N_ATT_HEADS:]
            score_ref[0, 2 * block + half] = s
            s_pages.append(s)
        g = jnp.sum(s_pages[0] + s_pages[1], axis=1, keepdims=True) * (1.0 / MOBA_BLOCK)
        gate = jnp.where(lane == block, g, gate)
    gate_ref[...] = gate

    @pl.when(j == pl.num_programs(1) - 1)
    def _():
        gate = gate_ref[...]
        rank = jnp.zeros((N_ATT_HEADS, LANES), F32)
        for n2 in range(nblk):
            col = gate[:, n2:n2 + 1]
            beats = (col > gate) | ((col == gate) & (lane > n2))
            rank = rank + jnp.where(beats, 1.0, 0.0)
        out = jnp.zeros((N_ATT_HEADS, LANES), jnp.int32)
        lane_f = lane.astype(F32)
        for r in range(MOBA_TOPK):
            pick = (rank == float(r)) & (lane < nblk)
            idx = jnp.sum(jnp.where(pick, lane_f, 0.0), axis=1, keepdims=True)
            out = jnp.where(lane == r, idx.astype(jnp.int32), out)
        sel_ref[0] = out


def _sample_select(page_table, cache_kt, q_s):
    Bs, n_pages = page_table.shape
    nblk = n_pages * PAGE_SIZE // MOBA_BLOCK
    assert MOBA_BLOCK == 2 * PAGE_SIZE and nblk <= LANES
    bps = SELECT_BLOCKS_PER_STEP
    assert nblk % bps == 0
    pages_per_step = 2 * bps

    def page_spec(p):
        return pl.BlockSpec((1, ATT_WIDTH, PAGE_SIZE), lambda b, j, pt: (pt[b, j * pages_per_step + p], 0, 0))

    grid_spec = pltpu.PrefetchScalarGridSpec(
        num_scalar_prefetch=1,
        grid=(Bs, nblk // bps),
        in_specs=[page_spec(p) for p in range(pages_per_step)]
                 + [pl.BlockSpec((1, 1, ATT_WIDTH), lambda b, j, pt: (b, 0, 0))],
        out_specs=(pl.BlockSpec((1, N_ATT_HEADS, LANES), lambda b, j, pt: (b, 0, 0)),
                   pl.BlockSpec((1, n_pages, N_ATT_HEADS, PAGE_SIZE), lambda b, j, pt: (b, 0, 0, 0))),
        scratch_shapes=[pltpu.VMEM((2 * N_ATT_HEADS, ATT_WIDTH), BF16),
                        pltpu.VMEM((N_ATT_HEADS, LANES), F32)],
    )
    return pl.pallas_call(
        functools.partial(_sample_select_kernel, nblk=nblk, blocks_per_step=bps),
        grid_spec=grid_spec,
        out_shape=(jax.ShapeDtypeStruct((Bs, N_ATT_HEADS, LANES), jnp.int32),
                   jax.ShapeDtypeStruct((Bs, n_pages, N_ATT_HEADS, PAGE_SIZE), F32)),
        compiler_params=pltpu.CompilerParams(dimension_semantics=("parallel", "arbitrary"),
                                             vmem_limit_bytes=VMEM_LIMIT),
        name="moba_sample_select",
    )(page_table, *([cache_kt] * pages_per_step), q_s)


def _sample_attn_kernel(pt_ref, sel_ref, *refs, n_pages):
    v_refs = refs[:HEAD_PAIR * n_pages]
    score_ref, q_ref, kn_ref, vn_ref, o_ref = refs[HEAD_PAIR * n_pages:]
    b = pl.program_id(0)
    pair = pl.program_id(1)
    ppb = MOBA_BLOCK // PAGE_SIZE
    scale = ATT_HEAD_DIM ** -0.5
    rows = 16
    outs = []
    for hh in range(HEAD_PAIR):
        h = pair * HEAD_PAIR + hh
        lanes = slice(hh * ATT_HEAD_DIM, (hh + 1) * ATT_HEAD_DIM)
        s_own = jnp.sum(q_ref[0][:, lanes] * kn_ref[0][:, lanes], axis=1, keepdims=True) * scale
        pages = [ppb * sel_ref[b, h * MOBA_TOPK + j // ppb] + j % ppb for j in range(n_pages)]
        s = jnp.concatenate([score_ref[0, pg, pl.ds(h, 1), :] for pg in pages], axis=1) * scale
        v_all = jnp.concatenate([v_refs[hh * n_pages + j][0].astype(BF16) for j in range(n_pages)], axis=1)
        m = jnp.maximum(s_own, jnp.max(s, axis=1, keepdims=True))
        p_own = jnp.exp(s_own - m)
        p = jnp.exp(s - m)
        l = p_own + jnp.sum(p, axis=1, keepdims=True)
        pv = _dot_nt(jnp.broadcast_to(p, (rows, n_pages * PAGE_SIZE)).astype(BF16), v_all)[0:1]
        outs.append((p_own * vn_ref[0][:, lanes] + pv) / l)
    o_ref[0] = jnp.concatenate(outs, axis=1)


def _sample_attention(page_table, sel, scores, cache_vt, q_s, k_s, v_s):
    Bs, n_seq_pages = page_table.shape
    ppb = MOBA_BLOCK // PAGE_SIZE
    n_pages = MOBA_TOPK * ppb

    def tile_spec(hh, j):
        def index(b, pair, pt, sl):
            h = pair * HEAD_PAIR + hh
            return (pt[b, ppb * sl[b, h * MOBA_TOPK + j // ppb] + j % ppb], h, 0)
        return pl.BlockSpec((1, ATT_HEAD_DIM, PAGE_SIZE), index)

    tiles = [tile_spec(hh, j) for hh in range(HEAD_PAIR) for j in range(n_pages)]
    row = pl.BlockSpec((1, 1, LANES), lambda b, pair, pt, sl: (b, 0, pair))
    grid_spec = pltpu.PrefetchScalarGridSpec(
        num_scalar_prefetch=2,
        grid=(Bs, N_ATT_HEADS // HEAD_PAIR),
        in_specs=tiles + [pl.BlockSpec((1, n_seq_pages, N_ATT_HEADS, PAGE_SIZE),
                                       lambda b, pair, pt, sl: (b, 0, 0, 0)), row, row, row],
        out_specs=row,
    )
    return pl.pallas_call(
        functools.partial(_sample_attn_kernel, n_pages=n_pages),
        grid_spec=grid_spec,
        out_shape=jax.ShapeDtypeStruct((Bs, 1, ATT_WIDTH), F32),
        compiler_params=pltpu.CompilerParams(dimension_semantics=("parallel", "arbitrary"),
                                             vmem_limit_bytes=VMEM_LIMIT),
        name="moba_sample_attn",
    )(page_table, sel, *([cache_vt] * len(tiles)), scores, q_s, k_s, v_s)


def _mlstm_step_kernel(bi_ref, bf_ref, m0_ref, mq_ref, mk_ref, mv_ref, g_ref, mo_ref, gout_ref,
                       c0_ref, n0_ref, ml_ref, c_out_ref, n_out_ref, m_out_ref):
    b = pl.program_id(0)
    dv = ML_V_DIM
    r_i = lax.broadcasted_iota(jnp.int32, (dv, dv), 0)
    c_i = lax.broadcasted_iota(jnp.int32, (dv, dv), 1)
    eye = r_i == c_i
    g = g_ref[0]
    for h in range(N_ML_HEADS):
        i_g = g[:, h:h + 1] + bi_ref[h]
        logf = _log_sigmoid(g[:, N_ML_HEADS + h:N_ML_HEADS + h + 1] + bf_ref[h])
        m_old = m0_ref[b, h]
        inter = logf + m_old
        m_t = jnp.maximum(inter, i_g)
        w_inter = jnp.exp(inter - m_t)
        w_in = jnp.exp(i_g - m_t)
        q = mq_ref[0, :, h * ML_QK_DIM:(h + 1) * ML_QK_DIM].astype(F32)
        k = mk_ref[0, :, h * ML_QK_DIM:(h + 1) * ML_QK_DIM].astype(F32)
        v = mv_ref[0, :, h * dv:(h + 1) * dv].astype(F32)
        c_old = c0_ref[0, 0, h]
        n_old = n0_ref[0, 0, h:h + 1, :]
        s = jnp.sum(q * k, axis=1, keepdims=True) * w_in
        v_col = jnp.sum(jnp.where(eye, v, 0.0), axis=1, keepdims=True)
        num = w_inter * jnp.sum(c_old * q, axis=1, keepdims=True) + s * v_col
        den = w_inter * jnp.sum(n_old * q, axis=1, keepdims=True) + s
        hid_col = num / jnp.maximum(jnp.abs(den), jnp.exp(-m_t))
        hid = jnp.sum(jnp.where(eye, hid_col, 0.0), axis=0, keepdims=True)
        c_out_ref[0, 0, h] = w_inter * c_old + (w_in * v_col) * k
        n_out_ref[0, 0, h:h + 1, :] = w_inter * n_old + w_in * k
        m_out_ref[0, h:h + 1, :] = jnp.broadcast_to(m_t, (1, LANES))
        hn = _rms(hid, gout_ref[h:h + 1, :])
        mo = mo_ref[0, :, h * dv:(h + 1) * dv]
        ml_ref[0, :, h * dv:(h + 1) * dv] = (hn * jax.nn.sigmoid(mo)).astype(BF16)


def _mlstm_sample(mq, mk, mv, gcol, mo, state_c, state_n, state_m, pw):
    Bs = mq.shape[0]
    row = lambda w: pl.BlockSpec((1, 1, w), lambda b, *_: (b, 0, 0))
    grid_spec = pltpu.PrefetchScalarGridSpec(
        num_scalar_prefetch=3,
        grid=(Bs,),
        in_specs=[row(ML_QK_WIDTH), row(ML_QK_WIDTH), row(ML_V_WIDTH), row(LANES), row(ML_V_WIDTH),
                  pl.BlockSpec((N_ML_HEADS, ML_V_DIM), lambda b, *_: (0, 0)),
                  pl.BlockSpec((1, 1, N_ML_HEADS, ML_V_DIM, ML_QK_DIM), lambda b, *_: (0, b, 0, 0, 0)),
                  pl.BlockSpec((1, 1, N_ML_HEADS, ML_QK_DIM), lambda b, *_: (0, b, 0, 0))],
        out_specs=(row(ML_V_WIDTH),
                   pl.BlockSpec((1, 1, N_ML_HEADS, ML_V_DIM, ML_QK_DIM), lambda b, *_: (0, b, 0, 0, 0)),
                   pl.BlockSpec((1, 1, N_ML_HEADS, ML_QK_DIM), lambda b, *_: (0, b, 0, 0)),
                   pl.BlockSpec((1, N_ML_HEADS, LANES), lambda b, *_: (b, 0, 0))),
    )
    return pl.pallas_call(
        _mlstm_step_kernel,
        grid_spec=grid_spec,
        out_shape=(jax.ShapeDtypeStruct((Bs, 1, ML_V_WIDTH), BF16),
                   jax.ShapeDtypeStruct(state_c.shape, F32),
                   jax.ShapeDtypeStruct(state_n.shape, F32),
                   jax.ShapeDtypeStruct((Bs, N_ML_HEADS, LANES), F32)),
        compiler_params=pltpu.CompilerParams(dimension_semantics=("arbitrary",),
                                             vmem_limit_bytes=VMEM_LIMIT),
        name="mlstm_sample",
    )(pw["b_i"], pw["b_f"], state_m[0], mq, mk, mv, gcol, mo, pw["g_ml_out"], state_c, state_n)


def _rope_angles(pos):
    freqs = ROPE_THETA ** (-jnp.arange(0, ROPE_DIMS, 2, dtype=F32) / ROPE_DIMS)
    return pos.astype(F32)[:, None] * freqs[None, :]


def _rope_tables_cols(pos):
    ang = _rope_angles(pos).T
    return jnp.cos(ang), jnp.sin(ang)


def _rope_tables_rows(pos):
    half = ROPE_DIMS // 2
    n = pos.shape[0]
    ang = _rope_angles(pos)
    cos, sin = jnp.cos(ang), jnp.sin(ang)
    rest = ATT_HEAD_DIM - ROPE_DIMS
    zh = jnp.zeros((n, half), F32)
    cos_h = jnp.concatenate([cos, cos, jnp.ones((n, rest), F32)], axis=-1)
    sa_h = jnp.concatenate([-sin, zh, jnp.zeros((n, rest), F32)], axis=-1)
    sb_h = jnp.concatenate([zh, sin, jnp.zeros((n, rest), F32)], axis=-1)
    tile = lambda t: jnp.tile(t, (1, HEAD_PAIR))
    return tile(cos_h), tile(sa_h), tile(sb_h)


def _prepare_weights(w_in, b_igate, b_fgate, g_mix_norm, g_q_norm, g_k_norm, g_ml_out_norm,
                     w_att_branch, w_ml_branch, w_out, g_ffn_norm, w_ffn_gate, w_ffn_up, w_ffn_down,
                     w_ple, g_ple_norm, w_ple_gate):
    w_gates = w_in[:, _C_GATES:_C_MO]
    head_of = jnp.arange(ATT_WIDTH, dtype=jnp.int32) // ATT_HEAD_DIM
    w_qkv = w_in[:, _C_Q:_C_MQ].astype(BF16)
    return {
        "w_qkv": w_qkv,
        "w_qkv_t": w_qkv.T,
        "w_m": w_in[:, _C_MQ:_C_GATES].astype(BF16),
        "g_q_col": g_q_norm.reshape(1, ATT_HEAD_DIM, 1),
        "g_k_col": g_k_norm.reshape(1, ATT_HEAD_DIM, 1),
        "w_g": jnp.pad(w_gates, ((0, 0), (0, LANES - 2 * N_ML_HEADS))).astype(BF16),
        "w_gt": jnp.pad(w_gates.T, ((0, 16 - 2 * N_ML_HEADS), (0, 0))).astype(BF16),
        "w_b": w_in[:, _C_MO:_C_END].astype(BF16),
        "e_head": (head_of[:, None] == head_of[None, :]).astype(BF16),
        "g_mix": g_mix_norm.reshape(1, D_MODEL),
        "g_q": jnp.tile(g_q_norm, N_ATT_HEADS).reshape(1, ATT_WIDTH),
        "g_k": jnp.tile(g_k_norm, N_ATT_HEADS).reshape(1, ATT_WIDTH),
        "g_ml_out": g_ml_out_norm,
        "b_i": b_igate, "b_f": b_fgate,
        "w_att": w_att_branch.astype(BF16), "w_ml": w_ml_branch.astype(BF16), "w_out": w_out.astype(BF16),
        "g_ffn": g_ffn_norm.reshape(1, D_MODEL),
        "w_ffn_gate": w_ffn_gate.astype(BF16), "w_ffn_up": w_ffn_up.astype(BF16),
        "w_ffn_down": w_ffn_down.astype(BF16),
        "w_ple": w_ple.astype(BF16), "g_ple": g_ple_norm.reshape(1, D_MODEL),
        "w_ple_gate": w_ple_gate.astype(BF16),
    }


def _prompt_layer(x, pe, pw):
    B, L, _ = x.shape
    pos = jnp.arange(L, dtype=jnp.int32)
    qt, kt, vt, mq, mk, mv, gcol, grow, mo, ga, gm = _project(x, pos, pw, tm=256, feature_major=True)
    att = _attention_prompt(qt, kt, vt)
    ml, c_fin, n_fin, m_pad = _mlstm_prompt(mq, mk, mv, grow, mo, pw)
    y = _post(x, att, ml, ga, gm, pe, pw, tm=256)
    tokens_major = lambda t: jnp.transpose(t.reshape(B, N_ATT_HEADS, ATT_HEAD_DIM, L), (0, 3, 1, 2))
    return y, tokens_major(kt), tokens_major(vt), c_fin, n_fin, m_pad[:, :, 0]


def _sample_layer(x, pe, cache_k, cache_v, page_table, state_c, state_n, state_m, pw):
    Bs, Ls, _ = x.shape
    assert Ls == 1
    past_len = page_table.shape[1] * PAGE_SIZE
    n_pool = cache_k.shape[0]
    pos = jnp.full((Bs,), past_len, dtype=jnp.int32)
    xt = x.reshape(1, Bs, D_MODEL)
    q, k, v, mq, mk, mv, gcol, _, mo, ga, gm = _project(xt, pos, pw, tm=Bs, feature_major=False)
    per_seq = lambda t: t.reshape(Bs, 1, t.shape[-1])
    pages_t = lambda c: jnp.transpose(c, (0, 2, 3, 1)).reshape(n_pool, ATT_WIDTH, PAGE_SIZE)
    cache_kt, cache_vt = pages_t(cache_k), pages_t(cache_v)
    sel, scores = _sample_select(page_table, cache_kt, per_seq(q))
    sel2 = sel[:, :, :MOBA_TOPK].reshape(Bs, N_ATT_HEADS * MOBA_TOPK)
    att = _sample_attention(page_table, sel2, scores, cache_vt, per_seq(q), per_seq(k), per_seq(v))
    ml, c_new, n_new, m_pad = _mlstm_sample(per_seq(mq), per_seq(mk), per_seq(mv), per_seq(gcol),
                                            per_seq(mo), state_c, state_n, state_m, pw)
    y = _post(xt, att.reshape(1, Bs, ATT_WIDTH), ml.reshape(1, Bs, ML_V_WIDTH), ga, gm,
              pe.reshape(1, Bs, pe.shape[-1]), pw, tm=Bs)
    return y.reshape(Bs, 1, D_MODEL), k, v, c_new, n_new, m_pad[:, :, 0]


def kernel(x_prompt, x_sample, cache_k, cache_v, state_mlstm_C, state_mlstm_n, state_mlstm_m, page_table,
           p_prompt, p_sample, w_in, b_igate, b_fgate, g_mix_norm, g_q_norm, g_k_norm, g_ml_out_norm,
           w_att_branch, w_ml_branch, w_out, g_ffn_norm, w_ffn_gate, w_ffn_up, w_ffn_down,
           w_ple, g_ple_norm, w_ple_gate):
    depth = w_in.shape[0]
    assert depth == 1
    Bp, Lp, _ = x_prompt.shape
    Bs, Ls, _ = x_sample.shape
    pw = _prepare_weights(w_in[0], b_igate[0], b_fgate[0], g_mix_norm[0], g_q_norm[0], g_k_norm[0],
                          g_ml_out_norm[0], w_att_branch[0], w_ml_branch[0], w_out[0], g_ffn_norm[0],
                          w_ffn_gate[0], w_ffn_up[0], w_ffn_down[0], w_ple[0], g_ple_norm[0], w_ple_gate[0])
    yp, kp, vp, cp, n_p, mp = _prompt_layer(x_prompt, p_prompt[0], pw)
    ys, ks, vs, cs, ns, ms = _sample_layer(x_sample, p_sample[0], cache_k[0], cache_v[0], page_table,
                                           state_mlstm_C, state_mlstm_n, state_mlstm_m, pw)
    heads = lambda t: t.reshape(1, Bs, Ls, N_ATT_HEADS, ATT_HEAD_DIM)
    return (yp, ys, kp[None], vp[None], heads(ks), heads(vs),
            cp[None], n_p[None], mp[None], cs, ns, ms[None])
```

```python
import functools

import jax
import jax.numpy as jnp
from jax import lax
from jax.experimental import pallas as pl
from jax.experimental.pallas import tpu as pltpu

F32 = jnp.float32
BF16 = jnp.bfloat16

D_MODEL = 1024
N_ATT_HEADS = 8
ATT_HEAD_DIM = 64
ATT_WIDTH = N_ATT_HEADS * ATT_HEAD_DIM
ROPE_DIMS = ATT_HEAD_DIM // 4
ROPE_THETA = 500000.0
MOBA_BLOCK = 256
MOBA_TOPK = 3
PAGE_SIZE = 128
N_ML_HEADS = 4
ML_QK_DIM = 128
ML_V_DIM = 256
ML_QK_WIDTH = N_ML_HEADS * ML_QK_DIM
ML_V_WIDTH = N_ML_HEADS * ML_V_DIM
NORM_EPS = 1e-6
NEG_INF = -1e30
LOG2_E = 1.4426950408889634

LANES = 128
HEAD_PAIR = LANES // ATT_HEAD_DIM
ML_CHUNK = 256
SELECT_BLOCKS_PER_STEP = 16
SAMPLE_HEADS_PER_STEP = 4
VMEM_LIMIT = 56 * 1024 * 1024

_C_Q, _C_K, _C_V = 0, ATT_WIDTH, 2 * ATT_WIDTH
_C_MQ = 3 * ATT_WIDTH
_C_MK = _C_MQ + ML_QK_WIDTH
_C_MV = _C_MK + ML_QK_WIDTH
_C_GATES = _C_MV + ML_V_WIDTH
_C_MO = _C_GATES + 2 * N_ML_HEADS
_C_GA = _C_MO + ML_V_WIDTH
_C_GM = _C_GA + D_MODEL
_C_END = _C_GM + D_MODEL


def _dot(a, b):
    return jnp.dot(a, b, preferred_element_type=F32)


def _dot_nt(a, b, precision=None):
    return lax.dot_general(a, b, (((1,), (1,)), ((), ())), precision=precision,
                           preferred_element_type=F32)


def _rms(x, g):
    return x * lax.rsqrt(jnp.mean(x * x, axis=-1, keepdims=True) + NORM_EPS) * g


def _log_sigmoid(x):
    return jnp.minimum(x, 0.0) - jnp.log1p(jnp.exp(-jnp.abs(x)))


def _proj_shared(xb, wm_ref, wg_ref, wgt_ref, wb_ref,
                 mq_ref, mk_ref, mv_ref, gcol_ref, grow_ref, mo_ref, ga_ref, gm_ref):
    mq_ref[0] = _dot(xb, wm_ref[:, 0:ML_QK_WIDTH]).astype(BF16)
    mk_ref[0] = (_dot(xb, wm_ref[:, ML_QK_WIDTH:2 * ML_QK_WIDTH]) * (ML_QK_DIM ** -0.5)).astype(BF16)
    mv_ref[0] = _dot(xb, wm_ref[:, 2 * ML_QK_WIDTH:2 * ML_QK_WIDTH + ML_V_WIDTH]).astype(BF16)
    gcol_ref[0] = _dot(xb, wg_ref[...])
    grow_ref[0] = _dot_nt(wgt_ref[...], xb)
    mo_ref[0] = _dot(xb, wb_ref[:, 0:ML_V_WIDTH])
    ga_ref[0] = _dot(xb, wb_ref[:, ML_V_WIDTH:ML_V_WIDTH + D_MODEL])
    gm_ref[0] = _dot(xb, wb_ref[:, ML_V_WIDTH + D_MODEL:ML_V_WIDTH + 2 * D_MODEL])


def _proj_kernel_rows(x_ref, cos_ref, sa_ref, sb_ref, gmix_ref, gq_ref, gk_ref, e_ref, wqkv_ref,
                      wm_ref, wg_ref, wgt_ref, wb_ref, q_ref, k_ref, v_ref, *rest_refs):
    xb = _rms(x_ref[0], gmix_ref[...]).astype(BF16)
    cos, sa, sb = cos_ref[...], sa_ref[...], sb_ref[...]
    e = e_ref[...]

    def head_norm_rope(z, g):
        zz = z * z
        hi = zz.astype(BF16)
        lo = (zz - hi.astype(F32)).astype(BF16)
        ss = _dot(hi, e) + _dot(lo, e)
        zn = z * lax.rsqrt(ss * (1.0 / ATT_HEAD_DIM) + NORM_EPS) * g
        outs = []
        for grp in range(ATT_WIDTH // LANES):
            zg = zn[:, grp * LANES:(grp + 1) * LANES]
            outs.append(zg * cos + pltpu.roll(zg, LANES - ROPE_DIMS // 2, 1) * sa
                        + pltpu.roll(zg, ROPE_DIMS // 2, 1) * sb)
        return jnp.concatenate(outs, axis=-1)

    q_ref[0] = head_norm_rope(_dot(xb, wqkv_ref[:, 0:ATT_WIDTH]), gq_ref[...])
    k_ref[0] = head_norm_rope(_dot(xb, wqkv_ref[:, ATT_WIDTH:2 * ATT_WIDTH]), gk_ref[...])
    v_ref[0] = _dot(xb, wqkv_ref[:, 2 * ATT_WIDTH:3 * ATT_WIDTH])
    _proj_shared(xb, wm_ref, wg_ref, wgt_ref, wb_ref, *rest_refs)


def _proj_kernel_cols(x_ref, cos_ref, sin_ref, gmix_ref, gq_ref, gk_ref, wqkvt_ref,
                      wm_ref, wg_ref, wgt_ref, wb_ref, qt_ref, kt_ref, vt_ref, *rest_refs):
    xb = _rms(x_ref[0], gmix_ref[...]).astype(BF16)
    tm = xb.shape[0]
    zt = _dot_nt(wqkvt_ref[...], xb)
    cos, sin = cos_ref[...][None], sin_ref[...][None]
    half = ROPE_DIMS // 2

    def head_norm_rope(z, g):
        z3 = z.reshape(N_ATT_HEADS, ATT_HEAD_DIM, tm)
        ss = jnp.sum(z3 * z3, axis=1, keepdims=True)
        zn = z3 * lax.rsqrt(ss * (1.0 / ATT_HEAD_DIM) + NORM_EPS) * g
        x1, x2 = zn[:, 0:half, :], zn[:, half:ROPE_DIMS, :]
        out = jnp.concatenate([x1 * cos - x2 * sin, x2 * cos + x1 * sin, zn[:, ROPE_DIMS:, :]], axis=1)
        return out.reshape(ATT_WIDTH, tm)

    qt_ref[0] = head_norm_rope(zt[0:ATT_WIDTH], gq_ref[...])
    kt_ref[0] = head_norm_rope(zt[ATT_WIDTH:2 * ATT_WIDTH], gk_ref[...])
    vt_ref[0] = zt[2 * ATT_WIDTH:3 * ATT_WIDTH]
    _proj_shared(xb, wm_ref, wg_ref, wgt_ref, wb_ref, *rest_refs)


def _project(x, pos, pw, tm, feature_major):
    B, L, _ = x.shape
    tok = lambda w: pl.BlockSpec((1, tm, w), lambda b, i: (b, i, 0))
    full = lambda a: pl.BlockSpec(a.shape, lambda b, i: (0,) * a.ndim, pipeline_mode=pl.Buffered(1))
    shared = (pw["w_m"], pw["w_g"], pw["w_gt"], pw["w_b"])
    rest_shape = (
        jax.ShapeDtypeStruct((B, L, ML_QK_WIDTH), BF16),
        jax.ShapeDtypeStruct((B, L, ML_QK_WIDTH), BF16),
        jax.ShapeDtypeStruct((B, L, ML_V_WIDTH), BF16),
        jax.ShapeDtypeStruct((B, L, LANES), F32),
        jax.ShapeDtypeStruct((B, 16, L), F32),
        jax.ShapeDtypeStruct((B, L, ML_V_WIDTH), F32),
        jax.ShapeDtypeStruct((B, L, D_MODEL), F32),
        jax.ShapeDtypeStruct((B, L, D_MODEL), F32),
    )
    rest_specs = (tok(ML_QK_WIDTH), tok(ML_QK_WIDTH), tok(ML_V_WIDTH), tok(LANES),
                  pl.BlockSpec((1, 16, tm), lambda b, i: (b, 0, i)),
                  tok(ML_V_WIDTH), tok(D_MODEL), tok(D_MODEL))
    if feature_major:
        cos, sin = _rope_tables_cols(pos)
        tab = pl.BlockSpec((ROPE_DIMS // 2, tm), lambda b, i: (0, i))
        consts = (pw["g_mix"], pw["g_q_col"], pw["g_k_col"], pw["w_qkv_t"]) + shared
        body, tables, tab_specs = _proj_kernel_cols, (cos, sin), [tab, tab]
        qkv_shape = (jax.ShapeDtypeStruct((B, ATT_WIDTH, L), F32),) * 3
        qkv_specs = (pl.BlockSpec((1, ATT_WIDTH, tm), lambda b, i: (b, 0, i)),) * 3
    else:
        tables = _rope_tables_rows(pos)
        tab = pl.BlockSpec((tm, LANES), lambda b, i: (i, 0))
        consts = (pw["g_mix"], pw["g_q"], pw["g_k"], pw["e_head"], pw["w_qkv"]) + shared
        body, tab_specs = _proj_kernel_rows, [tab, tab, tab]
        qkv_shape = (jax.ShapeDtypeStruct((B, L, ATT_WIDTH), F32),) * 3
        qkv_specs = (tok(ATT_WIDTH),) * 3
    return pl.pallas_call(
        body,
        grid=(B, L // tm),
        in_specs=[tok(D_MODEL)] + tab_specs + [full(a) for a in consts],
        out_specs=qkv_specs + rest_specs,
        out_shape=qkv_shape + rest_shape,
        compiler_params=pltpu.CompilerParams(dimension_semantics=("parallel", "parallel"),
                                             vmem_limit_bytes=VMEM_LIMIT),
        name="in_proj_cols" if feature_major else "in_proj_rows",
    )(x, *tables, *consts)


def _attn_kernel(q_ref, k_ref, v_ref, o_ref, kb_ref, vt_ref, kmean_ref, acc_ref, s0_ref, s1_ref, *, nblk):
    blk = MOBA_BLOCK
    tq = 2 * blk
    ones_rows = vt_ref.shape[1] - LANES
    t = pl.program_id(2)
    c0 = 2 * t

    nb = kmean_ref.shape[0]

    @pl.when(t == 0)
    def _():
        lane = lax.broadcasted_iota(jnp.int32, (blk, LANES), 1)
        kmean_ref[...] = jnp.zeros_like(kmean_ref)
        for n in range(nblk):
            rows = slice(n * blk, (n + 1) * blk)
            kf = k_ref[0, :, rows].T
            kb_ref[rows, 0:LANES] = kf.astype(BF16)
            kb_ref[rows, LANES:2 * LANES] = jnp.where(lane == n, 1.0, 0.0).astype(BF16)
            kmean_ref[n:n + 1, :] = jnp.sum(kf, axis=0, keepdims=True) * (1.0 / blk)
            vt_ref[n] = jnp.concatenate([v_ref[0, :, rows], jnp.ones((ones_rows, blk), F32)],
                                        axis=0).astype(BF16)

    q2 = q_ref[0]
    feat = lax.broadcasted_iota(jnp.int32, (LANES, tq), 0)
    blk_id = lax.broadcasted_iota(jnp.int32, (nb, tq), 0)
    col = lax.broadcasted_iota(jnp.int32, (nb, tq), 1)
    cur = c0 + (col >= blk).astype(jnp.int32)
    kmean = kmean_ref[...]
    bias_pad = jnp.zeros((LANES - nb, tq), F32)

    qas = []
    for h in range(HEAD_PAIR):
        qh = jnp.where((feat >= h * ATT_HEAD_DIM) & (feat < (h + 1) * ATT_HEAD_DIM), q2, 0.0)
        gate = jnp.dot(kmean, qh, precision=lax.Precision.HIGHEST,
                       preferred_element_type=F32)
        gate = jnp.where(blk_id < cur, gate, -jnp.inf)
        rank = jnp.zeros((nb, tq), F32)
        for n2 in range(nblk):
            row = gate[n2:n2 + 1, :]
            beats = (row > gate) | ((row == gate) & (blk_id > n2))
            rank = rank + jnp.where(beats, 1.0, 0.0)
        open_blk = ((blk_id < cur) & (rank < MOBA_TOPK)) | (blk_id == cur)
        bias = jnp.where(open_blk, 0.0, NEG_INF)
        qas.append(jnp.concatenate([qh * (ATT_HEAD_DIM ** -0.5 * LOG2_E), bias, bias_pad],
                                   axis=0).astype(BF16))

    def block_rows(n):
        return kb_ref[pl.ds(pl.multiple_of(n * blk, blk), blk), :]

    key_i = lax.broadcasted_iota(jnp.int32, (blk, tq), 0)
    qry_j = lax.broadcasted_iota(jnp.int32, (blk, tq), 1)
    s_refs = (s0_ref, s1_ref)

    def stage_a(first, slot, own):
        ks = [block_rows(first), block_rows(first + 1)]
        maxima = []
        for h in range(HEAD_PAIR):
            sts = [_dot(ks[0], qas[h]), _dot(ks[1], qas[h])]
            if own:
                sts = [jnp.where(key_i <= qry_j, sts[0], NEG_INF),
                       jnp.where(key_i <= qry_j - blk, sts[1], NEG_INF)]
            s_refs[slot][h, 0] = sts[0]
            s_refs[slot][h, 1] = sts[1]
            maxima.append(jnp.maximum(jnp.max(sts[0], axis=0, keepdims=True),
                                      jnp.max(sts[1], axis=0, keepdims=True)))
        return maxima

    def stage_b(first, slot, m_run, maxima):
        vts = [vt_ref[first], vt_ref[first + 1]]
        m_out = []
        for h in range(HEAD_PAIR):
            m_new = jnp.maximum(m_run[h], maxima[h])
            pv = (_dot(vts[0], jnp.exp2(s_refs[slot][h, 0] - m_new).astype(BF16))
                  + _dot(vts[1], jnp.exp2(s_refs[slot][h, 1] - m_new).astype(BF16)))
            acc_ref[h] = jnp.exp2(m_run[h] - m_new) * acc_ref[h] + pv
            m_out.append(m_new)
        return m_out

    def pending_first(j):
        return jnp.where(j == 0, c0, 2 * j - 2)

    acc_ref[...] = jnp.zeros_like(acc_ref)
    m_start = [jnp.full((1, tq), NEG_INF, F32)] * HEAD_PAIR
    carry = tuple(m_start) + tuple(stage_a(c0, 0, own=True))

    def trip(j, slot, carry):
        m_run, maxima = carry[:HEAD_PAIR], carry[HEAD_PAIR:]
        new_maxima = stage_a(2 * j, 1 - slot, own=False)
        m_new = stage_b(pending_first(j), slot, m_run, maxima)
        return tuple(m_new) + tuple(new_maxima)

    carry = lax.fori_loop(0, lax.shift_right_logical(t, 1),
                          lambda i, cr: trip(2 * i + 1, 1, trip(2 * i, 0, cr)), carry)

    def odd_tail(carry):
        carry = trip(t - 1, 0, carry)
        stage_b(pending_first(t), 1, carry[:HEAD_PAIR], carry[HEAD_PAIR:])
        return 0

    def even_tail(carry):
        stage_b(pending_first(t), 0, carry[:HEAD_PAIR], carry[HEAD_PAIR:])
        return 0

    lax.cond((t & 1) == 1, odd_tail, even_tail, carry)
    outs = [acc_ref[h, 0:LANES, :] / acc_ref[h, LANES:LANES + 1, :] for h in range(HEAD_PAIR)]
    ot = jnp.where(feat < ATT_HEAD_DIM, outs[0], outs[1])
    o_ref[0] = ot.T.astype(o_ref.dtype)


def _attention_prompt(qt, kt, vt):
    B, _, L = qt.shape
    blk = MOBA_BLOCK
    tq = 2 * blk
    assert L % tq == 0
    nblk = L // blk
    assert nblk <= LANES
    nb = -(-nblk // 8) * 8
    ones_rows = 16
    groups = ATT_WIDTH // LANES
    return pl.pallas_call(
        functools.partial(_attn_kernel, nblk=nblk),
        grid=(B, groups, L // tq),
        in_specs=[pl.BlockSpec((1, LANES, tq), lambda b, g, i: (b, g, i)),
                  pl.BlockSpec((1, LANES, L), lambda b, g, i: (b, g, 0)),
                  pl.BlockSpec((1, LANES, L), lambda b, g, i: (b, g, 0))],
        out_specs=pl.BlockSpec((1, tq, LANES), lambda b, g, i: (b, i, g)),
        out_shape=jax.ShapeDtypeStruct((B, L, ATT_WIDTH), BF16),
        scratch_shapes=[pltpu.VMEM((L, 2 * LANES), BF16),
                        pltpu.VMEM((nblk, LANES + ones_rows, blk), BF16),
                        pltpu.VMEM((nb, LANES), F32),
                        pltpu.VMEM((HEAD_PAIR, LANES + ones_rows, tq), F32),
                        pltpu.VMEM((HEAD_PAIR, 2, blk, tq), F32),
                        pltpu.VMEM((HEAD_PAIR, 2, blk, tq), F32)],
        compiler_params=pltpu.CompilerParams(dimension_semantics=("parallel", "parallel", "arbitrary"),
                                             vmem_limit_bytes=VMEM_LIMIT),
        name="moba_prompt",
    )(qt, kt, vt)


def _mlstm_kernel(bi_ref, bf_ref, mq_ref, mk_ref, mv_ref, grow_ref, mo_ref, gout_ref,
                  ml_ref, c_out_ref, n_out_ref, m_out_ref, cn_ref, m_ref, *, chunk):
    j = pl.program_id(1)
    dk, dv = ML_QK_DIM, ML_V_DIM

    @pl.when(j == 0)
    def _():
        cn_ref[...] = jnp.zeros_like(cn_ref)
        m_ref[...] = jnp.zeros_like(m_ref)

    t_i = lax.broadcasted_iota(jnp.int32, (chunk, chunk), 0)
    s_i = lax.broadcasted_iota(jnp.int32, (chunk, chunk), 1)
    tril = s_i <= t_i
    tril_b = jnp.where(tril, 1.0, 0.0).astype(BF16)
    triu_b = jnp.where(t_i <= s_i, 1.0, 0.0).astype(BF16)
    ones_cols = jnp.ones((chunk, LANES), BF16)
    pad_rows = jnp.zeros((13, chunk), F32)
    grow = grow_ref[0]
    for h in range(N_ML_HEADS):
        i_row = grow[h:h + 1, :] + bi_ref[h]
        f_row = _log_sigmoid(grow[N_ML_HEADS + h:N_ML_HEADS + h + 1, :] + bf_ref[h])
        hi = f_row.astype(BF16).astype(F32)
        mid = (f_row - hi).astype(BF16).astype(F32)
        pieces = jnp.concatenate([hi, mid, f_row - hi - mid, pad_rows], axis=0).astype(BF16)
        rows = _dot(pieces, triu_b)
        cols = _dot_nt(tril_b, pieces)
        b_row = rows[0:1] + rows[1:2] + rows[2:3]
        b_col = cols[:, 0:1] + cols[:, 1:2] + cols[:, 2:3]
        a_row = (i_row - b_row) * LOG2_E
        m_old = m_ref[h][:, 0:1]
        a_low = jnp.where(tril, a_row, NEG_INF)
        mm_col = jnp.maximum(jnp.max(a_low, axis=1, keepdims=True), m_old * LOG2_E)
        w_inter = jnp.exp2(m_old * LOG2_E - mm_col)
        floor_col = jnp.exp2(-(b_col * LOG2_E + mm_col))
        decay = jnp.exp2(a_low - mm_col)

        q = mq_ref[0, :, h * dk:(h + 1) * dk]
        k = mk_ref[0, :, h * dk:(h + 1) * dk]
        v_ones = jnp.concatenate([mv_ref[0, :, h * dv:(h + 1) * dv], ones_cols], axis=1)
        cn = cn_ref[h]
        s = _dot_nt(q, k) * decay
        inter = _dot(q, cn.astype(BF16))
        intra = _dot(s.astype(BF16), v_ones)
        num = w_inter * inter[:, 0:dv] + intra[:, 0:dv]
        den = w_inter * inter[:, dv:dv + 1] + intra[:, dv:dv + 1]
        hid = num / jnp.maximum(jnp.abs(den), floor_col)

        mm_end = mm_col[chunk - 1:chunk, :]
        w_row = jnp.exp2(a_row - mm_end)
        kw = (k.astype(F32).T * w_row).astype(BF16)
        cn_ref[h] = jnp.exp2(m_old * LOG2_E - mm_end) * cn + _dot(kw, v_ones)
        m_ref[h] = jnp.broadcast_to(b_row[:, chunk - 1:chunk] + mm_end * (1.0 / LOG2_E), (1, LANES))

        hn = _rms(hid, gout_ref[h:h + 1, :])
        mo = mo_ref[0, :, h * dv:(h + 1) * dv]
        ml_ref[0, :, h * dv:(h + 1) * dv] = (hn * jax.nn.sigmoid(mo)).astype(BF16)

    @pl.when(j == pl.num_programs(1) - 1)
    def _():
        for h in range(N_ML_HEADS):
            c_out_ref[0, h] = cn_ref[h, :, 0:dv].T
            n_out_ref[0, h:h + 1, :] = cn_ref[h, :, dv:dv + LANES].T[0:1, :]
            m_out_ref[0, h:h + 1, :] = m_ref[h]


def _mlstm_prompt(mq, mk, mv, grow, mo, pw):
    B, L, _ = mq.shape
    chunk = ML_CHUNK
    tok = lambda w: pl.BlockSpec((1, chunk, w), lambda b, j, *_: (b, j, 0))
    per_b = lambda shp: pl.BlockSpec((1,) + shp, lambda b, j, *_: (b,) + (0,) * len(shp))
    grid_spec = pltpu.PrefetchScalarGridSpec(
        num_scalar_prefetch=2,
        grid=(B, L // chunk),
        in_specs=[tok(ML_QK_WIDTH), tok(ML_QK_WIDTH), tok(ML_V_WIDTH),
                  pl.BlockSpec((1, 16, chunk), lambda b, j, *_: (b, 0, j)), tok(ML_V_WIDTH),
                  pl.BlockSpec((N_ML_HEADS, ML_V_DIM), lambda b, j, *_: (0, 0))],
        out_specs=(tok(ML_V_WIDTH), per_b((N_ML_HEADS, ML_V_DIM, ML_QK_DIM)),
                   per_b((N_ML_HEADS, ML_QK_DIM)), per_b((N_ML_HEADS, LANES))),
        scratch_shapes=[pltpu.VMEM((N_ML_HEADS, ML_QK_DIM, ML_V_DIM + LANES), F32),
                        pltpu.VMEM((N_ML_HEADS, 1, LANES), F32)],
    )
    return pl.pallas_call(
        functools.partial(_mlstm_kernel, chunk=chunk),
        grid_spec=grid_spec,
        out_shape=(jax.ShapeDtypeStruct((B, L, ML_V_WIDTH), BF16),
                   jax.ShapeDtypeStruct((B, N_ML_HEADS, ML_V_DIM, ML_QK_DIM), F32),
                   jax.ShapeDtypeStruct((B, N_ML_HEADS, ML_QK_DIM), F32),
                   jax.ShapeDtypeStruct((B, N_ML_HEADS, LANES), F32)),
        compiler_params=pltpu.CompilerParams(dimension_semantics=("parallel", "arbitrary"),
                                             vmem_limit_bytes=VMEM_LIMIT),
        name="mlstm_prompt",
    )(pw["b_i"], pw["b_f"], mq, mk, mv, grow, mo, pw["g_ml_out"])


def _post_kernel(x_ref, att_ref, ml_ref, ga_ref, gm_ref, pe_ref,
                 watt_ref, wml_ref, wout_ref, gffn_ref, wgate_ref, wup_ref, wdown_ref,
                 wple_ref, gple_ref, wpg_ref, y_ref):
    mix = (jax.nn.sigmoid(ga_ref[0]) * _dot(att_ref[0].astype(BF16), watt_ref[...])
           + jax.nn.sigmoid(gm_ref[0]) * _dot(ml_ref[0], wml_ref[...]))
    h = x_ref[0] + _dot(mix.astype(BF16), wout_ref[...])
    hb = _rms(h, gffn_ref[...]).astype(BF16)
    act = jax.nn.silu(_dot(hb, wgate_ref[...])) * _dot(hb, wup_ref[...])
    h = h + _dot(act.astype(BF16), wdown_ref[...])
    gate = jax.nn.sigmoid(_dot(_rms(h, gple_ref[...]).astype(BF16), wpg_ref[...]))
    y_ref[0] = h + gate * _dot(pe_ref[0].astype(BF16), wple_ref[...])


def _post(x, att, ml, ga, gm, pe, pw, tm):
    B, L, _ = x.shape
    tok = lambda w: pl.BlockSpec((1, tm, w), lambda b, i: (b, i, 0))
    full = lambda a: pl.BlockSpec(a.shape, lambda b, i: (0,) * a.ndim, pipeline_mode=pl.Buffered(1))
    consts = (pw["w_att"], pw["w_ml"], pw["w_out"], pw["g_ffn"], pw["w_ffn_gate"], pw["w_ffn_up"],
              pw["w_ffn_down"], pw["w_ple"], pw["g_ple"], pw["w_ple_gate"])
    return pl.pallas_call(
        _post_kernel,
        grid=(B, L // tm),
        in_specs=[tok(D_MODEL), tok(ATT_WIDTH), tok(ML_V_WIDTH), tok(D_MODEL), tok(D_MODEL),
                  tok(pe.shape[-1])] + [full(a) for a in consts],
        out_specs=tok(D_MODEL),
        out_shape=jax.ShapeDtypeStruct((B, L, D_MODEL), F32),
        compiler_params=pltpu.CompilerParams(dimension_semantics=("parallel", "parallel"),
                                             vmem_limit_bytes=VMEM_LIMIT),
        name="merge_ffn",
    )(x, att, ml, ga, gm, pe, *consts)


def _sample_select_kernel(pt_ref, *refs, nblk, blocks_per_step):
    page_refs = refs[:2 * blocks_per_step]
    q_ref, sel_ref, score_ref, qbd_ref, gate_ref = refs[2 * blocks_per_step:]
    j = pl.program_id(1)
    lane = lax.broadcasted_iota(jnp.int32, (N_ATT_HEADS, LANES), 1)

    @pl.when(j == 0)
    def _():
        head = lax.broadcasted_iota(jnp.int32, (N_ATT_HEADS, ATT_WIDTH), 0)
        col = lax.broadcasted_iota(jnp.int32, (N_ATT_HEADS, ATT_WIDTH), 1)
        in_head = (col >= head * ATT_HEAD_DIM) & (col < (head + 1) * ATT_HEAD_DIM)
        qbd = jnp.where(in_head, q_ref[0], 0.0)
        hi = qbd.astype(BF16).astype(F32)
        qbd_ref[...] = jnp.concatenate([hi, qbd - hi], axis=0).astype(BF16)
        gate_ref[...] = jnp.zeros_like(gate_ref)

    qbd = qbd_ref[...]
    gate = gate_ref[...]
    for i in range(blocks_per_step):
        block = j * blocks_per_step + i
        s_pages = []
        for half in range(2):
            s = _dot(qbd, page_refs[2 * i + half][0].astype(BF16))
            s = s[0:N_ATT_HEADS] + s[N_ATT_HEADS:]
            score_ref[0, 2 * block + half] = s
            s_pages.append(s)
        g = jnp.sum(s_pages[0] + s_pages[1], axis=1, keepdims=True) * (1.0 / MOBA_BLOCK)
        gate = jnp.where(lane == block, g, gate)
    gate_ref[...] = gate

    @pl.when(j == pl.num_programs(1) - 1)
    def _():
        gate = gate_ref[...]
        rank = jnp.zeros((N_ATT_HEADS, LANES), F32)
        for n2 in range(nblk):
            col = gate[:, n2:n2 + 1]
            beats = (col > gate) | ((col == gate) & (lane > n2))
            rank = rank + jnp.where(beats, 1.0, 0.0)
        out = jnp.zeros((N_ATT_HEADS, LANES), jnp.int32)
        lane_f = lane.astype(F32)
        for r in range(MOBA_TOPK):
            pick = (rank == float(r)) & (lane < nblk)
            idx = jnp.sum(jnp.where(pick, lane_f, 0.0), axis=1, keepdims=True)
            out = jnp.where(lane == r, idx.astype(jnp.int32), out)
        sel_ref[0] = out


def _sample_select(page_table, cache_kt, q_s):
    Bs, n_pages = page_table.shape
    nblk = n_pages * PAGE_SIZE // MOBA_BLOCK
    assert MOBA_BLOCK == 2 * PAGE_SIZE and nblk <= LANES
    bps = SELECT_BLOCKS_PER_STEP
    assert nblk % bps == 0
    pages_per_step = 2 * bps

    def page_spec(p):
        return pl.BlockSpec((1, ATT_WIDTH, PAGE_SIZE), lambda b, j, pt: (pt[b, j * pages_per_step + p], 0, 0))

    grid_spec = pltpu.PrefetchScalarGridSpec(
        num_scalar_prefetch=1,
        grid=(Bs, nblk // bps),
        in_specs=[page_spec(p) for p in range(pages_per_step)]
                 + [pl.BlockSpec((1, 1, ATT_WIDTH), lambda b, j, pt: (b, 0, 0))],
        out_specs=(pl.BlockSpec((1, N_ATT_HEADS, LANES), lambda b, j, pt: (b, 0, 0)),
                   pl.BlockSpec((1, n_pages, N_ATT_HEADS, PAGE_SIZE), lambda b, j, pt: (b, 0, 0, 0))),
        scratch_shapes=[pltpu.VMEM((2 * N_ATT_HEADS, ATT_WIDTH), BF16),
                        pltpu.VMEM((N_ATT_HEADS, LANES), F32)],
    )
    return pl.pallas_call(
        functools.partial(_sample_select_kernel, nblk=nblk, blocks_per_step=bps),
        grid_spec=grid_spec,
        out_shape=(jax.ShapeDtypeStruct((Bs, N_ATT_HEADS, LANES), jnp.int32),
                   jax.ShapeDtypeStruct((Bs, n_pages, N_ATT_HEADS, PAGE_SIZE), F32)),
        compiler_params=pltpu.CompilerParams(dimension_semantics=("parallel", "arbitrary"),
                                             vmem_limit_bytes=VMEM_LIMIT),
        name="moba_sample_select",
    )(page_table, *([cache_kt] * pages_per_step), q_s)


def _sample_attn_kernel(pool_ref, local_ref, *refs, n_pages, heads):
    v_refs = refs[:heads * n_pages]
    score_ref, q_ref, kn_ref, vn_ref, o_ref = refs[heads * n_pages:]
    b = pl.program_id(0)
    group = pl.program_id(1)
    scale = ATT_HEAD_DIM ** -0.5
    rows = 16
    outs = []
    for hh in range(heads):
        h = group * heads + hh
        lanes = slice(hh * ATT_HEAD_DIM, (hh + 1) * ATT_HEAD_DIM)
        s_own = jnp.sum(q_ref[0][:, lanes] * kn_ref[0][:, lanes], axis=1, keepdims=True) * scale
        pages = [local_ref[b, h * n_pages + j] for j in range(n_pages)]
        s = jnp.concatenate([score_ref[0, pg, pl.ds(h, 1), :] for pg in pages], axis=1) * scale
        v_all = jnp.concatenate([v_refs[hh * n_pages + j][0].astype(BF16) for j in range(n_pages)], axis=1)
        m = jnp.maximum(s_own, jnp.max(s, axis=1, keepdims=True))
        p_own = jnp.exp(s_own - m)
        p = jnp.exp(s - m)
        l = p_own + jnp.sum(p, axis=1, keepdims=True)
        pv = _dot_nt(jnp.broadcast_to(p, (rows, n_pages * PAGE_SIZE)).astype(BF16), v_all)[0:1]
        outs.append((p_own * vn_ref[0][:, lanes] + pv) / l)
    o_ref[0] = jnp.concatenate(outs, axis=1)


def _sample_attention(page_table, sel, scores, cache_vt, q_s, k_s, v_s):
    Bs, n_seq_pages = page_table.shape
    ppb = MOBA_BLOCK // PAGE_SIZE
    n_pages = MOBA_TOPK * ppb
    heads = SAMPLE_HEADS_PER_STEP
    local = (ppb * sel[..., None] + jnp.arange(ppb, dtype=jnp.int32)).reshape(Bs, N_ATT_HEADS * n_pages)
    pool = jnp.take_along_axis(page_table, local, axis=1)

    def tile_spec(hh, j):
        def index(b, group, pool_ref, local_ref):
            h = group * heads + hh
            return (pool_ref[b, h * n_pages + j], h, 0)
        return pl.BlockSpec((1, ATT_HEAD_DIM, PAGE_SIZE), index)

    tiles = [tile_spec(hh, j) for hh in range(heads) for j in range(n_pages)]
    row = pl.BlockSpec((1, 1, heads * ATT_HEAD_DIM), lambda b, group, pt, sl: (b, 0, group))
    grid_spec = pltpu.PrefetchScalarGridSpec(
        num_scalar_prefetch=2,
        grid=(Bs, N_ATT_HEADS // heads),
        in_specs=tiles + [pl.BlockSpec((1, n_seq_pages, N_ATT_HEADS, PAGE_SIZE),
                                       lambda b, group, pt, sl: (b, 0, 0, 0)), row, row, row],
        out_specs=row,
    )
    return pl.pallas_call(
        functools.partial(_sample_attn_kernel, n_pages=n_pages, heads=heads),
        grid_spec=grid_spec,
        out_shape=jax.ShapeDtypeStruct((Bs, 1, ATT_WIDTH), F32),
        compiler_params=pltpu.CompilerParams(dimension_semantics=("parallel", "arbitrary"),
                                             vmem_limit_bytes=VMEM_LIMIT),
        name="moba_sample_attn",
    )(pool, local, *([cache_vt] * len(tiles)), scores, q_s, k_s, v_s)


def _mlstm_step_kernel(bi_ref, bf_ref, m0_ref, mq_ref, mk_ref, mv_ref, g_ref, mo_ref, gout_ref,
                       c0_ref, n0_ref, ml_ref, c_out_ref, n_out_ref, m_out_ref):
    b = pl.program_id(0)
    dv = ML_V_DIM
    r_i = lax.broadcasted_iota(jnp.int32, (dv, dv), 0)
    c_i = lax.broadcasted_iota(jnp.int32, (dv, dv), 1)
    eye = r_i == c_i
    g = g_ref[0]
    for h in range(N_ML_HEADS):
        i_g = g[:, h:h + 1] + bi_ref[h]
        logf = _log_sigmoid(g[:, N_ML_HEADS + h:N_ML_HEADS + h + 1] + bf_ref[h])
        m_old = m0_ref[b, h]
        inter = logf + m_old
        m_t = jnp.maximum(inter, i_g)
        w_inter = jnp.exp(inter - m_t)
        w_in = jnp.exp(i_g - m_t)
        q = mq_ref[0, :, h * ML_QK_DIM:(h + 1) * ML_QK_DIM].astype(F32)
        k = mk_ref[0, :, h * ML_QK_DIM:(h + 1) * ML_QK_DIM].astype(F32)
        v = mv_ref[0, :, h * dv:(h + 1) * dv].astype(F32)
        c_old = c0_ref[0, 0, h]
        n_old = n0_ref[0, 0, h:h + 1, :]
        s = jnp.sum(q * k, axis=1, keepdims=True) * w_in
        v_col = jnp.sum(jnp.where(eye, v, 0.0), axis=1, keepdims=True)
        num = w_inter * jnp.sum(c_old * q, axis=1, keepdims=True) + s * v_col
        den = w_inter * jnp.sum(n_old * q, axis=1, keepdims=True) + s
        hid_col = num / jnp.maximum(jnp.abs(den), jnp.exp(-m_t))
        hid = jnp.sum(jnp.where(eye, hid_col, 0.0), axis=0, keepdims=True)
        c_out_ref[0, 0, h] = w_inter * c_old + (w_in * v_col) * k
        n_out_ref[0, 0, h:h + 1, :] = w_inter * n_old + w_in * k
        m_out_ref[0, h:h + 1, :] = jnp.broadcast_to(m_t, (1, LANES))
        hn = _rms(hid, gout_ref[h:h + 1, :])
        mo = mo_ref[0, :, h * dv:(h + 1) * dv]
        ml_ref[0, :, h * dv:(h + 1) * dv] = (hn * jax.nn.sigmoid(mo)).astype(BF16)


def _mlstm_sample(mq, mk, mv, gcol, mo, state_c, state_n, state_m, pw):
    Bs = mq.shape[0]
    row = lambda w: pl.BlockSpec((1, 1, w), lambda b, *_: (b, 0, 0))
    grid_spec = pltpu.PrefetchScalarGridSpec(
        num_scalar_prefetch=3,
        grid=(Bs,),
        in_specs=[row(ML_QK_WIDTH), row(ML_QK_WIDTH), row(ML_V_WIDTH), row(LANES), row(ML_V_WIDTH),
                  pl.BlockSpec((N_ML_HEADS, ML_V_DIM), lambda b, *_: (0, 0)),
                  pl.BlockSpec((1, 1, N_ML_HEADS, ML_V_DIM, ML_QK_DIM), lambda b, *_: (0, b, 0, 0, 0)),
                  pl.BlockSpec((1, 1, N_ML_HEADS, ML_QK_DIM), lambda b, *_: (0, b, 0, 0))],
        out_specs=(row(ML_V_WIDTH),
                   pl.BlockSpec((1, 1, N_ML_HEADS, ML_V_DIM, ML_QK_DIM), lambda b, *_: (0, b, 0, 0, 0)),
                   pl.BlockSpec((1, 1, N_ML_HEADS, ML_QK_DIM), lambda b, *_: (0, b, 0, 0)),
                   pl.BlockSpec((1, N_ML_HEADS, LANES), lambda b, *_: (b, 0, 0))),
    )
    return pl.pallas_call(
        _mlstm_step_kernel,
        grid_spec=grid_spec,
        out_shape=(jax.ShapeDtypeStruct((Bs, 1, ML_V_WIDTH), BF16),
                   jax.ShapeDtypeStruct(state_c.shape, F32),
                   jax.ShapeDtypeStruct(state_n.shape, F32),
                   jax.ShapeDtypeStruct((Bs, N_ML_HEADS, LANES), F32)),
        compiler_params=pltpu.CompilerParams(dimension_semantics=("arbitrary",),
                                             vmem_limit_bytes=VMEM_LIMIT),
        name="mlstm_sample",
    )(pw["b_i"], pw["b_f"], state_m[0], mq, mk, mv, gcol, mo, pw["g_ml_out"], state_c, state_n)


def _rope_angles(pos):
    freqs = ROPE_THETA ** (-jnp.arange(0, ROPE_DIMS, 2, dtype=F32) / ROPE_DIMS)
    return pos.astype(F32)[:, None] * freqs[None, :]


def _rope_tables_cols(pos):
    ang = _rope_angles(pos).T
    return jnp.cos(ang), jnp.sin(ang)


def _rope_tables_rows(pos):
    half = ROPE_DIMS // 2
    n = pos.shape[0]
    ang = _rope_angles(pos)
    cos, sin = jnp.cos(ang), jnp.sin(ang)
    rest = ATT_HEAD_DIM - ROPE_DIMS
    zh = jnp.zeros((n, half), F32)
    cos_h = jnp.concatenate([cos, cos, jnp.ones((n, rest), F32)], axis=-1)
    sa_h = jnp.concatenate([-sin, zh, jnp.zeros((n, rest), F32)], axis=-1)
    sb_h = jnp.concatenate([zh, sin, jnp.zeros((n, rest), F32)], axis=-1)
    tile = lambda t: jnp.tile(t, (1, HEAD_PAIR))
    return tile(cos_h), tile(sa_h), tile(sb_h)


def _prepare_weights(w_in, b_igate, b_fgate, g_mix_norm, g_q_norm, g_k_norm, g_ml_out_norm,
                     w_att_branch, w_ml_branch, w_out, g_ffn_norm, w_ffn_gate, w_ffn_up, w_ffn_down,
                     w_ple, g_ple_norm, w_ple_gate):
    w_gates = w_in[:, _C_GATES:_C_MO]
    head_of = jnp.arange(ATT_WIDTH, dtype=jnp.int32) // ATT_HEAD_DIM
    w_qkv = w_in[:, _C_Q:_C_MQ].astype(BF16)
    return {
        "w_qkv": w_qkv,
        "w_qkv_t": w_qkv.T,
        "w_m": w_in[:, _C_MQ:_C_GATES].astype(BF16),
        "g_q_col": g_q_norm.reshape(1, ATT_HEAD_DIM, 1),
        "g_k_col": g_k_norm.reshape(1, ATT_HEAD_DIM, 1),
        "w_g": jnp.pad(w_gates, ((0, 0), (0, LANES - 2 * N_ML_HEADS))).astype(BF16),
        "w_gt": jnp.pad(w_gates.T, ((0, 16 - 2 * N_ML_HEADS), (0, 0))).astype(BF16),
        "w_b": w_in[:, _C_MO:_C_END].astype(BF16),
        "e_head": (head_of[:, None] == head_of[None, :]).astype(BF16),
        "g_mix": g_mix_norm.reshape(1, D_MODEL),
        "g_q": jnp.tile(g_q_norm, N_ATT_HEADS).reshape(1, ATT_WIDTH),
        "g_k": jnp.tile(g_k_norm, N_ATT_HEADS).reshape(1, ATT_WIDTH),
        "g_ml_out": g_ml_out_norm,
        "b_i": b_igate, "b_f": b_fgate,
        "w_att": w_att_branch.astype(BF16), "w_ml": w_ml_branch.astype(BF16), "w_out": w_out.astype(BF16),
        "g_ffn": g_ffn_norm.reshape(1, D_MODEL),
        "w_ffn_gate": w_ffn_gate.astype(BF16), "w_ffn_up": w_ffn_up.astype(BF16),
        "w_ffn_down": w_ffn_down.astype(BF16),
        "w_ple": w_ple.astype(BF16), "g_ple": g_ple_norm.reshape(1, D_MODEL),
        "w_ple_gate": w_ple_gate.astype(BF16),
    }


def _prompt_layer(x, pe, pw):
    B, L, _ = x.shape
    pos = jnp.arange(L, dtype=jnp.int32)
    qt, kt, vt, mq, mk, mv, gcol, grow, mo, ga, gm = _project(x, pos, pw, tm=512, feature_major=True)
    att = _attention_prompt(qt, kt, vt)
    ml, c_fin, n_fin, m_pad = _mlstm_prompt(mq, mk, mv, grow, mo, pw)
    y = _post(x, att, ml, ga, gm, pe, pw, tm=256)
    tokens_major = lambda t: jnp.transpose(t.reshape(B, N_ATT_HEADS, ATT_HEAD_DIM, L), (0, 3, 1, 2))
    return y, tokens_major(kt), tokens_major(vt), c_fin, n_fin, m_pad[:, :, 0]


def _sample_layer(x, pe, cache_k, cache_v, page_table, state_c, state_n, state_m, pw):
    Bs, Ls, _ = x.shape
    assert Ls == 1
    past_len = page_table.shape[1] * PAGE_SIZE
    n_pool = cache_k.shape[0]
    pos = jnp.full((Bs,), past_len, dtype=jnp.int32)
    xt = x.reshape(1, Bs, D_MODEL)
    q, k, v, mq, mk, mv, gcol, _, mo, ga, gm = _project(xt, pos, pw, tm=Bs, feature_major=False)
    per_seq = lambda t: t.reshape(Bs, 1, t.shape[-1])
    pages_t = lambda c: jnp.transpose(c, (0, 2, 3, 1)).reshape(n_pool, ATT_WIDTH, PAGE_SIZE)
    cache_kt, cache_vt = pages_t(cache_k), pages_t(cache_v)
    sel, scores = _sample_select(page_table, cache_kt, per_seq(q))
    att = _sample_attention(page_table, sel[:, :, :MOBA_TOPK], scores, cache_vt,
                            per_seq(q), per_seq(k), per_seq(v))
    ml, c_new, n_new, m_pad = _mlstm_sample(per_seq(mq), per_seq(mk), per_seq(mv), per_seq(gcol),
                                            per_seq(mo), state_c, state_n, state_m, pw)
    y = _post(xt, att.reshape(1, Bs, ATT_WIDTH), ml.reshape(1, Bs, ML_V_WIDTH), ga, gm,
              pe.reshape(1, Bs, pe.shape[-1]), pw, tm=Bs)
    return y.reshape(Bs, 1, D_MODEL), k, v, c_new, n_new, m_pad[:, :, 0]


def kernel(x_prompt, x_sample, cache_k, cache_v, state_mlstm_C, state_mlstm_n, state_mlstm_m, page_table,
           p_prompt, p_sample, w_in, b_igate, b_fgate, g_mix_norm, g_q_norm, g_k_norm, g_ml_out_norm,
           w_att_branch, w_ml_branch, w_out, g_ffn_norm, w_ffn_gate, w_ffn_up, w_ffn_down,
           w_ple, g_ple_norm, w_ple_gate):
    depth = w_in.shape[0]
    assert depth == 1
    Bp, Lp, _ = x_prompt.shape
    Bs, Ls, _ = x_sample.shape
    pw = _prepare_weights(w_in[0], b_igate[0], b_fgate[0], g_mix_norm[0], g_q_norm[0], g_k_norm[0],
                          g_ml_out_norm[0], w_att_branch[0], w_ml_branch[0], w_out[0], g_ffn_norm[0],
                          w_ffn_gate[0], w_ffn_up[0], w_ffn_down[0], w_ple[0], g_ple_norm[0], w_ple_gate[0])
    yp, kp, vp, cp, n_p, mp = _prompt_layer(x_prompt, p_prompt[0], pw)
    ys, ks, vs, cs, ns, ms = _sample_layer(x_sample, p_sample[0], cache_k[0], cache_v[0], page_table,
                                           state_mlstm_C, state_mlstm_n, state_mlstm_m, pw)
    heads = lambda t: t.reshape(1, Bs, Ls, N_ATT_HEADS, ATT_HEAD_DIM)
    return (yp, ys, kp[None], vp[None], heads(ks), heads(vs),
            cp[None], n_p[None], mp[None], cs, ns, ms[None])
```

```python
import functools

import jax
import jax.numpy as jnp
from jax import lax
from jax.experimental import pallas as pl
from jax.experimental.pallas import tpu as pltpu

F32 = jnp.float32
BF16 = jnp.bfloat16

D_MODEL = 1024
N_ATT_HEADS = 8
ATT_HEAD_DIM = 64
ATT_WIDTH = N_ATT_HEADS * ATT_HEAD_DIM
ROPE_DIMS = ATT_HEAD_DIM // 4
ROPE_THETA = 500000.0
MOBA_BLOCK = 256
MOBA_TOPK = 3
PAGE_SIZE = 128
N_ML_HEADS = 4
ML_QK_DIM = 128
ML_V_DIM = 256
ML_QK_WIDTH = N_ML_HEADS * ML_QK_DIM
ML_V_WIDTH = N_ML_HEADS * ML_V_DIM
NORM_EPS = 1e-6
NEG_INF = -1e30
LOG2_E = 1.4426950408889634

LANES = 128
HEAD_PAIR = LANES // ATT_HEAD_DIM
ML_CHUNK = 256
SAMPLE_HEADS_PER_STEP = 4
VMEM_LIMIT = 56 * 1024 * 1024

_C_Q, _C_K, _C_V = 0, ATT_WIDTH, 2 * ATT_WIDTH
_C_MQ = 3 * ATT_WIDTH
_C_MK = _C_MQ + ML_QK_WIDTH
_C_MV = _C_MK + ML_QK_WIDTH
_C_GATES = _C_MV + ML_V_WIDTH
_C_MO = _C_GATES + 2 * N_ML_HEADS
_C_GA = _C_MO + ML_V_WIDTH
_C_GM = _C_GA + D_MODEL
_C_END = _C_GM + D_MODEL


def _dot(a, b):
    return jnp.dot(a, b, preferred_element_type=F32)


def _dot_nt(a, b, precision=None):
    return lax.dot_general(a, b, (((1,), (1,)), ((), ())), precision=precision,
                           preferred_element_type=F32)


def _rms(x, g):
    return x * lax.rsqrt(jnp.mean(x * x, axis=-1, keepdims=True) + NORM_EPS) * g


def _log_sigmoid(x):
    return jnp.minimum(x, 0.0) - jnp.log1p(jnp.exp(-jnp.abs(x)))


def _proj_shared(xb, wm_ref, wg_ref, wgt_ref, wb_ref,
                 mq_ref, mk_ref, mv_ref, gcol_ref, grow_ref, mo_ref, ga_ref, gm_ref):
    mq_ref[0] = _dot(xb, wm_ref[:, 0:ML_QK_WIDTH]).astype(BF16)
    mk_ref[0] = (_dot(xb, wm_ref[:, ML_QK_WIDTH:2 * ML_QK_WIDTH]) * (ML_QK_DIM ** -0.5)).astype(BF16)
    mv_ref[0] = _dot(xb, wm_ref[:, 2 * ML_QK_WIDTH:2 * ML_QK_WIDTH + ML_V_WIDTH]).astype(BF16)
    gcol_ref[0] = _dot(xb, wg_ref[...])
    grow_ref[0] = _dot_nt(wgt_ref[...], xb)
    mo_ref[0] = _dot(xb, wb_ref[:, 0:ML_V_WIDTH])
    ga_ref[0] = _dot(xb, wb_ref[:, ML_V_WIDTH:ML_V_WIDTH + D_MODEL])
    gm_ref[0] = _dot(xb, wb_ref[:, ML_V_WIDTH + D_MODEL:ML_V_WIDTH + 2 * D_MODEL])


def _proj_kernel_rows(x_ref, cos_ref, sa_ref, sb_ref, gmix_ref, gq_ref, gk_ref, e_ref, wqkv_ref,
                      wm_ref, wg_ref, wgt_ref, wb_ref, q_ref, k_ref, v_ref, *rest_refs):
    xb = _rms(x_ref[0], gmix_ref[...]).astype(BF16)
    cos, sa, sb = cos_ref[...], sa_ref[...], sb_ref[...]
    e = e_ref[...]

    def head_norm_rope(z, g):
        zz = z * z
        hi = zz.astype(BF16)
        lo = (zz - hi.astype(F32)).astype(BF16)
        ss = _dot(hi, e) + _dot(lo, e)
        zn = z * lax.rsqrt(ss * (1.0 / ATT_HEAD_DIM) + NORM_EPS) * g
        outs = []
        for grp in range(ATT_WIDTH // LANES):
            zg = zn[:, grp * LANES:(grp + 1) * LANES]
            outs.append(zg * cos + pltpu.roll(zg, LANES - ROPE_DIMS // 2, 1) * sa
                        + pltpu.roll(zg, ROPE_DIMS // 2, 1) * sb)
        return jnp.concatenate(outs, axis=-1)

    q_ref[0] = head_norm_rope(_dot(xb, wqkv_ref[:, 0:ATT_WIDTH]), gq_ref[...])
    k_ref[0] = head_norm_rope(_dot(xb, wqkv_ref[:, ATT_WIDTH:2 * ATT_WIDTH]), gk_ref[...])
    v_ref[0] = _dot(xb, wqkv_ref[:, 2 * ATT_WIDTH:3 * ATT_WIDTH])
    _proj_shared(xb, wm_ref, wg_ref, wgt_ref, wb_ref, *rest_refs)


def _proj_kernel_cols(x_ref, cos_ref, sin_ref, gmix_ref, gq_ref, gk_ref, wqkvt_ref,
                      wm_ref, wg_ref, wgt_ref, wb_ref, qt_ref, kt_ref, vt_ref, *rest_refs):
    xb = _rms(x_ref[0], gmix_ref[...]).astype(BF16)
    tm = xb.shape[0]
    zt = _dot_nt(wqkvt_ref[...], xb)
    cos, sin = cos_ref[...][None], sin_ref[...][None]
    half = ROPE_DIMS // 2

    def head_norm_rope(z, g):
        z3 = z.reshape(N_ATT_HEADS, ATT_HEAD_DIM, tm)
        ss = jnp.sum(z3 * z3, axis=1, keepdims=True)
        zn = z3 * lax.rsqrt(ss * (1.0 / ATT_HEAD_DIM) + NORM_EPS) * g
        x1, x2 = zn[:, 0:half, :], zn[:, half:ROPE_DIMS, :]
        out = jnp.concatenate([x1 * cos - x2 * sin, x2 * cos + x1 * sin, zn[:, ROPE_DIMS:, :]], axis=1)
        return out.reshape(ATT_WIDTH, tm)

    qt_ref[0] = head_norm_rope(zt[0:ATT_WIDTH], gq_ref[...])
    kt_ref[0] = head_norm_rope(zt[ATT_WIDTH:2 * ATT_WIDTH], gk_ref[...])
    vt_ref[0] = zt[2 * ATT_WIDTH:3 * ATT_WIDTH]
    _proj_shared(xb, wm_ref, wg_ref, wgt_ref, wb_ref, *rest_refs)


def _project(x, pos, pw, tm, feature_major):
    B, L, _ = x.shape
    tok = lambda w: pl.BlockSpec((1, tm, w), lambda b, i: (b, i, 0))
    full = lambda a: pl.BlockSpec(a.shape, lambda b, i: (0,) * a.ndim)
    shared = (pw["w_m"], pw["w_g"], pw["w_gt"], pw["w_b"])
    rest_shape = (
        jax.ShapeDtypeStruct((B, L, ML_QK_WIDTH), BF16),
        jax.ShapeDtypeStruct((B, L, ML_QK_WIDTH), BF16),
        jax.ShapeDtypeStruct((B, L, ML_V_WIDTH), BF16),
        jax.ShapeDtypeStruct((B, L, LANES), F32),
        jax.ShapeDtypeStruct((B, 16, L), F32),
        jax.ShapeDtypeStruct((B, L, ML_V_WIDTH), F32),
        jax.ShapeDtypeStruct((B, L, D_MODEL), F32),
        jax.ShapeDtypeStruct((B, L, D_MODEL), F32),
    )
    rest_specs = (tok(ML_QK_WIDTH), tok(ML_QK_WIDTH), tok(ML_V_WIDTH), tok(LANES),
                  pl.BlockSpec((1, 16, tm), lambda b, i: (b, 0, i)),
                  tok(ML_V_WIDTH), tok(D_MODEL), tok(D_MODEL))
    if feature_major:
        cos, sin = _rope_tables_cols(pos)
        tab = pl.BlockSpec((ROPE_DIMS // 2, tm), lambda b, i: (0, i))
        consts = (pw["g_mix"], pw["g_q_col"], pw["g_k_col"], pw["w_qkv_t"]) + shared
        body, tables, tab_specs = _proj_kernel_cols, (cos, sin), [tab, tab]
        qkv_shape = (jax.ShapeDtypeStruct((B, ATT_WIDTH, L), F32),) * 3
        qkv_specs = (pl.BlockSpec((1, ATT_WIDTH, tm), lambda b, i: (b, 0, i)),) * 3
    else:
        tables = _rope_tables_rows(pos)
        tab = pl.BlockSpec((tm, LANES), lambda b, i: (i, 0))
        consts = (pw["g_mix"], pw["g_q"], pw["g_k"], pw["e_head"], pw["w_qkv"]) + shared
        body, tab_specs = _proj_kernel_rows, [tab, tab, tab]
        qkv_shape = (jax.ShapeDtypeStruct((B, L, ATT_WIDTH), F32),) * 3
        qkv_specs = (tok(ATT_WIDTH),) * 3
    return pl.pallas_call(
        body,
        grid=(B, L // tm),
        in_specs=[tok(D_MODEL)] + tab_specs + [full(a) for a in consts],
        out_specs=qkv_specs + rest_specs,
        out_shape=qkv_shape + rest_shape,
        compiler_params=pltpu.CompilerParams(dimension_semantics=("parallel", "parallel"),
                                             vmem_limit_bytes=VMEM_LIMIT),
        name="in_proj_cols" if feature_major else "in_proj_rows",
    )(x, *tables, *consts)


def _gate_pages(part, page_refs, qbd_ref, score_ref, gate_ref):
    blocks_per_step = len(page_refs) // 2
    lane = lax.broadcasted_iota(jnp.int32, (N_ATT_HEADS, LANES), 1)
    qbd = qbd_ref[0]
    gate = gate_ref[...]
    for i in range(blocks_per_step):
        keys = jnp.concatenate([page_refs[2 * i][0].astype(BF16), page_refs[2 * i + 1][0].astype(BF16)], axis=1)
        s = _dot(qbd, keys)
        s = s[0:N_ATT_HEADS] + s[N_ATT_HEADS:]
        score_ref[0, 2 * i] = s[:, 0:PAGE_SIZE]
        score_ref[0, 2 * i + 1] = s[:, PAGE_SIZE:]
        g = jnp.sum(s, axis=1, keepdims=True) * (1.0 / MOBA_BLOCK)
        gate = jnp.where(lane == part * blocks_per_step + i, g, gate)
    gate_ref[...] = gate


def _gate_select(part, sel_ref, gate_ref, *, steps_per_seq, nblk_seq):
    lane = lax.broadcasted_iota(jnp.int32, (N_ATT_HEADS, LANES), 1)

    @pl.when(part == steps_per_seq - 1)
    def _():
        gate = gate_ref[...]
        rank = jnp.zeros((N_ATT_HEADS, LANES), F32)
        for n2 in range(nblk_seq):
            col = gate[:, n2:n2 + 1]
            beats = (col > gate) | ((col == gate) & (lane > n2))
            rank = rank + jnp.where(beats, 1.0, 0.0)
        out = jnp.zeros((N_ATT_HEADS, LANES), jnp.int32)
        lane_f = lane.astype(F32)
        for r in range(MOBA_TOPK):
            pick = (rank == float(r)) & (lane < nblk_seq)
            idx = jnp.sum(jnp.where(pick, lane_f, 0.0), axis=1, keepdims=True)
            out = jnp.where(lane == r, idx.astype(jnp.int32), out)
        sel_ref[0] = out


def _side_job(page_table, cache_kt, q_s, n_steps, n_grid, pt_pos, linear_step):
    Bs, n_seq_pages = page_table.shape
    assert (Bs * n_seq_pages) % n_steps == 0
    pps = Bs * n_seq_pages // n_steps
    assert MOBA_BLOCK == 2 * PAGE_SIZE and pps % 2 == 0 and n_seq_pages % pps == 0
    steps_per_seq = n_seq_pages // pps
    nblk_seq = n_seq_pages * PAGE_SIZE // MOBA_BLOCK
    assert nblk_seq <= LANES
    head_of = jnp.arange(ATT_WIDTH, dtype=jnp.int32) // ATT_HEAD_DIM
    q_heads = jnp.where(head_of[None, None, :] == jnp.arange(N_ATT_HEADS, dtype=jnp.int32)[None, :, None],
                        q_s[:, None, :], 0.0)
    q_lead = q_heads.astype(BF16)
    q_rows = jnp.concatenate([q_lead, (q_heads - q_lead.astype(F32)).astype(BF16)], axis=1)

    def seq_part(args):
        step = linear_step(*args[:n_grid])
        return step // steps_per_seq, step % steps_per_seq

    def page_spec(p):
        def index(*args):
            seq, part = seq_part(args)
            return (args[n_grid + pt_pos][seq, part * pps + p], 0, 0)
        return pl.BlockSpec((1, ATT_WIDTH, PAGE_SIZE), index)

    return dict(
        in_specs=[pl.BlockSpec((1, 2 * N_ATT_HEADS, ATT_WIDTH), lambda *args: (seq_part(args)[0], 0, 0))]
                 + [page_spec(p) for p in range(pps)],
        operands=[q_rows] + [cache_kt] * pps,
        out_specs=(pl.BlockSpec((1, N_ATT_HEADS, LANES), lambda *args: (seq_part(args)[0], 0, 0)),
                   pl.BlockSpec((1, pps, N_ATT_HEADS, PAGE_SIZE), lambda *args: seq_part(args) + (0, 0))),
        out_shape=(jax.ShapeDtypeStruct((Bs, N_ATT_HEADS, LANES), jnp.int32),
                   jax.ShapeDtypeStruct((Bs, n_seq_pages, N_ATT_HEADS, PAGE_SIZE), F32)),
        scratch=[pltpu.VMEM((N_ATT_HEADS, LANES), F32)],
        static=dict(n_side_pages=pps, steps_per_seq=steps_per_seq, nblk_seq=nblk_seq),
    )


def _attn_kernel(q_ref, k_ref, v_ref, o_ref, kb_ref, vt_ref, kmean_ref, acc_ref, s0_ref, s1_ref, *, nblk):
    blk = MOBA_BLOCK
    tq = 2 * blk
    ones_rows = vt_ref.shape[1] - LANES
    t = pl.program_id(2)
    c0 = 2 * t
    nb = kmean_ref.shape[0]

    @pl.when(t == 0)
    def _():
        lane = lax.broadcasted_iota(jnp.int32, (blk, LANES), 1)
        kmean_ref[...] = jnp.zeros_like(kmean_ref)
        for n in range(nblk):
            rows = slice(n * blk, (n + 1) * blk)
            kf = k_ref[0, :, rows].T
            kb_ref[rows, 0:LANES] = kf.astype(BF16)
            kb_ref[rows, LANES:2 * LANES] = jnp.where(lane == n, 1.0, 0.0).astype(BF16)
            kmean_ref[n:n + 1, :] = jnp.sum(kf, axis=0, keepdims=True) * (1.0 / blk)
            vt_ref[n] = jnp.concatenate([v_ref[0, :, rows], jnp.ones((ones_rows, blk), F32)],
                                        axis=0).astype(BF16)

    q2 = q_ref[0]
    feat = lax.broadcasted_iota(jnp.int32, (LANES, tq), 0)
    blk_id = lax.broadcasted_iota(jnp.int32, (nb, tq), 0)
    col = lax.broadcasted_iota(jnp.int32, (nb, tq), 1)
    cur = c0 + (col >= blk).astype(jnp.int32)
    kmean = kmean_ref[...]
    bias_pad = jnp.zeros((LANES - nb, tq), F32)

    qas = []
    for h in range(HEAD_PAIR):
        qh = jnp.where((feat >= h * ATT_HEAD_DIM) & (feat < (h + 1) * ATT_HEAD_DIM), q2, 0.0)
        gate = jnp.dot(kmean, qh, precision=lax.Precision.HIGHEST,
                       preferred_element_type=F32)
        gate = jnp.where(blk_id < cur, gate, -jnp.inf)
        rank = jnp.zeros((nb, tq), F32)
        for n2 in range(nblk):
            row = gate[n2:n2 + 1, :]
            beats = (row > gate) | ((row == gate) & (blk_id > n2))
            rank = rank + jnp.where(beats, 1.0, 0.0)
        open_blk = ((blk_id < cur) & (rank < MOBA_TOPK)) | (blk_id == cur)
        bias = jnp.where(open_blk, 0.0, NEG_INF)
        qas.append(jnp.concatenate([qh * (ATT_HEAD_DIM ** -0.5 * LOG2_E), bias, bias_pad],
                                   axis=0).astype(BF16))

    def block_rows(n):
        return kb_ref[pl.ds(pl.multiple_of(n * blk, blk), blk), :]

    key_i = lax.broadcasted_iota(jnp.int32, (blk, tq), 0)
    qry_j = lax.broadcasted_iota(jnp.int32, (blk, tq), 1)
    s_refs = (s0_ref, s1_ref)

    def stage_a(first, slot, own):
        ks = [block_rows(first), block_rows(first + 1)]
        maxima = []
        for h in range(HEAD_PAIR):
            sts = [_dot(ks[0], qas[h]), _dot(ks[1], qas[h])]
            if own:
                sts = [jnp.where(key_i <= qry_j, sts[0], NEG_INF),
                       jnp.where(key_i <= qry_j - blk, sts[1], NEG_INF)]
            s_refs[slot][h, 0] = sts[0]
            s_refs[slot][h, 1] = sts[1]
            maxima.append(jnp.maximum(jnp.max(sts[0], axis=0, keepdims=True),
                                      jnp.max(sts[1], axis=0, keepdims=True)))
        return maxima

    def stage_b(first, slot, m_run, maxima):
        vts = [vt_ref[first], vt_ref[first + 1]]
        m_out = []
        for h in range(HEAD_PAIR):
            m_new = jnp.maximum(m_run[h], maxima[h])
            pv = (_dot(vts[0], jnp.exp2(s_refs[slot][h, 0] - m_new).astype(BF16))
                  + _dot(vts[1], jnp.exp2(s_refs[slot][h, 1] - m_new).astype(BF16)))
            acc_ref[h] = jnp.exp2(m_run[h] - m_new) * acc_ref[h] + pv
            m_out.append(m_new)
        return m_out

    def pending_first(j):
        return jnp.where(j == 0, c0, 2 * j - 2)

    acc_ref[...] = jnp.zeros_like(acc_ref)
    m_start = [jnp.full((1, tq), NEG_INF, F32)] * HEAD_PAIR
    carry = tuple(m_start) + tuple(stage_a(c0, 0, own=True))

    def trip(j, slot, carry):
        m_run, maxima = carry[:HEAD_PAIR], carry[HEAD_PAIR:]
        new_maxima = stage_a(2 * j, 1 - slot, own=False)
        m_new = stage_b(pending_first(j), slot, m_run, maxima)
        return tuple(m_new) + tuple(new_maxima)

    carry = lax.fori_loop(0, lax.shift_right_logical(t, 1),
                          lambda i, cr: trip(2 * i + 1, 1, trip(2 * i, 0, cr)), carry)

    def odd_tail(carry):
        carry = trip(t - 1, 0, carry)
        stage_b(pending_first(t), 1, carry[:HEAD_PAIR], carry[HEAD_PAIR:])
        return 0

    def even_tail(carry):
        stage_b(pending_first(t), 0, carry[:HEAD_PAIR], carry[HEAD_PAIR:])
        return 0

    lax.cond((t & 1) == 1, odd_tail, even_tail, carry)
    outs = [acc_ref[h, 0:LANES, :] / acc_ref[h, LANES:LANES + 1, :] for h in range(HEAD_PAIR)]
    ot = jnp.where(feat < ATT_HEAD_DIM, outs[0], outs[1])
    o_ref[0] = ot.T.astype(o_ref.dtype)


def _attention_prompt(qt, kt, vt):
    B, _, L = qt.shape
    blk = MOBA_BLOCK
    tq = 2 * blk
    assert L % tq == 0
    nblk = L // blk
    assert nblk <= LANES
    nb = -(-nblk // 8) * 8
    ones_rows = 16
    groups = ATT_WIDTH // LANES
    return pl.pallas_call(
        functools.partial(_attn_kernel, nblk=nblk),
        grid=(B, groups, L // tq),
        in_specs=[pl.BlockSpec((1, LANES, tq), lambda b, g, i: (b, g, i)),
                  pl.BlockSpec((1, LANES, L), lambda b, g, i: (b, g, 0)),
                  pl.BlockSpec((1, LANES, L), lambda b, g, i: (b, g, 0))],
        out_specs=pl.BlockSpec((1, tq, LANES), lambda b, g, i: (b, i, g)),
        out_shape=jax.ShapeDtypeStruct((B, L, ATT_WIDTH), BF16),
        scratch_shapes=[pltpu.VMEM((L, 2 * LANES), BF16),
                        pltpu.VMEM((nblk, LANES + ones_rows, blk), BF16),
                        pltpu.VMEM((nb, LANES), F32),
                        pltpu.VMEM((HEAD_PAIR, LANES + ones_rows, tq), F32),
                        pltpu.VMEM((HEAD_PAIR, 2, blk, tq), F32),
                        pltpu.VMEM((HEAD_PAIR, 2, blk, tq), F32)],
        compiler_params=pltpu.CompilerParams(dimension_semantics=("parallel", "parallel", "arbitrary"),
                                             vmem_limit_bytes=VMEM_LIMIT),
        name="moba_prompt",
    )(qt, kt, vt)


def _mlstm_kernel(bi_ref, bf_ref, pt_ref, mq_ref, mk_ref, mv_ref, grow_ref, mo_ref, gout_ref, qbd_ref, *refs,
                  chunk, n_side_pages, steps_per_seq, nblk_seq):
    page_refs = refs[:n_side_pages]
    (ml_ref, c_out_ref, n_out_ref, m_out_ref, sel_ref, score_ref, cn_ref, m_ref,
     gate_ref) = refs[n_side_pages:]
    j = pl.program_id(1)
    dk, dv = ML_QK_DIM, ML_V_DIM
    part = lax.rem(pl.program_id(0) * pl.num_programs(1) + j, steps_per_seq)

    @pl.when(j == 0)
    def _():
        cn_ref[...] = jnp.zeros_like(cn_ref)
        m_ref[...] = jnp.zeros_like(m_ref)

    @pl.when(part == 0)
    def _():
        gate_ref[...] = jnp.zeros_like(gate_ref)

    _gate_pages(part, page_refs, qbd_ref, score_ref, gate_ref)

    t_i = lax.broadcasted_iota(jnp.int32, (chunk, chunk), 0)
    s_i = lax.broadcasted_iota(jnp.int32, (chunk, chunk), 1)
    tril = s_i <= t_i
    tril_b = jnp.where(tril, 1.0, 0.0).astype(BF16)
    triu_b = jnp.where(t_i <= s_i, 1.0, 0.0).astype(BF16)
    ones_cols = jnp.ones((chunk, LANES), BF16)
    pad_rows = jnp.zeros((13, chunk), F32)
    grow = grow_ref[0]
    for h in range(N_ML_HEADS):
        i_row = grow[h:h + 1, :] + bi_ref[h]
        f_row = _log_sigmoid(grow[N_ML_HEADS + h:N_ML_HEADS + h + 1, :] + bf_ref[h])
        hi = f_row.astype(BF16).astype(F32)
        mid = (f_row - hi).astype(BF16).astype(F32)
        pieces = jnp.concatenate([hi, mid, f_row - hi - mid, pad_rows], axis=0).astype(BF16)
        rows = _dot(pieces, triu_b)
        cols = _dot_nt(tril_b, pieces)
        b_row = rows[0:1] + rows[1:2] + rows[2:3]
        b_col = cols[:, 0:1] + cols[:, 1:2] + cols[:, 2:3]
        a_row = (i_row - b_row) * LOG2_E
        m_old = m_ref[h][:, 0:1]
        a_low = jnp.where(tril, a_row, NEG_INF)
        mm_col = jnp.maximum(jnp.max(a_low, axis=1, keepdims=True), m_old * LOG2_E)
        w_inter = jnp.exp2(m_old * LOG2_E - mm_col)
        floor_col = jnp.exp2(-(b_col * LOG2_E + mm_col))
        decay = jnp.exp2(a_low - mm_col)

        q = mq_ref[0, :, h * dk:(h + 1) * dk]
        k = mk_ref[0, :, h * dk:(h + 1) * dk]
        v_ones = jnp.concatenate([mv_ref[0, :, h * dv:(h + 1) * dv], ones_cols], axis=1)
        cn = cn_ref[h]
        s = _dot_nt(q, k) * decay
        inter = _dot(q, cn.astype(BF16))
        intra = _dot(s.astype(BF16), v_ones)
        num = w_inter * inter[:, 0:dv] + intra[:, 0:dv]
        den = w_inter * inter[:, dv:dv + 1] + intra[:, dv:dv + 1]
        hid = num / jnp.maximum(jnp.abs(den), floor_col)

        mm_end = mm_col[chunk - 1:chunk, :]
        w_row = jnp.exp2(a_row - mm_end)
        kw = (k.astype(F32).T * w_row).astype(BF16)
        cn_ref[h] = jnp.exp2(m_old * LOG2_E - mm_end) * cn + _dot(kw, v_ones)
        m_ref[h] = jnp.broadcast_to(b_row[:, chunk - 1:chunk] + mm_end * (1.0 / LOG2_E), (1, LANES))

        hn = _rms(hid, gout_ref[h:h + 1, :])
        mo = mo_ref[0, :, h * dv:(h + 1) * dv]
        ml_ref[0, :, h * dv:(h + 1) * dv] = (hn * jax.nn.sigmoid(mo)).astype(BF16)

    @pl.when(j == pl.num_programs(1) - 1)
    def _():
        for h in range(N_ML_HEADS):
            c_out_ref[0, h] = cn_ref[h, :, 0:dv].T
            n_out_ref[0, h:h + 1, :] = cn_ref[h, :, dv:dv + LANES].T[0:1, :]
            m_out_ref[0, h:h + 1, :] = m_ref[h]

    _gate_select(part, sel_ref, gate_ref, steps_per_seq=steps_per_seq, nblk_seq=nblk_seq)


def _mlstm_prompt(mq, mk, mv, grow, mo, pw, page_table, cache_kt, q_s):
    B, L, _ = mq.shape
    chunk = ML_CHUNK
    n_chunks = L // chunk
    side = _side_job(page_table, cache_kt, q_s, n_steps=B * n_chunks, n_grid=2, pt_pos=2,
                     linear_step=lambda b, j: b * n_chunks + j)
    tok = lambda w: pl.BlockSpec((1, chunk, w), lambda b, j, *_: (b, j, 0))
    per_b = lambda shp: pl.BlockSpec((1,) + shp, lambda b, j, *_: (b,) + (0,) * len(shp))
    grid_spec = pltpu.PrefetchScalarGridSpec(
        num_scalar_prefetch=3,
        grid=(B, n_chunks),
        in_specs=[tok(ML_QK_WIDTH), tok(ML_QK_WIDTH), tok(ML_V_WIDTH),
                  pl.BlockSpec((1, 16, chunk), lambda b, j, *_: (b, 0, j)), tok(ML_V_WIDTH),
                  pl.BlockSpec((N_ML_HEADS, ML_V_DIM), lambda b, j, *_: (0, 0))] + side["in_specs"],
        out_specs=(tok(ML_V_WIDTH), per_b((N_ML_HEADS, ML_V_DIM, ML_QK_DIM)),
                   per_b((N_ML_HEADS, ML_QK_DIM)), per_b((N_ML_HEADS, LANES))) + side["out_specs"],
        scratch_shapes=[pltpu.VMEM((N_ML_HEADS, ML_QK_DIM, ML_V_DIM + LANES), F32),
                        pltpu.VMEM((N_ML_HEADS, 1, LANES), F32)] + side["scratch"],
    )
    return pl.pallas_call(
        functools.partial(_mlstm_kernel, chunk=chunk, **side["static"]),
        grid_spec=grid_spec,
        out_shape=(jax.ShapeDtypeStruct((B, L, ML_V_WIDTH), BF16),
                   jax.ShapeDtypeStruct((B, N_ML_HEADS, ML_V_DIM, ML_QK_DIM), F32),
                   jax.ShapeDtypeStruct((B, N_ML_HEADS, ML_QK_DIM), F32),
                   jax.ShapeDtypeStruct((B, N_ML_HEADS, LANES), F32)) + side["out_shape"],
        compiler_params=pltpu.CompilerParams(dimension_semantics=("arbitrary", "arbitrary"),
                                             vmem_limit_bytes=VMEM_LIMIT),
        name="mlstm_prompt",
    )(pw["b_i"], pw["b_f"], page_table, mq, mk, mv, grow, mo, pw["g_ml_out"], *side["operands"])


def _post_kernel(x_ref, att_ref, ml_ref, ga_ref, gm_ref, pe_ref,
                 watt_ref, wml_ref, wout_ref, gffn_ref, wgate_ref, wup_ref, wdown_ref,
                 wple_ref, gple_ref, wpg_ref, y_ref):
    mix = (jax.nn.sigmoid(ga_ref[0]) * _dot(att_ref[0].astype(BF16), watt_ref[...])
           + jax.nn.sigmoid(gm_ref[0]) * _dot(ml_ref[0], wml_ref[...]))
    h = x_ref[0] + _dot(mix.astype(BF16), wout_ref[...])
    hb = _rms(h, gffn_ref[...]).astype(BF16)
    act = jax.nn.silu(_dot(hb, wgate_ref[...])) * _dot(hb, wup_ref[...])
    h = h + _dot(act.astype(BF16), wdown_ref[...])
    gate = jax.nn.sigmoid(_dot(_rms(h, gple_ref[...]).astype(BF16), wpg_ref[...]))
    y_ref[0] = h + gate * _dot(pe_ref[0].astype(BF16), wple_ref[...])


def _post(x, att, ml, ga, gm, pe, pw, tm):
    B, L, _ = x.shape
    tok = lambda w: pl.BlockSpec((1, tm, w), lambda b, i: (b, i, 0))
    full = lambda a: pl.BlockSpec(a.shape, lambda b, i: (0,) * a.ndim, pipeline_mode=pl.Buffered(1))
    consts = (pw["w_att"], pw["w_ml"], pw["w_out"], pw["g_ffn"], pw["w_ffn_gate"], pw["w_ffn_up"],
              pw["w_ffn_down"], pw["w_ple"], pw["g_ple"], pw["w_ple_gate"])
    return pl.pallas_call(
        _post_kernel,
        grid=(B, L // tm),
        in_specs=[tok(D_MODEL), tok(ATT_WIDTH), tok(ML_V_WIDTH), tok(D_MODEL), tok(D_MODEL),
                  tok(pe.shape[-1])] + [full(a) for a in consts],
        out_specs=tok(D_MODEL),
        out_shape=jax.ShapeDtypeStruct((B, L, D_MODEL), F32),
        compiler_params=pltpu.CompilerParams(dimension_semantics=("parallel", "parallel"),
                                             vmem_limit_bytes=VMEM_LIMIT),
        name="merge_ffn",
    )(x, att, ml, ga, gm, pe, *consts)


def _sample_attn_kernel(pool_ref, local_ref, *refs, n_pages, heads):
    v_refs = refs[:heads * n_pages]
    score_ref, q_ref, kn_ref, vn_ref, o_ref = refs[heads * n_pages:]
    b = pl.program_id(0)
    group = pl.program_id(1)
    scale = ATT_HEAD_DIM ** -0.5
    rows = 16
    outs = []
    for hh in range(heads):
        h = group * heads + hh
        lanes = slice(hh * ATT_HEAD_DIM, (hh + 1) * ATT_HEAD_DIM)
        s_own = jnp.sum(q_ref[0][:, lanes] * kn_ref[0][:, lanes], axis=1, keepdims=True) * scale
        pages = [local_ref[b, h * n_pages + j] for j in range(n_pages)]
        s = jnp.concatenate([score_ref[0, pg, pl.ds(h, 1), :] for pg in pages], axis=1) * scale
        v_all = jnp.concatenate([v_refs[hh * n_pages + j][0].astype(BF16) for j in range(n_pages)], axis=1)
        m = jnp.maximum(s_own, jnp.max(s, axis=1, keepdims=True))
        p_own = jnp.exp(s_own - m)
        p = jnp.exp(s - m)
        l = p_own + jnp.sum(p, axis=1, keepdims=True)
        pv = _dot_nt(jnp.broadcast_to(p, (rows, n_pages * PAGE_SIZE)).astype(BF16), v_all)[0:1]
        outs.append((p_own * vn_ref[0][:, lanes] + pv) / l)
    o_ref[0] = jnp.concatenate(outs, axis=1)


def _sample_attention(page_table, sel, scores, cache_vt, q_s, k_s, v_s):
    Bs, n_seq_pages = page_table.shape
    ppb = MOBA_BLOCK // PAGE_SIZE
    n_pages = MOBA_TOPK * ppb
    heads = SAMPLE_HEADS_PER_STEP
    local = (ppb * sel[..., None] + jnp.arange(ppb, dtype=jnp.int32)).reshape(Bs, N_ATT_HEADS * n_pages)
    pool = jnp.take_along_axis(page_table, local, axis=1)

    def tile_spec(hh, j):
        def index(b, group, pool_ref, local_ref):
            h = group * heads + hh
            return (pool_ref[b, h * n_pages + j], h, 0)
        return pl.BlockSpec((1, ATT_HEAD_DIM, PAGE_SIZE), index)

    tiles = [tile_spec(hh, j) for hh in range(heads) for j in range(n_pages)]
    row = pl.BlockSpec((1, 1, heads * ATT_HEAD_DIM), lambda b, group, pt, sl: (b, 0, group))
    grid_spec = pltpu.PrefetchScalarGridSpec(
        num_scalar_prefetch=2,
        grid=(Bs, N_ATT_HEADS // heads),
        in_specs=tiles + [pl.BlockSpec((1, n_seq_pages, N_ATT_HEADS, PAGE_SIZE),
                                       lambda b, group, pt, sl: (b, 0, 0, 0)), row, row, row],
        out_specs=row,
    )
    return pl.pallas_call(
        functools.partial(_sample_attn_kernel, n_pages=n_pages, heads=heads),
        grid_spec=grid_spec,
        out_shape=jax.ShapeDtypeStruct((Bs, 1, ATT_WIDTH), F32),
        compiler_params=pltpu.CompilerParams(dimension_semantics=("parallel", "arbitrary"),
                                             vmem_limit_bytes=VMEM_LIMIT),
        name="moba_sample_attn",
    )(pool, local, *([cache_vt] * len(tiles)), scores, q_s, k_s, v_s)


def _mlstm_step_kernel(bi_ref, bf_ref, m0_ref, mq_ref, mk_ref, mv_ref, g_ref, mo_ref, gout_ref,
                       c0_ref, n0_ref, ml_ref, c_out_ref, n_out_ref, m_out_ref):
    b = pl.program_id(0)
    dv = ML_V_DIM
    r_i = lax.broadcasted_iota(jnp.int32, (dv, dv), 0)
    c_i = lax.broadcasted_iota(jnp.int32, (dv, dv), 1)
    eye = r_i == c_i
    g = g_ref[0]
    for h in range(N_ML_HEADS):
        i_g = g[:, h:h + 1] + bi_ref[h]
        logf = _log_sigmoid(g[:, N_ML_HEADS + h:N_ML_HEADS + h + 1] + bf_ref[h])
        m_old = m0_ref[b, h]
        inter = logf + m_old
        m_t = jnp.maximum(inter, i_g)
        w_inter = jnp.exp(inter - m_t)
        w_in = jnp.exp(i_g - m_t)
        q = mq_ref[0, :, h * ML_QK_DIM:(h + 1) * ML_QK_DIM].astype(F32)
        k = mk_ref[0, :, h * ML_QK_DIM:(h + 1) * ML_QK_DIM].astype(F32)
        v = mv_ref[0, :, h * dv:(h + 1) * dv].astype(F32)
        c_old = c0_ref[0, 0, h]
        n_old = n0_ref[0, 0, h:h + 1, :]
        s = jnp.sum(q * k, axis=1, keepdims=True) * w_in
        v_col = jnp.sum(jnp.where(eye, v, 0.0), axis=1, keepdims=True)
        num = w_inter * jnp.sum(c_old * q, axis=1, keepdims=True) + s * v_col
        den = w_inter * jnp.sum(n_old * q, axis=1, keepdims=True) + s
        hid_col = num / jnp.maximum(jnp.abs(den), jnp.exp(-m_t))
        hid = jnp.sum(jnp.where(eye, hid_col, 0.0), axis=0, keepdims=True)
        c_out_ref[0, 0, h] = w_inter * c_old + (w_in * v_col) * k
        n_out_ref[0, 0, h:h + 1, :] = w_inter * n_old + w_in * k
        m_out_ref[0, h:h + 1, :] = jnp.broadcast_to(m_t, (1, LANES))
        hn = _rms(hid, gout_ref[h:h + 1, :])
        mo = mo_ref[0, :, h * dv:(h + 1) * dv]
        ml_ref[0, :, h * dv:(h + 1) * dv] = (hn * jax.nn.sigmoid(mo)).astype(BF16)


def _mlstm_sample(mq, mk, mv, gcol, mo, state_c, state_n, state_m, pw):
    Bs = mq.shape[0]
    row = lambda w: pl.BlockSpec((1, 1, w), lambda b, *_: (b, 0, 0))
    grid_spec = pltpu.PrefetchScalarGridSpec(
        num_scalar_prefetch=3,
        grid=(Bs,),
        in_specs=[row(ML_QK_WIDTH), row(ML_QK_WIDTH), row(ML_V_WIDTH), row(LANES), row(ML_V_WIDTH),
                  pl.BlockSpec((N_ML_HEADS, ML_V_DIM), lambda b, *_: (0, 0)),
                  pl.BlockSpec((1, 1, N_ML_HEADS, ML_V_DIM, ML_QK_DIM), lambda b, *_: (0, b, 0, 0, 0)),
                  pl.BlockSpec((1, 1, N_ML_HEADS, ML_QK_DIM), lambda b, *_: (0, b, 0, 0))],
        out_specs=(row(ML_V_WIDTH),
                   pl.BlockSpec((1, 1, N_ML_HEADS, ML_V_DIM, ML_QK_DIM), lambda b, *_: (0, b, 0, 0, 0)),
                   pl.BlockSpec((1, 1, N_ML_HEADS, ML_QK_DIM), lambda b, *_: (0, b, 0, 0)),
                   pl.BlockSpec((1, N_ML_HEADS, LANES), lambda b, *_: (b, 0, 0))),
    )
    return pl.pallas_call(
        _mlstm_step_kernel,
        grid_spec=grid_spec,
        out_shape=(jax.ShapeDtypeStruct((Bs, 1, ML_V_WIDTH), BF16),
                   jax.ShapeDtypeStruct(state_c.shape, F32),
                   jax.ShapeDtypeStruct(state_n.shape, F32),
                   jax.ShapeDtypeStruct((Bs, N_ML_HEADS, LANES), F32)),
        compiler_params=pltpu.CompilerParams(dimension_semantics=("arbitrary",),
                                             vmem_limit_bytes=VMEM_LIMIT),
        name="mlstm_sample",
    )(pw["b_i"], pw["b_f"], state_m[0], mq, mk, mv, gcol, mo, pw["g_ml_out"], state_c, state_n)


def _rope_angles(pos):
    freqs = ROPE_THETA ** (-jnp.arange(0, ROPE_DIMS, 2, dtype=F32) / ROPE_DIMS)
    return pos.astype(F32)[:, None] * freqs[None, :]


def _rope_tables_cols(pos):
    ang = _rope_angles(pos).T
    return jnp.cos(ang), jnp.sin(ang)


def _rope_tables_rows(pos):
    half = ROPE_DIMS // 2
    n = pos.shape[0]
    ang = _rope_angles(pos)
    cos, sin = jnp.cos(ang), jnp.sin(ang)
    rest = ATT_HEAD_DIM - ROPE_DIMS
    zh = jnp.zeros((n, half), F32)
    cos_h = jnp.concatenate([cos, cos, jnp.ones((n, rest), F32)], axis=-1)
    sa_h = jnp.concatenate([-sin, zh, jnp.zeros((n, rest), F32)], axis=-1)
    sb_h = jnp.concatenate([zh, sin, jnp.zeros((n, rest), F32)], axis=-1)
    tile = lambda t: jnp.tile(t, (1, HEAD_PAIR))
    return tile(cos_h), tile(sa_h), tile(sb_h)


def _prepare_weights(w_in, b_igate, b_fgate, g_mix_norm, g_q_norm, g_k_norm, g_ml_out_norm,
                     w_att_branch, w_ml_branch, w_out, g_ffn_norm, w_ffn_gate, w_ffn_up, w_ffn_down,
                     w_ple, g_ple_norm, w_ple_gate):
    w_gates = w_in[:, _C_GATES:_C_MO]
    head_of = jnp.arange(ATT_WIDTH, dtype=jnp.int32) // ATT_HEAD_DIM
    w_qkv = w_in[:, _C_Q:_C_MQ].astype(BF16)
    return {
        "w_qkv": w_qkv,
        "w_qkv_t": w_qkv.T,
        "w_m": w_in[:, _C_MQ:_C_GATES].astype(BF16),
        "g_q_col": g_q_norm.reshape(1, ATT_HEAD_DIM, 1),
        "g_k_col": g_k_norm.reshape(1, ATT_HEAD_DIM, 1),
        "w_g": jnp.pad(w_gates, ((0, 0), (0, LANES - 2 * N_ML_HEADS))).astype(BF16),
        "w_gt": jnp.pad(w_gates.T, ((0, 16 - 2 * N_ML_HEADS), (0, 0))).astype(BF16),
        "w_b": w_in[:, _C_MO:_C_END].astype(BF16),
        "e_head": (head_of[:, None] == head_of[None, :]).astype(BF16),
        "g_mix": g_mix_norm.reshape(1, D_MODEL),
        "g_q": jnp.tile(g_q_norm, N_ATT_HEADS).reshape(1, ATT_WIDTH),
        "g_k": jnp.tile(g_k_norm, N_ATT_HEADS).reshape(1, ATT_WIDTH),
        "g_ml_out": g_ml_out_norm,
        "b_i": b_igate, "b_f": b_fgate,
        "w_att": w_att_branch.astype(BF16), "w_ml": w_ml_branch.astype(BF16), "w_out": w_out.astype(BF16),
        "g_ffn": g_ffn_norm.reshape(1, D_MODEL),
        "w_ffn_gate": w_ffn_gate.astype(BF16), "w_ffn_up": w_ffn_up.astype(BF16),
        "w_ffn_down": w_ffn_down.astype(BF16),
        "w_ple": w_ple.astype(BF16), "g_ple": g_ple_norm.reshape(1, D_MODEL),
        "w_ple_gate": w_ple_gate.astype(BF16),
    }


def _layer(x_p, pe_p, x_s, pe_s, cache_k, cache_v, page_table, state_c, state_n, state_m, pw):
    B, L, _ = x_p.shape
    Bs, Ls, _ = x_s.shape
    assert Ls == 1
    past_len = page_table.shape[1] * PAGE_SIZE
    n_pool = cache_k.shape[0]
    per_seq = lambda t: t.reshape(Bs, 1, t.shape[-1])
    pages_t = lambda c: jnp.transpose(c, (0, 2, 3, 1)).reshape(n_pool, ATT_WIDTH, PAGE_SIZE)
    cache_kt, cache_vt = pages_t(cache_k), pages_t(cache_v)

    xt = x_s.reshape(1, Bs, D_MODEL)
    pos_s = jnp.full((Bs,), past_len, dtype=jnp.int32)
    q, k, v, mq_s, mk_s, mv_s, gcol_s, _, mo_s, ga_s, gm_s = _project(xt, pos_s, pw, tm=Bs, feature_major=False)

    pos_p = jnp.arange(L, dtype=jnp.int32)
    qt, kt, vt, mq, mk, mv, _, grow, mo, ga, gm = _project(x_p, pos_p, pw, tm=256, feature_major=True)
    att_p = _attention_prompt(qt, kt, vt)
    ml_p, c_p, n_p, m_p, sel, scores = _mlstm_prompt(mq, mk, mv, grow, mo, pw, page_table, cache_kt,
                                                     q.reshape(Bs, ATT_WIDTH))
    y_p = _post(x_p, att_p, ml_p, ga, gm, pe_p, pw, tm=256)
    tokens_major = lambda t: jnp.transpose(t.reshape(B, N_ATT_HEADS, ATT_HEAD_DIM, L), (0, 3, 1, 2))

    att_s = _sample_attention(page_table, sel[:, :, :MOBA_TOPK], scores, cache_vt,
                              per_seq(q), per_seq(k), per_seq(v))
    ml_s, c_s, n_s, m_s = _mlstm_sample(per_seq(mq_s), per_seq(mk_s), per_seq(mv_s), per_seq(gcol_s),
                                        per_seq(mo_s), state_c, state_n, state_m, pw)
    y_s = _post(xt, att_s.reshape(1, Bs, ATT_WIDTH), ml_s.reshape(1, Bs, ML_V_WIDTH), ga_s, gm_s,
                pe_s.reshape(1, Bs, pe_s.shape[-1]), pw, tm=Bs)
    prompt = (y_p, tokens_major(kt), tokens_major(vt), c_p, n_p, m_p[:, :, 0])
    sample = (y_s.reshape(Bs, 1, D_MODEL), k, v, c_s, n_s, m_s[:, :, 0])
    return prompt, sample


def kernel(x_prompt, x_sample, cache_k, cache_v, state_mlstm_C, state_mlstm_n, state_mlstm_m, page_table,
           p_prompt, p_sample, w_in, b_igate, b_fgate, g_mix_norm, g_q_norm, g_k_norm, g_ml_out_norm,
           w_att_branch, w_ml_branch, w_out, g_ffn_norm, w_ffn_gate, w_ffn_up, w_ffn_down,
           w_ple, g_ple_norm, w_ple_gate):
    depth = w_in.shape[0]
    assert depth == 1
    Bp, Lp, _ = x_prompt.shape
    Bs, Ls, _ = x_sample.shape
    pw = _prepare_weights(w_in[0], b_igate[0], b_fgate[0], g_mix_norm[0], g_q_norm[0], g_k_norm[0],
                          g_ml_out_norm[0], w_att_branch[0], w_ml_branch[0], w_out[0], g_ffn_norm[0],
                          w_ffn_gate[0], w_ffn_up[0], w_ffn_down[0], w_ple[0], g_ple_norm[0], w_ple_gate[0])
    (yp, kp, vp, cp, n_p, mp), (ys, ks, vs, cs, ns, ms) = _layer(
        x_prompt, p_prompt[0], x_sample, p_sample[0], cache_k[0], cache_v[0], page_table,
        state_mlstm_C, state_mlstm_n, state_mlstm_m, pw)
    heads = lambda t: t.reshape(1, Bs, Ls, N_ATT_HEADS, ATT_HEAD_DIM)
    return (yp, ys, kp[None], vp[None], heads(ks), heads(vs),
            cp[None], n_p[None], mp[None], cs, ns, ms[None])
```

```python
import functools

import jax
import jax.numpy as jnp
from jax import lax
from jax.experimental import pallas as pl
from jax.experimental.pallas import tpu as pltpu

F32 = jnp.float32
BF16 = jnp.bfloat16

D_MODEL = 1024
N_ATT_HEADS = 8
ATT_HEAD_DIM = 64
ATT_WIDTH = N_ATT_HEADS * ATT_HEAD_DIM
ROPE_DIMS = ATT_HEAD_DIM // 4
ROPE_THETA = 500000.0
MOBA_BLOCK = 256
MOBA_TOPK = 3
PAGE_SIZE = 128
N_ML_HEADS = 4
ML_QK_DIM = 128
ML_V_DIM = 256
ML_QK_WIDTH = N_ML_HEADS * ML_QK_DIM
ML_V_WIDTH = N_ML_HEADS * ML_V_DIM
NORM_EPS = 1e-6
NEG_INF = -1e30
LOG2_E = 1.4426950408889634

LANES = 128
HEAD_PAIR = LANES // ATT_HEAD_DIM
ML_CHUNK = 256
SAMPLE_HEADS_PER_STEP = 4
VMEM_LIMIT = 56 * 1024 * 1024

_C_Q, _C_K, _C_V = 0, ATT_WIDTH, 2 * ATT_WIDTH
_C_MQ = 3 * ATT_WIDTH
_C_MK = _C_MQ + ML_QK_WIDTH
_C_MV = _C_MK + ML_QK_WIDTH
_C_GATES = _C_MV + ML_V_WIDTH
_C_MO = _C_GATES + 2 * N_ML_HEADS
_C_GA = _C_MO + ML_V_WIDTH
_C_GM = _C_GA + D_MODEL
_C_END = _C_GM + D_MODEL


def _dot(a, b):
    return jnp.dot(a, b, preferred_element_type=F32)


def _dot_nt(a, b, precision=None):
    return lax.dot_general(a, b, (((1,), (1,)), ((), ())), precision=precision,
                           preferred_element_type=F32)


def _rms(x, g):
    return x * lax.rsqrt(jnp.mean(x * x, axis=-1, keepdims=True) + NORM_EPS) * g


def _log_sigmoid(x):
    return jnp.minimum(x, 0.0) - jnp.log1p(jnp.exp(-jnp.abs(x)))


def _proj_shared(xb, wm_ref, wg_ref, wgt_ref, wb_ref,
                 mq_ref, mk_ref, mv_ref, gcol_ref, grow_ref, mo_ref, ga_ref, gm_ref):
    mq_ref[0] = _dot(xb, wm_ref[:, 0:ML_QK_WIDTH]).astype(BF16)
    mk_ref[0] = (_dot(xb, wm_ref[:, ML_QK_WIDTH:2 * ML_QK_WIDTH]) * (ML_QK_DIM ** -0.5)).astype(BF16)
    mv_ref[0] = _dot(xb, wm_ref[:, 2 * ML_QK_WIDTH:2 * ML_QK_WIDTH + ML_V_WIDTH]).astype(BF16)
    gcol_ref[0] = _dot(xb, wg_ref[...])
    grow_ref[0] = _dot_nt(wgt_ref[...], xb)
    mo_ref[0] = _dot(xb, wb_ref[:, 0:ML_V_WIDTH])
    ga_ref[0] = _dot(xb, wb_ref[:, ML_V_WIDTH:ML_V_WIDTH + D_MODEL])
    gm_ref[0] = _dot(xb, wb_ref[:, ML_V_WIDTH + D_MODEL:ML_V_WIDTH + 2 * D_MODEL])


def _proj_kernel_rows(x_ref, cos_ref, sa_ref, sb_ref, gmix_ref, gq_ref, gk_ref, e_ref, wqkv_ref,
                      wm_ref, wg_ref, wgt_ref, wb_ref, q_ref, k_ref, v_ref, *rest_refs):
    xb = _rms(x_ref[0], gmix_ref[...]).astype(BF16)
    cos, sa, sb = cos_ref[...], sa_ref[...], sb_ref[...]
    e = e_ref[...]

    def head_norm_rope(z, g):
        zz = z * z
        hi = zz.astype(BF16)
        lo = (zz - hi.astype(F32)).astype(BF16)
        ss = _dot(hi, e) + _dot(lo, e)
        zn = z * lax.rsqrt(ss * (1.0 / ATT_HEAD_DIM) + NORM_EPS) * g
        outs = []
        for grp in range(ATT_WIDTH // LANES):
            zg = zn[:, grp * LANES:(grp + 1) * LANES]
            outs.append(zg * cos + pltpu.roll(zg, LANES - ROPE_DIMS // 2, 1) * sa
                        + pltpu.roll(zg, ROPE_DIMS // 2, 1) * sb)
        return jnp.concatenate(outs, axis=-1)

    q_ref[0] = head_norm_rope(_dot(xb, wqkv_ref[:, 0:ATT_WIDTH]), gq_ref[...])
    k_ref[0] = head_norm_rope(_dot(xb, wqkv_ref[:, ATT_WIDTH:2 * ATT_WIDTH]), gk_ref[...])
    v_ref[0] = _dot(xb, wqkv_ref[:, 2 * ATT_WIDTH:3 * ATT_WIDTH])
    _proj_shared(xb, wm_ref, wg_ref, wgt_ref, wb_ref, *rest_refs)


def _proj_kernel_cols(x_ref, cos_ref, sin_ref, gmix_ref, gq_ref, gk_ref, wqkvt_ref,
                      wm_ref, wg_ref, wgt_ref, wb_ref, qt_ref, kt_ref, vt_ref, *rest_refs):
    xb = _rms(x_ref[0], gmix_ref[...]).astype(BF16)
    tm = xb.shape[0]
    cos, sin = cos_ref[...][None], sin_ref[...][None]
    half = ROPE_DIMS // 2

    def head_norm_rope(z, g):
        z3 = z.reshape(N_ATT_HEADS, ATT_HEAD_DIM, tm)
        ss = jnp.sum(z3 * z3, axis=1, keepdims=True)
        zn = z3 * lax.rsqrt(ss * (1.0 / ATT_HEAD_DIM) + NORM_EPS) * g
        x1, x2 = zn[:, 0:half, :], zn[:, half:ROPE_DIMS, :]
        out = jnp.concatenate([x1 * cos - x2 * sin, x2 * cos + x1 * sin, zn[:, ROPE_DIMS:, :]], axis=1)
        return out.reshape(ATT_WIDTH, tm)

    zt = _dot_nt(wqkvt_ref[...], xb)
    qt_ref[0] = head_norm_rope(zt[0:ATT_WIDTH], gq_ref[...])
    kt_ref[0] = head_norm_rope(zt[ATT_WIDTH:2 * ATT_WIDTH], gk_ref[...])
    vt_ref[0] = zt[2 * ATT_WIDTH:3 * ATT_WIDTH]
    _proj_shared(xb, wm_ref, wg_ref, wgt_ref, wb_ref, *rest_refs)


def _project(x, pos, pw, tm, feature_major):
    B, L, _ = x.shape
    tok = lambda w: pl.BlockSpec((1, tm, w), lambda b, i: (b, i, 0))
    full = lambda a: pl.BlockSpec(a.shape, lambda b, i: (0,) * a.ndim)
    shared = (pw["w_m"], pw["w_g"], pw["w_gt"], pw["w_b"])
    rest_shape = (
        jax.ShapeDtypeStruct((B, L, ML_QK_WIDTH), BF16),
        jax.ShapeDtypeStruct((B, L, ML_QK_WIDTH), BF16),
        jax.ShapeDtypeStruct((B, L, ML_V_WIDTH), BF16),
        jax.ShapeDtypeStruct((B, L, LANES), F32),
        jax.ShapeDtypeStruct((B, 16, L), F32),
        jax.ShapeDtypeStruct((B, L, ML_V_WIDTH), F32),
        jax.ShapeDtypeStruct((B, L, D_MODEL), F32),
        jax.ShapeDtypeStruct((B, L, D_MODEL), F32),
    )
    rest_specs = (tok(ML_QK_WIDTH), tok(ML_QK_WIDTH), tok(ML_V_WIDTH), tok(LANES),
                  pl.BlockSpec((1, 16, tm), lambda b, i: (b, 0, i)),
                  tok(ML_V_WIDTH), tok(D_MODEL), tok(D_MODEL))
    if feature_major:
        cos, sin = _rope_tables_cols(pos)
        tab = pl.BlockSpec((ROPE_DIMS // 2, tm), lambda b, i: (0, i))
        consts = (pw["g_mix"], pw["g_q_col"], pw["g_k_col"], pw["w_qkv_t"]) + shared
        body, tables, tab_specs = _proj_kernel_cols, (cos, sin), [tab, tab]
        qkv_shape = (jax.ShapeDtypeStruct((B, ATT_WIDTH, L), F32),) * 3
        qkv_specs = (pl.BlockSpec((1, ATT_WIDTH, tm), lambda b, i: (b, 0, i)),) * 3
    else:
        tables = _rope_tables_rows(pos)
        tab = pl.BlockSpec((tm, LANES), lambda b, i: (i, 0))
        consts = (pw["g_mix"], pw["g_q"], pw["g_k"], pw["e_head"], pw["w_qkv"]) + shared
        body, tab_specs = _proj_kernel_rows, [tab, tab, tab]
        qkv_shape = (jax.ShapeDtypeStruct((B, L, ATT_WIDTH), F32),) * 3
        qkv_specs = (tok(ATT_WIDTH),) * 3
    return pl.pallas_call(
        body,
        grid=(B, L // tm),
        in_specs=[tok(D_MODEL)] + tab_specs + [full(a) for a in consts],
        out_specs=qkv_specs + rest_specs,
        out_shape=qkv_shape + rest_shape,
        compiler_params=pltpu.CompilerParams(dimension_semantics=("parallel", "parallel"),
                                             vmem_limit_bytes=VMEM_LIMIT),
        name="in_proj_cols" if feature_major else "in_proj_rows",
    )(x, *tables, *consts)


def _gate_pages(part, page_refs, qbd_ref, score_ref, gate_ref):
    blocks_per_step = len(page_refs) // 2
    lane = lax.broadcasted_iota(jnp.int32, (N_ATT_HEADS, LANES), 1)
    qbd = qbd_ref[0]
    gate = gate_ref[...]
    for i in range(blocks_per_step):
        keys = jnp.concatenate([page_refs[2 * i][0].astype(BF16), page_refs[2 * i + 1][0].astype(BF16)], axis=1)
        s = _dot(qbd, keys)
        s = s[0:N_ATT_HEADS] + s[N_ATT_HEADS:]
        score_ref[0, 2 * i] = s[:, 0:PAGE_SIZE]
        score_ref[0, 2 * i + 1] = s[:, PAGE_SIZE:]
        g = jnp.sum(s, axis=1, keepdims=True) * (1.0 / MOBA_BLOCK)
        gate = jnp.where(lane == part * blocks_per_step + i, g, gate)
    gate_ref[...] = gate


def _gate_select(part, sel_ref, gate_ref, *, steps_per_seq, nblk_seq):
    lane = lax.broadcasted_iota(jnp.int32, (N_ATT_HEADS, LANES), 1)

    @pl.when(part == steps_per_seq - 1)
    def _():
        gate = gate_ref[...]
        rank = jnp.zeros((N_ATT_HEADS, LANES), F32)
        for n2 in range(nblk_seq):
            col = gate[:, n2:n2 + 1]
            beats = (col > gate) | ((col == gate) & (lane > n2))
            rank = rank + jnp.where(beats, 1.0, 0.0)
        out = jnp.zeros((N_ATT_HEADS, LANES), jnp.int32)
        lane_f = lane.astype(F32)
        for r in range(MOBA_TOPK):
            pick = (rank == float(r)) & (lane < nblk_seq)
            idx = jnp.sum(jnp.where(pick, lane_f, 0.0), axis=1, keepdims=True)
            out = jnp.where(lane == r, idx.astype(jnp.int32), out)
        sel_ref[0] = out


def _side_job(page_table, cache_kt, q_s, n_steps, n_grid, pt_pos, linear_step):
    Bs, n_seq_pages = page_table.shape
    assert (Bs * n_seq_pages) % n_steps == 0
    pps = Bs * n_seq_pages // n_steps
    assert MOBA_BLOCK == 2 * PAGE_SIZE and pps % 2 == 0 and n_seq_pages % pps == 0
    steps_per_seq = n_seq_pages // pps
    nblk_seq = n_seq_pages * PAGE_SIZE // MOBA_BLOCK
    assert nblk_seq <= LANES
    head_of = jnp.arange(ATT_WIDTH, dtype=jnp.int32) // ATT_HEAD_DIM
    q_heads = jnp.where(head_of[None, None, :] == jnp.arange(N_ATT_HEADS, dtype=jnp.int32)[None, :, None],
                        q_s[:, None, :], 0.0)
    q_lead = q_heads.astype(BF16)
    q_rows = jnp.concatenate([q_lead, (q_heads - q_lead.astype(F32)).astype(BF16)], axis=1)

    def seq_part(args):
        step = linear_step(*args[:n_grid])
        return step // steps_per_seq, step % steps_per_seq

    def page_spec(p):
        def index(*args):
            seq, part = seq_part(args)
            return (args[n_grid + pt_pos][seq, part * pps + p], 0, 0)
        return pl.BlockSpec((1, ATT_WIDTH, PAGE_SIZE), index)

    return dict(
        in_specs=[pl.BlockSpec((1, 2 * N_ATT_HEADS, ATT_WIDTH), lambda *args: (seq_part(args)[0], 0, 0))]
                 + [page_spec(p) for p in range(pps)],
        operands=[q_rows] + [cache_kt] * pps,
        out_specs=(pl.BlockSpec((1, N_ATT_HEADS, LANES), lambda *args: (seq_part(args)[0], 0, 0)),
                   pl.BlockSpec((1, pps, N_ATT_HEADS, PAGE_SIZE), lambda *args: seq_part(args) + (0, 0))),
        out_shape=(jax.ShapeDtypeStruct((Bs, N_ATT_HEADS, LANES), jnp.int32),
                   jax.ShapeDtypeStruct((Bs, n_seq_pages, N_ATT_HEADS, PAGE_SIZE), F32)),
        scratch=[pltpu.VMEM((N_ATT_HEADS, LANES), F32)],
        static=dict(n_side_pages=pps, steps_per_seq=steps_per_seq, nblk_seq=nblk_seq),
    )


def _attn_kernel(q_ref, k_ref, v_ref, o_ref, kb_ref, vt_ref, kmean_ref, acc_ref, s0_ref, s1_ref, *, nblk):
    blk = MOBA_BLOCK
    tq = 2 * blk
    ones_rows = vt_ref.shape[1] - LANES
    t = pl.program_id(2)
    c0 = 2 * t
    nb = kmean_ref.shape[0]

    @pl.when(t == 0)
    def _():
        lane = lax.broadcasted_iota(jnp.int32, (blk, LANES), 1)
        kmean_ref[...] = jnp.zeros_like(kmean_ref)
        for n in range(nblk):
            rows = slice(n * blk, (n + 1) * blk)
            kf = k_ref[0, :, rows].T
            kb_ref[rows, 0:LANES] = kf.astype(BF16)
            kb_ref[rows, LANES:2 * LANES] = jnp.where(lane == n, 1.0, 0.0).astype(BF16)
            kmean_ref[n:n + 1, :] = jnp.sum(kf, axis=0, keepdims=True) * (1.0 / blk)
            vt_ref[n] = jnp.concatenate([v_ref[0, :, rows], jnp.ones((ones_rows, blk), F32)],
                                        axis=0).astype(BF16)

    q2 = q_ref[0]
    feat = lax.broadcasted_iota(jnp.int32, (LANES, tq), 0)
    blk_id = lax.broadcasted_iota(jnp.int32, (nb, tq), 0)
    col = lax.broadcasted_iota(jnp.int32, (nb, tq), 1)
    cur = c0 + (col >= blk).astype(jnp.int32)
    kmean = kmean_ref[...]
    bias_pad = jnp.zeros((LANES - nb, tq), F32)

    qas = []
    for h in range(HEAD_PAIR):
        qh = jnp.where((feat >= h * ATT_HEAD_DIM) & (feat < (h + 1) * ATT_HEAD_DIM), q2, 0.0)
        gate = jnp.dot(kmean, qh, precision=lax.Precision.HIGHEST,
                       preferred_element_type=F32)
        gate = jnp.where(blk_id < cur, gate, -jnp.inf)
        rank = jnp.zeros((nb, tq), F32)
        for n2 in range(nblk):
            row = gate[n2:n2 + 1, :]
            beats = (row > gate) | ((row == gate) & (blk_id > n2))
            rank = rank + jnp.where(beats, 1.0, 0.0)
        open_blk = ((blk_id < cur) & (rank < MOBA_TOPK)) | (blk_id == cur)
        bias = jnp.where(open_blk, 0.0, NEG_INF)
        qas.append(jnp.concatenate([qh * (ATT_HEAD_DIM ** -0.5 * LOG2_E), bias, bias_pad],
                                   axis=0).astype(BF16))

    def block_rows(n):
        return kb_ref[pl.ds(pl.multiple_of(n * blk, blk), blk), :]

    key_i = lax.broadcasted_iota(jnp.int32, (blk, tq), 0)
    qry_j = lax.broadcasted_iota(jnp.int32, (blk, tq), 1)
    s_refs = (s0_ref, s1_ref)

    def stage_a(first, slot, own):
        ks = [block_rows(first), block_rows(first + 1)]
        maxima = []
        for h in range(HEAD_PAIR):
            sts = [_dot(ks[0], qas[h]), _dot(ks[1], qas[h])]
            if own:
                sts = [jnp.where(key_i <= qry_j, sts[0], NEG_INF),
                       jnp.where(key_i <= qry_j - blk, sts[1], NEG_INF)]
            s_refs[slot][h, 0] = sts[0]
            s_refs[slot][h, 1] = sts[1]
            maxima.append(jnp.maximum(jnp.max(sts[0], axis=0, keepdims=True),
                                      jnp.max(sts[1], axis=0, keepdims=True)))
        return maxima

    def stage_b(first, slot, m_run, maxima):
        vts = [vt_ref[first], vt_ref[first + 1]]
        m_out = []
        for h in range(HEAD_PAIR):
            m_new = jnp.maximum(m_run[h], maxima[h])
            pv = (_dot(vts[0], jnp.exp2(s_refs[slot][h, 0] - m_new).astype(BF16))
                  + _dot(vts[1], jnp.exp2(s_refs[slot][h, 1] - m_new).astype(BF16)))
            acc_ref[h] = jnp.exp2(m_run[h] - m_new) * acc_ref[h] + pv
            m_out.append(m_new)
        return m_out

    def pending_first(j):
        return jnp.where(j == 0, c0, 2 * j - 2)

    acc_ref[...] = jnp.zeros_like(acc_ref)
    m_start = [jnp.full((1, tq), NEG_INF, F32)] * HEAD_PAIR
    carry = tuple(m_start) + tuple(stage_a(c0, 0, own=True))

    def trip(j, slot, carry):
        m_run, maxima = carry[:HEAD_PAIR], carry[HEAD_PAIR:]
        new_maxima = stage_a(2 * j, 1 - slot, own=False)
        m_new = stage_b(pending_first(j), slot, m_run, maxima)
        return tuple(m_new) + tuple(new_maxima)

    carry = lax.fori_loop(0, lax.shift_right_logical(t, 1),
                          lambda i, cr: trip(2 * i + 1, 1, trip(2 * i, 0, cr)), carry)

    def odd_tail(carry):
        carry = trip(t - 1, 0, carry)
        stage_b(pending_first(t), 1, carry[:HEAD_PAIR], carry[HEAD_PAIR:])
        return 0

    def even_tail(carry):
        stage_b(pending_first(t), 0, carry[:HEAD_PAIR], carry[HEAD_PAIR:])
        return 0

    lax.cond((t & 1) == 1, odd_tail, even_tail, carry)
    outs = [acc_ref[h, 0:LANES, :] / acc_ref[h, LANES:LANES + 1, :] for h in range(HEAD_PAIR)]
    ot = jnp.where(feat < ATT_HEAD_DIM, outs[0], outs[1])
    o_ref[0] = ot.T.astype(o_ref.dtype)


def _attention_prompt(qt, kt, vt):
    B, _, L = qt.shape
    blk = MOBA_BLOCK
    tq = 2 * blk
    assert L % tq == 0
    nblk = L // blk
    assert nblk <= LANES
    nb = -(-nblk // 8) * 8
    ones_rows = 16
    groups = ATT_WIDTH // LANES
    return pl.pallas_call(
        functools.partial(_attn_kernel, nblk=nblk),
        grid=(B, groups, L // tq),
        in_specs=[pl.BlockSpec((1, LANES, tq), lambda b, g, i: (b, g, i)),
                  pl.BlockSpec((1, LANES, L), lambda b, g, i: (b, g, 0)),
                  pl.BlockSpec((1, LANES, L), lambda b, g, i: (b, g, 0))],
        out_specs=pl.BlockSpec((1, tq, LANES), lambda b, g, i: (b, i, g)),
        out_shape=jax.ShapeDtypeStruct((B, L, ATT_WIDTH), BF16),
        scratch_shapes=[pltpu.VMEM((L, 2 * LANES), BF16),
                        pltpu.VMEM((nblk, LANES + ones_rows, blk), BF16),
                        pltpu.VMEM((nb, LANES), F32),
                        pltpu.VMEM((HEAD_PAIR, LANES + ones_rows, tq), F32),
                        pltpu.VMEM((HEAD_PAIR, 2, blk, tq), F32),
                        pltpu.VMEM((HEAD_PAIR, 2, blk, tq), F32)],
        compiler_params=pltpu.CompilerParams(dimension_semantics=("parallel", "parallel", "arbitrary"),
                                             vmem_limit_bytes=VMEM_LIMIT),
        name="moba_prompt",
    )(qt, kt, vt)


def _mlstm_kernel(bi_ref, bf_ref, pt_ref, mq_ref, mk_ref, mv_ref, grow_ref, mo_ref, gout_ref, qbd_ref, *refs,
                  chunk, n_side_pages, steps_per_seq, nblk_seq):
    page_refs = refs[:n_side_pages]
    (ml_ref, c_out_ref, n_out_ref, m_out_ref, sel_ref, score_ref, cn_ref, m_ref,
     gate_ref) = refs[n_side_pages:]
    j = pl.program_id(1)
    dk, dv = ML_QK_DIM, ML_V_DIM
    part = lax.rem(pl.program_id(0) * pl.num_programs(1) + j, steps_per_seq)

    @pl.when(j == 0)
    def _():
        cn_ref[...] = jnp.zeros_like(cn_ref)
        m_ref[...] = jnp.zeros_like(m_ref)

    @pl.when(part == 0)
    def _():
        gate_ref[...] = jnp.zeros_like(gate_ref)

    t_i = lax.broadcasted_iota(jnp.int32, (chunk, chunk), 0)
    s_i = lax.broadcasted_iota(jnp.int32, (chunk, chunk), 1)
    tril = s_i <= t_i
    tril_b = jnp.where(tril, 1.0, 0.0).astype(BF16)
    triu_b = jnp.where(t_i <= s_i, 1.0, 0.0).astype(BF16)
    ones_cols = jnp.ones((chunk, LANES), BF16)
    pad_rows = jnp.zeros((13, chunk), F32)
    grow = grow_ref[0]
    gates = []
    for h in range(N_ML_HEADS):
        i_row = grow[h:h + 1, :] + bi_ref[h]
        f_row = _log_sigmoid(grow[N_ML_HEADS + h:N_ML_HEADS + h + 1, :] + bf_ref[h])
        hi = f_row.astype(BF16).astype(F32)
        mid = (f_row - hi).astype(BF16).astype(F32)
        pieces = jnp.concatenate([hi, mid, f_row - hi - mid, pad_rows], axis=0).astype(BF16)
        rows = _dot(pieces, triu_b)
        cols = _dot_nt(tril_b, pieces)
        b_row = rows[0:1] + rows[1:2] + rows[2:3]
        b_col = cols[:, 0:1] + cols[:, 1:2] + cols[:, 2:3]
        a_row = (i_row - b_row) * LOG2_E
        m_old = m_ref[h][:, 0:1]
        a_low = jnp.where(tril, a_row, NEG_INF)
        mm_col = jnp.maximum(jnp.max(a_low, axis=1, keepdims=True), m_old * LOG2_E)
        w_inter = jnp.exp2(m_old * LOG2_E - mm_col)
        floor_col = jnp.exp2(-(b_col * LOG2_E + mm_col))
        decay = jnp.exp2(a_low - mm_col)
        gates.append((b_row, a_row, m_old, mm_col, w_inter, floor_col, decay))

    _gate_pages(part, page_refs, qbd_ref, score_ref, gate_ref)

    hids = []
    for h in range(N_ML_HEADS):
        b_row, a_row, m_old, mm_col, w_inter, floor_col, decay = gates[h]
        q = mq_ref[0, :, h * dk:(h + 1) * dk]
        k = mk_ref[0, :, h * dk:(h + 1) * dk]
        v_ones = jnp.concatenate([mv_ref[0, :, h * dv:(h + 1) * dv], ones_cols], axis=1)
        cn = cn_ref[h]
        s = _dot_nt(q, k) * decay
        inter = _dot(q, cn.astype(BF16))
        intra = _dot(s.astype(BF16), v_ones)
        num = w_inter * inter[:, 0:dv] + intra[:, 0:dv]
        den = w_inter * inter[:, dv:dv + 1] + intra[:, dv:dv + 1]
        hids.append(num / jnp.maximum(jnp.abs(den), floor_col))

        mm_end = mm_col[chunk - 1:chunk, :]
        w_row = jnp.exp2(a_row - mm_end)
        kw = (k.astype(F32).T * w_row).astype(BF16)
        cn_ref[h] = jnp.exp2(m_old * LOG2_E - mm_end) * cn + _dot(kw, v_ones)
        m_ref[h] = jnp.broadcast_to(b_row[:, chunk - 1:chunk] + mm_end * (1.0 / LOG2_E), (1, LANES))

    for h in range(N_ML_HEADS):
        hn = _rms(hids[h], gout_ref[h:h + 1, :])
        mo = mo_ref[0, :, h * dv:(h + 1) * dv]
        ml_ref[0, :, h * dv:(h + 1) * dv] = (hn * jax.nn.sigmoid(mo)).astype(BF16)

    @pl.when(j == pl.num_programs(1) - 1)
    def _():
        for h in range(N_ML_HEADS):
            c_out_ref[0, h] = cn_ref[h, :, 0:dv].T
            n_out_ref[0, h:h + 1, :] = cn_ref[h, :, dv:dv + LANES].T[0:1, :]
            m_out_ref[0, h:h + 1, :] = m_ref[h]

    _gate_select(part, sel_ref, gate_ref, steps_per_seq=steps_per_seq, nblk_seq=nblk_seq)


def _mlstm_prompt(mq, mk, mv, grow, mo, pw, page_table, cache_kt, q_s):
    B, L, _ = mq.shape
    chunk = ML_CHUNK
    n_chunks = L // chunk
    side = _side_job(page_table, cache_kt, q_s, n_steps=B * n_chunks, n_grid=2, pt_pos=2,
                     linear_step=lambda b, j: b * n_chunks + j)
    tok = lambda w: pl.BlockSpec((1, chunk, w), lambda b, j, *_: (b, j, 0))
    per_b = lambda shp: pl.BlockSpec((1,) + shp, lambda b, j, *_: (b,) + (0,) * len(shp))
    grid_spec = pltpu.PrefetchScalarGridSpec(
        num_scalar_prefetch=3,
        grid=(B, n_chunks),
        in_specs=[tok(ML_QK_WIDTH), tok(ML_QK_WIDTH), tok(ML_V_WIDTH),
                  pl.BlockSpec((1, 16, chunk), lambda b, j, *_: (b, 0, j)), tok(ML_V_WIDTH),
                  pl.BlockSpec((N_ML_HEADS, ML_V_DIM), lambda b, j, *_: (0, 0))] + side["in_specs"],
        out_specs=(tok(ML_V_WIDTH), per_b((N_ML_HEADS, ML_V_DIM, ML_QK_DIM)),
                   per_b((N_ML_HEADS, ML_QK_DIM)), per_b((N_ML_HEADS, LANES))) + side["out_specs"],
        scratch_shapes=[pltpu.VMEM((N_ML_HEADS, ML_QK_DIM, ML_V_DIM + LANES), F32),
                        pltpu.VMEM((N_ML_HEADS, 1, LANES), F32)] + side["scratch"],
    )
    return pl.pallas_call(
        functools.partial(_mlstm_kernel, chunk=chunk, **side["static"]),
        grid_spec=grid_spec,
        out_shape=(jax.ShapeDtypeStruct((B, L, ML_V_WIDTH), BF16),
                   jax.ShapeDtypeStruct((B, N_ML_HEADS, ML_V_DIM, ML_QK_DIM), F32),
                   jax.ShapeDtypeStruct((B, N_ML_HEADS, ML_QK_DIM), F32),
                   jax.ShapeDtypeStruct((B, N_ML_HEADS, LANES), F32)) + side["out_shape"],
        compiler_params=pltpu.CompilerParams(dimension_semantics=("arbitrary", "arbitrary"),
                                             vmem_limit_bytes=VMEM_LIMIT),
        name="mlstm_prompt",
    )(pw["b_i"], pw["b_f"], page_table, mq, mk, mv, grow, mo, pw["g_ml_out"], *side["operands"])


def _post_kernel(x_ref, att_ref, ml_ref, ga_ref, gm_ref, pe_ref,
                 watt_ref, wml_ref, wout_ref, gffn_ref, wgate_ref, wup_ref, wdown_ref,
                 wple_ref, gple_ref, wpg_ref, y_ref):
    mix = (jax.nn.sigmoid(ga_ref[0]) * _dot(att_ref[0].astype(BF16), watt_ref[...])
           + jax.nn.sigmoid(gm_ref[0]) * _dot(ml_ref[0], wml_ref[...]))
    h = x_ref[0] + _dot(mix.astype(BF16), wout_ref[...])
    hb = _rms(h, gffn_ref[...]).astype(BF16)
    act = jax.nn.silu(_dot(hb, wgate_ref[...])) * _dot(hb, wup_ref[...])
    h = h + _dot(act.astype(BF16), wdown_ref[...])
    gate = jax.nn.sigmoid(_dot(_rms(h, gple_ref[...]).astype(BF16), wpg_ref[...]))
    y_ref[0] = h + gate * _dot(pe_ref[0].astype(BF16), wple_ref[...])


def _post(x, att, ml, ga, gm, pe, pw, tm):
    B, L, _ = x.shape
    tok = lambda w: pl.BlockSpec((1, tm, w), lambda b, i: (b, i, 0))
    full = lambda a: pl.BlockSpec(a.shape, lambda b, i: (0,) * a.ndim, pipeline_mode=pl.Buffered(1))
    consts = (pw["w_att"], pw["w_ml"], pw["w_out"], pw["g_ffn"], pw["w_ffn_gate"], pw["w_ffn_up"],
              pw["w_ffn_down"], pw["w_ple"], pw["g_ple"], pw["w_ple_gate"])
    return pl.pallas_call(
        _post_kernel,
        grid=(B, L // tm),
        in_specs=[tok(D_MODEL), tok(ATT_WIDTH), tok(ML_V_WIDTH), tok(D_MODEL), tok(D_MODEL),
                  tok(pe.shape[-1])] + [full(a) for a in consts],
        out_specs=tok(D_MODEL),
        out_shape=jax.ShapeDtypeStruct((B, L, D_MODEL), F32),
        compiler_params=pltpu.CompilerParams(dimension_semantics=("parallel", "parallel"),
                                             vmem_limit_bytes=VMEM_LIMIT),
        name="merge_ffn",
    )(x, att, ml, ga, gm, pe, *consts)


def _sample_attn_kernel(pool_ref, local_ref, *refs, n_pages, heads):
    v_refs = refs[:heads * n_pages]
    score_ref, q_ref, kn_ref, vn_ref, o_ref = refs[heads * n_pages:]
    b = pl.program_id(0)
    group = pl.program_id(1)
    scale = ATT_HEAD_DIM ** -0.5
    rows = 16
    outs = []
    for hh in range(heads):
        h = group * heads + hh
        lanes = slice(hh * ATT_HEAD_DIM, (hh + 1) * ATT_HEAD_DIM)
        s_own = jnp.sum(q_ref[0][:, lanes] * kn_ref[0][:, lanes], axis=1, keepdims=True) * scale
        pages = [local_ref[b, h * n_pages + j] for j in range(n_pages)]
        s = jnp.concatenate([score_ref[0, pg, pl.ds(h, 1), :] for pg in pages], axis=1) * scale
        v_all = jnp.concatenate([v_refs[hh * n_pages + j][0].astype(BF16) for j in range(n_pages)], axis=1)
        m = jnp.maximum(s_own, jnp.max(s, axis=1, keepdims=True))
        p_own = jnp.exp(s_own - m)
        p = jnp.exp(s - m)
        l = p_own + jnp.sum(p, axis=1, keepdims=True)
        pv = _dot_nt(jnp.broadcast_to(p, (rows, n_pages * PAGE_SIZE)).astype(BF16), v_all)[0:1]
        outs.append((p_own * vn_ref[0][:, lanes] + pv) / l)
    o_ref[0] = jnp.concatenate(outs, axis=1)


def _sample_attention(page_table, sel, scores, cache_vt, q_s, k_s, v_s):
    Bs, n_seq_pages = page_table.shape
    ppb = MOBA_BLOCK // PAGE_SIZE
    n_pages = MOBA_TOPK * ppb
    heads = SAMPLE_HEADS_PER_STEP
    local = (ppb * sel[..., None] + jnp.arange(ppb, dtype=jnp.int32)).reshape(Bs, N_ATT_HEADS * n_pages)
    pool = jnp.take_along_axis(page_table, local, axis=1)

    def tile_spec(hh, j):
        def index(b, group, pool_ref, local_ref):
            h = group * heads + hh
            return (pool_ref[b, h * n_pages + j], h, 0)
        return pl.BlockSpec((1, ATT_HEAD_DIM, PAGE_SIZE), index)

    tiles = [tile_spec(hh, j) for hh in range(heads) for j in range(n_pages)]
    row = pl.BlockSpec((1, 1, heads * ATT_HEAD_DIM), lambda b, group, pt, sl: (b, 0, group))
    grid_spec = pltpu.PrefetchScalarGridSpec(
        num_scalar_prefetch=2,
        grid=(Bs, N_ATT_HEADS // heads),
        in_specs=tiles + [pl.BlockSpec((1, n_seq_pages, N_ATT_HEADS, PAGE_SIZE),
                                       lambda b, group, pt, sl: (b, 0, 0, 0)), row, row, row],
        out_specs=row,
    )
    return pl.pallas_call(
        functools.partial(_sample_attn_kernel, n_pages=n_pages, heads=heads),
        grid_spec=grid_spec,
        out_shape=jax.ShapeDtypeStruct((Bs, 1, ATT_WIDTH), F32),
        compiler_params=pltpu.CompilerParams(dimension_semantics=("parallel", "arbitrary"),
                                             vmem_limit_bytes=VMEM_LIMIT),
        name="moba_sample_attn",
    )(pool, local, *([cache_vt] * len(tiles)), scores, q_s, k_s, v_s)


def _mlstm_step_kernel(bi_ref, bf_ref, m0_ref, mq_ref, mk_ref, mv_ref, g_ref, mo_ref, gout_ref,
                       c0_ref, n0_ref, ml_ref, c_out_ref, n_out_ref, m_out_ref):
    b = pl.program_id(0)
    dv = ML_V_DIM
    r_i = lax.broadcasted_iota(jnp.int32, (dv, dv), 0)
    c_i = lax.broadcasted_iota(jnp.int32, (dv, dv), 1)
    eye = r_i == c_i
    g = g_ref[0]
    for h in range(N_ML_HEADS):
        i_g = g[:, h:h + 1] + bi_ref[h]
        logf = _log_sigmoid(g[:, N_ML_HEADS + h:N_ML_HEADS + h + 1] + bf_ref[h])
        m_old = m0_ref[b, h]
        inter = logf + m_old
        m_t = jnp.maximum(inter, i_g)
        w_inter = jnp.exp(inter - m_t)
        w_in = jnp.exp(i_g - m_t)
        q = mq_ref[0, :, h * ML_QK_DIM:(h + 1) * ML_QK_DIM].astype(F32)
        k = mk_ref[0, :, h * ML_QK_DIM:(h + 1) * ML_QK_DIM].astype(F32)
        v = mv_ref[0, :, h * dv:(h + 1) * dv].astype(F32)
        c_old = c0_ref[0, 0, h]
        n_old = n0_ref[0, 0, h:h + 1, :]
        s = jnp.sum(q * k, axis=1, keepdims=True) * w_in
        v_col = jnp.sum(jnp.where(eye, v, 0.0), axis=1, keepdims=True)
        q_rows = jnp.broadcast_to(mq_ref[0, :, h * ML_QK_DIM:(h + 1) * ML_QK_DIM], (16, ML_QK_DIM))
        cq = _dot_nt(q_rows, c_old.astype(BF16))[0:1]
        num = w_inter * cq + s * v
        den = w_inter * jnp.sum(n_old * q, axis=1, keepdims=True) + s
        hid = num / jnp.maximum(jnp.abs(den), jnp.exp(-m_t))
        c_out_ref[0, 0, h] = w_inter * c_old + (w_in * v_col) * k
        n_out_ref[0, 0, h:h + 1, :] = w_inter * n_old + w_in * k
        m_out_ref[0, h:h + 1, :] = jnp.broadcast_to(m_t, (1, LANES))
        hn = _rms(hid, gout_ref[h:h + 1, :])
        mo = mo_ref[0, :, h * dv:(h + 1) * dv]
        ml_ref[0, :, h * dv:(h + 1) * dv] = (hn * jax.nn.sigmoid(mo)).astype(BF16)


def _mlstm_sample(mq, mk, mv, gcol, mo, state_c, state_n, state_m, pw):
    Bs = mq.shape[0]
    row = lambda w: pl.BlockSpec((1, 1, w), lambda b, *_: (b, 0, 0))
    grid_spec = pltpu.PrefetchScalarGridSpec(
        num_scalar_prefetch=3,
        grid=(Bs,),
        in_specs=[row(ML_QK_WIDTH), row(ML_QK_WIDTH), row(ML_V_WIDTH), row(LANES), row(ML_V_WIDTH),
                  pl.BlockSpec((N_ML_HEADS, ML_V_DIM), lambda b, *_: (0, 0)),
                  pl.BlockSpec((1, 1, N_ML_HEADS, ML_V_DIM, ML_QK_DIM), lambda b, *_: (0, b, 0, 0, 0)),
                  pl.BlockSpec((1, 1, N_ML_HEADS, ML_QK_DIM), lambda b, *_: (0, b, 0, 0))],
        out_specs=(row(ML_V_WIDTH),
                   pl.BlockSpec((1, 1, N_ML_HEADS, ML_V_DIM, ML_QK_DIM), lambda b, *_: (0, b, 0, 0, 0)),
                   pl.BlockSpec((1, 1, N_ML_HEADS, ML_QK_DIM), lambda b, *_: (0, b, 0, 0)),
                   pl.BlockSpec((1, N_ML_HEADS, LANES), lambda b, *_: (b, 0, 0))),
    )
    return pl.pallas_call(
        _mlstm_step_kernel,
        grid_spec=grid_spec,
        out_shape=(jax.ShapeDtypeStruct((Bs, 1, ML_V_WIDTH), BF16),
                   jax.ShapeDtypeStruct(state_c.shape, F32),
                   jax.ShapeDtypeStruct(state_n.shape, F32),
                   jax.ShapeDtypeStruct((Bs, N_ML_HEADS, LANES), F32)),
        compiler_params=pltpu.CompilerParams(dimension_semantics=("arbitrary",),
                                             vmem_limit_bytes=VMEM_LIMIT),
        name="mlstm_sample",
    )(pw["b_i"], pw["b_f"], state_m[0], mq, mk, mv, gcol, mo, pw["g_ml_out"], state_c, state_n)


def _rope_angles(pos):
    freqs = ROPE_THETA ** (-jnp.arange(0, ROPE_DIMS, 2, dtype=F32) / ROPE_DIMS)
    return pos.astype(F32)[:, None] * freqs[None, :]


def _rope_tables_cols(pos):
    ang = _rope_angles(pos).T
    return jnp.cos(ang), jnp.sin(ang)


def _rope_tables_rows(pos):
    half = ROPE_DIMS // 2
    n = pos.shape[0]
    ang = _rope_angles(pos)
    cos, sin = jnp.cos(ang), jnp.sin(ang)
    rest = ATT_HEAD_DIM - ROPE_DIMS
    zh = jnp.zeros((n, half), F32)
    cos_h = jnp.concatenate([cos, cos, jnp.ones((n, rest), F32)], axis=-1)
    sa_h = jnp.concatenate([-sin, zh, jnp.zeros((n, rest), F32)], axis=-1)
    sb_h = jnp.concatenate([zh, sin, jnp.zeros((n, rest), F32)], axis=-1)
    tile = lambda t: jnp.tile(t, (1, HEAD_PAIR))
    return tile(cos_h), tile(sa_h), tile(sb_h)


def _prepare_weights(w_in, b_igate, b_fgate, g_mix_norm, g_q_norm, g_k_norm, g_ml_out_norm,
                     w_att_branch, w_ml_branch, w_out, g_ffn_norm, w_ffn_gate, w_ffn_up, w_ffn_down,
                     w_ple, g_ple_norm, w_ple_gate):
    w_gates = w_in[:, _C_GATES:_C_MO]
    head_of = jnp.arange(ATT_WIDTH, dtype=jnp.int32) // ATT_HEAD_DIM
    w_qkv = w_in[:, _C_Q:_C_MQ].astype(BF16)
    return {
        "w_qkv": w_qkv,
        "w_qkv_t": w_qkv.T,
        "w_m": w_in[:, _C_MQ:_C_GATES].astype(BF16),
        "g_q_col": g_q_norm.reshape(1, ATT_HEAD_DIM, 1),
        "g_k_col": g_k_norm.reshape(1, ATT_HEAD_DIM, 1),
        "w_g": jnp.pad(w_gates, ((0, 0), (0, LANES - 2 * N_ML_HEADS))).astype(BF16),
        "w_gt": jnp.pad(w_gates.T, ((0, 16 - 2 * N_ML_HEADS), (0, 0))).astype(BF16),
        "w_b": w_in[:, _C_MO:_C_END].astype(BF16),
        "e_head": (head_of[:, None] == head_of[None, :]).astype(BF16),
        "g_mix": g_mix_norm.reshape(1, D_MODEL),
        "g_q": jnp.tile(g_q_norm, N_ATT_HEADS).reshape(1, ATT_WIDTH),
        "g_k": jnp.tile(g_k_norm, N_ATT_HEADS).reshape(1, ATT_WIDTH),
        "g_ml_out": g_ml_out_norm,
        "b_i": b_igate, "b_f": b_fgate,
        "w_att": w_att_branch.astype(BF16), "w_ml": w_ml_branch.astype(BF16), "w_out": w_out.astype(BF16),
        "g_ffn": g_ffn_norm.reshape(1, D_MODEL),
        "w_ffn_gate": w_ffn_gate.astype(BF16), "w_ffn_up": w_ffn_up.astype(BF16),
        "w_ffn_down": w_ffn_down.astype(BF16),
        "w_ple": w_ple.astype(BF16), "g_ple": g_ple_norm.reshape(1, D_MODEL),
        "w_ple_gate": w_ple_gate.astype(BF16),
    }


def _layer(x_p, pe_p, x_s, pe_s, cache_k, cache_v, page_table, state_c, state_n, state_m, pw):
    B, L, _ = x_p.shape
    Bs, Ls, _ = x_s.shape
    assert Ls == 1
    past_len = page_table.shape[1] * PAGE_SIZE
    n_pool = cache_k.shape[0]
    per_seq = lambda t: t.reshape(Bs, 1, t.shape[-1])
    pages_t = lambda c: jnp.transpose(c, (0, 2, 3, 1)).reshape(n_pool, ATT_WIDTH, PAGE_SIZE)
    cache_kt, cache_vt = pages_t(cache_k), pages_t(cache_v)

    xt = x_s.reshape(1, Bs, D_MODEL)
    pos_s = jnp.full((Bs,), past_len, dtype=jnp.int32)
    q, k, v, mq_s, mk_s, mv_s, gcol_s, _, mo_s, ga_s, gm_s = _project(xt, pos_s, pw, tm=Bs, feature_major=False)

    pos_p = jnp.arange(L, dtype=jnp.int32)
    qt, kt, vt, mq, mk, mv, _, grow, mo, ga, gm = _project(x_p, pos_p, pw, tm=256, feature_major=True)
    att_p = _attention_prompt(qt, kt, vt)
    ml_p, c_p, n_p, m_p, sel, scores = _mlstm_prompt(mq, mk, mv, grow, mo, pw, page_table, cache_kt,
                                                     q.reshape(Bs, ATT_WIDTH))
    y_p = _post(x_p, att_p, ml_p, ga, gm, pe_p, pw, tm=256)
    tokens_major = lambda t: jnp.transpose(t.reshape(B, N_ATT_HEADS, ATT_HEAD_DIM, L), (0, 3, 1, 2))

    att_s = _sample_attention(page_table, sel[:, :, :MOBA_TOPK], scores, cache_vt,
                              per_seq(q), per_seq(k), per_seq(v))
    ml_s, c_s, n_s, m_s = _mlstm_sample(per_seq(mq_s), per_seq(mk_s), per_seq(mv_s), per_seq(gcol_s),
                                        per_seq(mo_s), state_c, state_n, state_m, pw)
    y_s = _post(xt, att_s.reshape(1, Bs, ATT_WIDTH), ml_s.reshape(1, Bs, ML_V_WIDTH), ga_s, gm_s,
                pe_s.reshape(1, Bs, pe_s.shape[-1]), pw, tm=Bs)
    prompt = (y_p, tokens_major(kt), tokens_major(vt), c_p, n_p, m_p[:, :, 0])
    sample = (y_s.reshape(Bs, 1, D_MODEL), k, v, c_s, n_s, m_s[:, :, 0])
    return prompt, sample


def kernel(x_prompt, x_sample, cache_k, cache_v, state_mlstm_C, state_mlstm_n, state_mlstm_m, page_table,
           p_prompt, p_sample, w_in, b_igate, b_fgate, g_mix_norm, g_q_norm, g_k_norm, g_ml_out_norm,
           w_att_branch, w_ml_branch, w_out, g_ffn_norm, w_ffn_gate, w_ffn_up, w_ffn_down,
           w_ple, g_ple_norm, w_ple_gate):
    depth = w_in.shape[0]
    assert depth == 1
    Bp, Lp, _ = x_prompt.shape
    Bs, Ls, _ = x_sample.shape
    pw = _prepare_weights(w_in[0], b_igate[0], b_fgate[0], g_mix_norm[0], g_q_norm[0], g_k_norm[0],
                          g_ml_out_norm[0], w_att_branch[0], w_ml_branch[0], w_out[0], g_ffn_norm[0],
                          w_ffn_gate[0], w_ffn_up[0], w_ffn_down[0], w_ple[0], g_ple_norm[0], w_ple_gate[0])
    (yp, kp, vp, cp, n_p, mp), (ys, ks, vs, cs, ns, ms) = _layer(
        x_prompt, p_prompt[0], x_sample, p_sample[0], cache_k[0], cache_v[0], page_table,
        state_mlstm_C, state_mlstm_n, state_mlstm_m, pw)
    heads = lambda t: t.reshape(1, Bs, Ls, N_ATT_HEADS, ATT_HEAD_DIM)
    return (yp, ys, kp[None], vp[None], heads(ks), heads(vs),
            cp[None], n_p[None], mp[None], cs, ns, ms[None])
```

```python
import functools

import jax
import jax.numpy as jnp
from jax import lax
from jax.experimental import pallas as pl
from jax.experimental.pallas import tpu as pltpu

F32 = jnp.float32
BF16 = jnp.bfloat16

D_MODEL = 1024
N_ATT_HEADS = 8
ATT_HEAD_DIM = 64
ATT_WIDTH = N_ATT_HEADS * ATT_HEAD_DIM
ROPE_DIMS = ATT_HEAD_DIM // 4
ROPE_THETA = 500000.0
MOBA_BLOCK = 256
MOBA_TOPK = 3
PAGE_SIZE = 128
N_ML_HEADS = 4
ML_QK_DIM = 128
ML_V_DIM = 256
ML_QK_WIDTH = N_ML_HEADS * ML_QK_DIM
ML_V_WIDTH = N_ML_HEADS * ML_V_DIM
NORM_EPS = 1e-6
NEG_INF = -1e30
LOG2_E = 1.4426950408889634

LANES = 128
HEAD_PAIR = LANES // ATT_HEAD_DIM
ML_CHUNK = 256
SAMPLE_HEADS_PER_STEP = 8
VMEM_LIMIT = 56 * 1024 * 1024

_C_Q, _C_K, _C_V = 0, ATT_WIDTH, 2 * ATT_WIDTH
_C_MQ = 3 * ATT_WIDTH
_C_MK = _C_MQ + ML_QK_WIDTH
_C_MV = _C_MK + ML_QK_WIDTH
_C_GATES = _C_MV + ML_V_WIDTH
_C_MO = _C_GATES + 2 * N_ML_HEADS
_C_GA = _C_MO + ML_V_WIDTH
_C_GM = _C_GA + D_MODEL
_C_END = _C_GM + D_MODEL


def _dot(a, b):
    return jnp.dot(a, b, preferred_element_type=F32)


def _dot_nt(a, b, precision=None):
    return lax.dot_general(a, b, (((1,), (1,)), ((), ())), precision=precision,
                           preferred_element_type=F32)


def _rms(x, g):
    return x * lax.rsqrt(jnp.mean(x * x, axis=-1, keepdims=True) + NORM_EPS) * g


def _log_sigmoid(x):
    return jnp.minimum(x, 0.0) - jnp.log1p(jnp.exp(-jnp.abs(x)))


def _proj_shared(xb, wm_ref, wg_ref, wgt_ref, wb_ref,
                 mq_ref, mk_ref, mv_ref, gcol_ref, grow_ref, mo_ref, ga_ref, gm_ref):
    mq_ref[0] = _dot(xb, wm_ref[:, 0:ML_QK_WIDTH]).astype(BF16)
    mk_ref[0] = (_dot(xb, wm_ref[:, ML_QK_WIDTH:2 * ML_QK_WIDTH]) * (ML_QK_DIM ** -0.5)).astype(BF16)
    mv_ref[0] = _dot(xb, wm_ref[:, 2 * ML_QK_WIDTH:2 * ML_QK_WIDTH + ML_V_WIDTH]).astype(BF16)
    gcol_ref[0] = _dot(xb, wg_ref[...])
    grow_ref[0] = _dot_nt(wgt_ref[...], xb)
    mo_ref[0] = _dot(xb, wb_ref[:, 0:ML_V_WIDTH])
    ga_ref[0] = _dot(xb, wb_ref[:, ML_V_WIDTH:ML_V_WIDTH + D_MODEL])
    gm_ref[0] = _dot(xb, wb_ref[:, ML_V_WIDTH + D_MODEL:ML_V_WIDTH + 2 * D_MODEL])


def _proj_kernel_rows(x_ref, cos_ref, sa_ref, sb_ref, gmix_ref, gq_ref, gk_ref, e_ref, wqkv_ref,
                      wm_ref, wg_ref, wgt_ref, wb_ref, q_ref, k_ref, v_ref, *rest_refs):
    xb = _rms(x_ref[0], gmix_ref[...]).astype(BF16)
    cos, sa, sb = cos_ref[...], sa_ref[...], sb_ref[...]
    e = e_ref[...]

    def head_norm_rope(z, g):
        zz = z * z
        hi = zz.astype(BF16)
        lo = (zz - hi.astype(F32)).astype(BF16)
        ss = _dot(hi, e) + _dot(lo, e)
        zn = z * lax.rsqrt(ss * (1.0 / ATT_HEAD_DIM) + NORM_EPS) * g
        outs = []
        for grp in range(ATT_WIDTH // LANES):
            zg = zn[:, grp * LANES:(grp + 1) * LANES]
            outs.append(zg * cos + pltpu.roll(zg, LANES - ROPE_DIMS // 2, 1) * sa
                        + pltpu.roll(zg, ROPE_DIMS // 2, 1) * sb)
        return jnp.concatenate(outs, axis=-1)

    q_ref[0] = head_norm_rope(_dot(xb, wqkv_ref[:, 0:ATT_WIDTH]), gq_ref[...])
    k_ref[0] = head_norm_rope(_dot(xb, wqkv_ref[:, ATT_WIDTH:2 * ATT_WIDTH]), gk_ref[...])
    v_ref[0] = _dot(xb, wqkv_ref[:, 2 * ATT_WIDTH:3 * ATT_WIDTH])
    _proj_shared(xb, wm_ref, wg_ref, wgt_ref, wb_ref, *rest_refs)


def _proj_kernel_cols(x_ref, cos_ref, sin_ref, gmix_ref, gq_ref, gk_ref, wqkvt_ref,
                      wm_ref, wg_ref, wgt_ref, wb_ref, qt_ref, kt_ref, vt_ref, *rest_refs):
    xb = _rms(x_ref[0], gmix_ref[...]).astype(BF16)
    tm = xb.shape[0]
    cos, sin = cos_ref[...][None], sin_ref[...][None]
    half = ROPE_DIMS // 2

    def head_norm_rope(z, g):
        z3 = z.reshape(N_ATT_HEADS, ATT_HEAD_DIM, tm)
        ss = jnp.sum(z3 * z3, axis=1, keepdims=True)
        zn = z3 * lax.rsqrt(ss * (1.0 / ATT_HEAD_DIM) + NORM_EPS) * g
        x1, x2 = zn[:, 0:half, :], zn[:, half:ROPE_DIMS, :]
        out = jnp.concatenate([x1 * cos - x2 * sin, x2 * cos + x1 * sin, zn[:, ROPE_DIMS:, :]], axis=1)
        return out.reshape(ATT_WIDTH, tm)

    zt = _dot_nt(wqkvt_ref[...], xb)
    qt_ref[0] = head_norm_rope(zt[0:ATT_WIDTH], gq_ref[...])
    kt_ref[0] = head_norm_rope(zt[ATT_WIDTH:2 * ATT_WIDTH], gk_ref[...])
    vt_ref[0] = zt[2 * ATT_WIDTH:3 * ATT_WIDTH]
    _proj_shared(xb, wm_ref, wg_ref, wgt_ref, wb_ref, *rest_refs)


def _project(x, pos, pw, tm, feature_major):
    B, L, _ = x.shape
    tok = lambda w: pl.BlockSpec((1, tm, w), lambda b, i: (b, i, 0))
    full = lambda a: pl.BlockSpec(a.shape, lambda b, i: (0,) * a.ndim)
    shared = (pw["w_m"], pw["w_g"], pw["w_gt"], pw["w_b"])
    rest_shape = (
        jax.ShapeDtypeStruct((B, L, ML_QK_WIDTH), BF16),
        jax.ShapeDtypeStruct((B, L, ML_QK_WIDTH), BF16),
        jax.ShapeDtypeStruct((B, L, ML_V_WIDTH), BF16),
        jax.ShapeDtypeStruct((B, L, LANES), F32),
        jax.ShapeDtypeStruct((B, 16, L), F32),
        jax.ShapeDtypeStruct((B, L, ML_V_WIDTH), F32),
        jax.ShapeDtypeStruct((B, L, D_MODEL), F32),
        jax.ShapeDtypeStruct((B, L, D_MODEL), F32),
    )
    rest_specs = (tok(ML_QK_WIDTH), tok(ML_QK_WIDTH), tok(ML_V_WIDTH), tok(LANES),
                  pl.BlockSpec((1, 16, tm), lambda b, i: (b, 0, i)),
                  tok(ML_V_WIDTH), tok(D_MODEL), tok(D_MODEL))
    if feature_major:
        cos, sin = _rope_tables_cols(pos)
        tab = pl.BlockSpec((ROPE_DIMS // 2, tm), lambda b, i: (0, i))
        consts = (pw["g_mix"], pw["g_q_col"], pw["g_k_col"], pw["w_qkv_t"]) + shared
        body, tables, tab_specs = _proj_kernel_cols, (cos, sin), [tab, tab]
        qkv_shape = (jax.ShapeDtypeStruct((B, ATT_WIDTH, L), F32),) * 3
        qkv_specs = (pl.BlockSpec((1, ATT_WIDTH, tm), lambda b, i: (b, 0, i)),) * 3
    else:
        tables = _rope_tables_rows(pos)
        tab = pl.BlockSpec((tm, LANES), lambda b, i: (i, 0))
        consts = (pw["g_mix"], pw["g_q"], pw["g_k"], pw["e_head"], pw["w_qkv"]) + shared
        body, tab_specs = _proj_kernel_rows, [tab, tab, tab]
        qkv_shape = (jax.ShapeDtypeStruct((B, L, ATT_WIDTH), F32),) * 3
        qkv_specs = (tok(ATT_WIDTH),) * 3
    return pl.pallas_call(
        body,
        grid=(B, L // tm),
        in_specs=[tok(D_MODEL)] + tab_specs + [full(a) for a in consts],
        out_specs=qkv_specs + rest_specs,
        out_shape=qkv_shape + rest_shape,
        compiler_params=pltpu.CompilerParams(dimension_semantics=("parallel", "parallel"),
                                             vmem_limit_bytes=VMEM_LIMIT),
        name="in_proj_cols" if feature_major else "in_proj_rows",
    )(x, *tables, *consts)


def _gate_pages(part, page_refs, qbd_ref, score_ref, gate_ref):
    blocks_per_step = len(page_refs) // 2
    lane = lax.broadcasted_iota(jnp.int32, (N_ATT_HEADS, LANES), 1)
    qbd = qbd_ref[0]
    gate = gate_ref[...]
    for i in range(blocks_per_step):
        keys = jnp.concatenate([page_refs[2 * i][0].astype(BF16), page_refs[2 * i + 1][0].astype(BF16)], axis=1)
        s = _dot(qbd, keys)
        s = s[0:N_ATT_HEADS] + s[N_ATT_HEADS:]
        score_ref[0, 2 * i] = s[:, 0:PAGE_SIZE]
        score_ref[0, 2 * i + 1] = s[:, PAGE_SIZE:]
        g = jnp.sum(s, axis=1, keepdims=True) * (1.0 / MOBA_BLOCK)
        gate = jnp.where(lane == part * blocks_per_step + i, g, gate)
    gate_ref[...] = gate


def _gate_select(part, sel_ref, gate_ref, *, steps_per_seq, nblk_seq):
    lane = lax.broadcasted_iota(jnp.int32, (N_ATT_HEADS, LANES), 1)

    @pl.when(part == steps_per_seq - 1)
    def _():
        gate = gate_ref[...]
        rank = jnp.zeros((N_ATT_HEADS, LANES), F32)
        for n2 in range(nblk_seq):
            col = gate[:, n2:n2 + 1]
            beats = (col > gate) | ((col == gate) & (lane > n2))
            rank = rank + jnp.where(beats, 1.0, 0.0)
        out = jnp.zeros((N_ATT_HEADS, LANES), jnp.int32)
        lane_f = lane.astype(F32)
        for r in range(MOBA_TOPK):
            pick = (rank == float(r)) & (lane < nblk_seq)
            idx = jnp.sum(jnp.where(pick, lane_f, 0.0), axis=1, keepdims=True)
            out = jnp.where(lane == r, idx.astype(jnp.int32), out)
        sel_ref[0] = out


def _side_job(page_table, cache_kt, q_s, n_steps, n_grid, pt_pos, linear_step):
    Bs, n_seq_pages = page_table.shape
    assert (Bs * n_seq_pages) % n_steps == 0
    pps = Bs * n_seq_pages // n_steps
    assert MOBA_BLOCK == 2 * PAGE_SIZE and pps % 2 == 0 and n_seq_pages % pps == 0
    steps_per_seq = n_seq_pages // pps
    nblk_seq = n_seq_pages * PAGE_SIZE // MOBA_BLOCK
    assert nblk_seq <= LANES
    head_of = jnp.arange(ATT_WIDTH, dtype=jnp.int32) // ATT_HEAD_DIM
    q_heads = jnp.where(head_of[None, None, :] == jnp.arange(N_ATT_HEADS, dtype=jnp.int32)[None, :, None],
                        q_s[:, None, :], 0.0)
    q_lead = q_heads.astype(BF16)
    q_rows = jnp.concatenate([q_lead, (q_heads - q_lead.astype(F32)).astype(BF16)], axis=1)

    def seq_part(args):
        step = linear_step(*args[:n_grid])
        return step // steps_per_seq, step % steps_per_seq

    def page_spec(p):
        def index(*args):
            seq, part = seq_part(args)
            return (args[n_grid + pt_pos][seq, part * pps + p], 0, 0)
        return pl.BlockSpec((1, ATT_WIDTH, PAGE_SIZE), index)

    return dict(
        in_specs=[pl.BlockSpec((1, 2 * N_ATT_HEADS, ATT_WIDTH), lambda *args: (seq_part(args)[0], 0, 0))]
                 + [page_spec(p) for p in range(pps)],
        operands=[q_rows] + [cache_kt] * pps,
        out_specs=(pl.BlockSpec((1, N_ATT_HEADS, LANES), lambda *args: (seq_part(args)[0], 0, 0)),
                   pl.BlockSpec((1, pps, N_ATT_HEADS, PAGE_SIZE), lambda *args: seq_part(args) + (0, 0))),
        out_shape=(jax.ShapeDtypeStruct((Bs, N_ATT_HEADS, LANES), jnp.int32),
                   jax.ShapeDtypeStruct((Bs, n_seq_pages, N_ATT_HEADS, PAGE_SIZE), F32)),
        scratch=[pltpu.VMEM((N_ATT_HEADS, LANES), F32)],
        static=dict(n_side_pages=pps, steps_per_seq=steps_per_seq, nblk_seq=nblk_seq),
    )


def _attn_kernel(q_ref, k_ref, v_ref, o_ref, kb_ref, vt_ref, kmean_ref, acc_ref, s0_ref, s1_ref, *, nblk):
    blk = MOBA_BLOCK
    tq = 2 * blk
    ones_rows = vt_ref.shape[1] - LANES
    t = pl.program_id(2)
    c0 = 2 * t
    nb = kmean_ref.shape[0]

    @pl.when(t == 0)
    def _():
        lane = lax.broadcasted_iota(jnp.int32, (blk, LANES), 1)
        kmean_ref[...] = jnp.zeros_like(kmean_ref)
        for n in range(nblk):
            rows = slice(n * blk, (n + 1) * blk)
            kf = k_ref[0, :, rows].T
            kb_ref[rows, 0:LANES] = kf.astype(BF16)
            kb_ref[rows, LANES:2 * LANES] = jnp.where(lane == n, 1.0, 0.0).astype(BF16)
            kmean_ref[n:n + 1, :] = jnp.sum(kf, axis=0, keepdims=True) * (1.0 / blk)
            vt_ref[n] = jnp.concatenate([v_ref[0, :, rows], jnp.ones((ones_rows, blk), F32)],
                                        axis=0).astype(BF16)

    q2 = q_ref[0]
    feat = lax.broadcasted_iota(jnp.int32, (LANES, tq), 0)
    blk_id = lax.broadcasted_iota(jnp.int32, (nb, tq), 0)
    col = lax.broadcasted_iota(jnp.int32, (nb, tq), 1)
    cur = c0 + (col >= blk).astype(jnp.int32)
    kmean = kmean_ref[...]
    bias_pad = jnp.zeros((LANES - nb, tq), F32)

    qas = []
    for h in range(HEAD_PAIR):
        qh = jnp.where((feat >= h * ATT_HEAD_DIM) & (feat < (h + 1) * ATT_HEAD_DIM), q2, 0.0)
        gate = jnp.dot(kmean, qh, precision=lax.Precision.HIGHEST,
                       preferred_element_type=F32)
        gate = jnp.where(blk_id < cur, gate, -jnp.inf)
        rank = jnp.zeros((nb, tq), F32)
        for n2 in range(nblk):
            row = gate[n2:n2 + 1, :]
            beats = (row > gate) | ((row == gate) & (blk_id > n2))
            rank = rank + jnp.where(beats, 1.0, 0.0)
        open_blk = ((blk_id < cur) & (rank < MOBA_TOPK)) | (blk_id == cur)
        bias = jnp.where(open_blk, 0.0, NEG_INF)
        qas.append(jnp.concatenate([qh * (ATT_HEAD_DIM ** -0.5 * LOG2_E), bias, bias_pad],
                                   axis=0).astype(BF16))

    def block_rows(n):
        return kb_ref[pl.ds(pl.multiple_of(n * blk, blk), blk), :]

    key_i = lax.broadcasted_iota(jnp.int32, (blk, tq), 0)
    qry_j = lax.broadcasted_iota(jnp.int32, (blk, tq), 1)
    s_refs = (s0_ref, s1_ref)

    def stage_a(first, slot, own):
        ks = [block_rows(first), block_rows(first + 1)]
        maxima = []
        for h in range(HEAD_PAIR):
            sts = [_dot(ks[0], qas[h]), _dot(ks[1], qas[h])]
            if own:
                sts = [jnp.where(key_i <= qry_j, sts[0], NEG_INF),
                       jnp.where(key_i <= qry_j - blk, sts[1], NEG_INF)]
            s_refs[slot][h, 0] = sts[0]
            s_refs[slot][h, 1] = sts[1]
            maxima.append(jnp.maximum(jnp.max(sts[0], axis=0, keepdims=True),
                                      jnp.max(sts[1], axis=0, keepdims=True)))
        return maxima

    def stage_b(first, slot, m_run, maxima):
        vts = [vt_ref[first], vt_ref[first + 1]]
        m_out = []
        for h in range(HEAD_PAIR):
            m_new = jnp.maximum(m_run[h], maxima[h])
            pv = (_dot(vts[0], jnp.exp2(s_refs[slot][h, 0] - m_new).astype(BF16))
                  + _dot(vts[1], jnp.exp2(s_refs[slot][h, 1] - m_new).astype(BF16)))
            acc_ref[h] = jnp.exp2(m_run[h] - m_new) * acc_ref[h] + pv
            m_out.append(m_new)
        return m_out

    def pending_first(j):
        return jnp.where(j == 0, c0, 2 * j - 2)

    acc_ref[...] = jnp.zeros_like(acc_ref)
    m_start = [jnp.full((1, tq), NEG_INF, F32)] * HEAD_PAIR
    carry = tuple(m_start) + tuple(stage_a(c0, 0, own=True))

    def trip(j, slot, carry):
        m_run, maxima = carry[:HEAD_PAIR], carry[HEAD_PAIR:]
        new_maxima = stage_a(2 * j, 1 - slot, own=False)
        m_new = stage_b(pending_first(j), slot, m_run, maxima)
        return tuple(m_new) + tuple(new_maxima)

    carry = lax.fori_loop(0, lax.shift_right_logical(t, 1),
                          lambda i, cr: trip(2 * i + 1, 1, trip(2 * i, 0, cr)), carry)

    def odd_tail(carry):
        carry = trip(t - 1, 0, carry)
        stage_b(pending_first(t), 1, carry[:HEAD_PAIR], carry[HEAD_PAIR:])
        return 0

    def even_tail(carry):
        stage_b(pending_first(t), 0, carry[:HEAD_PAIR], carry[HEAD_PAIR:])
        return 0

    lax.cond((t & 1) == 1, odd_tail, even_tail, carry)
    outs = [acc_ref[h, 0:LANES, :] / acc_ref[h, LANES:LANES + 1, :] for h in range(HEAD_PAIR)]
    ot = jnp.where(feat < ATT_HEAD_DIM, outs[0], outs[1])
    o_ref[0] = ot.T.astype(o_ref.dtype)


def _attention_prompt(qt, kt, vt):
    B, _, L = qt.shape
    blk = MOBA_BLOCK
    tq = 2 * blk
    assert L % tq == 0
    nblk = L // blk
    assert nblk <= LANES
    nb = -(-nblk // 8) * 8
    ones_rows = 16
    groups = ATT_WIDTH // LANES
    return pl.pallas_call(
        functools.partial(_attn_kernel, nblk=nblk),
        grid=(B, groups, L // tq),
        in_specs=[pl.BlockSpec((1, LANES, tq), lambda b, g, i: (b, g, i)),
                  pl.BlockSpec((1, LANES, L), lambda b, g, i: (b, g, 0)),
                  pl.BlockSpec((1, LANES, L), lambda b, g, i: (b, g, 0))],
        out_specs=pl.BlockSpec((1, tq, LANES), lambda b, g, i: (b, i, g)),
        out_shape=jax.ShapeDtypeStruct((B, L, ATT_WIDTH), BF16),
        scratch_shapes=[pltpu.VMEM((L, 2 * LANES), BF16),
                        pltpu.VMEM((nblk, LANES + ones_rows, blk), BF16),
                        pltpu.VMEM((nb, LANES), F32),
                        pltpu.VMEM((HEAD_PAIR, LANES + ones_rows, tq), F32),
                        pltpu.VMEM((HEAD_PAIR, 2, blk, tq), F32),
                        pltpu.VMEM((HEAD_PAIR, 2, blk, tq), F32)],
        compiler_params=pltpu.CompilerParams(dimension_semantics=("parallel", "parallel", "arbitrary"),
                                             vmem_limit_bytes=VMEM_LIMIT),
        name="moba_prompt",
    )(qt, kt, vt)


def _mlstm_kernel(bi_ref, bf_ref, pt_ref, mq_ref, mk_ref, mv_ref, grow_ref, mo_ref, gout_ref, qbd_ref, *refs,
                  chunk, n_side_pages, steps_per_seq, nblk_seq):
    page_refs = refs[:n_side_pages]
    (ml_ref, c_out_ref, n_out_ref, m_out_ref, sel_ref, score_ref, cn_ref, m_ref,
     gate_ref) = refs[n_side_pages:]
    j = pl.program_id(1)
    dk, dv = ML_QK_DIM, ML_V_DIM
    part = lax.rem(pl.program_id(0) * pl.num_programs(1) + j, steps_per_seq)

    @pl.when(j == 0)
    def _():
        cn_ref[...] = jnp.zeros_like(cn_ref)
        m_ref[...] = jnp.zeros_like(m_ref)

    @pl.when(part == 0)
    def _():
        gate_ref[...] = jnp.zeros_like(gate_ref)

    t_i = lax.broadcasted_iota(jnp.int32, (chunk, chunk), 0)
    s_i = lax.broadcasted_iota(jnp.int32, (chunk, chunk), 1)
    tril = s_i <= t_i
    tril_b = jnp.where(tril, 1.0, 0.0).astype(BF16)
    triu_b = jnp.where(t_i <= s_i, 1.0, 0.0).astype(BF16)
    ones_cols = jnp.ones((chunk, LANES), BF16)
    grow = grow_ref[0]
    f_rows = _log_sigmoid(grow[N_ML_HEADS:2 * N_ML_HEADS, :]
                          + jnp.concatenate([jnp.full((1, 1), bf_ref[h], F32) for h in range(N_ML_HEADS)], axis=0))
    hi = f_rows.astype(BF16).astype(F32)
    mid = (f_rows - hi).astype(BF16).astype(F32)
    pieces = jnp.concatenate([hi, mid, f_rows - hi - mid], axis=1)
    pieces = jnp.concatenate([pieces, jnp.zeros((16 - N_ML_HEADS, 3 * chunk), F32)], axis=0).astype(BF16)
    rows = _dot(pieces, jnp.concatenate([triu_b] * 3, axis=0))
    cols = _dot_nt(jnp.concatenate([tril_b] * 3, axis=1), pieces)
    gates = []
    for h in range(N_ML_HEADS):
        i_row = grow[h:h + 1, :] + bi_ref[h]
        b_row = rows[h:h + 1]
        b_col = cols[:, h:h + 1]
        a_row = (i_row - b_row) * LOG2_E
        m_old = m_ref[h][:, 0:1]
        a_low = jnp.where(tril, a_row, NEG_INF)
        mm_col = jnp.maximum(jnp.max(a_low, axis=1, keepdims=True), m_old * LOG2_E)
        w_inter = jnp.exp2(m_old * LOG2_E - mm_col)
        floor_col = jnp.exp2(-(b_col * LOG2_E + mm_col))
        decay = jnp.exp2(a_low - mm_col)
        gates.append((b_row, a_row, m_old, mm_col, w_inter, floor_col, decay))

    _gate_pages(part, page_refs, qbd_ref, score_ref, gate_ref)

    hids = []
    for h in range(N_ML_HEADS):
        b_row, a_row, m_old, mm_col, w_inter, floor_col, decay = gates[h]
        q = mq_ref[0, :, h * dk:(h + 1) * dk]
        k = mk_ref[0, :, h * dk:(h + 1) * dk]
        v_ones = jnp.concatenate([mv_ref[0, :, h * dv:(h + 1) * dv], ones_cols], axis=1)
        cn = cn_ref[h]
        s = _dot_nt(q, k) * decay
        inter = _dot(q, cn.astype(BF16))
        intra = _dot(s.astype(BF16), v_ones)
        num = w_inter * inter[:, 0:dv] + intra[:, 0:dv]
        den = w_inter * inter[:, dv:dv + 1] + intra[:, dv:dv + 1]
        hids.append(num / jnp.maximum(jnp.abs(den), floor_col))

        mm_end = mm_col[chunk - 1:chunk, :]
        w_row = jnp.exp2(a_row - mm_end)
        kw = (k.astype(F32).T * w_row).astype(BF16)
        cn_ref[h] = jnp.exp2(m_old * LOG2_E - mm_end) * cn + _dot(kw, v_ones)
        m_ref[h] = jnp.broadcast_to(b_row[:, chunk - 1:chunk] + mm_end * (1.0 / LOG2_E), (1, LANES))

    for h in range(N_ML_HEADS):
        hn = _rms(hids[h], gout_ref[h:h + 1, :])
        mo = mo_ref[0, :, h * dv:(h + 1) * dv]
        ml_ref[0, :, h * dv:(h + 1) * dv] = (hn * jax.nn.sigmoid(mo)).astype(BF16)

    @pl.when(j == pl.num_programs(1) - 1)
    def _():
        for h in range(N_ML_HEADS):
            c_out_ref[0, h] = cn_ref[h, :, 0:dv].T
            n_out_ref[0, h:h + 1, :] = cn_ref[h, :, dv:dv + LANES].T[0:1, :]
            m_out_ref[0, h:h + 1, :] = m_ref[h]

    _gate_select(part, sel_ref, gate_ref, steps_per_seq=steps_per_seq, nblk_seq=nblk_seq)


def _mlstm_prompt(mq, mk, mv, grow, mo, pw, page_table, cache_kt, q_s):
    B, L, _ = mq.shape
    chunk = ML_CHUNK
    n_chunks = L // chunk
    side = _side_job(page_table, cache_kt, q_s, n_steps=B * n_chunks, n_grid=2, pt_pos=2,
                     linear_step=lambda b, j: b * n_chunks + j)
    tok = lambda w: pl.BlockSpec((1, chunk, w), lambda b, j, *_: (b, j, 0))
    per_b = lambda shp: pl.BlockSpec((1,) + shp, lambda b, j, *_: (b,) + (0,) * len(shp))
    grid_spec = pltpu.PrefetchScalarGridSpec(
        num_scalar_prefetch=3,
        grid=(B, n_chunks),
        in_specs=[tok(ML_QK_WIDTH), tok(ML_QK_WIDTH), tok(ML_V_WIDTH),
                  pl.BlockSpec((1, 16, chunk), lambda b, j, *_: (b, 0, j)), tok(ML_V_WIDTH),
                  pl.BlockSpec((N_ML_HEADS, ML_V_DIM), lambda b, j, *_: (0, 0))] + side["in_specs"],
        out_specs=(tok(ML_V_WIDTH), per_b((N_ML_HEADS, ML_V_DIM, ML_QK_DIM)),
                   per_b((N_ML_HEADS, ML_QK_DIM)), per_b((N_ML_HEADS, LANES))) + side["out_specs"],
        scratch_shapes=[pltpu.VMEM((N_ML_HEADS, ML_QK_DIM, ML_V_DIM + LANES), F32),
                        pltpu.VMEM((N_ML_HEADS, 1, LANES), F32)] + side["scratch"],
    )
    return pl.pallas_call(
        functools.partial(_mlstm_kernel, chunk=chunk, **side["static"]),
        grid_spec=grid_spec,
        out_shape=(jax.ShapeDtypeStruct((B, L, ML_V_WIDTH), BF16),
                   jax.ShapeDtypeStruct((B, N_ML_HEADS, ML_V_DIM, ML_QK_DIM), F32),
                   jax.ShapeDtypeStruct((B, N_ML_HEADS, ML_QK_DIM), F32),
                   jax.ShapeDtypeStruct((B, N_ML_HEADS, LANES), F32)) + side["out_shape"],
        compiler_params=pltpu.CompilerParams(dimension_semantics=("arbitrary", "arbitrary"),
                                             vmem_limit_bytes=VMEM_LIMIT),
        name="mlstm_prompt",
    )(pw["b_i"], pw["b_f"], page_table, mq, mk, mv, grow, mo, pw["g_ml_out"], *side["operands"])


def _post_kernel(x_ref, att_ref, ml_ref, ga_ref, gm_ref, pe_ref,
                 watt_ref, wml_ref, wout_ref, gffn_ref, wgate_ref, wup_ref, wdown_ref,
                 wple_ref, gple_ref, wpg_ref, y_ref):
    mix = (jax.nn.sigmoid(ga_ref[0]) * _dot(att_ref[0].astype(BF16), watt_ref[...])
           + jax.nn.sigmoid(gm_ref[0]) * _dot(ml_ref[0], wml_ref[...]))
    h = x_ref[0] + _dot(mix.astype(BF16), wout_ref[...])
    hb = _rms(h, gffn_ref[...]).astype(BF16)
    act = jax.nn.silu(_dot(hb, wgate_ref[...])) * _dot(hb, wup_ref[...])
    h = h + _dot(act.astype(BF16), wdown_ref[...])
    gate = jax.nn.sigmoid(_dot(_rms(h, gple_ref[...]).astype(BF16), wpg_ref[...]))
    y_ref[0] = h + gate * _dot(pe_ref[0].astype(BF16), wple_ref[...])


def _post(x, att, ml, ga, gm, pe, pw, tm):
    B, L, _ = x.shape
    tok = lambda w: pl.BlockSpec((1, tm, w), lambda b, i: (b, i, 0))
    full = lambda a: pl.BlockSpec(a.shape, lambda b, i: (0,) * a.ndim, pipeline_mode=pl.Buffered(1))
    consts = (pw["w_att"], pw["w_ml"], pw["w_out"], pw["g_ffn"], pw["w_ffn_gate"], pw["w_ffn_up"],
              pw["w_ffn_down"], pw["w_ple"], pw["g_ple"], pw["w_ple_gate"])
    return pl.pallas_call(
        _post_kernel,
        grid=(B, L // tm),
        in_specs=[tok(D_MODEL), tok(ATT_WIDTH), tok(ML_V_WIDTH), tok(D_MODEL), tok(D_MODEL),
                  tok(pe.shape[-1])] + [full(a) for a in consts],
        out_specs=tok(D_MODEL),
        out_shape=jax.ShapeDtypeStruct((B, L, D_MODEL), F32),
        compiler_params=pltpu.CompilerParams(dimension_semantics=("parallel", "parallel"),
                                             vmem_limit_bytes=VMEM_LIMIT),
        name="merge_ffn",
    )(x, att, ml, ga, gm, pe, *consts)


def _sample_attn_kernel(pool_ref, local_ref, *refs, n_pages, heads):
    v_refs = refs[:heads * n_pages]
    score_ref, q_ref, kn_ref, vn_ref, o_ref = refs[heads * n_pages:]
    b = pl.program_id(0)
    group = pl.program_id(1)
    scale = ATT_HEAD_DIM ** -0.5
    rows = 16
    outs = []
    for hh in range(heads):
        h = group * heads + hh
        lanes = slice(hh * ATT_HEAD_DIM, (hh + 1) * ATT_HEAD_DIM)
        s_own = jnp.sum(q_ref[0][:, lanes] * kn_ref[0][:, lanes], axis=1, keepdims=True) * scale
        pages = [local_ref[b, h * n_pages + j] for j in range(n_pages)]
        s = jnp.concatenate([score_ref[0, pg, pl.ds(h, 1), :] for pg in pages], axis=1) * scale
        v_all = jnp.concatenate([v_refs[hh * n_pages + j][0].astype(BF16) for j in range(n_pages)], axis=1)
        m = jnp.maximum(s_own, jnp.max(s, axis=1, keepdims=True))
        p_own = jnp.exp(s_own - m)
        p = jnp.exp(s - m)
        l = p_own + jnp.sum(p, axis=1, keepdims=True)
        pv = _dot_nt(jnp.broadcast_to(p, (rows, n_pages * PAGE_SIZE)).astype(BF16), v_all)[0:1]
        outs.append((p_own * vn_ref[0][:, lanes] + pv) / l)
    o_ref[0] = jnp.concatenate(outs, axis=1)


def _sample_attention(page_table, sel, scores, cache_vt, q_s, k_s, v_s):
    Bs, n_seq_pages = page_table.shape
    ppb = MOBA_BLOCK // PAGE_SIZE
    n_pages = MOBA_TOPK * ppb
    heads = SAMPLE_HEADS_PER_STEP
    local = (ppb * sel[..., None] + jnp.arange(ppb, dtype=jnp.int32)).reshape(Bs, N_ATT_HEADS * n_pages)
    pool = jnp.take_along_axis(page_table, local, axis=1)

    def tile_spec(hh, j):
        def index(b, group, pool_ref, local_ref):
            h = group * heads + hh
            return (pool_ref[b, h * n_pages + j], h, 0)
        return pl.BlockSpec((1, ATT_HEAD_DIM, PAGE_SIZE), index)

    tiles = [tile_spec(hh, j) for hh in range(heads) for j in range(n_pages)]
    row = pl.BlockSpec((1, 1, heads * ATT_HEAD_DIM), lambda b, group, pt, sl: (b, 0, group))
    grid_spec = pltpu.PrefetchScalarGridSpec(
        num_scalar_prefetch=2,
        grid=(Bs, N_ATT_HEADS // heads),
        in_specs=tiles + [pl.BlockSpec((1, n_seq_pages, N_ATT_HEADS, PAGE_SIZE),
                                       lambda b, group, pt, sl: (b, 0, 0, 0)), row, row, row],
        out_specs=row,
    )
    return pl.pallas_call(
        functools.partial(_sample_attn_kernel, n_pages=n_pages, heads=heads),
        grid_spec=grid_spec,
        out_shape=jax.ShapeDtypeStruct((Bs, 1, ATT_WIDTH), F32),
        compiler_params=pltpu.CompilerParams(dimension_semantics=("parallel", "arbitrary"),
                                             vmem_limit_bytes=VMEM_LIMIT),
        name="moba_sample_attn",
    )(pool, local, *([cache_vt] * len(tiles)), scores, q_s, k_s, v_s)


def _mlstm_step_kernel(bi_ref, bf_ref, m0_ref, mq_ref, mk_ref, mv_ref, g_ref, mo_ref, gout_ref,
                       c0_ref, n0_ref, ml_ref, c_out_ref, n_out_ref, m_out_ref):
    b = pl.program_id(0)
    dv = ML_V_DIM
    r_i = lax.broadcasted_iota(jnp.int32, (dv, dv), 0)
    c_i = lax.broadcasted_iota(jnp.int32, (dv, dv), 1)
    eye = r_i == c_i
    g = g_ref[0]
    for h in range(N_ML_HEADS):
        i_g = g[:, h:h + 1] + bi_ref[h]
        logf = _log_sigmoid(g[:, N_ML_HEADS + h:N_ML_HEADS + h + 1] + bf_ref[h])
        m_old = m0_ref[b, h]
        inter = logf + m_old
        m_t = jnp.maximum(inter, i_g)
        w_inter = jnp.exp(inter - m_t)
        w_in = jnp.exp(i_g - m_t)
        q = mq_ref[0, :, h * ML_QK_DIM:(h + 1) * ML_QK_DIM].astype(F32)
        k = mk_ref[0, :, h * ML_QK_DIM:(h + 1) * ML_QK_DIM].astype(F32)
        v = mv_ref[0, :, h * dv:(h + 1) * dv].astype(F32)
        c_old = c0_ref[0, 0, h]
        n_old = n0_ref[0, 0, h:h + 1, :]
        s = jnp.sum(q * k, axis=1, keepdims=True) * w_in
        v_col = jnp.sum(jnp.where(eye, v, 0.0), axis=1, keepdims=True)
        q_rows = jnp.broadcast_to(mq_ref[0, :, h * ML_QK_DIM:(h + 1) * ML_QK_DIM], (16, ML_QK_DIM))
        cq = _dot_nt(q_rows, c_old.astype(BF16))[0:1]
        num = w_inter * cq + s * v
        den = w_inter * jnp.sum(n_old * q, axis=1, keepdims=True) + s
        hid = num / jnp.maximum(jnp.abs(den), jnp.exp(-m_t))
        c_out_ref[0, 0, h] = w_inter * c_old + (w_in * v_col) * k
        n_out_ref[0, 0, h:h + 1, :] = w_inter * n_old + w_in * k
        m_out_ref[0, h:h + 1, :] = jnp.broadcast_to(m_t, (1, LANES))
        hn = _rms(hid, gout_ref[h:h + 1, :])
        mo = mo_ref[0, :, h * dv:(h + 1) * dv]
        ml_ref[0, :, h * dv:(h + 1) * dv] = (hn * jax.nn.sigmoid(mo)).astype(BF16)


def _mlstm_sample(mq, mk, mv, gcol, mo, state_c, state_n, state_m, pw):
    Bs = mq.shape[0]
    row = lambda w: pl.BlockSpec((1, 1, w), lambda b, *_: (b, 0, 0))
    grid_spec = pltpu.PrefetchScalarGridSpec(
        num_scalar_prefetch=3,
        grid=(Bs,),
        in_specs=[row(ML_QK_WIDTH), row(ML_QK_WIDTH), row(ML_V_WIDTH), row(LANES), row(ML_V_WIDTH),
                  pl.BlockSpec((N_ML_HEADS, ML_V_DIM), lambda b, *_: (0, 0)),
                  pl.BlockSpec((1, 1, N_ML_HEADS, ML_V_DIM, ML_QK_DIM), lambda b, *_: (0, b, 0, 0, 0)),
                  pl.BlockSpec((1, 1, N_ML_HEADS, ML_QK_DIM), lambda b, *_: (0, b, 0, 0))],
        out_specs=(row(ML_V_WIDTH),
                   pl.BlockSpec((1, 1, N_ML_HEADS, ML_V_DIM, ML_QK_DIM), lambda b, *_: (0, b, 0, 0, 0)),
                   pl.BlockSpec((1, 1, N_ML_HEADS, ML_QK_DIM), lambda b, *_: (0, b, 0, 0)),
                   pl.BlockSpec((1, N_ML_HEADS, LANES), lambda b, *_: (b, 0, 0))),
    )
    return pl.pallas_call(
        _mlstm_step_kernel,
        grid_spec=grid_spec,
        out_shape=(jax.ShapeDtypeStruct((Bs, 1, ML_V_WIDTH), BF16),
                   jax.ShapeDtypeStruct(state_c.shape, F32),
                   jax.ShapeDtypeStruct(state_n.shape, F32),
                   jax.ShapeDtypeStruct((Bs, N_ML_HEADS, LANES), F32)),
        compiler_params=pltpu.CompilerParams(dimension_semantics=("arbitrary",),
                                             vmem_limit_bytes=VMEM_LIMIT),
        name="mlstm_sample",
    )(pw["b_i"], pw["b_f"], state_m[0], mq, mk, mv, gcol, mo, pw["g_ml_out"], state_c, state_n)


def _rope_angles(pos):
    freqs = ROPE_THETA ** (-jnp.arange(0, ROPE_DIMS, 2, dtype=F32) / ROPE_DIMS)
    return pos.astype(F32)[:, None] * freqs[None, :]


def _rope_tables_cols(pos):
    ang = _rope_angles(pos).T
    return jnp.cos(ang), jnp.sin(ang)


def _rope_tables_rows(pos):
    half = ROPE_DIMS // 2
    n = pos.shape[0]
    ang = _rope_angles(pos)
    cos, sin = jnp.cos(ang), jnp.sin(ang)
    rest = ATT_HEAD_DIM - ROPE_DIMS
    zh = jnp.zeros((n, half), F32)
    cos_h = jnp.concatenate([cos, cos, jnp.ones((n, rest), F32)], axis=-1)
    sa_h = jnp.concatenate([-sin, zh, jnp.zeros((n, rest), F32)], axis=-1)
    sb_h = jnp.concatenate([zh, sin, jnp.zeros((n, rest), F32)], axis=-1)
    tile = lambda t: jnp.tile(t, (1, HEAD_PAIR))
    return tile(cos_h), tile(sa_h), tile(sb_h)


def _prepare_weights(w_in, b_igate, b_fgate, g_mix_norm, g_q_norm, g_k_norm, g_ml_out_norm,
                     w_att_branch, w_ml_branch, w_out, g_ffn_norm, w_ffn_gate, w_ffn_up, w_ffn_down,
                     w_ple, g_ple_norm, w_ple_gate):
    w_gates = w_in[:, _C_GATES:_C_MO]
    head_of = jnp.arange(ATT_WIDTH, dtype=jnp.int32) // ATT_HEAD_DIM
    w_qkv = w_in[:, _C_Q:_C_MQ].astype(BF16)
    return {
        "w_qkv": w_qkv,
        "w_qkv_t": w_qkv.T,
        "w_m": w_in[:, _C_MQ:_C_GATES].astype(BF16),
        "g_q_col": g_q_norm.reshape(1, ATT_HEAD_DIM, 1),
        "g_k_col": g_k_norm.reshape(1, ATT_HEAD_DIM, 1),
        "w_g": jnp.pad(w_gates, ((0, 0), (0, LANES - 2 * N_ML_HEADS))).astype(BF16),
        "w_gt": jnp.pad(w_gates.T, ((0, 16 - 2 * N_ML_HEADS), (0, 0))).astype(BF16),
        "w_b": w_in[:, _C_MO:_C_END].astype(BF16),
        "e_head": (head_of[:, None] == head_of[None, :]).astype(BF16),
        "g_mix": g_mix_norm.reshape(1, D_MODEL),
        "g_q": jnp.tile(g_q_norm, N_ATT_HEADS).reshape(1, ATT_WIDTH),
        "g_k": jnp.tile(g_k_norm, N_ATT_HEADS).reshape(1, ATT_WIDTH),
        "g_ml_out": g_ml_out_norm,
        "b_i": b_igate, "b_f": b_fgate,
        "w_att": w_att_branch.astype(BF16), "w_ml": w_ml_branch.astype(BF16), "w_out": w_out.astype(BF16),
        "g_ffn": g_ffn_norm.reshape(1, D_MODEL),
        "w_ffn_gate": w_ffn_gate.astype(BF16), "w_ffn_up": w_ffn_up.astype(BF16),
        "w_ffn_down": w_ffn_down.astype(BF16),
        "w_ple": w_ple.astype(BF16), "g_ple": g_ple_norm.reshape(1, D_MODEL),
        "w_ple_gate": w_ple_gate.astype(BF16),
    }


def _layer(x_p, pe_p, x_s, pe_s, cache_k, cache_v, page_table, state_c, state_n, state_m, pw):
    B, L, _ = x_p.shape
    Bs, Ls, _ = x_s.shape
    assert Ls == 1
    past_len = page_table.shape[1] * PAGE_SIZE
    n_pool = cache_k.shape[0]
    per_seq = lambda t: t.reshape(Bs, 1, t.shape[-1])
    pages_t = lambda c: jnp.transpose(c, (0, 2, 3, 1)).reshape(n_pool, ATT_WIDTH, PAGE_SIZE)
    cache_kt, cache_vt = pages_t(cache_k), pages_t(cache_v)

    xt = x_s.reshape(1, Bs, D_MODEL)
    pos_s = jnp.full((Bs,), past_len, dtype=jnp.int32)
    q, k, v, mq_s, mk_s, mv_s, gcol_s, _, mo_s, ga_s, gm_s = _project(xt, pos_s, pw, tm=Bs, feature_major=False)

    pos_p = jnp.arange(L, dtype=jnp.int32)
    qt, kt, vt, mq, mk, mv, _, grow, mo, ga, gm = _project(x_p, pos_p, pw, tm=256, feature_major=True)
    att_p = _attention_prompt(qt, kt, vt)
    ml_p, c_p, n_p, m_p, sel, scores = _mlstm_prompt(mq, mk, mv, grow, mo, pw, page_table, cache_kt,
                                                     q.reshape(Bs, ATT_WIDTH))
    y_p = _post(x_p, att_p, ml_p, ga, gm, pe_p, pw, tm=256)
    tokens_major = lambda t: jnp.transpose(t.reshape(B, N_ATT_HEADS, ATT_HEAD_DIM, L), (0, 3, 1, 2))

    att_s = _sample_attention(page_table, sel[:, :, :MOBA_TOPK], scores, cache_vt,
                              per_seq(q), per_seq(k), per_seq(v))
    ml_s, c_s, n_s, m_s = _mlstm_sample(per_seq(mq_s), per_seq(mk_s), per_seq(mv_s), per_seq(gcol_s),
                                        per_seq(mo_s), state_c, state_n, state_m, pw)
    y_s = _post(xt, att_s.reshape(1, Bs, ATT_WIDTH), ml_s.reshape(1, Bs, ML_V_WIDTH), ga_s, gm_s,
                pe_s.reshape(1, Bs, pe_s.shape[-1]), pw, tm=Bs)
    prompt = (y_p, tokens_major(kt), tokens_major(vt), c_p, n_p, m_p[:, :, 0])
    sample = (y_s.reshape(Bs, 1, D_MODEL), k, v, c_s, n_s, m_s[:, :, 0])
    return prompt, sample


def kernel(x_prompt, x_sample, cache_k, cache_v, state_mlstm_C, state_mlstm_n, state_mlstm_m, page_table,
           p_prompt, p_sample, w_in, b_igate, b_fgate, g_mix_norm, g_q_norm, g_k_norm, g_ml_out_norm,
           w_att_branch, w_ml_branch, w_out, g_ffn_norm, w_ffn_gate, w_ffn_up, w_ffn_down,
           w_ple, g_ple_norm, w_ple_gate):
    depth = w_in.shape[0]
    assert depth == 1
    Bp, Lp, _ = x_prompt.shape
    Bs, Ls, _ = x_sample.shape
    pw = _prepare_weights(w_in[0], b_igate[0], b_fgate[0], g_mix_norm[0], g_q_norm[0], g_k_norm[0],
                          g_ml_out_norm[0], w_att_branch[0], w_ml_branch[0], w_out[0], g_ffn_norm[0],
                          w_ffn_gate[0], w_ffn_up[0], w_ffn_down[0], w_ple[0], g_ple_norm[0], w_ple_gate[0])
    (yp, kp, vp, cp, n_p, mp), (ys, ks, vs, cs, ns, ms) = _layer(
        x_prompt, p_prompt[0], x_sample, p_sample[0], cache_k[0], cache_v[0], page_table,
        state_mlstm_C, state_mlstm_n, state_mlstm_m, pw)
    heads = lambda t: t.reshape(1, Bs, Ls, N_ATT_HEADS, ATT_HEAD_DIM)
    return (yp, ys, kp[None], vp[None], heads(ks), heads(vs),
            cp[None], n_p[None], mp[None], cs, ns, ms[None])
```

```python
import functools

import jax
import jax.numpy as jnp
from jax import lax
from jax.experimental import pallas as pl
from jax.experimental.pallas import tpu as pltpu

F32 = jnp.float32
BF16 = jnp.bfloat16

D_MODEL = 1024
N_ATT_HEADS = 8
ATT_HEAD_DIM = 64
ATT_WIDTH = N_ATT_HEADS * ATT_HEAD_DIM
ROPE_DIMS = ATT_HEAD_DIM // 4
ROPE_THETA = 500000.0
MOBA_BLOCK = 256
MOBA_TOPK = 3
PAGE_SIZE = 128
N_ML_HEADS = 4
ML_QK_DIM = 128
ML_V_DIM = 256
ML_QK_WIDTH = N_ML_HEADS * ML_QK_DIM
ML_V_WIDTH = N_ML_HEADS * ML_V_DIM
NORM_EPS = 1e-6
NEG_INF = -1e30
LOG2_E = 1.4426950408889634

LANES = 128
HEAD_PAIR = LANES // ATT_HEAD_DIM
ML_CHUNK = 256
VMEM_LIMIT = 56 * 1024 * 1024

_C_Q, _C_K, _C_V = 0, ATT_WIDTH, 2 * ATT_WIDTH
_C_MQ = 3 * ATT_WIDTH
_C_MK = _C_MQ + ML_QK_WIDTH
_C_MV = _C_MK + ML_QK_WIDTH
_C_GATES = _C_MV + ML_V_WIDTH
_C_MO = _C_GATES + 2 * N_ML_HEADS
_C_GA = _C_MO + ML_V_WIDTH
_C_GM = _C_GA + D_MODEL
_C_END = _C_GM + D_MODEL


def _dot(a, b):
    return jnp.dot(a, b, preferred_element_type=F32)


def _dot_nt(a, b, precision=None):
    return lax.dot_general(a, b, (((1,), (1,)), ((), ())), precision=precision,
                           preferred_element_type=F32)


def _rms(x, g):
    return x * lax.rsqrt(jnp.mean(x * x, axis=-1, keepdims=True) + NORM_EPS) * g


def _log_sigmoid(x):
    return jnp.minimum(x, 0.0) - jnp.log1p(jnp.exp(-jnp.abs(x)))


def _proj_shared(xb, wm_ref, wg_ref, wgt_ref, wb_ref,
                 mq_ref, mk_ref, mv_ref, gcol_ref, grow_ref, mo_ref, ga_ref, gm_ref):
    mq_ref[0] = _dot(xb, wm_ref[:, 0:ML_QK_WIDTH]).astype(BF16)
    mk_ref[0] = (_dot(xb, wm_ref[:, ML_QK_WIDTH:2 * ML_QK_WIDTH]) * (ML_QK_DIM ** -0.5)).astype(BF16)
    mv_ref[0] = _dot(xb, wm_ref[:, 2 * ML_QK_WIDTH:2 * ML_QK_WIDTH + ML_V_WIDTH]).astype(BF16)
    gcol_ref[0] = _dot(xb, wg_ref[...])
    grow_ref[0] = _dot_nt(wgt_ref[...], xb)
    mo_ref[0] = _dot(xb, wb_ref[:, 0:ML_V_WIDTH])
    ga_ref[0] = _dot(xb, wb_ref[:, ML_V_WIDTH:ML_V_WIDTH + D_MODEL])
    gm_ref[0] = _dot(xb, wb_ref[:, ML_V_WIDTH + D_MODEL:ML_V_WIDTH + 2 * D_MODEL])


def _proj_kernel_rows(x_ref, cos_ref, sa_ref, sb_ref, gmix_ref, gq_ref, gk_ref, e_ref, wqkv_ref,
                      wm_ref, wg_ref, wgt_ref, wb_ref, q_ref, k_ref, v_ref, *rest_refs):
    xb = _rms(x_ref[0], gmix_ref[...]).astype(BF16)
    cos, sa, sb = cos_ref[...], sa_ref[...], sb_ref[...]
    e = e_ref[...]

    def head_norm_rope(z, g):
        zz = z * z
        hi = zz.astype(BF16)
        lo = (zz - hi.astype(F32)).astype(BF16)
        ss = _dot(hi, e) + _dot(lo, e)
        zn = z * lax.rsqrt(ss * (1.0 / ATT_HEAD_DIM) + NORM_EPS) * g
        outs = []
        for grp in range(ATT_WIDTH // LANES):
            zg = zn[:, grp * LANES:(grp + 1) * LANES]
            outs.append(zg * cos + pltpu.roll(zg, LANES - ROPE_DIMS // 2, 1) * sa
                        + pltpu.roll(zg, ROPE_DIMS // 2, 1) * sb)
        return jnp.concatenate(outs, axis=-1)

    q_ref[0] = head_norm_rope(_dot(xb, wqkv_ref[:, 0:ATT_WIDTH]), gq_ref[...])
    k_ref[0] = head_norm_rope(_dot(xb, wqkv_ref[:, ATT_WIDTH:2 * ATT_WIDTH]), gk_ref[...])
    v_ref[0] = _dot(xb, wqkv_ref[:, 2 * ATT_WIDTH:3 * ATT_WIDTH])
    _proj_shared(xb, wm_ref, wg_ref, wgt_ref, wb_ref, *rest_refs)


def _proj_kernel_cols(x_ref, cos_ref, sin_ref, gmix_ref, gq_ref, gk_ref, wqkvt_ref,
                      wm_ref, wg_ref, wgt_ref, wb_ref, qt_ref, kt_ref, vt_ref, *rest_refs):
    xb = _rms(x_ref[0], gmix_ref[...]).astype(BF16)
    tm = xb.shape[0]
    cos, sin = cos_ref[...][None], sin_ref[...][None]
    half = ROPE_DIMS // 2

    def head_norm_rope(z, g):
        z3 = z.reshape(N_ATT_HEADS, ATT_HEAD_DIM, tm)
        ss = jnp.sum(z3 * z3, axis=1, keepdims=True)
        zn = z3 * lax.rsqrt(ss * (1.0 / ATT_HEAD_DIM) + NORM_EPS) * g
        x1, x2 = zn[:, 0:half, :], zn[:, half:ROPE_DIMS, :]
        out = jnp.concatenate([x1 * cos - x2 * sin, x2 * cos + x1 * sin, zn[:, ROPE_DIMS:, :]], axis=1)
        return out.reshape(ATT_WIDTH, tm)

    zt = _dot_nt(wqkvt_ref[...], xb)
    qt_ref[0] = head_norm_rope(zt[0:ATT_WIDTH], gq_ref[...])
    kt_ref[0] = head_norm_rope(zt[ATT_WIDTH:2 * ATT_WIDTH], gk_ref[...])
    vt_ref[0] = zt[2 * ATT_WIDTH:3 * ATT_WIDTH]
    _proj_shared(xb, wm_ref, wg_ref, wgt_ref, wb_ref, *rest_refs)


def _project(x, pos, pw, tm, feature_major):
    B, L, _ = x.shape
    tok = lambda w: pl.BlockSpec((1, tm, w), lambda b, i: (b, i, 0))
    full = lambda a: pl.BlockSpec(a.shape, lambda b, i: (0,) * a.ndim)
    shared = (pw["w_m"], pw["w_g"], pw["w_gt"], pw["w_b"])
    rest_shape = (
        jax.ShapeDtypeStruct((B, L, ML_QK_WIDTH), BF16),
        jax.ShapeDtypeStruct((B, L, ML_QK_WIDTH), BF16),
        jax.ShapeDtypeStruct((B, L, ML_V_WIDTH), BF16),
        jax.ShapeDtypeStruct((B, L, LANES), F32),
        jax.ShapeDtypeStruct((B, 16, L), F32),
        jax.ShapeDtypeStruct((B, L, ML_V_WIDTH), F32),
        jax.ShapeDtypeStruct((B, L, D_MODEL), F32),
        jax.ShapeDtypeStruct((B, L, D_MODEL), F32),
    )
    rest_specs = (tok(ML_QK_WIDTH), tok(ML_QK_WIDTH), tok(ML_V_WIDTH), tok(LANES),
                  pl.BlockSpec((1, 16, tm), lambda b, i: (b, 0, i)),
                  tok(ML_V_WIDTH), tok(D_MODEL), tok(D_MODEL))
    if feature_major:
        cos, sin = _rope_tables_cols(pos)
        tab = pl.BlockSpec((ROPE_DIMS // 2, tm), lambda b, i: (0, i))
        consts = (pw["g_mix"], pw["g_q_col"], pw["g_k_col"], pw["w_qkv_t"]) + shared
        body, tables, tab_specs = _proj_kernel_cols, (cos, sin), [tab, tab]
        qkv_shape = (jax.ShapeDtypeStruct((B, ATT_WIDTH, L), F32),) * 3
        qkv_specs = (pl.BlockSpec((1, ATT_WIDTH, tm), lambda b, i: (b, 0, i)),) * 3
    else:
        tables = _rope_tables_rows(pos)
        tab = pl.BlockSpec((tm, LANES), lambda b, i: (i, 0))
        consts = (pw["g_mix"], pw["g_q"], pw["g_k"], pw["e_head"], pw["w_qkv"]) + shared
        body, tab_specs = _proj_kernel_rows, [tab, tab, tab]
        qkv_shape = (jax.ShapeDtypeStruct((B, L, ATT_WIDTH), F32),) * 3
        qkv_specs = (tok(ATT_WIDTH),) * 3
    return pl.pallas_call(
        body,
        grid=(B, L // tm),
        in_specs=[tok(D_MODEL)] + tab_specs + [full(a) for a in consts],
        out_specs=qkv_specs + rest_specs,
        out_shape=qkv_shape + rest_shape,
        compiler_params=pltpu.CompilerParams(dimension_semantics=("parallel", "parallel"),
                                             vmem_limit_bytes=VMEM_LIMIT),
        name="in_proj_cols" if feature_major else "in_proj_rows",
    )(x, *tables, *consts)


def _gate_pages(part, page_refs, qbd_ref, score_ref, gate_ref):
    blocks_per_step = len(page_refs) // 2
    lane = lax.broadcasted_iota(jnp.int32, (N_ATT_HEADS, LANES), 1)
    qbd = qbd_ref[0]
    gate = gate_ref[...]
    for i in range(blocks_per_step):
        keys = jnp.concatenate([page_refs[2 * i][0].astype(BF16), page_refs[2 * i + 1][0].astype(BF16)], axis=1)
        s = _dot(qbd, keys)
        s = s[0:N_ATT_HEADS] + s[N_ATT_HEADS:]
        score_ref[0, 2 * i] = s[:, 0:PAGE_SIZE]
        score_ref[0, 2 * i + 1] = s[:, PAGE_SIZE:]
        g = jnp.sum(s, axis=1, keepdims=True) * (1.0 / MOBA_BLOCK)
        gate = jnp.where(lane == part * blocks_per_step + i, g, gate)
    gate_ref[...] = gate


def _gate_select(part, sel_ref, gate_ref, *, steps_per_seq, nblk_seq):
    lane = lax.broadcasted_iota(jnp.int32, (N_ATT_HEADS, LANES), 1)

    @pl.when(part == steps_per_seq - 1)
    def _():
        gate = gate_ref[...]
        rank = jnp.zeros((N_ATT_HEADS, LANES), F32)
        for n2 in range(nblk_seq):
            col = gate[:, n2:n2 + 1]
            beats = (col > gate) | ((col == gate) & (lane > n2))
            rank = rank + jnp.where(beats, 1.0, 0.0)
        out = jnp.zeros((N_ATT_HEADS, LANES), jnp.int32)
        lane_f = lane.astype(F32)
        for r in range(MOBA_TOPK):
            pick = (rank == float(r)) & (lane < nblk_seq)
            idx = jnp.sum(jnp.where(pick, lane_f, 0.0), axis=1, keepdims=True)
            out = jnp.where(lane == r, idx.astype(jnp.int32), out)
        sel_ref[0] = out


def _side_job(page_table, cache_kt, q_s, n_steps, n_grid, pt_pos, linear_step):
    Bs, n_seq_pages = page_table.shape
    assert (Bs * n_seq_pages) % n_steps == 0
    pps = Bs * n_seq_pages // n_steps
    assert MOBA_BLOCK == 2 * PAGE_SIZE and pps % 2 == 0 and n_seq_pages % pps == 0
    steps_per_seq = n_seq_pages // pps
    nblk_seq = n_seq_pages * PAGE_SIZE // MOBA_BLOCK
    assert nblk_seq <= LANES
    head_of = jnp.arange(ATT_WIDTH, dtype=jnp.int32) // ATT_HEAD_DIM
    q_heads = jnp.where(head_of[None, None, :] == jnp.arange(N_ATT_HEADS, dtype=jnp.int32)[None, :, None],
                        q_s[:, None, :], 0.0)
    q_lead = q_heads.astype(BF16)
    q_rows = jnp.concatenate([q_lead, (q_heads - q_lead.astype(F32)).astype(BF16)], axis=1)

    def seq_part(args):
        step = linear_step(*args[:n_grid])
        return step // steps_per_seq, step % steps_per_seq

    def page_spec(p):
        def index(*args):
            seq, part = seq_part(args)
            return (args[n_grid + pt_pos][seq, part * pps + p], 0, 0)
        return pl.BlockSpec((1, ATT_WIDTH, PAGE_SIZE), index)

    return dict(
        in_specs=[pl.BlockSpec((1, 2 * N_ATT_HEADS, ATT_WIDTH), lambda *args: (seq_part(args)[0], 0, 0))]
                 + [page_spec(p) for p in range(pps)],
        operands=[q_rows] + [cache_kt] * pps,
        out_specs=(pl.BlockSpec((1, N_ATT_HEADS, LANES), lambda *args: (seq_part(args)[0], 0, 0)),
                   pl.BlockSpec((1, pps, N_ATT_HEADS, PAGE_SIZE), lambda *args: seq_part(args) + (0, 0))),
        out_shape=(jax.ShapeDtypeStruct((Bs, N_ATT_HEADS, LANES), jnp.int32),
                   jax.ShapeDtypeStruct((Bs, n_seq_pages, N_ATT_HEADS, PAGE_SIZE), F32)),
        scratch=[pltpu.VMEM((N_ATT_HEADS, LANES), F32)],
        static=dict(n_side_pages=pps, steps_per_seq=steps_per_seq, nblk_seq=nblk_seq),
    )


def _attn_kernel(q_ref, k_ref, v_ref, o_ref, kb_ref, vt_ref, kmean_ref, acc_ref, s0_ref, s1_ref, *, nblk):
    blk = MOBA_BLOCK
    tq = 2 * blk
    ones_rows = vt_ref.shape[2] - ATT_HEAD_DIM
    t = pl.program_id(2)
    c0 = 2 * t
    nb = kmean_ref.shape[0]

    @pl.when(t == 0)
    def _():
        lane = lax.broadcasted_iota(jnp.int32, (blk, LANES), 1)
        kmean_ref[...] = jnp.zeros_like(kmean_ref)
        for n in range(nblk):
            rows = slice(n * blk, (n + 1) * blk)
            kf = k_ref[0, :, rows].T
            kb_ref[rows, 0:LANES] = kf.astype(BF16)
            kb_ref[rows, LANES:2 * LANES] = jnp.where(lane == n, 1.0, 0.0).astype(BF16)
            kmean_ref[n:n + 1, :] = jnp.sum(kf, axis=0, keepdims=True) * (1.0 / blk)
            for h in range(HEAD_PAIR):
                vh = v_ref[0, h * ATT_HEAD_DIM:(h + 1) * ATT_HEAD_DIM, rows]
                vt_ref[n, h] = jnp.concatenate([vh, jnp.ones((ones_rows, blk), F32)], axis=0).astype(BF16)

    q2 = q_ref[0]
    feat = lax.broadcasted_iota(jnp.int32, (LANES, tq), 0)
    blk_id = lax.broadcasted_iota(jnp.int32, (nb, tq), 0)
    col = lax.broadcasted_iota(jnp.int32, (nb, tq), 1)
    cur = c0 + (col >= blk).astype(jnp.int32)
    kmean = kmean_ref[...]
    bias_pad = jnp.zeros((LANES - nb, tq), F32)

    qas = []
    for h in range(HEAD_PAIR):
        qh = jnp.where((feat >= h * ATT_HEAD_DIM) & (feat < (h + 1) * ATT_HEAD_DIM), q2, 0.0)
        gate = jnp.dot(kmean, qh, precision=lax.Precision.HIGHEST,
                       preferred_element_type=F32)
        gate = jnp.where(blk_id < cur, gate, -jnp.inf)
        rank = jnp.zeros((nb, tq), F32)
        for n2 in range(nblk):
            row = gate[n2:n2 + 1, :]
            beats = (row > gate) | ((row == gate) & (blk_id > n2))
            rank = rank + jnp.where(beats, 1.0, 0.0)
        open_blk = ((blk_id < cur) & (rank < MOBA_TOPK)) | (blk_id == cur)
        bias = jnp.where(open_blk, 0.0, NEG_INF)
        qas.append(jnp.concatenate([qh * (ATT_HEAD_DIM ** -0.5 * LOG2_E), bias, bias_pad],
                                   axis=0).astype(BF16))

    def block_rows(n):
        return kb_ref[pl.ds(pl.multiple_of(n * blk, blk), blk), :]

    key_i = lax.broadcasted_iota(jnp.int32, (blk, tq), 0)
    qry_j = lax.broadcasted_iota(jnp.int32, (blk, tq), 1)
    s_refs = (s0_ref, s1_ref)

    def stage_a(first, slot, own):
        ks = [block_rows(first), block_rows(first + 1)]
        maxima = []
        for h in range(HEAD_PAIR):
            sts = [_dot(ks[0], qas[h]), _dot(ks[1], qas[h])]
            if own:
                sts = [jnp.where(key_i <= qry_j, sts[0], NEG_INF),
                       jnp.where(key_i <= qry_j - blk, sts[1], NEG_INF)]
            s_refs[slot][h, 0] = sts[0]
            s_refs[slot][h, 1] = sts[1]
            maxima.append(jnp.maximum(jnp.max(sts[0], axis=0, keepdims=True),
                                      jnp.max(sts[1], axis=0, keepdims=True)))
        return maxima

    def stage_b(first, slot, m_run, maxima):
        m_out = []
        for h in range(HEAD_PAIR):
            m_new = jnp.maximum(m_run[h], maxima[h])
            pv = (_dot(vt_ref[first, h], jnp.exp2(s_refs[slot][h, 0] - m_new).astype(BF16))
                  + _dot(vt_ref[first + 1, h], jnp.exp2(s_refs[slot][h, 1] - m_new).astype(BF16)))
            acc_ref[h] = jnp.exp2(m_run[h] - m_new) * acc_ref[h] + pv
            m_out.append(m_new)
        return m_out

    def pending_first(j):
        return jnp.where(j == 0, c0, 2 * j - 2)

    acc_ref[...] = jnp.zeros_like(acc_ref)
    m_start = [jnp.full((1, tq), NEG_INF, F32)] * HEAD_PAIR
    carry = tuple(m_start) + tuple(stage_a(c0, 0, own=True))

    def trip(j, slot, carry):
        m_run, maxima = carry[:HEAD_PAIR], carry[HEAD_PAIR:]
        new_maxima = stage_a(2 * j, 1 - slot, own=False)
        m_new = stage_b(pending_first(j), slot, m_run, maxima)
        return tuple(m_new) + tuple(new_maxima)

    carry = lax.fori_loop(0, lax.shift_right_logical(t, 1),
                          lambda i, cr: trip(2 * i + 1, 1, trip(2 * i, 0, cr)), carry)

    def odd_tail(carry):
        carry = trip(t - 1, 0, carry)
        stage_b(pending_first(t), 1, carry[:HEAD_PAIR], carry[HEAD_PAIR:])
        return 0

    def even_tail(carry):
        stage_b(pending_first(t), 0, carry[:HEAD_PAIR], carry[HEAD_PAIR:])
        return 0

    lax.cond((t & 1) == 1, odd_tail, even_tail, carry)
    outs = [acc_ref[h, 0:ATT_HEAD_DIM, :] / acc_ref[h, ATT_HEAD_DIM:ATT_HEAD_DIM + 1, :]
            for h in range(HEAD_PAIR)]
    o_ref[0] = jnp.concatenate(outs, axis=0).T.astype(o_ref.dtype)


def _attention_prompt(qt, kt, vt):
    B, _, L = qt.shape
    blk = MOBA_BLOCK
    tq = 2 * blk
    assert L % tq == 0
    nblk = L // blk
    assert nblk <= LANES
    nb = -(-nblk // 8) * 8
    ones_rows = 16
    groups = ATT_WIDTH // LANES
    return pl.pallas_call(
        functools.partial(_attn_kernel, nblk=nblk),
        grid=(B, groups, L // tq),
        in_specs=[pl.BlockSpec((1, LANES, tq), lambda b, g, i: (b, g, i)),
                  pl.BlockSpec((1, LANES, L), lambda b, g, i: (b, g, 0)),
                  pl.BlockSpec((1, LANES, L), lambda b, g, i: (b, g, 0))],
        out_specs=pl.BlockSpec((1, tq, LANES), lambda b, g, i: (b, i, g)),
        out_shape=jax.ShapeDtypeStruct((B, L, ATT_WIDTH), BF16),
        scratch_shapes=[pltpu.VMEM((L, 2 * LANES), BF16),
                        pltpu.VMEM((nblk, HEAD_PAIR, ATT_HEAD_DIM + ones_rows, blk), BF16),
                        pltpu.VMEM((nb, LANES), F32),
                        pltpu.VMEM((HEAD_PAIR, ATT_HEAD_DIM + ones_rows, tq), F32),
                        pltpu.VMEM((HEAD_PAIR, 2, blk, tq), F32),
                        pltpu.VMEM((HEAD_PAIR, 2, blk, tq), F32)],
        compiler_params=pltpu.CompilerParams(dimension_semantics=("parallel", "parallel", "arbitrary"),
                                             vmem_limit_bytes=VMEM_LIMIT),
        name="moba_prompt",
    )(qt, kt, vt)


def _mlstm_kernel(bi_ref, bf_ref, pt_ref, mq_ref, mk_ref, mv_ref, grow_ref, mo_ref, gout_ref, qbd_ref, *refs,
                  chunk, n_side_pages, steps_per_seq, nblk_seq):
    page_refs = refs[:n_side_pages]
    (ml_ref, c_out_ref, n_out_ref, m_out_ref, sel_ref, score_ref, cn_ref, m_ref,
     gate_ref) = refs[n_side_pages:]
    j = pl.program_id(1)
    dk, dv = ML_QK_DIM, ML_V_DIM
    part = lax.rem(pl.program_id(0) * pl.num_programs(1) + j, steps_per_seq)

    @pl.when(j == 0)
    def _():
        cn_ref[...] = jnp.zeros_like(cn_ref)
        m_ref[...] = jnp.zeros_like(m_ref)

    @pl.when(part == 0)
    def _():
        gate_ref[...] = jnp.zeros_like(gate_ref)

    t_i = lax.broadcasted_iota(jnp.int32, (chunk, chunk), 0)
    s_i = lax.broadcasted_iota(jnp.int32, (chunk, chunk), 1)
    tril = s_i <= t_i
    tril_b = jnp.where(tril, 1.0, 0.0).astype(BF16)
    triu_b = jnp.where(t_i <= s_i, 1.0, 0.0).astype(BF16)
    ones_cols = jnp.ones((chunk, LANES), BF16)
    grow = grow_ref[0]
    f_rows = _log_sigmoid(grow[N_ML_HEADS:2 * N_ML_HEADS, :]
                          + jnp.concatenate([jnp.full((1, 1), bf_ref[h], F32) for h in range(N_ML_HEADS)], axis=0))
    hi = f_rows.astype(BF16).astype(F32)
    mid = (f_rows - hi).astype(BF16).astype(F32)
    pieces = jnp.concatenate([hi, mid, f_rows - hi - mid], axis=1)
    pieces = jnp.concatenate([pieces, jnp.zeros((16 - N_ML_HEADS, 3 * chunk), F32)], axis=0).astype(BF16)
    rows = _dot(pieces, jnp.concatenate([triu_b] * 3, axis=0))
    cols = _dot_nt(jnp.concatenate([tril_b] * 3, axis=1), pieces)
    gates = []
    for h in range(N_ML_HEADS):
        i_row = grow[h:h + 1, :] + bi_ref[h]
        b_row = rows[h:h + 1]
        b_col = cols[:, h:h + 1]
        a_row = (i_row - b_row) * LOG2_E
        m_old = m_ref[h][:, 0:1]
        a_low = jnp.where(tril, a_row, NEG_INF)
        mm_col = jnp.maximum(jnp.max(a_low, axis=1, keepdims=True), m_old * LOG2_E)
        w_inter = jnp.exp2(m_old * LOG2_E - mm_col)
        floor_col = jnp.exp2(-(b_col * LOG2_E + mm_col))
        decay = jnp.exp2(a_low - mm_col)
        gates.append((b_row, a_row, m_old, mm_col, w_inter, floor_col, decay))

    _gate_pages(part, page_refs, qbd_ref, score_ref, gate_ref)

    hids = []
    for h in range(N_ML_HEADS):
        b_row, a_row, m_old, mm_col, w_inter, floor_col, decay = gates[h]
        q = mq_ref[0, :, h * dk:(h + 1) * dk]
        k = mk_ref[0, :, h * dk:(h + 1) * dk]
        v_ones = jnp.concatenate([mv_ref[0, :, h * dv:(h + 1) * dv], ones_cols], axis=1)
        cn = cn_ref[h]
        s = _dot_nt(q, k) * decay
        inter = _dot(q, cn.astype(BF16))
        intra = _dot(s.astype(BF16), v_ones)
        num = w_inter * inter[:, 0:dv] + intra[:, 0:dv]
        den = w_inter * inter[:, dv:dv + 1] + intra[:, dv:dv + 1]
        hids.append(num / jnp.maximum(jnp.abs(den), floor_col))

        mm_end = mm_col[chunk - 1:chunk, :]
        w_row = jnp.exp2(a_row - mm_end)
        kw = (k.astype(F32).T * w_row).astype(BF16)
        cn_ref[h] = jnp.exp2(m_old * LOG2_E - mm_end) * cn + _dot(kw, v_ones)
        m_ref[h] = jnp.broadcast_to(b_row[:, chunk - 1:chunk] + mm_end * (1.0 / LOG2_E), (1, LANES))

    for h in range(N_ML_HEADS):
        hn = _rms(hids[h], gout_ref[h:h + 1, :])
        mo = mo_ref[0, :, h * dv:(h + 1) * dv]
        ml_ref[0, :, h * dv:(h + 1) * dv] = (hn * jax.nn.sigmoid(mo)).astype(BF16)

    @pl.when(j == pl.num_programs(1) - 1)
    def _():
        for h in range(N_ML_HEADS):
            c_out_ref[0, h] = cn_ref[h, :, 0:dv].T
            n_out_ref[0, h:h + 1, :] = cn_ref[h, :, dv:dv + LANES].T[0:1, :]
            m_out_ref[0, h:h + 1, :] = m_ref[h]

    _gate_select(part, sel_ref, gate_ref, steps_per_seq=steps_per_seq, nblk_seq=nblk_seq)


def _mlstm_prompt(mq, mk, mv, grow, mo, pw, page_table, cache_kt, q_s):
    B, L, _ = mq.shape
    chunk = ML_CHUNK
    n_chunks = L // chunk
    side = _side_job(page_table, cache_kt, q_s, n_steps=B * n_chunks, n_grid=2, pt_pos=2,
                     linear_step=lambda b, j: b * n_chunks + j)
    tok = lambda w: pl.BlockSpec((1, chunk, w), lambda b, j, *_: (b, j, 0))
    per_b = lambda shp: pl.BlockSpec((1,) + shp, lambda b, j, *_: (b,) + (0,) * len(shp))
    grid_spec = pltpu.PrefetchScalarGridSpec(
        num_scalar_prefetch=3,
        grid=(B, n_chunks),
        in_specs=[tok(ML_QK_WIDTH), tok(ML_QK_WIDTH), tok(ML_V_WIDTH),
                  pl.BlockSpec((1, 16, chunk), lambda b, j, *_: (b, 0, j)), tok(ML_V_WIDTH),
                  pl.BlockSpec((N_ML_HEADS, ML_V_DIM), lambda b, j, *_: (0, 0))] + side["in_specs"],
        out_specs=(tok(ML_V_WIDTH), per_b((N_ML_HEADS, ML_V_DIM, ML_QK_DIM)),
                   per_b((N_ML_HEADS, ML_QK_DIM)), per_b((N_ML_HEADS, LANES))) + side["out_specs"],
        scratch_shapes=[pltpu.VMEM((N_ML_HEADS, ML_QK_DIM, ML_V_DIM + LANES), F32),
                        pltpu.VMEM((N_ML_HEADS, 1, LANES), F32)] + side["scratch"],
    )
    return pl.pallas_call(
        functools.partial(_mlstm_kernel, chunk=chunk, **side["static"]),
        grid_spec=grid_spec,
        out_shape=(jax.ShapeDtypeStruct((B, L, ML_V_WIDTH), BF16),
                   jax.ShapeDtypeStruct((B, N_ML_HEADS, ML_V_DIM, ML_QK_DIM), F32),
                   jax.ShapeDtypeStruct((B, N_ML_HEADS, ML_QK_DIM), F32),
                   jax.ShapeDtypeStruct((B, N_ML_HEADS, LANES), F32)) + side["out_shape"],
        compiler_params=pltpu.CompilerParams(dimension_semantics=("arbitrary", "arbitrary"),
                                             vmem_limit_bytes=VMEM_LIMIT),
        name="mlstm_prompt",
    )(pw["b_i"], pw["b_f"], page_table, mq, mk, mv, grow, mo, pw["g_ml_out"], *side["operands"])


def _post_kernel(x_ref, att_ref, ml_ref, ga_ref, gm_ref, pe_ref,
                 watt_ref, wml_ref, wout_ref, gffn_ref, wgate_ref, wup_ref, wdown_ref,
                 wple_ref, gple_ref, wpg_ref, y_ref):
    mix = (jax.nn.sigmoid(ga_ref[0]) * _dot(att_ref[0].astype(BF16), watt_ref[...])
           + jax.nn.sigmoid(gm_ref[0]) * _dot(ml_ref[0].astype(BF16), wml_ref[...]))
    h = x_ref[0] + _dot(mix.astype(BF16), wout_ref[...])
    hb = _rms(h, gffn_ref[...]).astype(BF16)
    act = jax.nn.silu(_dot(hb, wgate_ref[...])) * _dot(hb, wup_ref[...])
    h = h + _dot(act.astype(BF16), wdown_ref[...])
    gate = jax.nn.sigmoid(_dot(_rms(h, gple_ref[...]).astype(BF16), wpg_ref[...]))
    y_ref[0] = h + gate * _dot(pe_ref[0].astype(BF16), wple_ref[...])


def _post(x, att, ml, ga, gm, pe, pw, tm):
    B, L, _ = x.shape
    tok = lambda w: pl.BlockSpec((1, tm, w), lambda b, i: (b, i, 0))
    full = lambda a: pl.BlockSpec(a.shape, lambda b, i: (0,) * a.ndim, pipeline_mode=pl.Buffered(1))
    consts = (pw["w_att"], pw["w_ml"], pw["w_out"], pw["g_ffn"], pw["w_ffn_gate"], pw["w_ffn_up"],
              pw["w_ffn_down"], pw["w_ple"], pw["g_ple"], pw["w_ple_gate"])
    return pl.pallas_call(
        _post_kernel,
        grid=(B, L // tm),
        in_specs=[tok(D_MODEL), tok(ATT_WIDTH), tok(ML_V_WIDTH), tok(D_MODEL), tok(D_MODEL),
                  tok(pe.shape[-1])] + [full(a) for a in consts],
        out_specs=tok(D_MODEL),
        out_shape=jax.ShapeDtypeStruct((B, L, D_MODEL), F32),
        compiler_params=pltpu.CompilerParams(dimension_semantics=("parallel", "parallel"),
                                             vmem_limit_bytes=VMEM_LIMIT),
        name="merge_ffn",
    )(x, att, ml, ga, gm, pe, *consts)


def _row(ref, b):
    if ref.dtype == F32:
        return ref[0, pl.ds(b, 1), :]
    group = ref[0, pl.ds(pl.multiple_of(lax.shift_right_logical(b, 4) * 16, 16), 16), :].astype(F32)
    pick = lax.broadcasted_iota(jnp.int32, group.shape, 0) == (b & 15)
    return jnp.sum(jnp.where(pick, group, 0.0), axis=0, keepdims=True)


def _sample_attn_kernel(pool_ref, local_ref, *refs, n_pages):
    v_refs = refs[:N_ATT_HEADS * n_pages]
    score_ref, q_ref, kn_ref, vn_ref, o_ref = refs[N_ATT_HEADS * n_pages:]
    b = pl.program_id(0)
    scale = ATT_HEAD_DIM ** -0.5
    rows = 16
    q_row, kn_row, vn_row = _row(q_ref, b), _row(kn_ref, b), _row(vn_ref, b)
    outs = []
    for h in range(N_ATT_HEADS):
        lanes = slice(h * ATT_HEAD_DIM, (h + 1) * ATT_HEAD_DIM)
        s_own = jnp.sum(q_row[:, lanes] * kn_row[:, lanes], axis=1, keepdims=True) * scale
        pages = [local_ref[b, h * n_pages + j] for j in range(n_pages)]
        s = jnp.concatenate([score_ref[0, pg, h:h + 1, :] for pg in pages], axis=1) * scale
        v_all = jnp.concatenate([v_refs[h * n_pages + j][0].astype(BF16) for j in range(n_pages)], axis=1)
        m = jnp.maximum(s_own, jnp.max(s, axis=1, keepdims=True))
        p_own = jnp.exp(s_own - m)
        p = jnp.exp(s - m)
        l = p_own + jnp.sum(p, axis=1, keepdims=True)
        pv = _dot_nt(jnp.broadcast_to(p, (rows, n_pages * PAGE_SIZE)).astype(BF16), v_all)[0:1]
        outs.append((p_own * vn_row[:, lanes] + pv) / l)
    o_ref[0, pl.ds(b, 1), :] = jnp.concatenate(outs, axis=1)


def _sample_attention(page_table, sel, scores, cache_vt, q_s, k_s, v_s):
    Bs, n_seq_pages = page_table.shape
    ppb = MOBA_BLOCK // PAGE_SIZE
    n_pages = MOBA_TOPK * ppb
    local = (ppb * sel[..., None] + jnp.arange(ppb, dtype=jnp.int32)).reshape(Bs, N_ATT_HEADS * n_pages)
    pool = jnp.take_along_axis(page_table, local, axis=1)

    def tile_spec(h, j):
        return pl.BlockSpec((1, ATT_HEAD_DIM, PAGE_SIZE),
                            lambda b, pool_ref, local_ref: (pool_ref[b, h * n_pages + j], h, 0))

    tiles = [tile_spec(h, j) for h in range(N_ATT_HEADS) for j in range(n_pages)]
    rows = pl.BlockSpec((1, Bs, ATT_WIDTH), lambda b, *_: (0, 0, 0))
    grid_spec = pltpu.PrefetchScalarGridSpec(
        num_scalar_prefetch=2,
        grid=(Bs,),
        in_specs=tiles + [pl.BlockSpec((1, n_seq_pages, N_ATT_HEADS, PAGE_SIZE), lambda b, *_: (b, 0, 0, 0)),
                          rows, rows, rows],
        out_specs=rows,
    )
    return pl.pallas_call(
        functools.partial(_sample_attn_kernel, n_pages=n_pages),
        grid_spec=grid_spec,
        out_shape=jax.ShapeDtypeStruct((1, Bs, ATT_WIDTH), F32),
        compiler_params=pltpu.CompilerParams(dimension_semantics=("arbitrary",),
                                             vmem_limit_bytes=VMEM_LIMIT),
        name="moba_sample_attn",
    )(pool, local, *([cache_vt] * len(tiles)), scores, q_s, k_s, v_s)


def _mlstm_step_kernel(bi_ref, bf_ref, m0_ref, mq_ref, mk_ref, mv_ref, g_ref, mo_ref, gout_ref,
                       c0_ref, n0_ref, ml_ref, c_out_ref, n_out_ref, m_out_ref):
    b = pl.program_id(0)
    dv = ML_V_DIM
    r_i = lax.broadcasted_iota(jnp.int32, (dv, dv), 0)
    c_i = lax.broadcasted_iota(jnp.int32, (dv, dv), 1)
    eye = r_i == c_i
    g = _row(g_ref, b)
    q_all, k_all, v_all, mo_all = _row(mq_ref, b), _row(mk_ref, b), _row(mv_ref, b), _row(mo_ref, b)
    outs = []
    for h in range(N_ML_HEADS):
        i_g = g[:, h:h + 1] + bi_ref[h]
        logf = _log_sigmoid(g[:, N_ML_HEADS + h:N_ML_HEADS + h + 1] + bf_ref[h])
        m_old = m0_ref[b, h]
        inter = logf + m_old
        m_t = jnp.maximum(inter, i_g)
        w_inter = jnp.exp(inter - m_t)
        w_in = jnp.exp(i_g - m_t)
        q = q_all[:, h * ML_QK_DIM:(h + 1) * ML_QK_DIM]
        k = k_all[:, h * ML_QK_DIM:(h + 1) * ML_QK_DIM]
        v = v_all[:, h * dv:(h + 1) * dv]
        c_old = c0_ref[0, 0, h]
        n_old = n0_ref[0, 0, h:h + 1, :]
        s = jnp.sum(q * k, axis=1, keepdims=True) * w_in
        v_col = jnp.sum(jnp.where(eye, v, 0.0), axis=1, keepdims=True)
        q_rows = jnp.broadcast_to(q, (16, ML_QK_DIM)).astype(BF16)
        cq = _dot_nt(q_rows, c_old.astype(BF16))[0:1]
        num = w_inter * cq + s * v
        den = w_inter * jnp.sum(n_old * q, axis=1, keepdims=True) + s
        hid = num / jnp.maximum(jnp.abs(den), jnp.exp(-m_t))
        c_out_ref[0, 0, h] = w_inter * c_old + (w_in * v_col) * k
        n_out_ref[0, 0, h:h + 1, :] = w_inter * n_old + w_in * k
        m_out_ref[0, h:h + 1, :] = jnp.broadcast_to(m_t, (1, LANES))
        hn = _rms(hid, gout_ref[h:h + 1, :])
        outs.append(hn * jax.nn.sigmoid(mo_all[:, h * dv:(h + 1) * dv]))
    ml_ref[0, pl.ds(b, 1), :] = jnp.concatenate(outs, axis=1)


def _mlstm_sample(mq, mk, mv, gcol, mo, state_c, state_n, state_m, pw):
    Bs = mq.shape[1]
    assert Bs % 16 == 0
    rows = lambda w: pl.BlockSpec((1, Bs, w), lambda b, *_: (0, 0, 0))
    grid_spec = pltpu.PrefetchScalarGridSpec(
        num_scalar_prefetch=3,
        grid=(Bs,),
        in_specs=[rows(ML_QK_WIDTH), rows(ML_QK_WIDTH), rows(ML_V_WIDTH), rows(LANES), rows(ML_V_WIDTH),
                  pl.BlockSpec((N_ML_HEADS, ML_V_DIM), lambda b, *_: (0, 0)),
                  pl.BlockSpec((1, 1, N_ML_HEADS, ML_V_DIM, ML_QK_DIM), lambda b, *_: (0, b, 0, 0, 0)),
                  pl.BlockSpec((1, 1, N_ML_HEADS, ML_QK_DIM), lambda b, *_: (0, b, 0, 0))],
        out_specs=(rows(ML_V_WIDTH),
                   pl.BlockSpec((1, 1, N_ML_HEADS, ML_V_DIM, ML_QK_DIM), lambda b, *_: (0, b, 0, 0, 0)),
                   pl.BlockSpec((1, 1, N_ML_HEADS, ML_QK_DIM), lambda b, *_: (0, b, 0, 0)),
                   pl.BlockSpec((1, N_ML_HEADS, LANES), lambda b, *_: (b, 0, 0))),
    )
    return pl.pallas_call(
        _mlstm_step_kernel,
        grid_spec=grid_spec,
        out_shape=(jax.ShapeDtypeStruct((1, Bs, ML_V_WIDTH), F32),
                   jax.ShapeDtypeStruct(state_c.shape, F32),
                   jax.ShapeDtypeStruct(state_n.shape, F32),
                   jax.ShapeDtypeStruct((Bs, N_ML_HEADS, LANES), F32)),
        compiler_params=pltpu.CompilerParams(dimension_semantics=("arbitrary",),
                                             vmem_limit_bytes=VMEM_LIMIT),
        name="mlstm_sample",
    )(pw["b_i"], pw["b_f"], state_m[0], mq, mk, mv, gcol, mo, pw["g_ml_out"], state_c, state_n)


def _rope_angles(pos):
    freqs = ROPE_THETA ** (-jnp.arange(0, ROPE_DIMS, 2, dtype=F32) / ROPE_DIMS)
    return pos.astype(F32)[:, None] * freqs[None, :]


def _rope_tables_cols(pos):
    ang = _rope_angles(pos).T
    return jnp.cos(ang), jnp.sin(ang)


def _rope_tables_rows(pos):
    half = ROPE_DIMS // 2
    n = pos.shape[0]
    ang = _rope_angles(pos)
    cos, sin = jnp.cos(ang), jnp.sin(ang)
    rest = ATT_HEAD_DIM - ROPE_DIMS
    zh = jnp.zeros((n, half), F32)
    cos_h = jnp.concatenate([cos, cos, jnp.ones((n, rest), F32)], axis=-1)
    sa_h = jnp.concatenate([-sin, zh, jnp.zeros((n, rest), F32)], axis=-1)
    sb_h = jnp.concatenate([zh, sin, jnp.zeros((n, rest), F32)], axis=-1)
    tile = lambda t: jnp.tile(t, (1, HEAD_PAIR))
    return tile(cos_h), tile(sa_h), tile(sb_h)


def _prepare_weights(w_in, b_igate, b_fgate, g_mix_norm, g_q_norm, g_k_norm, g_ml_out_norm,
                     w_att_branch, w_ml_branch, w_out, g_ffn_norm, w_ffn_gate, w_ffn_up, w_ffn_down,
                     w_ple, g_ple_norm, w_ple_gate):
    w_gates = w_in[:, _C_GATES:_C_MO]
    head_of = jnp.arange(ATT_WIDTH, dtype=jnp.int32) // ATT_HEAD_DIM
    w_qkv = w_in[:, _C_Q:_C_MQ].astype(BF16)
    return {
        "w_qkv": w_qkv,
        "w_qkv_t": w_qkv.T,
        "w_m": w_in[:, _C_MQ:_C_GATES].astype(BF16),
        "g_q_col": g_q_norm.reshape(1, ATT_HEAD_DIM, 1),
        "g_k_col": g_k_norm.reshape(1, ATT_HEAD_DIM, 1),
        "w_g": jnp.pad(w_gates, ((0, 0), (0, LANES - 2 * N_ML_HEADS))).astype(BF16),
        "w_gt": jnp.pad(w_gates.T, ((0, 16 - 2 * N_ML_HEADS), (0, 0))).astype(BF16),
        "w_b": w_in[:, _C_MO:_C_END].astype(BF16),
        "e_head": (head_of[:, None] == head_of[None, :]).astype(BF16),
        "g_mix": g_mix_norm.reshape(1, D_MODEL),
        "g_q": jnp.tile(g_q_norm, N_ATT_HEADS).reshape(1, ATT_WIDTH),
        "g_k": jnp.tile(g_k_norm, N_ATT_HEADS).reshape(1, ATT_WIDTH),
        "g_ml_out": g_ml_out_norm,
        "b_i": b_igate, "b_f": b_fgate,
        "w_att": w_att_branch.astype(BF16), "w_ml": w_ml_branch.astype(BF16), "w_out": w_out.astype(BF16),
        "g_ffn": g_ffn_norm.reshape(1, D_MODEL),
        "w_ffn_gate": w_ffn_gate.astype(BF16), "w_ffn_up": w_ffn_up.astype(BF16),
        "w_ffn_down": w_ffn_down.astype(BF16),
        "w_ple": w_ple.astype(BF16), "g_ple": g_ple_norm.reshape(1, D_MODEL),
        "w_ple_gate": w_ple_gate.astype(BF16),
    }


def _layer(x_p, pe_p, x_s, pe_s, cache_k, cache_v, page_table, state_c, state_n, state_m, pw):
    B, L, _ = x_p.shape
    Bs, Ls, _ = x_s.shape
    assert Ls == 1
    past_len = page_table.shape[1] * PAGE_SIZE
    n_pool = cache_k.shape[0]
    pages_t = lambda c: jnp.transpose(c, (0, 2, 3, 1)).reshape(n_pool, ATT_WIDTH, PAGE_SIZE)
    cache_kt, cache_vt = pages_t(cache_k), pages_t(cache_v)

    xt = x_s.reshape(1, Bs, D_MODEL)
    pos_s = jnp.full((Bs,), past_len, dtype=jnp.int32)
    q, k, v, mq_s, mk_s, mv_s, gcol_s, _, mo_s, ga_s, gm_s = _project(xt, pos_s, pw, tm=Bs, feature_major=False)

    pos_p = jnp.arange(L, dtype=jnp.int32)
    qt, kt, vt, mq, mk, mv, _, grow, mo, ga, gm = _project(x_p, pos_p, pw, tm=256, feature_major=True)
    att_p = _attention_prompt(qt, kt, vt)
    ml_p, c_p, n_p, m_p, sel, scores = _mlstm_prompt(mq, mk, mv, grow, mo, pw, page_table, cache_kt,
                                                     q.reshape(Bs, ATT_WIDTH))
    y_p = _post(x_p, att_p, ml_p, ga, gm, pe_p, pw, tm=256)
    tokens_major = lambda t: jnp.transpose(t.reshape(B, N_ATT_HEADS, ATT_HEAD_DIM, L), (0, 3, 1, 2))

    att_s = _sample_attention(page_table, sel[:, :, :MOBA_TOPK], scores, cache_vt, q, k, v)
    ml_s, c_s, n_s, m_s = _mlstm_sample(mq_s, mk_s, mv_s, gcol_s, mo_s, state_c, state_n, state_m, pw)
    y_s = _post(xt, att_s, ml_s, ga_s, gm_s, pe_s.reshape(1, Bs, pe_s.shape[-1]), pw, tm=Bs)
    prompt = (y_p, tokens_major(kt), tokens_major(vt), c_p, n_p, m_p[:, :, 0])
    sample = (y_s.reshape(Bs, 1, D_MODEL), k, v, c_s, n_s, m_s[:, :, 0])
    return prompt, sample


def kernel(x_prompt, x_sample, cache_k, cache_v, state_mlstm_C, state_mlstm_n, state_mlstm_m, page_table,
           p_prompt, p_sample, w_in, b_igate, b_fgate, g_mix_norm, g_q_norm, g_k_norm, g_ml_out_norm,
           w_att_branch, w_ml_branch, w_out, g_ffn_norm, w_ffn_gate, w_ffn_up, w_ffn_down,
           w_ple, g_ple_norm, w_ple_gate):
    depth = w_in.shape[0]
    assert depth == 1
    Bp, Lp, _ = x_prompt.shape
    Bs, Ls, _ = x_sample.shape
    pw = _prepare_weights(w_in[0], b_igate[0], b_fgate[0], g_mix_norm[0], g_q_norm[0], g_k_norm[0],
                          g_ml_out_norm[0], w_att_branch[0], w_ml_branch[0], w_out[0], g_ffn_norm[0],
                          w_ffn_gate[0], w_ffn_up[0], w_ffn_down[0], w_ple[0], g_ple_norm[0], w_ple_gate[0])
    (yp, kp, vp, cp, n_p, mp), (ys, ks, vs, cs, ns, ms) = _layer(
        x_prompt, p_prompt[0], x_sample, p_sample[0], cache_k[0], cache_v[0], page_table,
        state_mlstm_C, state_mlstm_n, state_mlstm_m, pw)
    heads = lambda t: t.reshape(1, Bs, Ls, N_ATT_HEADS, ATT_HEAD_DIM)
    return (yp, ys, kp[None], vp[None], heads(ks), heads(vs),
            cp[None], n_p[None], mp[None], cs, ns, ms[None])
```

```python
import functools

import jax
import jax.numpy as jnp
from jax import lax
from jax.experimental import pallas as pl
from jax.experimental.pallas import tpu as pltpu

F32 = jnp.float32
BF16 = jnp.bfloat16

D_MODEL = 1024
N_ATT_HEADS = 8
ATT_HEAD_DIM = 64
ATT_WIDTH = N_ATT_HEADS * ATT_HEAD_DIM
ROPE_DIMS = ATT_HEAD_DIM // 4
ROPE_THETA = 500000.0
MOBA_BLOCK = 256
MOBA_TOPK = 3
PAGE_SIZE = 128
N_ML_HEADS = 4
ML_QK_DIM = 128
ML_V_DIM = 256
ML_QK_WIDTH = N_ML_HEADS * ML_QK_DIM
ML_V_WIDTH = N_ML_HEADS * ML_V_DIM
NORM_EPS = 1e-6
NEG_INF = -1e30
LOG2_E = 1.4426950408889634

LANES = 128
HEAD_PAIR = LANES // ATT_HEAD_DIM
ML_CHUNK = 256
VMEM_LIMIT = 56 * 1024 * 1024

_C_Q, _C_K, _C_V = 0, ATT_WIDTH, 2 * ATT_WIDTH
_C_MQ = 3 * ATT_WIDTH
_C_MK = _C_MQ + ML_QK_WIDTH
_C_MV = _C_MK + ML_QK_WIDTH
_C_GATES = _C_MV + ML_V_WIDTH
_C_MO = _C_GATES + 2 * N_ML_HEADS
_C_GA = _C_MO + ML_V_WIDTH
_C_GM = _C_GA + D_MODEL
_C_END = _C_GM + D_MODEL


def _dot(a, b):
    return jnp.dot(a, b, preferred_element_type=F32)


def _dot_nt(a, b, precision=None):
    return lax.dot_general(a, b, (((1,), (1,)), ((), ())), precision=precision,
                           preferred_element_type=F32)


def _rms(x, g):
    return x * lax.rsqrt(jnp.mean(x * x, axis=-1, keepdims=True) + NORM_EPS) * g


def _log_sigmoid(x):
    return jnp.minimum(x, 0.0) - jnp.log1p(jnp.exp(-jnp.abs(x)))


def _proj_shared(xb, wm_ref, wg_ref, wgt_ref, wb_ref,
                 mq_ref, mk_ref, mv_ref, gcol_ref, grow_ref, mo_ref, ga_ref, gm_ref):
    mq_ref[0] = _dot(xb, wm_ref[:, 0:ML_QK_WIDTH]).astype(BF16)
    mk_ref[0] = (_dot(xb, wm_ref[:, ML_QK_WIDTH:2 * ML_QK_WIDTH]) * (ML_QK_DIM ** -0.5)).astype(BF16)
    mv_ref[0] = _dot(xb, wm_ref[:, 2 * ML_QK_WIDTH:2 * ML_QK_WIDTH + ML_V_WIDTH]).astype(BF16)
    gcol_ref[0] = _dot(xb, wg_ref[...])
    grow_ref[0] = _dot_nt(wgt_ref[...], xb)
    mo_ref[0] = _dot(xb, wb_ref[:, 0:ML_V_WIDTH])
    ga_ref[0] = _dot(xb, wb_ref[:, ML_V_WIDTH:ML_V_WIDTH + D_MODEL])
    gm_ref[0] = _dot(xb, wb_ref[:, ML_V_WIDTH + D_MODEL:ML_V_WIDTH + 2 * D_MODEL])


def _proj_kernel_rows(x_ref, cos_ref, sa_ref, sb_ref, gmix_ref, gq_ref, gk_ref, e_ref, wqkv_ref,
                      wm_ref, wg_ref, wgt_ref, wb_ref, q_ref, k_ref, v_ref, *rest_refs):
    xb = _rms(x_ref[0], gmix_ref[...]).astype(BF16)
    cos, sa, sb = cos_ref[...], sa_ref[...], sb_ref[...]
    e = e_ref[...]

    def head_norm_rope(z, g):
        zz = z * z
        hi = zz.astype(BF16)
        lo = (zz - hi.astype(F32)).astype(BF16)
        ss = _dot(hi, e) + _dot(lo, e)
        zn = z * lax.rsqrt(ss * (1.0 / ATT_HEAD_DIM) + NORM_EPS) * g
        outs = []
        for grp in range(ATT_WIDTH // LANES):
            zg = zn[:, grp * LANES:(grp + 1) * LANES]
            outs.append(zg * cos + pltpu.roll(zg, LANES - ROPE_DIMS // 2, 1) * sa
                        + pltpu.roll(zg, ROPE_DIMS // 2, 1) * sb)
        return jnp.concatenate(outs, axis=-1)

    q_ref[0] = head_norm_rope(_dot(xb, wqkv_ref[:, 0:ATT_WIDTH]), gq_ref[...])
    k_ref[0] = head_norm_rope(_dot(xb, wqkv_ref[:, ATT_WIDTH:2 * ATT_WIDTH]), gk_ref[...])
    v_ref[0] = _dot(xb, wqkv_ref[:, 2 * ATT_WIDTH:3 * ATT_WIDTH])
    _proj_shared(xb, wm_ref, wg_ref, wgt_ref, wb_ref, *rest_refs)


def _proj_kernel_cols(x_ref, cos_ref, sin_ref, gmix_ref, gq_ref, gk_ref, wqkvt_ref,
                      wm_ref, wg_ref, wgt_ref, wb_ref, qt_ref, kt_ref, vt_ref, *rest_refs):
    xb = _rms(x_ref[0], gmix_ref[...]).astype(BF16)
    tm = xb.shape[0]
    cos, sin = cos_ref[...][None], sin_ref[...][None]
    half = ROPE_DIMS // 2

    def head_norm_rope(z, g):
        z3 = z.reshape(N_ATT_HEADS, ATT_HEAD_DIM, tm)
        ss = jnp.sum(z3 * z3, axis=1, keepdims=True)
        zn = z3 * lax.rsqrt(ss * (1.0 / ATT_HEAD_DIM) + NORM_EPS) * g
        x1, x2 = zn[:, 0:half, :], zn[:, half:ROPE_DIMS, :]
        out = jnp.concatenate([x1 * cos - x2 * sin, x2 * cos + x1 * sin, zn[:, ROPE_DIMS:, :]], axis=1)
        return out.reshape(ATT_WIDTH, tm)

    zt = _dot_nt(wqkvt_ref[...], xb)
    qt_ref[0] = head_norm_rope(zt[0:ATT_WIDTH], gq_ref[...])
    kt_ref[0] = head_norm_rope(zt[ATT_WIDTH:2 * ATT_WIDTH], gk_ref[...])
    vt_ref[0] = zt[2 * ATT_WIDTH:3 * ATT_WIDTH]
    _proj_shared(xb, wm_ref, wg_ref, wgt_ref, wb_ref, *rest_refs)


def _project(x, pos, pw, tm, feature_major):
    B, L, _ = x.shape
    tok = lambda w: pl.BlockSpec((1, tm, w), lambda b, i: (b, i, 0))
    full = lambda a: pl.BlockSpec(a.shape, lambda b, i: (0,) * a.ndim)
    shared = (pw["w_m"], pw["w_g"], pw["w_gt"], pw["w_b"])
    rest_shape = (
        jax.ShapeDtypeStruct((B, L, ML_QK_WIDTH), BF16),
        jax.ShapeDtypeStruct((B, L, ML_QK_WIDTH), BF16),
        jax.ShapeDtypeStruct((B, L, ML_V_WIDTH), BF16),
        jax.ShapeDtypeStruct((B, L, LANES), F32),
        jax.ShapeDtypeStruct((B, 16, L), F32),
        jax.ShapeDtypeStruct((B, L, ML_V_WIDTH), F32),
        jax.ShapeDtypeStruct((B, L, D_MODEL), F32),
        jax.ShapeDtypeStruct((B, L, D_MODEL), F32),
    )
    rest_specs = (tok(ML_QK_WIDTH), tok(ML_QK_WIDTH), tok(ML_V_WIDTH), tok(LANES),
                  pl.BlockSpec((1, 16, tm), lambda b, i: (b, 0, i)),
                  tok(ML_V_WIDTH), tok(D_MODEL), tok(D_MODEL))
    if feature_major:
        cos, sin = _rope_tables_cols(pos)
        tab = pl.BlockSpec((ROPE_DIMS // 2, tm), lambda b, i: (0, i))
        consts = (pw["g_mix"], pw["g_q_col"], pw["g_k_col"], pw["w_qkv_t"]) + shared
        body, tables, tab_specs = _proj_kernel_cols, (cos, sin), [tab, tab]
        qkv_shape = (jax.ShapeDtypeStruct((B, ATT_WIDTH, L), F32),) * 3
        qkv_specs = (pl.BlockSpec((1, ATT_WIDTH, tm), lambda b, i: (b, 0, i)),) * 3
    else:
        tables = _rope_tables_rows(pos)
        tab = pl.BlockSpec((tm, LANES), lambda b, i: (i, 0))
        consts = (pw["g_mix"], pw["g_q"], pw["g_k"], pw["e_head"], pw["w_qkv"]) + shared
        body, tab_specs = _proj_kernel_rows, [tab, tab, tab]
        qkv_shape = (jax.ShapeDtypeStruct((B, L, ATT_WIDTH), F32),) * 3
        qkv_specs = (tok(ATT_WIDTH),) * 3
    return pl.pallas_call(
        body,
        grid=(B, L // tm),
        in_specs=[tok(D_MODEL)] + tab_specs + [full(a) for a in consts],
        out_specs=qkv_specs + rest_specs,
        out_shape=qkv_shape + rest_shape,
        compiler_params=pltpu.CompilerParams(dimension_semantics=("parallel", "parallel"),
                                             vmem_limit_bytes=VMEM_LIMIT),
        name="in_proj_cols" if feature_major else "in_proj_rows",
    )(x, *tables, *consts)


def _gate_pages(part, page_refs, qbd_ref, score_ref, gate_ref):
    blocks_per_step = len(page_refs) // 2
    lane = lax.broadcasted_iota(jnp.int32, (N_ATT_HEADS, LANES), 1)
    qbd = qbd_ref[0]
    gate = gate_ref[...]
    for i in range(blocks_per_step):
        keys = jnp.concatenate([page_refs[2 * i][0].astype(BF16), page_refs[2 * i + 1][0].astype(BF16)], axis=1)
        s = _dot(qbd, keys)
        s = s[0:N_ATT_HEADS] + s[N_ATT_HEADS:]
        score_ref[0, 2 * i] = s[:, 0:PAGE_SIZE]
        score_ref[0, 2 * i + 1] = s[:, PAGE_SIZE:]
        g = jnp.sum(s, axis=1, keepdims=True) * (1.0 / MOBA_BLOCK)
        gate = jnp.where(lane == part * blocks_per_step + i, g, gate)
    gate_ref[...] = gate


def _gate_select(part, sel_ref, gate_ref, *, steps_per_seq, nblk_seq):
    lane = lax.broadcasted_iota(jnp.int32, (N_ATT_HEADS, LANES), 1)

    @pl.when(part == steps_per_seq - 1)
    def _():
        gate = gate_ref[...]
        rank = jnp.zeros((N_ATT_HEADS, LANES), F32)
        for n2 in range(nblk_seq):
            col = gate[:, n2:n2 + 1]
            beats = (col > gate) | ((col == gate) & (lane > n2))
            rank = rank + jnp.where(beats, 1.0, 0.0)
        out = jnp.zeros((N_ATT_HEADS, LANES), jnp.int32)
        lane_f = lane.astype(F32)
        for r in range(MOBA_TOPK):
            pick = (rank == float(r)) & (lane < nblk_seq)
            idx = jnp.sum(jnp.where(pick, lane_f, 0.0), axis=1, keepdims=True)
            out = jnp.where(lane == r, idx.astype(jnp.int32), out)
        sel_ref[0] = out


def _side_job(page_table, cache_kt, q_s, n_steps, n_grid, pt_pos, linear_step):
    Bs, n_seq_pages = page_table.shape
    assert (Bs * n_seq_pages) % n_steps == 0
    pps = Bs * n_seq_pages // n_steps
    assert MOBA_BLOCK == 2 * PAGE_SIZE and pps % 2 == 0 and n_seq_pages % pps == 0
    steps_per_seq = n_seq_pages // pps
    nblk_seq = n_seq_pages * PAGE_SIZE // MOBA_BLOCK
    assert nblk_seq <= LANES
    head_of = jnp.arange(ATT_WIDTH, dtype=jnp.int32) // ATT_HEAD_DIM
    q_heads = jnp.where(head_of[None, None, :] == jnp.arange(N_ATT_HEADS, dtype=jnp.int32)[None, :, None],
                        q_s[:, None, :], 0.0)
    q_lead = q_heads.astype(BF16)
    q_rows = jnp.concatenate([q_lead, (q_heads - q_lead.astype(F32)).astype(BF16)], axis=1)

    def seq_part(args):
        step = linear_step(*args[:n_grid])
        return step // steps_per_seq, step % steps_per_seq

    def page_spec(p):
        def index(*args):
            seq, part = seq_part(args)
            return (args[n_grid + pt_pos][seq, part * pps + p], 0, 0)
        return pl.BlockSpec((1, ATT_WIDTH, PAGE_SIZE), index)

    return dict(
        in_specs=[pl.BlockSpec((1, 2 * N_ATT_HEADS, ATT_WIDTH), lambda *args: (seq_part(args)[0], 0, 0))]
                 + [page_spec(p) for p in range(pps)],
        operands=[q_rows] + [cache_kt] * pps,
        out_specs=(pl.BlockSpec((1, N_ATT_HEADS, LANES), lambda *args: (seq_part(args)[0], 0, 0)),
                   pl.BlockSpec((1, pps, N_ATT_HEADS, PAGE_SIZE), lambda *args: seq_part(args) + (0, 0))),
        out_shape=(jax.ShapeDtypeStruct((Bs, N_ATT_HEADS, LANES), jnp.int32),
                   jax.ShapeDtypeStruct((Bs, n_seq_pages, N_ATT_HEADS, PAGE_SIZE), F32)),
        scratch=[pltpu.VMEM((N_ATT_HEADS, LANES), F32)],
        static=dict(n_side_pages=pps, steps_per_seq=steps_per_seq, nblk_seq=nblk_seq),
    )


def _attn_kernel(q_ref, k_ref, v_ref, o_ref, kb_ref, vt_ref, kmean_ref, acc_ref, s0_ref, s1_ref, *, nblk):
    blk = MOBA_BLOCK
    tq = 2 * blk
    ones_rows = vt_ref.shape[2] - ATT_HEAD_DIM
    t = pl.program_id(2)
    c0 = 2 * t
    nb = kmean_ref.shape[0]

    @pl.when(t == 0)
    def _():
        lane = lax.broadcasted_iota(jnp.int32, (blk, LANES), 1)
        kmean_ref[...] = jnp.zeros_like(kmean_ref)
        for n in range(nblk):
            rows = slice(n * blk, (n + 1) * blk)
            kf = k_ref[0, :, rows].T
            kb_ref[rows, 0:LANES] = kf.astype(BF16)
            kb_ref[rows, LANES:2 * LANES] = jnp.where(lane == n, 1.0, 0.0).astype(BF16)
            kmean_ref[n:n + 1, :] = jnp.sum(kf, axis=0, keepdims=True) * (1.0 / blk)
            for h in range(HEAD_PAIR):
                vh = v_ref[0, h * ATT_HEAD_DIM:(h + 1) * ATT_HEAD_DIM, rows]
                vt_ref[n, h] = jnp.concatenate([vh, jnp.ones((ones_rows, blk), F32)], axis=0).astype(BF16)

    q2 = q_ref[0]
    feat = lax.broadcasted_iota(jnp.int32, (LANES, tq), 0)
    blk_id = lax.broadcasted_iota(jnp.int32, (nb, tq), 0)
    col = lax.broadcasted_iota(jnp.int32, (nb, tq), 1)
    cur = c0 + (col >= blk).astype(jnp.int32)
    kmean = kmean_ref[...]
    bias_pad = jnp.zeros((LANES - nb, tq), F32)

    qas = []
    for h in range(HEAD_PAIR):
        qh = jnp.where((feat >= h * ATT_HEAD_DIM) & (feat < (h + 1) * ATT_HEAD_DIM), q2, 0.0)
        gate = jnp.dot(kmean, qh, precision=lax.Precision.HIGHEST,
                       preferred_element_type=F32)
        gate = jnp.where(blk_id < cur, gate, -jnp.inf)
        rank = jnp.zeros((nb, tq), F32)
        for n2 in range(nblk):
            row = gate[n2:n2 + 1, :]
            beats = (row > gate) | ((row == gate) & (blk_id > n2))
            rank = rank + jnp.where(beats, 1.0, 0.0)
        open_blk = ((blk_id < cur) & (rank < MOBA_TOPK)) | (blk_id == cur)
        bias = jnp.where(open_blk, 0.0, NEG_INF)
        qas.append(jnp.concatenate([qh * (ATT_HEAD_DIM ** -0.5 * LOG2_E), bias, bias_pad],
                                   axis=0).astype(BF16))

    def block_rows(n):
        return kb_ref[pl.ds(pl.multiple_of(n * blk, blk), blk), :]

    key_i = lax.broadcasted_iota(jnp.int32, (blk, tq), 0)
    qry_j = lax.broadcasted_iota(jnp.int32, (blk, tq), 1)
    s_refs = (s0_ref, s1_ref)

    def stage_a(first, slot, own, heads=range(HEAD_PAIR)):
        ks = [block_rows(first), block_rows(first + 1)]
        maxima = []
        for h in heads:
            tops = []
            for i in range(2):
                st = _dot(ks[i], qas[h])
                if own:
                    st = jnp.where(key_i <= qry_j - i * blk, st, NEG_INF)
                s_refs[slot][h, i] = st
                tops.append(jnp.max(st, axis=0, keepdims=True))
            maxima.append(jnp.maximum(tops[0], tops[1]))
        return maxima

    def stage_b(first, slot, m_run, maxima, heads=range(HEAD_PAIR)):
        m_out = []
        for h in heads:
            m_new = jnp.maximum(m_run[h], maxima[h])
            pv = (_dot(vt_ref[first, h], jnp.exp2(s_refs[slot][h, 0] - m_new).astype(BF16))
                  + _dot(vt_ref[first + 1, h], jnp.exp2(s_refs[slot][h, 1] - m_new).astype(BF16)))
            acc_ref[h] = jnp.exp2(m_run[h] - m_new) * acc_ref[h] + pv
            m_out.append(m_new)
        return m_out

    def pending_first(j):
        return jnp.where(j == 0, c0, 2 * j - 2)

    acc_ref[...] = jnp.zeros_like(acc_ref)
    m_start = [jnp.full((1, tq), NEG_INF, F32)] * HEAD_PAIR
    carry = tuple(m_start) + tuple(stage_a(c0, 0, own=True))

    def trip(j, slot, carry):
        m_run, maxima = carry[:HEAD_PAIR], carry[HEAD_PAIR:]
        new_maxima, m_new = [], []
        for h in range(HEAD_PAIR):
            new_maxima += stage_a(2 * j, 1 - slot, own=False, heads=(h,))
            m_new += stage_b(pending_first(j), slot, m_run, maxima, heads=(h,))
        return tuple(m_new) + tuple(new_maxima)

    carry = lax.fori_loop(0, lax.shift_right_logical(t, 1),
                          lambda i, cr: trip(2 * i + 1, 1, trip(2 * i, 0, cr)), carry)

    def odd_tail(carry):
        carry = trip(t - 1, 0, carry)
        stage_b(pending_first(t), 1, carry[:HEAD_PAIR], carry[HEAD_PAIR:])
        return 0

    def even_tail(carry):
        stage_b(pending_first(t), 0, carry[:HEAD_PAIR], carry[HEAD_PAIR:])
        return 0

    lax.cond((t & 1) == 1, odd_tail, even_tail, carry)
    outs = [acc_ref[h, 0:ATT_HEAD_DIM, :] / acc_ref[h, ATT_HEAD_DIM:ATT_HEAD_DIM + 1, :]
            for h in range(HEAD_PAIR)]
    o_ref[0] = jnp.concatenate(outs, axis=0).T.astype(o_ref.dtype)


def _attention_prompt(qt, kt, vt):
    B, _, L = qt.shape
    blk = MOBA_BLOCK
    tq = 2 * blk
    assert L % tq == 0
    nblk = L // blk
    assert nblk <= LANES
    nb = -(-nblk // 8) * 8
    ones_rows = 16
    groups = ATT_WIDTH // LANES
    return pl.pallas_call(
        functools.partial(_attn_kernel, nblk=nblk),
        grid=(B, groups, L // tq),
        in_specs=[pl.BlockSpec((1, LANES, tq), lambda b, g, i: (b, g, i)),
                  pl.BlockSpec((1, LANES, L), lambda b, g, i: (b, g, 0)),
                  pl.BlockSpec((1, LANES, L), lambda b, g, i: (b, g, 0))],
        out_specs=pl.BlockSpec((1, tq, LANES), lambda b, g, i: (b, i, g)),
        out_shape=jax.ShapeDtypeStruct((B, L, ATT_WIDTH), BF16),
        scratch_shapes=[pltpu.VMEM((L, 2 * LANES), BF16),
                        pltpu.VMEM((nblk, HEAD_PAIR, ATT_HEAD_DIM + ones_rows, blk), BF16),
                        pltpu.VMEM((nb, LANES), F32),
                        pltpu.VMEM((HEAD_PAIR, ATT_HEAD_DIM + ones_rows, tq), F32),
                        pltpu.VMEM((HEAD_PAIR, 2, blk, tq), F32),
                        pltpu.VMEM((HEAD_PAIR, 2, blk, tq), F32)],
        compiler_params=pltpu.CompilerParams(dimension_semantics=("parallel", "parallel", "arbitrary"),
                                             vmem_limit_bytes=VMEM_LIMIT),
        name="moba_prompt",
    )(qt, kt, vt)


def _mlstm_kernel(bi_ref, bf_ref, pt_ref, mq_ref, mk_ref, mv_ref, grow_ref, mo_ref, gout_ref, qbd_ref, *refs,
                  chunk, n_side_pages, steps_per_seq, nblk_seq):
    page_refs = refs[:n_side_pages]
    (ml_ref, c_out_ref, n_out_ref, m_out_ref, sel_ref, score_ref, cn_ref, m_ref,
     gate_ref) = refs[n_side_pages:]
    j = pl.program_id(1)
    dk, dv = ML_QK_DIM, ML_V_DIM
    part = lax.rem(pl.program_id(0) * pl.num_programs(1) + j, steps_per_seq)

    @pl.when(j == 0)
    def _():
        cn_ref[...] = jnp.zeros_like(cn_ref)
        m_ref[...] = jnp.zeros_like(m_ref)

    @pl.when(part == 0)
    def _():
        gate_ref[...] = jnp.zeros_like(gate_ref)

    t_i = lax.broadcasted_iota(jnp.int32, (chunk, chunk), 0)
    s_i = lax.broadcasted_iota(jnp.int32, (chunk, chunk), 1)
    tril = s_i <= t_i
    tril_b = jnp.where(tril, 1.0, 0.0).astype(BF16)
    triu_b = jnp.where(t_i <= s_i, 1.0, 0.0).astype(BF16)
    ones_cols = jnp.ones((chunk, LANES), BF16)
    grow = grow_ref[0]
    f_rows = _log_sigmoid(grow[N_ML_HEADS:2 * N_ML_HEADS, :]
                          + jnp.concatenate([jnp.full((1, 1), bf_ref[h], F32) for h in range(N_ML_HEADS)], axis=0))
    hi = f_rows.astype(BF16).astype(F32)
    mid = (f_rows - hi).astype(BF16).astype(F32)
    pieces = jnp.concatenate([hi, mid, f_rows - hi - mid], axis=1)
    pieces = jnp.concatenate([pieces, jnp.zeros((16 - N_ML_HEADS, 3 * chunk), F32)], axis=0).astype(BF16)
    rows = _dot(pieces, jnp.concatenate([triu_b] * 3, axis=0))
    cols = _dot_nt(jnp.concatenate([tril_b] * 3, axis=1), pieces)
    gates = []
    for h in range(N_ML_HEADS):
        i_row = grow[h:h + 1, :] + bi_ref[h]
        b_row = rows[h:h + 1]
        b_col = cols[:, h:h + 1]
        a_row = (i_row - b_row) * LOG2_E
        m_old = m_ref[h][:, 0:1]
        a_low = jnp.where(tril, a_row, NEG_INF)
        mm_col = jnp.maximum(jnp.max(a_low, axis=1, keepdims=True), m_old * LOG2_E)
        w_inter = jnp.exp2(m_old * LOG2_E - mm_col)
        floor_col = jnp.exp2(-(b_col * LOG2_E + mm_col))
        decay = jnp.exp2(a_low - mm_col)
        gates.append((b_row, a_row, m_old, mm_col, w_inter, floor_col, decay))

    _gate_pages(part, page_refs, qbd_ref, score_ref, gate_ref)

    hids = []
    for h in range(N_ML_HEADS):
        b_row, a_row, m_old, mm_col, w_inter, floor_col, decay = gates[h]
        q = mq_ref[0, :, h * dk:(h + 1) * dk]
        k = mk_ref[0, :, h * dk:(h + 1) * dk]
        v_ones = jnp.concatenate([mv_ref[0, :, h * dv:(h + 1) * dv], ones_cols], axis=1)
        cn = cn_ref[h]
        s = _dot_nt(q, k) * decay
        inter = _dot(q, cn.astype(BF16))
        intra = _dot(s.astype(BF16), v_ones)
        num = w_inter * inter[:, 0:dv] + intra[:, 0:dv]
        den = w_inter * inter[:, dv:dv + 1] + intra[:, dv:dv + 1]
        hids.append(num / jnp.maximum(jnp.abs(den), floor_col))

        mm_end = mm_col[chunk - 1:chunk, :]
        w_row = jnp.exp2(a_row - mm_end)
        kw = (k.astype(F32).T * w_row).astype(BF16)
        cn_ref[h] = jnp.exp2(m_old * LOG2_E - mm_end) * cn + _dot(kw, v_ones)
        m_ref[h] = jnp.broadcast_to(b_row[:, chunk - 1:chunk] + mm_end * (1.0 / LOG2_E), (1, LANES))

    for h in range(N_ML_HEADS):
        hn = _rms(hids[h], gout_ref[h:h + 1, :])
        mo = mo_ref[0, :, h * dv:(h + 1) * dv]
        ml_ref[0, :, h * dv:(h + 1) * dv] = (hn * jax.nn.sigmoid(mo)).astype(BF16)

    @pl.when(j == pl.num_programs(1) - 1)
    def _():
        for h in range(N_ML_HEADS):
            c_out_ref[0, h] = cn_ref[h, :, 0:dv].T
            n_out_ref[0, h:h + 1, :] = cn_ref[h, :, dv:dv + LANES].T[0:1, :]
            m_out_ref[0, h:h + 1, :] = m_ref[h]

    _gate_select(part, sel_ref, gate_ref, steps_per_seq=steps_per_seq, nblk_seq=nblk_seq)


def _mlstm_prompt(mq, mk, mv, grow, mo, pw, page_table, cache_kt, q_s):
    B, L, _ = mq.shape
    chunk = ML_CHUNK
    n_chunks = L // chunk
    side = _side_job(page_table, cache_kt, q_s, n_steps=B * n_chunks, n_grid=2, pt_pos=2,
                     linear_step=lambda b, j: b * n_chunks + j)
    tok = lambda w: pl.BlockSpec((1, chunk, w), lambda b, j, *_: (b, j, 0))
    per_b = lambda shp: pl.BlockSpec((1,) + shp, lambda b, j, *_: (b,) + (0,) * len(shp))
    grid_spec = pltpu.PrefetchScalarGridSpec(
        num_scalar_prefetch=3,
        grid=(B, n_chunks),
        in_specs=[tok(ML_QK_WIDTH), tok(ML_QK_WIDTH), tok(ML_V_WIDTH),
                  pl.BlockSpec((1, 16, chunk), lambda b, j, *_: (b, 0, j)), tok(ML_V_WIDTH),
                  pl.BlockSpec((N_ML_HEADS, ML_V_DIM), lambda b, j, *_: (0, 0))] + side["in_specs"],
        out_specs=(tok(ML_V_WIDTH), per_b((N_ML_HEADS, ML_V_DIM, ML_QK_DIM)),
                   per_b((N_ML_HEADS, ML_QK_DIM)), per_b((N_ML_HEADS, LANES))) + side["out_specs"],
        scratch_shapes=[pltpu.VMEM((N_ML_HEADS, ML_QK_DIM, ML_V_DIM + LANES), F32),
                        pltpu.VMEM((N_ML_HEADS, 1, LANES), F32)] + side["scratch"],
    )
    return pl.pallas_call(
        functools.partial(_mlstm_kernel, chunk=chunk, **side["static"]),
        grid_spec=grid_spec,
        out_shape=(jax.ShapeDtypeStruct((B, L, ML_V_WIDTH), BF16),
                   jax.ShapeDtypeStruct((B, N_ML_HEADS, ML_V_DIM, ML_QK_DIM), F32),
                   jax.ShapeDtypeStruct((B, N_ML_HEADS, ML_QK_DIM), F32),
                   jax.ShapeDtypeStruct((B, N_ML_HEADS, LANES), F32)) + side["out_shape"],
        compiler_params=pltpu.CompilerParams(dimension_semantics=("arbitrary", "arbitrary"),
                                             vmem_limit_bytes=VMEM_LIMIT),
        name="mlstm_prompt",
    )(pw["b_i"], pw["b_f"], page_table, mq, mk, mv, grow, mo, pw["g_ml_out"], *side["operands"])


def _post_kernel(x_ref, att_ref, ml_ref, ga_ref, gm_ref, pe_ref,
                 watt_ref, wml_ref, wout_ref, gffn_ref, wgate_ref, wup_ref, wdown_ref,
                 wple_ref, gple_ref, wpg_ref, y_ref):
    mix = (jax.nn.sigmoid(ga_ref[0]) * _dot(att_ref[0].astype(BF16), watt_ref[...])
           + jax.nn.sigmoid(gm_ref[0]) * _dot(ml_ref[0].astype(BF16), wml_ref[...]))
    h = x_ref[0] + _dot(mix.astype(BF16), wout_ref[...])
    hb = _rms(h, gffn_ref[...]).astype(BF16)
    act = jax.nn.silu(_dot(hb, wgate_ref[...])) * _dot(hb, wup_ref[...])
    h = h + _dot(act.astype(BF16), wdown_ref[...])
    gate = jax.nn.sigmoid(_dot(_rms(h, gple_ref[...]).astype(BF16), wpg_ref[...]))
    y_ref[0] = h + gate * _dot(pe_ref[0].astype(BF16), wple_ref[...])


def _post(x, att, ml, ga, gm, pe, pw, tm):
    B, L, _ = x.shape
    tok = lambda w: pl.BlockSpec((1, tm, w), lambda b, i: (b, i, 0))
    full = lambda a: pl.BlockSpec(a.shape, lambda b, i: (0,) * a.ndim, pipeline_mode=pl.Buffered(1))
    consts = (pw["w_att"], pw["w_ml"], pw["w_out"], pw["g_ffn"], pw["w_ffn_gate"], pw["w_ffn_up"],
              pw["w_ffn_down"], pw["w_ple"], pw["g_ple"], pw["w_ple_gate"])
    return pl.pallas_call(
        _post_kernel,
        grid=(B, L // tm),
        in_specs=[tok(D_MODEL), tok(ATT_WIDTH), tok(ML_V_WIDTH), tok(D_MODEL), tok(D_MODEL),
                  tok(pe.shape[-1])] + [full(a) for a in consts],
        out_specs=tok(D_MODEL),
        out_shape=jax.ShapeDtypeStruct((B, L, D_MODEL), F32),
        compiler_params=pltpu.CompilerParams(dimension_semantics=("parallel", "parallel"),
                                             vmem_limit_bytes=VMEM_LIMIT),
        name="merge_ffn",
    )(x, att, ml, ga, gm, pe, *consts)


def _row(ref, b):
    if ref.dtype == F32:
        return ref[0, pl.ds(b, 1), :]
    group = ref[0, pl.ds(pl.multiple_of(lax.shift_right_logical(b, 4) * 16, 16), 16), :].astype(F32)
    pick = lax.broadcasted_iota(jnp.int32, group.shape, 0) == (b & 15)
    return jnp.sum(jnp.where(pick, group, 0.0), axis=0, keepdims=True)


def _sample_attn_kernel(pool_ref, local_ref, *refs, n_pages):
    v_refs = refs[:N_ATT_HEADS * n_pages]
    score_ref, q_ref, kn_ref, vn_ref, o_ref = refs[N_ATT_HEADS * n_pages:]
    b = pl.program_id(0)
    scale = ATT_HEAD_DIM ** -0.5
    rows = 16
    q_row, kn_row, vn_row = _row(q_ref, b), _row(kn_ref, b), _row(vn_ref, b)
    outs = []
    for h in range(N_ATT_HEADS):
        lanes = slice(h * ATT_HEAD_DIM, (h + 1) * ATT_HEAD_DIM)
        s_own = jnp.sum(q_row[:, lanes] * kn_row[:, lanes], axis=1, keepdims=True) * scale
        pages = [local_ref[b, h * n_pages + j] for j in range(n_pages)]
        s = jnp.concatenate([score_ref[0, pg, h:h + 1, :] for pg in pages], axis=1) * scale
        v_all = jnp.concatenate([v_refs[h * n_pages + j][0].astype(BF16) for j in range(n_pages)], axis=1)
        m = jnp.maximum(s_own, jnp.max(s, axis=1, keepdims=True))
        p_own = jnp.exp(s_own - m)
        p = jnp.exp(s - m)
        l = p_own + jnp.sum(p, axis=1, keepdims=True)
        pv = _dot_nt(jnp.broadcast_to(p, (rows, n_pages * PAGE_SIZE)).astype(BF16), v_all)[0:1]
        outs.append((p_own * vn_row[:, lanes] + pv) / l)
    o_ref[0, pl.ds(b, 1), :] = jnp.concatenate(outs, axis=1)


def _sample_attention(page_table, sel, scores, cache_vt, q_s, k_s, v_s):
    Bs, n_seq_pages = page_table.shape
    ppb = MOBA_BLOCK // PAGE_SIZE
    n_pages = MOBA_TOPK * ppb
    local = (ppb * sel[..., None] + jnp.arange(ppb, dtype=jnp.int32)).reshape(Bs, N_ATT_HEADS * n_pages)
    pool = jnp.take_along_axis(page_table, local, axis=1)

    def tile_spec(h, j):
        return pl.BlockSpec((1, ATT_HEAD_DIM, PAGE_SIZE),
                            lambda b, pool_ref, local_ref: (pool_ref[b, h * n_pages + j], h, 0))

    tiles = [tile_spec(h, j) for h in range(N_ATT_HEADS) for j in range(n_pages)]
    rows = pl.BlockSpec((1, Bs, ATT_WIDTH), lambda b, *_: (0, 0, 0))
    grid_spec = pltpu.PrefetchScalarGridSpec(
        num_scalar_prefetch=2,
        grid=(Bs,),
        in_specs=tiles + [pl.BlockSpec((1, n_seq_pages, N_ATT_HEADS, PAGE_SIZE), lambda b, *_: (b, 0, 0, 0)),
                          rows, rows, rows],
        out_specs=rows,
    )
    return pl.pallas_call(
        functools.partial(_sample_attn_kernel, n_pages=n_pages),
        grid_spec=grid_spec,
        out_shape=jax.ShapeDtypeStruct((1, Bs, ATT_WIDTH), F32),
        compiler_params=pltpu.CompilerParams(dimension_semantics=("arbitrary",),
                                             vmem_limit_bytes=VMEM_LIMIT),
        name="moba_sample_attn",
    )(pool, local, *([cache_vt] * len(tiles)), scores, q_s, k_s, v_s)


def _mlstm_step_kernel(bi_ref, bf_ref, m0_ref, mq_ref, mk_ref, mv_ref, g_ref, mo_ref, gout_ref,
                       c0_ref, n0_ref, ml_ref, c_out_ref, n_out_ref, m_out_ref):
    b = pl.program_id(0)
    dv = ML_V_DIM
    r_i = lax.broadcasted_iota(jnp.int32, (dv, dv), 0)
    c_i = lax.broadcasted_iota(jnp.int32, (dv, dv), 1)
    eye = r_i == c_i
    g = _row(g_ref, b)
    q_all, k_all, v_all, mo_all = _row(mq_ref, b), _row(mk_ref, b), _row(mv_ref, b), _row(mo_ref, b)
    outs = []
    for h in range(N_ML_HEADS):
        i_g = g[:, h:h + 1] + bi_ref[h]
        logf = _log_sigmoid(g[:, N_ML_HEADS + h:N_ML_HEADS + h + 1] + bf_ref[h])
        m_old = m0_ref[b, h]
        inter = logf + m_old
        m_t = jnp.maximum(inter, i_g)
        w_inter = jnp.exp(inter - m_t)
        w_in = jnp.exp(i_g - m_t)
        q = q_all[:, h * ML_QK_DIM:(h + 1) * ML_QK_DIM]
        k = k_all[:, h * ML_QK_DIM:(h + 1) * ML_QK_DIM]
        v = v_all[:, h * dv:(h + 1) * dv]
        c_old = c0_ref[0, 0, h]
        n_old = n0_ref[0, 0, h:h + 1, :]
        s = jnp.sum(q * k, axis=1, keepdims=True) * w_in
        v_col = jnp.sum(jnp.where(eye, v, 0.0), axis=1, keepdims=True)
        q_rows = jnp.broadcast_to(q, (16, ML_QK_DIM)).astype(BF16)
        cq = _dot_nt(q_rows, c_old.astype(BF16))[0:1]
        num = w_inter * cq + s * v
        den = w_inter * jnp.sum(n_old * q, axis=1, keepdims=True) + s
        hid = num / jnp.maximum(jnp.abs(den), jnp.exp(-m_t))
        c_out_ref[0, 0, h] = w_inter * c_old + (w_in * v_col) * k
        n_out_ref[0, 0, h:h + 1, :] = w_inter * n_old + w_in * k
        m_out_ref[0, h:h + 1, :] = jnp.broadcast_to(m_t, (1, LANES))
        hn = _rms(hid, gout_ref[h:h + 1, :])
        outs.append(hn * jax.nn.sigmoid(mo_all[:, h * dv:(h + 1) * dv]))
    ml_ref[0, pl.ds(b, 1), :] = jnp.concatenate(outs, axis=1)


def _mlstm_sample(mq, mk, mv, gcol, mo, state_c, state_n, state_m, pw):
    Bs = mq.shape[1]
    assert Bs % 16 == 0
    rows = lambda w: pl.BlockSpec((1, Bs, w), lambda b, *_: (0, 0, 0))
    grid_spec = pltpu.PrefetchScalarGridSpec(
        num_scalar_prefetch=3,
        grid=(Bs,),
        in_specs=[rows(ML_QK_WIDTH), rows(ML_QK_WIDTH), rows(ML_V_WIDTH), rows(LANES), rows(ML_V_WIDTH),
                  pl.BlockSpec((N_ML_HEADS, ML_V_DIM), lambda b, *_: (0, 0)),
                  pl.BlockSpec((1, 1, N_ML_HEADS, ML_V_DIM, ML_QK_DIM), lambda b, *_: (0, b, 0, 0, 0)),
                  pl.BlockSpec((1, 1, N_ML_HEADS, ML_QK_DIM), lambda b, *_: (0, b, 0, 0))],
        out_specs=(rows(ML_V_WIDTH),
                   pl.BlockSpec((1, 1, N_ML_HEADS, ML_V_DIM, ML_QK_DIM), lambda b, *_: (0, b, 0, 0, 0)),
                   pl.BlockSpec((1, 1, N_ML_HEADS, ML_QK_DIM), lambda b, *_: (0, b, 0, 0)),
                   pl.BlockSpec((1, N_ML_HEADS, LANES), lambda b, *_: (b, 0, 0))),
    )
    return pl.pallas_call(
        _mlstm_step_kernel,
        grid_spec=grid_spec,
        out_shape=(jax.ShapeDtypeStruct((1, Bs, ML_V_WIDTH), F32),
                   jax.ShapeDtypeStruct(state_c.shape, F32),
                   jax.ShapeDtypeStruct(state_n.shape, F32),
                   jax.ShapeDtypeStruct((Bs, N_ML_HEADS, LANES), F32)),
        compiler_params=pltpu.CompilerParams(dimension_semantics=("arbitrary",),
                                             vmem_limit_bytes=VMEM_LIMIT),
        name="mlstm_sample",
    )(pw["b_i"], pw["b_f"], state_m[0], mq, mk, mv, gcol, mo, pw["g_ml_out"], state_c, state_n)


def _rope_angles(pos):
    freqs = ROPE_THETA ** (-jnp.arange(0, ROPE_DIMS, 2, dtype=F32) / ROPE_DIMS)
    return pos.astype(F32)[:, None] * freqs[None, :]


def _rope_tables_cols(pos):
    ang = _rope_angles(pos).T
    return jnp.cos(ang), jnp.sin(ang)


def _rope_tables_rows(pos):
    half = ROPE_DIMS // 2
    n = pos.shape[0]
    ang = _rope_angles(pos)
    cos, sin = jnp.cos(ang), jnp.sin(ang)
    rest = ATT_HEAD_DIM - ROPE_DIMS
    zh = jnp.zeros((n, half), F32)
    cos_h = jnp.concatenate([cos, cos, jnp.ones((n, rest), F32)], axis=-1)
    sa_h = jnp.concatenate([-sin, zh, jnp.zeros((n, rest), F32)], axis=-1)
    sb_h = jnp.concatenate([zh, sin, jnp.zeros((n, rest), F32)], axis=-1)
    tile = lambda t: jnp.tile(t, (1, HEAD_PAIR))
    return tile(cos_h), tile(sa_h), tile(sb_h)


def _prepare_weights(w_in, b_igate, b_fgate, g_mix_norm, g_q_norm, g_k_norm, g_ml_out_norm,
                     w_att_branch, w_ml_branch, w_out, g_ffn_norm, w_ffn_gate, w_ffn_up, w_ffn_down,
                     w_ple, g_ple_norm, w_ple_gate):
    w_gates = w_in[:, _C_GATES:_C_MO]
    head_of = jnp.arange(ATT_WIDTH, dtype=jnp.int32) // ATT_HEAD_DIM
    w_qkv = w_in[:, _C_Q:_C_MQ].astype(BF16)
    return {
        "w_qkv": w_qkv,
        "w_qkv_t": w_qkv.T,
        "w_m": w_in[:, _C_MQ:_C_GATES].astype(BF16),
        "g_q_col": g_q_norm.reshape(1, ATT_HEAD_DIM, 1),
        "g_k_col": g_k_norm.reshape(1, ATT_HEAD_DIM, 1),
        "w_g": jnp.pad(w_gates, ((0, 0), (0, LANES - 2 * N_ML_HEADS))).astype(BF16),
        "w_gt": jnp.pad(w_gates.T, ((0, 16 - 2 * N_ML_HEADS), (0, 0))).astype(BF16),
        "w_b": w_in[:, _C_MO:_C_END].astype(BF16),
        "e_head": (head_of[:, None] == head_of[None, :]).astype(BF16),
        "g_mix": g_mix_norm.reshape(1, D_MODEL),
        "g_q": jnp.tile(g_q_norm, N_ATT_HEADS).reshape(1, ATT_WIDTH),
        "g_k": jnp.tile(g_k_norm, N_ATT_HEADS).reshape(1, ATT_WIDTH),
        "g_ml_out": g_ml_out_norm,
        "b_i": b_igate, "b_f": b_fgate,
        "w_att": w_att_branch.astype(BF16), "w_ml": w_ml_branch.astype(BF16), "w_out": w_out.astype(BF16),
        "g_ffn": g_ffn_norm.reshape(1, D_MODEL),
        "w_ffn_gate": w_ffn_gate.astype(BF16), "w_ffn_up": w_ffn_up.astype(BF16),
        "w_ffn_down": w_ffn_down.astype(BF16),
        "w_ple": w_ple.astype(BF16), "g_ple": g_ple_norm.reshape(1, D_MODEL),
        "w_ple_gate": w_ple_gate.astype(BF16),
    }


def _layer(x_p, pe_p, x_s, pe_s, cache_k, cache_v, page_table, state_c, state_n, state_m, pw):
    B, L, _ = x_p.shape
    Bs, Ls, _ = x_s.shape
    assert Ls == 1
    past_len = page_table.shape[1] * PAGE_SIZE
    n_pool = cache_k.shape[0]
    pages_t = lambda c: jnp.transpose(c, (0, 2, 3, 1)).reshape(n_pool, ATT_WIDTH, PAGE_SIZE)
    cache_kt, cache_vt = pages_t(cache_k), pages_t(cache_v)

    xt = x_s.reshape(1, Bs, D_MODEL)
    pos_s = jnp.full((Bs,), past_len, dtype=jnp.int32)
    q, k, v, mq_s, mk_s, mv_s, gcol_s, _, mo_s, ga_s, gm_s = _project(xt, pos_s, pw, tm=Bs, feature_major=False)

    pos_p = jnp.arange(L, dtype=jnp.int32)
    qt, kt, vt, mq, mk, mv, _, grow, mo, ga, gm = _project(x_p, pos_p, pw, tm=256, feature_major=True)
    att_p = _attention_prompt(qt, kt, vt)
    ml_p, c_p, n_p, m_p, sel, scores = _mlstm_prompt(mq, mk, mv, grow, mo, pw, page_table, cache_kt,
                                                     q.reshape(Bs, ATT_WIDTH))
    y_p = _post(x_p, att_p, ml_p, ga, gm, pe_p, pw, tm=256)
    tokens_major = lambda t: jnp.transpose(t.reshape(B, N_ATT_HEADS, ATT_HEAD_DIM, L), (0, 3, 1, 2))

    att_s = _sample_attention(page_table, sel[:, :, :MOBA_TOPK], scores, cache_vt, q, k, v)
    ml_s, c_s, n_s, m_s = _mlstm_sample(mq_s, mk_s, mv_s, gcol_s, mo_s, state_c, state_n, state_m, pw)
    y_s = _post(xt, att_s, ml_s, ga_s, gm_s, pe_s.reshape(1, Bs, pe_s.shape[-1]), pw, tm=Bs)
    prompt = (y_p, tokens_major(kt), tokens_major(vt), c_p, n_p, m_p[:, :, 0])
    sample = (y_s.reshape(Bs, 1, D_MODEL), k, v, c_s, n_s, m_s[:, :, 0])
    return prompt, sample


def kernel(x_prompt, x_sample, cache_k, cache_v, state_mlstm_C, state_mlstm_n, state_mlstm_m, page_table,
           p_prompt, p_sample, w_in, b_igate, b_fgate, g_mix_norm, g_q_norm, g_k_norm, g_ml_out_norm,
           w_att_branch, w_ml_branch, w_out, g_ffn_norm, w_ffn_gate, w_ffn_up, w_ffn_down,
           w_ple, g_ple_norm, w_ple_gate):
    depth = w_in.shape[0]
    assert depth == 1
    Bp, Lp, _ = x_prompt.shape
    Bs, Ls, _ = x_sample.shape
    pw = _prepare_weights(w_in[0], b_igate[0], b_fgate[0], g_mix_norm[0], g_q_norm[0], g_k_norm[0],
                          g_ml_out_norm[0], w_att_branch[0], w_ml_branch[0], w_out[0], g_ffn_norm[0],
                          w_ffn_gate[0], w_ffn_up[0], w_ffn_down[0], w_ple[0], g_ple_norm[0], w_ple_gate[0])
    (yp, kp, vp, cp, n_p, mp), (ys, ks, vs, cs, ns, ms) = _layer(
        x_prompt, p_prompt[0], x_sample, p_sample[0], cache_k[0], cache_v[0], page_table,
        state_mlstm_C, state_mlstm_n, state_mlstm_m, pw)
    heads = lambda t: t.reshape(1, Bs, Ls, N_ATT_HEADS, ATT_HEAD_DIM)
    return (yp, ys, kp[None], vp[None], heads(ks), heads(vs),
            cp[None], n_p[None], mp[None], cs, ns, ms[None])
```

```python
import functools

import jax
import jax.numpy as jnp
from jax import lax
from jax.experimental import pallas as pl
from jax.experimental.pallas import tpu as pltpu

F32 = jnp.float32
BF16 = jnp.bfloat16

D_MODEL = 1024
N_ATT_HEADS = 8
ATT_HEAD_DIM = 64
ATT_WIDTH = N_ATT_HEADS * ATT_HEAD_DIM
ROPE_DIMS = ATT_HEAD_DIM // 4
ROPE_THETA = 500000.0
MOBA_BLOCK = 256
MOBA_TOPK = 3
PAGE_SIZE = 128
N_ML_HEADS = 4
ML_QK_DIM = 128
ML_V_DIM = 256
ML_QK_WIDTH = N_ML_HEADS * ML_QK_DIM
ML_V_WIDTH = N_ML_HEADS * ML_V_DIM
NORM_EPS = 1e-6
NEG_INF = -1e30
LOG2_E = 1.4426950408889634

LANES = 128
HEAD_PAIR = LANES // ATT_HEAD_DIM
ML_CHUNK = 256
VMEM_LIMIT = 56 * 1024 * 1024

_C_Q, _C_K, _C_V = 0, ATT_WIDTH, 2 * ATT_WIDTH
_C_MQ = 3 * ATT_WIDTH
_C_MK = _C_MQ + ML_QK_WIDTH
_C_MV = _C_MK + ML_QK_WIDTH
_C_GATES = _C_MV + ML_V_WIDTH
_C_MO = _C_GATES + 2 * N_ML_HEADS
_C_GA = _C_MO + ML_V_WIDTH
_C_GM = _C_GA + D_MODEL
_C_END = _C_GM + D_MODEL


def _dot(a, b):
    return jnp.dot(a, b, preferred_element_type=F32)


def _dot_nt(a, b, precision=None):
    return lax.dot_general(a, b, (((1,), (1,)), ((), ())), precision=precision,
                           preferred_element_type=F32)


def _rms(x, g):
    return x * lax.rsqrt(jnp.mean(x * x, axis=-1, keepdims=True) + NORM_EPS) * g


def _log_sigmoid(x):
    return jnp.minimum(x, 0.0) - jnp.log1p(jnp.exp(-jnp.abs(x)))


def _proj_shared(xb, wm_ref, wg_ref, wgt_ref, wb_ref,
                 mq_ref, mk_ref, mv_ref, gcol_ref, grow_ref, mo_ref, ga_ref, gm_ref):
    mq_ref[0] = _dot(xb, wm_ref[:, 0:ML_QK_WIDTH]).astype(BF16)
    mk_ref[0] = (_dot(xb, wm_ref[:, ML_QK_WIDTH:2 * ML_QK_WIDTH]) * (ML_QK_DIM ** -0.5)).astype(BF16)
    mv_ref[0] = _dot(xb, wm_ref[:, 2 * ML_QK_WIDTH:2 * ML_QK_WIDTH + ML_V_WIDTH]).astype(BF16)
    gcol_ref[0] = _dot(xb, wg_ref[...])
    grow_ref[0] = _dot_nt(wgt_ref[...], xb)
    mo_ref[0] = _dot(xb, wb_ref[:, 0:ML_V_WIDTH])
    ga_ref[0] = _dot(xb, wb_ref[:, ML_V_WIDTH:ML_V_WIDTH + D_MODEL])
    gm_ref[0] = _dot(xb, wb_ref[:, ML_V_WIDTH + D_MODEL:ML_V_WIDTH + 2 * D_MODEL])


def _proj_kernel_rows(x_ref, cos_ref, sa_ref, sb_ref, gmix_ref, gq_ref, gk_ref, e_ref, wqkv_ref,
                      wm_ref, wg_ref, wgt_ref, wb_ref, q_ref, k_ref, v_ref, *rest_refs):
    xb = _rms(x_ref[0], gmix_ref[...]).astype(BF16)
    cos, sa, sb = cos_ref[...], sa_ref[...], sb_ref[...]
    e = e_ref[...]

    def head_norm_rope(z, g):
        zz = z * z
        hi = zz.astype(BF16)
        lo = (zz - hi.astype(F32)).astype(BF16)
        ss = _dot(hi, e) + _dot(lo, e)
        zn = z * lax.rsqrt(ss * (1.0 / ATT_HEAD_DIM) + NORM_EPS) * g
        outs = []
        for grp in range(ATT_WIDTH // LANES):
            zg = zn[:, grp * LANES:(grp + 1) * LANES]
            outs.append(zg * cos + pltpu.roll(zg, LANES - ROPE_DIMS // 2, 1) * sa
                        + pltpu.roll(zg, ROPE_DIMS // 2, 1) * sb)
        return jnp.concatenate(outs, axis=-1)

    q_ref[0] = head_norm_rope(_dot(xb, wqkv_ref[:, 0:ATT_WIDTH]), gq_ref[...])
    k_ref[0] = head_norm_rope(_dot(xb, wqkv_ref[:, ATT_WIDTH:2 * ATT_WIDTH]), gk_ref[...])
    v_ref[0] = _dot(xb, wqkv_ref[:, 2 * ATT_WIDTH:3 * ATT_WIDTH])
    _proj_shared(xb, wm_ref, wg_ref, wgt_ref, wb_ref, *rest_refs)


def _proj_kernel_cols(x_ref, cos_ref, sin_ref, gmix_ref, gq_ref, gk_ref, wqkvt_ref,
                      wm_ref, wg_ref, wgt_ref, wb_ref, qt_ref, kt_ref, vt_ref, *rest_refs):
    xb = _rms(x_ref[0], gmix_ref[...]).astype(BF16)
    tm = xb.shape[0]
    cos, sin = cos_ref[...][None], sin_ref[...][None]
    half = ROPE_DIMS // 2

    def head_norm_rope(z, g):
        z3 = z.reshape(N_ATT_HEADS, ATT_HEAD_DIM, tm)
        ss = jnp.sum(z3 * z3, axis=1, keepdims=True)
        zn = z3 * lax.rsqrt(ss * (1.0 / ATT_HEAD_DIM) + NORM_EPS) * g
        x1, x2 = zn[:, 0:half, :], zn[:, half:ROPE_DIMS, :]
        out = jnp.concatenate([x1 * cos - x2 * sin, x2 * cos + x1 * sin, zn[:, ROPE_DIMS:, :]], axis=1)
        return out.reshape(ATT_WIDTH, tm)

    zt = _dot_nt(wqkvt_ref[...], xb)
    qt_ref[0] = head_norm_rope(zt[0:ATT_WIDTH], gq_ref[...])
    kt_ref[0] = head_norm_rope(zt[ATT_WIDTH:2 * ATT_WIDTH], gk_ref[...])
    vt_ref[0] = zt[2 * ATT_WIDTH:3 * ATT_WIDTH]
    _proj_shared(xb, wm_ref, wg_ref, wgt_ref, wb_ref, *rest_refs)


def _project(x, pos, pw, tm, feature_major):
    B, L, _ = x.shape
    tok = lambda w: pl.BlockSpec((1, tm, w), lambda b, i: (b, i, 0))
    full = lambda a: pl.BlockSpec(a.shape, lambda b, i: (0,) * a.ndim)
    shared = (pw["w_m"], pw["w_g"], pw["w_gt"], pw["w_b"])
    rest_shape = (
        jax.ShapeDtypeStruct((B, L, ML_QK_WIDTH), BF16),
        jax.ShapeDtypeStruct((B, L, ML_QK_WIDTH), BF16),
        jax.ShapeDtypeStruct((B, L, ML_V_WIDTH), BF16),
        jax.ShapeDtypeStruct((B, L, LANES), F32),
        jax.ShapeDtypeStruct((B, 16, L), F32),
        jax.ShapeDtypeStruct((B, L, ML_V_WIDTH), F32),
        jax.ShapeDtypeStruct((B, L, D_MODEL), F32),
        jax.ShapeDtypeStruct((B, L, D_MODEL), F32),
    )
    rest_specs = (tok(ML_QK_WIDTH), tok(ML_QK_WIDTH), tok(ML_V_WIDTH), tok(LANES),
                  pl.BlockSpec((1, 16, tm), lambda b, i: (b, 0, i)),
                  tok(ML_V_WIDTH), tok(D_MODEL), tok(D_MODEL))
    if feature_major:
        cos, sin = _rope_tables_cols(pos)
        tab = pl.BlockSpec((ROPE_DIMS // 2, tm), lambda b, i: (0, i))
        consts = (pw["g_mix"], pw["g_q_col"], pw["g_k_col"], pw["w_qkv_t"]) + shared
        body, tables, tab_specs = _proj_kernel_cols, (cos, sin), [tab, tab]
        qkv_shape = (jax.ShapeDtypeStruct((B, ATT_WIDTH, L), F32),) * 3
        qkv_specs = (pl.BlockSpec((1, ATT_WIDTH, tm), lambda b, i: (b, 0, i)),) * 3
    else:
        tables = _rope_tables_rows(pos)
        tab = pl.BlockSpec((tm, LANES), lambda b, i: (i, 0))
        consts = (pw["g_mix"], pw["g_q"], pw["g_k"], pw["e_head"], pw["w_qkv"]) + shared
        body, tab_specs = _proj_kernel_rows, [tab, tab, tab]
        qkv_shape = (jax.ShapeDtypeStruct((B, L, ATT_WIDTH), F32),) * 3
        qkv_specs = (tok(ATT_WIDTH),) * 3
    return pl.pallas_call(
        body,
        grid=(B, L // tm),
        in_specs=[tok(D_MODEL)] + tab_specs + [full(a) for a in consts],
        out_specs=qkv_specs + rest_specs,
        out_shape=qkv_shape + rest_shape,
        compiler_params=pltpu.CompilerParams(dimension_semantics=("parallel", "parallel"),
                                             vmem_limit_bytes=VMEM_LIMIT),
        name="in_proj_cols" if feature_major else "in_proj_rows",
    )(x, *tables, *consts)


def _gate_pages(part, page_refs, qbd_ref, score_ref, gate_ref):
    blocks_per_step = len(page_refs) // 2
    lane = lax.broadcasted_iota(jnp.int32, (N_ATT_HEADS, LANES), 1)
    qbd = qbd_ref[0]
    gate = gate_ref[...]
    for i in range(blocks_per_step):
        keys = jnp.concatenate([page_refs[2 * i][0].astype(BF16), page_refs[2 * i + 1][0].astype(BF16)], axis=1)
        s = _dot(qbd, keys)
        s = s[0:N_ATT_HEADS] + s[N_ATT_HEADS:]
        score_ref[0, 2 * i] = s[:, 0:PAGE_SIZE]
        score_ref[0, 2 * i + 1] = s[:, PAGE_SIZE:]
        g = jnp.sum(s, axis=1, keepdims=True) * (1.0 / MOBA_BLOCK)
        gate = jnp.where(lane == part * blocks_per_step + i, g, gate)
    gate_ref[...] = gate


def _gate_select(part, sel_ref, gate_ref, *, steps_per_seq, nblk_seq):
    lane = lax.broadcasted_iota(jnp.int32, (N_ATT_HEADS, LANES), 1)

    @pl.when(part == steps_per_seq - 1)
    def _():
        gate = gate_ref[...]
        rank = jnp.zeros((N_ATT_HEADS, LANES), F32)
        for n2 in range(nblk_seq):
            col = gate[:, n2:n2 + 1]
            beats = (col > gate) | ((col == gate) & (lane > n2))
            rank = rank + jnp.where(beats, 1.0, 0.0)
        out = jnp.zeros((N_ATT_HEADS, LANES), jnp.int32)
        lane_f = lane.astype(F32)
        for r in range(MOBA_TOPK):
            pick = (rank == float(r)) & (lane < nblk_seq)
            idx = jnp.sum(jnp.where(pick, lane_f, 0.0), axis=1, keepdims=True)
            out = jnp.where(lane == r, idx.astype(jnp.int32), out)
        sel_ref[0] = out


def _side_job(page_table, cache_kt, q_s, n_steps, n_grid, pt_pos, linear_step):
    Bs, n_seq_pages = page_table.shape
    assert (Bs * n_seq_pages) % n_steps == 0
    pps = Bs * n_seq_pages // n_steps
    assert MOBA_BLOCK == 2 * PAGE_SIZE and pps % 2 == 0 and n_seq_pages % pps == 0
    steps_per_seq = n_seq_pages // pps
    nblk_seq = n_seq_pages * PAGE_SIZE // MOBA_BLOCK
    assert nblk_seq <= LANES
    head_of = jnp.arange(ATT_WIDTH, dtype=jnp.int32) // ATT_HEAD_DIM
    q_heads = jnp.where(head_of[None, None, :] == jnp.arange(N_ATT_HEADS, dtype=jnp.int32)[None, :, None],
                        q_s[:, None, :], 0.0)
    q_lead = q_heads.astype(BF16)
    q_rows = jnp.concatenate([q_lead, (q_heads - q_lead.astype(F32)).astype(BF16)], axis=1)

    def seq_part(args):
        step = linear_step(*args[:n_grid])
        return step // steps_per_seq, step % steps_per_seq

    def page_spec(p):
        def index(*args):
            seq, part = seq_part(args)
            return (args[n_grid + pt_pos][seq, part * pps + p], 0, 0)
        return pl.BlockSpec((1, ATT_WIDTH, PAGE_SIZE), index)

    return dict(
        in_specs=[pl.BlockSpec((1, 2 * N_ATT_HEADS, ATT_WIDTH), lambda *args: (seq_part(args)[0], 0, 0))]
                 + [page_spec(p) for p in range(pps)],
        operands=[q_rows] + [cache_kt] * pps,
        out_specs=(pl.BlockSpec((1, N_ATT_HEADS, LANES), lambda *args: (seq_part(args)[0], 0, 0)),
                   pl.BlockSpec((1, pps, N_ATT_HEADS, PAGE_SIZE), lambda *args: seq_part(args) + (0, 0))),
        out_shape=(jax.ShapeDtypeStruct((Bs, N_ATT_HEADS, LANES), jnp.int32),
                   jax.ShapeDtypeStruct((Bs, n_seq_pages, N_ATT_HEADS, PAGE_SIZE), F32)),
        scratch=[pltpu.VMEM((N_ATT_HEADS, LANES), F32)],
        static=dict(n_side_pages=pps, steps_per_seq=steps_per_seq, nblk_seq=nblk_seq),
    )


def _attn_kernel(q_ref, k_ref, v_ref, o_ref, kb_ref, vt_ref, kmean_ref, *tile_scratch, nblk):
    blk = MOBA_BLOCK
    tq = 2 * blk
    ones_rows = vt_ref.shape[2] - ATT_HEAD_DIM
    step = pl.program_id(2)
    nb = kmean_ref.shape[0]

    @pl.when(step == 0)
    def _():
        lane = lax.broadcasted_iota(jnp.int32, (blk, LANES), 1)
        kmean_ref[...] = jnp.zeros_like(kmean_ref)
        for n in range(nblk):
            rows = slice(n * blk, (n + 1) * blk)
            kf = k_ref[0, :, rows].T
            kb_ref[rows, 0:LANES] = kf.astype(BF16)
            kb_ref[rows, LANES:2 * LANES] = jnp.where(lane == n, 1.0, 0.0).astype(BF16)
            kmean_ref[n:n + 1, :] = jnp.sum(kf, axis=0, keepdims=True) * (1.0 / blk)
            for h in range(HEAD_PAIR):
                vh = v_ref[0, h * ATT_HEAD_DIM:(h + 1) * ATT_HEAD_DIM, rows]
                vt_ref[n, h] = jnp.concatenate([vh, jnp.ones((ones_rows, blk), F32)], axis=0).astype(BF16)

    for idx in range(2):
        _attn_tile(idx, 2 * step + idx, q_ref, o_ref, kb_ref, vt_ref, kmean_ref,
                   *tile_scratch[3 * idx:3 * idx + 3], nblk=nblk)


def _attn_tile(idx, t, q_ref, o_ref, kb_ref, vt_ref, kmean_ref, acc_ref, s0_ref, s1_ref, *, nblk):
    blk = MOBA_BLOCK
    tq = 2 * blk
    nb = kmean_ref.shape[0]
    c0 = 2 * t
    q2 = q_ref[0, :, idx * tq:(idx + 1) * tq]
    feat = lax.broadcasted_iota(jnp.int32, (LANES, tq), 0)
    blk_id = lax.broadcasted_iota(jnp.int32, (nb, tq), 0)
    col = lax.broadcasted_iota(jnp.int32, (nb, tq), 1)
    cur = c0 + (col >= blk).astype(jnp.int32)
    kmean = kmean_ref[...]
    bias_pad = jnp.zeros((LANES - nb, tq), F32)

    qas = []
    for h in range(HEAD_PAIR):
        qh = jnp.where((feat >= h * ATT_HEAD_DIM) & (feat < (h + 1) * ATT_HEAD_DIM), q2, 0.0)
        gate = jnp.dot(kmean, qh, precision=lax.Precision.HIGHEST,
                       preferred_element_type=F32)
        gate = jnp.where(blk_id < cur, gate, -jnp.inf)
        rank = jnp.zeros((nb, tq), F32)
        for n2 in range(nblk):
            row = gate[n2:n2 + 1, :]
            beats = (row > gate) | ((row == gate) & (blk_id > n2))
            rank = rank + jnp.where(beats, 1.0, 0.0)
        open_blk = ((blk_id < cur) & (rank < MOBA_TOPK)) | (blk_id == cur)
        bias = jnp.where(open_blk, 0.0, NEG_INF)
        qas.append(jnp.concatenate([qh * (ATT_HEAD_DIM ** -0.5 * LOG2_E), bias, bias_pad],
                                   axis=0).astype(BF16))

    def block_rows(n):
        return kb_ref[pl.ds(pl.multiple_of(n * blk, blk), blk), :]

    key_i = lax.broadcasted_iota(jnp.int32, (blk, tq), 0)
    qry_j = lax.broadcasted_iota(jnp.int32, (blk, tq), 1)
    s_refs = (s0_ref, s1_ref)

    def stage_a(first, slot, own, heads=range(HEAD_PAIR)):
        ks = [block_rows(first), block_rows(first + 1)]
        maxima = []
        for h in heads:
            tops = []
            for i in range(2):
                st = _dot(ks[i], qas[h])
                if own:
                    st = jnp.where(key_i <= qry_j - i * blk, st, NEG_INF)
                s_refs[slot][h, i] = st
                tops.append(jnp.max(st, axis=0, keepdims=True))
            maxima.append(jnp.maximum(tops[0], tops[1]))
        return maxima

    def stage_b(first, slot, m_run, maxima, heads=range(HEAD_PAIR)):
        m_out = []
        for h in heads:
            m_new = jnp.maximum(m_run[h], maxima[h])
            pv = (_dot(vt_ref[first, h], jnp.exp2(s_refs[slot][h, 0] - m_new).astype(BF16))
                  + _dot(vt_ref[first + 1, h], jnp.exp2(s_refs[slot][h, 1] - m_new).astype(BF16)))
            acc_ref[h] = jnp.exp2(m_run[h] - m_new) * acc_ref[h] + pv
            m_out.append(m_new)
        return m_out

    def pending_first(j):
        return jnp.where(j == 0, c0, 2 * j - 2)

    acc_ref[...] = jnp.zeros_like(acc_ref)
    m_start = [jnp.full((1, tq), NEG_INF, F32)] * HEAD_PAIR
    carry = tuple(m_start) + tuple(stage_a(c0, 0, own=True))

    def trip(j, slot, carry):
        m_run, maxima = carry[:HEAD_PAIR], carry[HEAD_PAIR:]
        new_maxima, m_new = [], []
        for h in range(HEAD_PAIR):
            new_maxima += stage_a(2 * j, 1 - slot, own=False, heads=(h,))
            m_new += stage_b(pending_first(j), slot, m_run, maxima, heads=(h,))
        return tuple(m_new) + tuple(new_maxima)

    carry = lax.fori_loop(0, lax.shift_right_logical(t, 1),
                          lambda i, cr: trip(2 * i + 1, 1, trip(2 * i, 0, cr)), carry)

    slot = 0
    if idx == 1:
        carry = trip(t - 1, 0, carry)
        slot = 1
    stage_b(pending_first(t), slot, carry[:HEAD_PAIR], carry[HEAD_PAIR:])
    outs = [acc_ref[h, 0:ATT_HEAD_DIM, :] / acc_ref[h, ATT_HEAD_DIM:ATT_HEAD_DIM + 1, :]
            for h in range(HEAD_PAIR)]
    o_ref[0, idx * tq:(idx + 1) * tq, :] = jnp.concatenate(outs, axis=0).T.astype(o_ref.dtype)


def _attention_prompt(qt, kt, vt):
    B, _, L = qt.shape
    blk = MOBA_BLOCK
    tq = 2 * blk
    span = 2 * tq
    assert L % span == 0
    nblk = L // blk
    assert nblk <= LANES
    nb = -(-nblk // 8) * 8
    ones_rows = 16
    groups = ATT_WIDTH // LANES
    per_tile = [pltpu.VMEM((HEAD_PAIR, ATT_HEAD_DIM + ones_rows, tq), F32),
                pltpu.VMEM((HEAD_PAIR, 2, blk, tq), F32),
                pltpu.VMEM((HEAD_PAIR, 2, blk, tq), F32)]
    return pl.pallas_call(
        functools.partial(_attn_kernel, nblk=nblk),
        grid=(B, groups, L // span),
        in_specs=[pl.BlockSpec((1, LANES, span), lambda b, g, i: (b, g, i)),
                  pl.BlockSpec((1, LANES, L), lambda b, g, i: (b, g, 0)),
                  pl.BlockSpec((1, LANES, L), lambda b, g, i: (b, g, 0))],
        out_specs=pl.BlockSpec((1, span, LANES), lambda b, g, i: (b, i, g)),
        out_shape=jax.ShapeDtypeStruct((B, L, ATT_WIDTH), BF16),
        scratch_shapes=[pltpu.VMEM((L, 2 * LANES), BF16),
                        pltpu.VMEM((nblk, HEAD_PAIR, ATT_HEAD_DIM + ones_rows, blk), BF16),
                        pltpu.VMEM((nb, LANES), F32)] + per_tile + per_tile,
        compiler_params=pltpu.CompilerParams(dimension_semantics=("parallel", "parallel", "arbitrary"),
                                             vmem_limit_bytes=VMEM_LIMIT),
        name="moba_prompt",
    )(qt, kt, vt)


def _mlstm_kernel(bi_ref, bf_ref, pt_ref, mq_ref, mk_ref, mv_ref, grow_ref, mo_ref, gout_ref, qbd_ref, *refs,
                  chunk, n_side_pages, steps_per_seq, nblk_seq):
    page_refs = refs[:n_side_pages]
    (ml_ref, c_out_ref, n_out_ref, m_out_ref, sel_ref, score_ref, cn_ref, m_ref,
     gate_ref) = refs[n_side_pages:]
    j = pl.program_id(1)
    dk, dv = ML_QK_DIM, ML_V_DIM
    part = lax.rem(pl.program_id(0) * pl.num_programs(1) + j, steps_per_seq)

    @pl.when(j == 0)
    def _():
        cn_ref[...] = jnp.zeros_like(cn_ref)
        m_ref[...] = jnp.zeros_like(m_ref)

    @pl.when(part == 0)
    def _():
        gate_ref[...] = jnp.zeros_like(gate_ref)

    t_i = lax.broadcasted_iota(jnp.int32, (chunk, chunk), 0)
    s_i = lax.broadcasted_iota(jnp.int32, (chunk, chunk), 1)
    tril = s_i <= t_i
    tril_b = jnp.where(tril, 1.0, 0.0).astype(BF16)
    triu_b = jnp.where(t_i <= s_i, 1.0, 0.0).astype(BF16)
    ones_cols = jnp.ones((chunk, LANES), BF16)
    grow = grow_ref[0]
    f_rows = _log_sigmoid(grow[N_ML_HEADS:2 * N_ML_HEADS, :]
                          + jnp.concatenate([jnp.full((1, 1), bf_ref[h], F32) for h in range(N_ML_HEADS)], axis=0))
    hi = f_rows.astype(BF16).astype(F32)
    mid = (f_rows - hi).astype(BF16).astype(F32)
    pieces = jnp.concatenate([hi, mid, f_rows - hi - mid], axis=1)
    pieces = jnp.concatenate([pieces, jnp.zeros((16 - N_ML_HEADS, 3 * chunk), F32)], axis=0).astype(BF16)
    rows = _dot(pieces, jnp.concatenate([triu_b] * 3, axis=0))
    cols = _dot_nt(jnp.concatenate([tril_b] * 3, axis=1), pieces)
    gates = []
    for h in range(N_ML_HEADS):
        i_row = grow[h:h + 1, :] + bi_ref[h]
        b_row = rows[h:h + 1]
        b_col = cols[:, h:h + 1]
        a_row = (i_row - b_row) * LOG2_E
        m_old = m_ref[h][:, 0:1]
        a_low = jnp.where(tril, a_row, NEG_INF)
        mm_col = jnp.maximum(jnp.max(a_low, axis=1, keepdims=True), m_old * LOG2_E)
        w_inter = jnp.exp2(m_old * LOG2_E - mm_col)
        floor_col = jnp.exp2(-(b_col * LOG2_E + mm_col))
        decay = jnp.exp2(a_low - mm_col)
        gates.append((b_row, a_row, m_old, mm_col, w_inter, floor_col, decay))

    _gate_pages(part, page_refs, qbd_ref, score_ref, gate_ref)

    hids = []
    for h in range(N_ML_HEADS):
        b_row, a_row, m_old, mm_col, w_inter, floor_col, decay = gates[h]
        q = mq_ref[0, :, h * dk:(h + 1) * dk]
        k = mk_ref[0, :, h * dk:(h + 1) * dk]
        v_ones = jnp.concatenate([mv_ref[0, :, h * dv:(h + 1) * dv], ones_cols], axis=1)
        cn = cn_ref[h]
        s = _dot_nt(q, k) * decay
        inter = _dot(q, cn.astype(BF16))
        intra = _dot(s.astype(BF16), v_ones)
        num = w_inter * inter[:, 0:dv] + intra[:, 0:dv]
        den = w_inter * inter[:, dv:dv + 1] + intra[:, dv:dv + 1]
        hids.append(num / jnp.maximum(jnp.abs(den), floor_col))

        mm_end = mm_col[chunk - 1:chunk, :]
        w_row = jnp.exp2(a_row - mm_end)
        kw = (k.astype(F32).T * w_row).astype(BF16)
        cn_ref[h] = jnp.exp2(m_old * LOG2_E - mm_end) * cn + _dot(kw, v_ones)
        m_ref[h] = jnp.broadcast_to(b_row[:, chunk - 1:chunk] + mm_end * (1.0 / LOG2_E), (1, LANES))

    for h in range(N_ML_HEADS):
        hn = _rms(hids[h], gout_ref[h:h + 1, :])
        mo = mo_ref[0, :, h * dv:(h + 1) * dv]
        ml_ref[0, :, h * dv:(h + 1) * dv] = (hn * jax.nn.sigmoid(mo)).astype(BF16)

    @pl.when(j == pl.num_programs(1) - 1)
    def _():
        for h in range(N_ML_HEADS):
            c_out_ref[0, h] = cn_ref[h, :, 0:dv].T
            n_out_ref[0, h:h + 1, :] = cn_ref[h, :, dv:dv + LANES].T[0:1, :]
            m_out_ref[0, h:h + 1, :] = m_ref[h]

    _gate_select(part, sel_ref, gate_ref, steps_per_seq=steps_per_seq, nblk_seq=nblk_seq)


def _mlstm_prompt(mq, mk, mv, grow, mo, pw, page_table, cache_kt, q_s):
    B, L, _ = mq.shape
    chunk = ML_CHUNK
    n_chunks = L // chunk
    side = _side_job(page_table, cache_kt, q_s, n_steps=B * n_chunks, n_grid=2, pt_pos=2,
                     linear_step=lambda b, j: b * n_chunks + j)
    tok = lambda w: pl.BlockSpec((1, chunk, w), lambda b, j, *_: (b, j, 0))
    per_b = lambda shp: pl.BlockSpec((1,) + shp, lambda b, j, *_: (b,) + (0,) * len(shp))
    grid_spec = pltpu.PrefetchScalarGridSpec(
        num_scalar_prefetch=3,
        grid=(B, n_chunks),
        in_specs=[tok(ML_QK_WIDTH), tok(ML_QK_WIDTH), tok(ML_V_WIDTH),
                  pl.BlockSpec((1, 16, chunk), lambda b, j, *_: (b, 0, j)), tok(ML_V_WIDTH),
                  pl.BlockSpec((N_ML_HEADS, ML_V_DIM), lambda b, j, *_: (0, 0))] + side["in_specs"],
        out_specs=(tok(ML_V_WIDTH), per_b((N_ML_HEADS, ML_V_DIM, ML_QK_DIM)),
                   per_b((N_ML_HEADS, ML_QK_DIM)), per_b((N_ML_HEADS, LANES))) + side["out_specs"],
        scratch_shapes=[pltpu.VMEM((N_ML_HEADS, ML_QK_DIM, ML_V_DIM + LANES), F32),
                        pltpu.VMEM((N_ML_HEADS, 1, LANES), F32)] + side["scratch"],
    )
    return pl.pallas_call(
        functools.partial(_mlstm_kernel, chunk=chunk, **side["static"]),
        grid_spec=grid_spec,
        out_shape=(jax.ShapeDtypeStruct((B, L, ML_V_WIDTH), BF16),
                   jax.ShapeDtypeStruct((B, N_ML_HEADS, ML_V_DIM, ML_QK_DIM), F32),
                   jax.ShapeDtypeStruct((B, N_ML_HEADS, ML_QK_DIM), F32),
                   jax.ShapeDtypeStruct((B, N_ML_HEADS, LANES), F32)) + side["out_shape"],
        compiler_params=pltpu.CompilerParams(dimension_semantics=("arbitrary", "arbitrary"),
                                             vmem_limit_bytes=VMEM_LIMIT),
        name="mlstm_prompt",
    )(pw["b_i"], pw["b_f"], page_table, mq, mk, mv, grow, mo, pw["g_ml_out"], *side["operands"])


def _post_kernel(x_ref, att_ref, ml_ref, ga_ref, gm_ref, pe_ref,
                 watt_ref, wml_ref, wout_ref, gffn_ref, wgate_ref, wup_ref, wdown_ref,
                 wple_ref, gple_ref, wpg_ref, y_ref):
    mix = (jax.nn.sigmoid(ga_ref[0]) * _dot(att_ref[0].astype(BF16), watt_ref[...])
           + jax.nn.sigmoid(gm_ref[0]) * _dot(ml_ref[0].astype(BF16), wml_ref[...]))
    h = x_ref[0] + _dot(mix.astype(BF16), wout_ref[...])
    hb = _rms(h, gffn_ref[...]).astype(BF16)
    act = jax.nn.silu(_dot(hb, wgate_ref[...])) * _dot(hb, wup_ref[...])
    h = h + _dot(act.astype(BF16), wdown_ref[...])
    gate = jax.nn.sigmoid(_dot(_rms(h, gple_ref[...]).astype(BF16), wpg_ref[...]))
    y_ref[0] = h + gate * _dot(pe_ref[0].astype(BF16), wple_ref[...])


def _post(x, att, ml, ga, gm, pe, pw, tm):
    B, L, _ = x.shape
    tok = lambda w: pl.BlockSpec((1, tm, w), lambda b, i: (b, i, 0))
    full = lambda a: pl.BlockSpec(a.shape, lambda b, i: (0,) * a.ndim, pipeline_mode=pl.Buffered(1))
    consts = (pw["w_att"], pw["w_ml"], pw["w_out"], pw["g_ffn"], pw["w_ffn_gate"], pw["w_ffn_up"],
              pw["w_ffn_down"], pw["w_ple"], pw["g_ple"], pw["w_ple_gate"])
    return pl.pallas_call(
        _post_kernel,
        grid=(B, L // tm),
        in_specs=[tok(D_MODEL), tok(ATT_WIDTH), tok(ML_V_WIDTH), tok(D_MODEL), tok(D_MODEL),
                  tok(pe.shape[-1])] + [full(a) for a in consts],
        out_specs=tok(D_MODEL),
        out_shape=jax.ShapeDtypeStruct((B, L, D_MODEL), F32),
        compiler_params=pltpu.CompilerParams(dimension_semantics=("parallel", "parallel"),
                                             vmem_limit_bytes=VMEM_LIMIT),
        name="merge_ffn",
    )(x, att, ml, ga, gm, pe, *consts)


def _row(ref, b):
    if ref.dtype == F32:
        return ref[0, pl.ds(b, 1), :]
    group = ref[0, pl.ds(pl.multiple_of(lax.shift_right_logical(b, 4) * 16, 16), 16), :].astype(F32)
    pick = lax.broadcasted_iota(jnp.int32, group.shape, 0) == (b & 15)
    return jnp.sum(jnp.where(pick, group, 0.0), axis=0, keepdims=True)


def _sample_attn_kernel(pool_ref, local_ref, *refs, n_pages):
    v_refs = refs[:N_ATT_HEADS * n_pages]
    score_ref, q_ref, kn_ref, vn_ref, o_ref = refs[N_ATT_HEADS * n_pages:]
    b = pl.program_id(0)
    scale = ATT_HEAD_DIM ** -0.5
    rows = 16
    q_row, kn_row, vn_row = _row(q_ref, b), _row(kn_ref, b), _row(vn_ref, b)
    outs = []
    for h in range(N_ATT_HEADS):
        lanes = slice(h * ATT_HEAD_DIM, (h + 1) * ATT_HEAD_DIM)
        s_own = jnp.sum(q_row[:, lanes] * kn_row[:, lanes], axis=1, keepdims=True) * scale
        pages = [local_ref[b, h * n_pages + j] for j in range(n_pages)]
        s = jnp.concatenate([score_ref[0, pg, h:h + 1, :] for pg in pages], axis=1) * scale
        v_all = jnp.concatenate([v_refs[h * n_pages + j][0].astype(BF16) for j in range(n_pages)], axis=1)
        m = jnp.maximum(s_own, jnp.max(s, axis=1, keepdims=True))
        p_own = jnp.exp(s_own - m)
        p = jnp.exp(s - m)
        l = p_own + jnp.sum(p, axis=1, keepdims=True)
        pv = _dot_nt(jnp.broadcast_to(p, (rows, n_pages * PAGE_SIZE)).astype(BF16), v_all)[0:1]
        outs.append((p_own * vn_row[:, lanes] + pv) / l)
    o_ref[0, pl.ds(b, 1), :] = jnp.concatenate(outs, axis=1)


def _sample_attention(page_table, sel, scores, cache_vt, q_s, k_s, v_s):
    Bs, n_seq_pages = page_table.shape
    ppb = MOBA_BLOCK // PAGE_SIZE
    n_pages = MOBA_TOPK * ppb
    local = (ppb * sel[..., None] + jnp.arange(ppb, dtype=jnp.int32)).reshape(Bs, N_ATT_HEADS * n_pages)
    pool = jnp.take_along_axis(page_table, local, axis=1)

    def tile_spec(h, j):
        return pl.BlockSpec((1, ATT_HEAD_DIM, PAGE_SIZE),
                            lambda b, pool_ref, local_ref: (pool_ref[b, h * n_pages + j], h, 0))

    tiles = [tile_spec(h, j) for h in range(N_ATT_HEADS) for j in range(n_pages)]
    rows = pl.BlockSpec((1, Bs, ATT_WIDTH), lambda b, *_: (0, 0, 0))
    grid_spec = pltpu.PrefetchScalarGridSpec(
        num_scalar_prefetch=2,
        grid=(Bs,),
        in_specs=tiles + [pl.BlockSpec((1, n_seq_pages, N_ATT_HEADS, PAGE_SIZE), lambda b, *_: (b, 0, 0, 0)),
                          rows, rows, rows],
        out_specs=rows,
    )
    return pl.pallas_call(
        functools.partial(_sample_attn_kernel, n_pages=n_pages),
        grid_spec=grid_spec,
        out_shape=jax.ShapeDtypeStruct((1, Bs, ATT_WIDTH), F32),
        compiler_params=pltpu.CompilerParams(dimension_semantics=("arbitrary",),
                                             vmem_limit_bytes=VMEM_LIMIT),
        name="moba_sample_attn",
    )(pool, local, *([cache_vt] * len(tiles)), scores, q_s, k_s, v_s)


def _mlstm_step_kernel(bi_ref, bf_ref, m0_ref, mq_ref, mk_ref, mv_ref, g_ref, mo_ref, gout_ref,
                       c0_ref, n0_ref, ml_ref, c_out_ref, n_out_ref, m_out_ref):
    b = pl.program_id(0)
    dv = ML_V_DIM
    r_i = lax.broadcasted_iota(jnp.int32, (dv, dv), 0)
    c_i = lax.broadcasted_iota(jnp.int32, (dv, dv), 1)
    eye = r_i == c_i
    g = _row(g_ref, b)
    q_all, k_all, v_all, mo_all = _row(mq_ref, b), _row(mk_ref, b), _row(mv_ref, b), _row(mo_ref, b)
    outs = []
    for h in range(N_ML_HEADS):
        i_g = g[:, h:h + 1] + bi_ref[h]
        logf = _log_sigmoid(g[:, N_ML_HEADS + h:N_ML_HEADS + h + 1] + bf_ref[h])
        m_old = m0_ref[b, h]
        inter = logf + m_old
        m_t = jnp.maximum(inter, i_g)
        w_inter = jnp.exp(inter - m_t)
        w_in = jnp.exp(i_g - m_t)
        q = q_all[:, h * ML_QK_DIM:(h + 1) * ML_QK_DIM]
        k = k_all[:, h * ML_QK_DIM:(h + 1) * ML_QK_DIM]
        v = v_all[:, h * dv:(h + 1) * dv]
        c_old = c0_ref[0, 0, h]
        n_old = n0_ref[0, 0, h:h + 1, :]
        s = jnp.sum(q * k, axis=1, keepdims=True) * w_in
        v_col = jnp.sum(jnp.where(eye, v, 0.0), axis=1, keepdims=True)
        q_rows = jnp.broadcast_to(q, (16, ML_QK_DIM)).astype(BF16)
        cq = _dot_nt(q_rows, c_old.astype(BF16))[0:1]
        num = w_inter * cq + s * v
        den = w_inter * jnp.sum(n_old * q, axis=1, keepdims=True) + s
        hid = num / jnp.maximum(jnp.abs(den), jnp.exp(-m_t))
        c_out_ref[0, 0, h] = w_inter * c_old + (w_in * v_col) * k
        n_out_ref[0, 0, h:h + 1, :] = w_inter * n_old + w_in * k
        m_out_ref[0, h:h + 1, :] = jnp.broadcast_to(m_t, (1, LANES))
        hn = _rms(hid, gout_ref[h:h + 1, :])
        outs.append(hn * jax.nn.sigmoid(mo_all[:, h * dv:(h + 1) * dv]))
    ml_ref[0, pl.ds(b, 1), :] = jnp.concatenate(outs, axis=1)


def _mlstm_sample(mq, mk, mv, gcol, mo, state_c, state_n, state_m, pw):
    Bs = mq.shape[1]
    assert Bs % 16 == 0
    rows = lambda w: pl.BlockSpec((1, Bs, w), lambda b, *_: (0, 0, 0))
    grid_spec = pltpu.PrefetchScalarGridSpec(
        num_scalar_prefetch=3,
        grid=(Bs,),
        in_specs=[rows(ML_QK_WIDTH), rows(ML_QK_WIDTH), rows(ML_V_WIDTH), rows(LANES), rows(ML_V_WIDTH),
                  pl.BlockSpec((N_ML_HEADS, ML_V_DIM), lambda b, *_: (0, 0)),
                  pl.BlockSpec((1, 1, N_ML_HEADS, ML_V_DIM, ML_QK_DIM), lambda b, *_: (0, b, 0, 0, 0)),
                  pl.BlockSpec((1, 1, N_ML_HEADS, ML_QK_DIM), lambda b, *_: (0, b, 0, 0))],
        out_specs=(rows(ML_V_WIDTH),
                   pl.BlockSpec((1, 1, N_ML_HEADS, ML_V_DIM, ML_QK_DIM), lambda b, *_: (0, b, 0, 0, 0)),
                   pl.BlockSpec((1, 1, N_ML_HEADS, ML_QK_DIM), lambda b, *_: (0, b, 0, 0)),
                   pl.BlockSpec((1, N_ML_HEADS, LANES), lambda b, *_: (b, 0, 0))),
    )
    return pl.pallas_call(
        _mlstm_step_kernel,
        grid_spec=grid_spec,
        out_shape=(jax.ShapeDtypeStruct((1, Bs, ML_V_WIDTH), F32),
                   jax.ShapeDtypeStruct(state_c.shape, F32),
                   jax.ShapeDtypeStruct(state_n.shape, F32),
                   jax.ShapeDtypeStruct((Bs, N_ML_HEADS, LANES), F32)),
        compiler_params=pltpu.CompilerParams(dimension_semantics=("arbitrary",),
                                             vmem_limit_bytes=VMEM_LIMIT),
        name="mlstm_sample",
    )(pw["b_i"], pw["b_f"], state_m[0], mq, mk, mv, gcol, mo, pw["g_ml_out"], state_c, state_n)


def _rope_angles(pos):
    freqs = ROPE_THETA ** (-jnp.arange(0, ROPE_DIMS, 2, dtype=F32) / ROPE_DIMS)
    return pos.astype(F32)[:, None] * freqs[None, :]


def _rope_tables_cols(pos):
    ang = _rope_angles(pos).T
    return jnp.cos(ang), jnp.sin(ang)


def _rope_tables_rows(pos):
    half = ROPE_DIMS // 2
    n = pos.shape[0]
    ang = _rope_angles(pos)
    cos, sin = jnp.cos(ang), jnp.sin(ang)
    rest = ATT_HEAD_DIM - ROPE_DIMS
    zh = jnp.zeros((n, half), F32)
    cos_h = jnp.concatenate([cos, cos, jnp.ones((n, rest), F32)], axis=-1)
    sa_h = jnp.concatenate([-sin, zh, jnp.zeros((n, rest), F32)], axis=-1)
    sb_h = jnp.concatenate([zh, sin, jnp.zeros((n, rest), F32)], axis=-1)
    tile = lambda t: jnp.tile(t, (1, HEAD_PAIR))
    return tile(cos_h), tile(sa_h), tile(sb_h)


def _prepare_weights(w_in, b_igate, b_fgate, g_mix_norm, g_q_norm, g_k_norm, g_ml_out_norm,
                     w_att_branch, w_ml_branch, w_out, g_ffn_norm, w_ffn_gate, w_ffn_up, w_ffn_down,
                     w_ple, g_ple_norm, w_ple_gate):
    w_gates = w_in[:, _C_GATES:_C_MO]
    head_of = jnp.arange(ATT_WIDTH, dtype=jnp.int32) // ATT_HEAD_DIM
    w_qkv = w_in[:, _C_Q:_C_MQ].astype(BF16)
    return {
        "w_qkv": w_qkv,
        "w_qkv_t": w_qkv.T,
        "w_m": w_in[:, _C_MQ:_C_GATES].astype(BF16),
        "g_q_col": g_q_norm.reshape(1, ATT_HEAD_DIM, 1),
        "g_k_col": g_k_norm.reshape(1, ATT_HEAD_DIM, 1),
        "w_g": jnp.pad(w_gates, ((0, 0), (0, LANES - 2 * N_ML_HEADS))).astype(BF16),
        "w_gt": jnp.pad(w_gates.T, ((0, 16 - 2 * N_ML_HEADS), (0, 0))).astype(BF16),
        "w_b": w_in[:, _C_MO:_C_END].astype(BF16),
        "e_head": (head_of[:, None] == head_of[None, :]).astype(BF16),
        "g_mix": g_mix_norm.reshape(1, D_MODEL),
        "g_q": jnp.tile(g_q_norm, N_ATT_HEADS).reshape(1, ATT_WIDTH),
        "g_k": jnp.tile(g_k_norm, N_ATT_HEADS).reshape(1, ATT_WIDTH),
        "g_ml_out": g_ml_out_norm,
        "b_i": b_igate, "b_f": b_fgate,
        "w_att": w_att_branch.astype(BF16), "w_ml": w_ml_branch.astype(BF16), "w_out": w_out.astype(BF16),
        "g_ffn": g_ffn_norm.reshape(1, D_MODEL),
        "w_ffn_gate": w_ffn_gate.astype(BF16), "w_ffn_up": w_ffn_up.astype(BF16),
        "w_ffn_down": w_ffn_down.astype(BF16),
        "w_ple": w_ple.astype(BF16), "g_ple": g_ple_norm.reshape(1, D_MODEL),
        "w_ple_gate": w_ple_gate.astype(BF16),
    }


def _layer(x_p, pe_p, x_s, pe_s, cache_k, cache_v, page_table, state_c, state_n, state_m, pw):
    B, L, _ = x_p.shape
    Bs, Ls, _ = x_s.shape
    assert Ls == 1
    past_len = page_table.shape[1] * PAGE_SIZE
    n_pool = cache_k.shape[0]
    pages_t = lambda c: jnp.transpose(c, (0, 2, 3, 1)).reshape(n_pool, ATT_WIDTH, PAGE_SIZE)
    cache_kt, cache_vt = pages_t(cache_k), pages_t(cache_v)

    xt = x_s.reshape(1, Bs, D_MODEL)
    pos_s = jnp.full((Bs,), past_len, dtype=jnp.int32)
    q, k, v, mq_s, mk_s, mv_s, gcol_s, _, mo_s, ga_s, gm_s = _project(xt, pos_s, pw, tm=Bs, feature_major=False)

    pos_p = jnp.arange(L, dtype=jnp.int32)
    qt, kt, vt, mq, mk, mv, _, grow, mo, ga, gm = _project(x_p, pos_p, pw, tm=256, feature_major=True)
    att_p = _attention_prompt(qt, kt, vt)
    ml_p, c_p, n_p, m_p, sel, scores = _mlstm_prompt(mq, mk, mv, grow, mo, pw, page_table, cache_kt,
                                                     q.reshape(Bs, ATT_WIDTH))
    y_p = _post(x_p, att_p, ml_p, ga, gm, pe_p, pw, tm=256)
    tokens_major = lambda t: jnp.transpose(t.reshape(B, N_ATT_HEADS, ATT_HEAD_DIM, L), (0, 3, 1, 2))

    att_s = _sample_attention(page_table, sel[:, :, :MOBA_TOPK], scores, cache_vt, q, k, v)
    ml_s, c_s, n_s, m_s = _mlstm_sample(mq_s, mk_s, mv_s, gcol_s, mo_s, state_c, state_n, state_m, pw)
    y_s = _post(xt, att_s, ml_s, ga_s, gm_s, pe_s.reshape(1, Bs, pe_s.shape[-1]), pw, tm=Bs)
    prompt = (y_p, tokens_major(kt), tokens_major(vt), c_p, n_p, m_p[:, :, 0])
    sample = (y_s.reshape(Bs, 1, D_MODEL), k, v, c_s, n_s, m_s[:, :, 0])
    return prompt, sample


def kernel(x_prompt, x_sample, cache_k, cache_v, state_mlstm_C, state_mlstm_n, state_mlstm_m, page_table,
           p_prompt, p_sample, w_in, b_igate, b_fgate, g_mix_norm, g_q_norm, g_k_norm, g_ml_out_norm,
           w_att_branch, w_ml_branch, w_out, g_ffn_norm, w_ffn_gate, w_ffn_up, w_ffn_down,
           w_ple, g_ple_norm, w_ple_gate):
    depth = w_in.shape[0]
    assert depth == 1
    Bp, Lp, _ = x_prompt.shape
    Bs, Ls, _ = x_sample.shape
    pw = _prepare_weights(w_in[0], b_igate[0], b_fgate[0], g_mix_norm[0], g_q_norm[0], g_k_norm[0],
                          g_ml_out_norm[0], w_att_branch[0], w_ml_branch[0], w_out[0], g_ffn_norm[0],
                          w_ffn_gate[0], w_ffn_up[0], w_ffn_down[0], w_ple[0], g_ple_norm[0], w_ple_gate[0])
    (yp, kp, vp, cp, n_p, mp), (ys, ks, vs, cs, ns, ms) = _layer(
        x_prompt, p_prompt[0], x_sample, p_sample[0], cache_k[0], cache_v[0], page_table,
        state_mlstm_C, state_mlstm_n, state_mlstm_m, pw)
    heads = lambda t: t.reshape(1, Bs, Ls, N_ATT_HEADS, ATT_HEAD_DIM)
    return (yp, ys, kp[None], vp[None], heads(ks), heads(vs),
            cp[None], n_p[None], mp[None], cs, ns, ms[None])
```

```python
import functools

import jax
import jax.numpy as jnp
from jax import lax
from jax.experimental import pallas as pl
from jax.experimental.pallas import tpu as pltpu

F32 = jnp.float32
BF16 = jnp.bfloat16

D_MODEL = 1024
N_ATT_HEADS = 8
ATT_HEAD_DIM = 64
ATT_WIDTH = N_ATT_HEADS * ATT_HEAD_DIM
ROPE_DIMS = ATT_HEAD_DIM // 4
ROPE_THETA = 500000.0
MOBA_BLOCK = 256
MOBA_TOPK = 3
PAGE_SIZE = 128
N_ML_HEADS = 4
ML_QK_DIM = 128
ML_V_DIM = 256
ML_QK_WIDTH = N_ML_HEADS * ML_QK_DIM
ML_V_WIDTH = N_ML_HEADS * ML_V_DIM
NORM_EPS = 1e-6
NEG_INF = -1e30
LOG2_E = 1.4426950408889634

LANES = 128
HEAD_PAIR = LANES // ATT_HEAD_DIM
ML_CHUNK = 256
VMEM_LIMIT = 56 * 1024 * 1024

_C_Q, _C_K, _C_V = 0, ATT_WIDTH, 2 * ATT_WIDTH
_C_MQ = 3 * ATT_WIDTH
_C_MK = _C_MQ + ML_QK_WIDTH
_C_MV = _C_MK + ML_QK_WIDTH
_C_GATES = _C_MV + ML_V_WIDTH
_C_MO = _C_GATES + 2 * N_ML_HEADS
_C_GA = _C_MO + ML_V_WIDTH
_C_GM = _C_GA + D_MODEL
_C_END = _C_GM + D_MODEL


def _dot(a, b):
    return jnp.dot(a, b, preferred_element_type=F32)


def _dot_nt(a, b, precision=None):
    return lax.dot_general(a, b, (((1,), (1,)), ((), ())), precision=precision,
                           preferred_element_type=F32)


def _rms(x, g):
    return x * lax.rsqrt(jnp.mean(x * x, axis=-1, keepdims=True) + NORM_EPS) * g


def _log_sigmoid(x):
    return jnp.minimum(x, 0.0) - jnp.log1p(jnp.exp(-jnp.abs(x)))


def _proj_shared(xb, wm_ref, wg_ref, wgt_ref, wb_ref,
                 mq_ref, mk_ref, mv_ref, gcol_ref, grow_ref, mo_ref, ga_ref, gm_ref):
    mq_ref[0] = _dot(xb, wm_ref[:, 0:ML_QK_WIDTH]).astype(BF16)
    mk_ref[0] = (_dot(xb, wm_ref[:, ML_QK_WIDTH:2 * ML_QK_WIDTH]) * (ML_QK_DIM ** -0.5)).astype(BF16)
    mv_ref[0] = _dot(xb, wm_ref[:, 2 * ML_QK_WIDTH:2 * ML_QK_WIDTH + ML_V_WIDTH]).astype(BF16)
    gcol_ref[0] = _dot(xb, wg_ref[...])
    grow_ref[0] = _dot_nt(wgt_ref[...], xb)
    mo_ref[0] = _dot(xb, wb_ref[:, 0:ML_V_WIDTH])
    ga_ref[0] = _dot(xb, wb_ref[:, ML_V_WIDTH:ML_V_WIDTH + D_MODEL])
    gm_ref[0] = _dot(xb, wb_ref[:, ML_V_WIDTH + D_MODEL:ML_V_WIDTH + 2 * D_MODEL])


def _proj_kernel_rows(x_ref, cos_ref, sa_ref, sb_ref, gmix_ref, gq_ref, gk_ref, e_ref, wqkv_ref,
                      wm_ref, wg_ref, wgt_ref, wb_ref, q_ref, k_ref, v_ref, *rest_refs):
    xb = _rms(x_ref[0], gmix_ref[...]).astype(BF16)
    cos, sa, sb = cos_ref[...], sa_ref[...], sb_ref[...]
    e = e_ref[...]

    def head_norm_rope(z, g):
        zz = z * z
        hi = zz.astype(BF16)
        lo = (zz - hi.astype(F32)).astype(BF16)
        ss = _dot(hi, e) + _dot(lo, e)
        zn = z * lax.rsqrt(ss * (1.0 / ATT_HEAD_DIM) + NORM_EPS) * g
        outs = []
        for grp in range(ATT_WIDTH // LANES):
            zg = zn[:, grp * LANES:(grp + 1) * LANES]
            outs.append(zg * cos + pltpu.roll(zg, LANES - ROPE_DIMS // 2, 1) * sa
                        + pltpu.roll(zg, ROPE_DIMS // 2, 1) * sb)
        return jnp.concatenate(outs, axis=-1)

    q_ref[0] = head_norm_rope(_dot(xb, wqkv_ref[:, 0:ATT_WIDTH]), gq_ref[...])
    k_ref[0] = head_norm_rope(_dot(xb, wqkv_ref[:, ATT_WIDTH:2 * ATT_WIDTH]), gk_ref[...])
    v_ref[0] = _dot(xb, wqkv_ref[:, 2 * ATT_WIDTH:3 * ATT_WIDTH])
    _proj_shared(xb, wm_ref, wg_ref, wgt_ref, wb_ref, *rest_refs)


def _proj_kernel_cols(x_ref, cos_ref, sin_ref, gmix_ref, gq_ref, gk_ref, wqkvt_ref,
                      wm_ref, wg_ref, wgt_ref, wb_ref, qt_ref, kt_ref, vt_ref, *rest_refs):
    xb = _rms(x_ref[0], gmix_ref[...]).astype(BF16)
    tm = xb.shape[0]
    cos, sin = cos_ref[...][None], sin_ref[...][None]
    half = ROPE_DIMS // 2

    def head_norm_rope(z, g):
        z3 = z.reshape(N_ATT_HEADS, ATT_HEAD_DIM, tm)
        ss = jnp.sum(z3 * z3, axis=1, keepdims=True)
        zn = z3 * lax.rsqrt(ss * (1.0 / ATT_HEAD_DIM) + NORM_EPS) * g
        x1, x2 = zn[:, 0:half, :], zn[:, half:ROPE_DIMS, :]
        out = jnp.concatenate([x1 * cos - x2 * sin, x2 * cos + x1 * sin, zn[:, ROPE_DIMS:, :]], axis=1)
        return out.reshape(ATT_WIDTH, tm)

    zt = _dot_nt(wqkvt_ref[...], xb)
    qt_ref[0] = head_norm_rope(zt[0:ATT_WIDTH], gq_ref[...])
    kt_ref[0] = head_norm_rope(zt[ATT_WIDTH:2 * ATT_WIDTH], gk_ref[...])
    vt_ref[0] = zt[2 * ATT_WIDTH:3 * ATT_WIDTH]
    _proj_shared(xb, wm_ref, wg_ref, wgt_ref, wb_ref, *rest_refs)


def _project(x, pos, pw, tm, feature_major):
    B, L, _ = x.shape
    tok = lambda w: pl.BlockSpec((1, tm, w), lambda b, i: (b, i, 0))
    full = lambda a: pl.BlockSpec(a.shape, lambda b, i: (0,) * a.ndim)
    shared = (pw["w_m"], pw["w_g"], pw["w_gt"], pw["w_b"])
    rest_shape = (
        jax.ShapeDtypeStruct((B, L, ML_QK_WIDTH), BF16),
        jax.ShapeDtypeStruct((B, L, ML_QK_WIDTH), BF16),
        jax.ShapeDtypeStruct((B, L, ML_V_WIDTH), BF16),
        jax.ShapeDtypeStruct((B, L, LANES), F32),
        jax.ShapeDtypeStruct((B, 16, L), F32),
        jax.ShapeDtypeStruct((B, L, ML_V_WIDTH), F32),
        jax.ShapeDtypeStruct((B, L, D_MODEL), F32),
        jax.ShapeDtypeStruct((B, L, D_MODEL), F32),
    )
    rest_specs = (tok(ML_QK_WIDTH), tok(ML_QK_WIDTH), tok(ML_V_WIDTH), tok(LANES),
                  pl.BlockSpec((1, 16, tm), lambda b, i: (b, 0, i)),
                  tok(ML_V_WIDTH), tok(D_MODEL), tok(D_MODEL))
    if feature_major:
        cos, sin = _rope_tables_cols(pos)
        tab = pl.BlockSpec((ROPE_DIMS // 2, tm), lambda b, i: (0, i))
        consts = (pw["g_mix"], pw["g_q_col"], pw["g_k_col"], pw["w_qkv_t"]) + shared
        body, tables, tab_specs = _proj_kernel_cols, (cos, sin), [tab, tab]
        qkv_shape = (jax.ShapeDtypeStruct((B, ATT_WIDTH, L), F32),) * 3
        qkv_specs = (pl.BlockSpec((1, ATT_WIDTH, tm), lambda b, i: (b, 0, i)),) * 3
    else:
        tables = _rope_tables_rows(pos)
        tab = pl.BlockSpec((tm, LANES), lambda b, i: (i, 0))
        consts = (pw["g_mix"], pw["g_q"], pw["g_k"], pw["e_head"], pw["w_qkv"]) + shared
        body, tab_specs = _proj_kernel_rows, [tab, tab, tab]
        qkv_shape = (jax.ShapeDtypeStruct((B, L, ATT_WIDTH), F32),) * 3
        qkv_specs = (tok(ATT_WIDTH),) * 3
    return pl.pallas_call(
        body,
        grid=(B, L // tm),
        in_specs=[tok(D_MODEL)] + tab_specs + [full(a) for a in consts],
        out_specs=qkv_specs + rest_specs,
        out_shape=qkv_shape + rest_shape,
        compiler_params=pltpu.CompilerParams(dimension_semantics=("parallel", "parallel"),
                                             vmem_limit_bytes=VMEM_LIMIT),
        name="in_proj_cols" if feature_major else "in_proj_rows",
    )(x, *tables, *consts)


def _gate_pages(part, page_refs, qbd_ref, score_ref, gate_ref):
    blocks_per_step = len(page_refs) // 2
    lane = lax.broadcasted_iota(jnp.int32, (N_ATT_HEADS, LANES), 1)
    qbd = qbd_ref[0]
    gate = gate_ref[...]
    for i in range(blocks_per_step):
        keys = jnp.concatenate([page_refs[2 * i][0].astype(BF16), page_refs[2 * i + 1][0].astype(BF16)], axis=1)
        s = _dot(qbd, keys)
        s = s[0:N_ATT_HEADS] + s[N_ATT_HEADS:]
        score_ref[0, 2 * i] = s[:, 0:PAGE_SIZE]
        score_ref[0, 2 * i + 1] = s[:, PAGE_SIZE:]
        g = jnp.sum(s, axis=1, keepdims=True) * (1.0 / MOBA_BLOCK)
        gate = jnp.where(lane == part * blocks_per_step + i, g, gate)
    gate_ref[...] = gate


def _gate_select(part, sel_ref, gate_ref, *, steps_per_seq, nblk_seq):
    lane = lax.broadcasted_iota(jnp.int32, (N_ATT_HEADS, LANES), 1)

    @pl.when(part == steps_per_seq - 1)
    def _():
        gate = gate_ref[...]
        rank = jnp.zeros((N_ATT_HEADS, LANES), F32)
        for n2 in range(nblk_seq):
            col = gate[:, n2:n2 + 1]
            beats = (col > gate) | ((col == gate) & (lane > n2))
            rank = rank + jnp.where(beats, 1.0, 0.0)
        out = jnp.zeros((N_ATT_HEADS, LANES), jnp.int32)
        lane_f = lane.astype(F32)
        for r in range(MOBA_TOPK):
            pick = (rank == float(r)) & (lane < nblk_seq)
            idx = jnp.sum(jnp.where(pick, lane_f, 0.0), axis=1, keepdims=True)
            out = jnp.where(lane == r, idx.astype(jnp.int32), out)
        sel_ref[0] = out


def _side_job(page_table, cache_kt, q_s, n_steps, n_grid, pt_pos, linear_step):
    Bs, n_seq_pages = page_table.shape
    assert (Bs * n_seq_pages) % n_steps == 0
    pps = Bs * n_seq_pages // n_steps
    assert MOBA_BLOCK == 2 * PAGE_SIZE and pps % 2 == 0 and n_seq_pages % pps == 0
    steps_per_seq = n_seq_pages // pps
    nblk_seq = n_seq_pages * PAGE_SIZE // MOBA_BLOCK
    assert nblk_seq <= LANES
    head_of = jnp.arange(ATT_WIDTH, dtype=jnp.int32) // ATT_HEAD_DIM
    q_heads = jnp.where(head_of[None, None, :] == jnp.arange(N_ATT_HEADS, dtype=jnp.int32)[None, :, None],
                        q_s[:, None, :], 0.0)
    q_lead = q_heads.astype(BF16)
    q_rows = jnp.concatenate([q_lead, (q_heads - q_lead.astype(F32)).astype(BF16)], axis=1)

    def seq_part(args):
        step = linear_step(*args[:n_grid])
        return step // steps_per_seq, step % steps_per_seq

    def page_spec(p):
        def index(*args):
            seq, part = seq_part(args)
            return (args[n_grid + pt_pos][seq, part * pps + p], 0, 0)
        return pl.BlockSpec((1, ATT_WIDTH, PAGE_SIZE), index)

    return dict(
        in_specs=[pl.BlockSpec((1, 2 * N_ATT_HEADS, ATT_WIDTH), lambda *args: (seq_part(args)[0], 0, 0))]
                 + [page_spec(p) for p in range(pps)],
        operands=[q_rows] + [cache_kt] * pps,
        out_specs=(pl.BlockSpec((1, N_ATT_HEADS, LANES), lambda *args: (seq_part(args)[0], 0, 0)),
                   pl.BlockSpec((1, pps, N_ATT_HEADS, PAGE_SIZE), lambda *args: seq_part(args) + (0, 0))),
        out_shape=(jax.ShapeDtypeStruct((Bs, N_ATT_HEADS, LANES), jnp.int32),
                   jax.ShapeDtypeStruct((Bs, n_seq_pages, N_ATT_HEADS, PAGE_SIZE), F32)),
        scratch=[pltpu.VMEM((N_ATT_HEADS, LANES), F32)],
        static=dict(n_side_pages=pps, steps_per_seq=steps_per_seq, nblk_seq=nblk_seq),
    )


def _attn_kernel(q_ref, k_ref, v_ref, o_ref, kb_ref, vt_ref, kmean_ref, acc_ref, s0_ref, s1_ref, *, nblk):
    blk = MOBA_BLOCK
    tq = 2 * blk
    ones_rows = vt_ref.shape[2] - ATT_HEAD_DIM
    t = pl.program_id(2)
    c0 = 2 * t
    nb = kmean_ref.shape[0]

    @pl.when(t == 0)
    def _():
        lane = lax.broadcasted_iota(jnp.int32, (blk, LANES), 1)
        kmean_ref[...] = jnp.zeros_like(kmean_ref)
        for n in range(nblk):
            rows = slice(n * blk, (n + 1) * blk)
            kf = k_ref[0, :, rows].T
            kb_ref[rows, 0:LANES] = kf.astype(BF16)
            kb_ref[rows, LANES:2 * LANES] = jnp.where(lane == n, 1.0, 0.0).astype(BF16)
            kmean_ref[n:n + 1, :] = jnp.sum(kf, axis=0, keepdims=True) * (1.0 / blk)
            for h in range(HEAD_PAIR):
                vh = v_ref[0, h * ATT_HEAD_DIM:(h + 1) * ATT_HEAD_DIM, rows]
                vt_ref[n, h] = jnp.concatenate([vh, jnp.ones((ones_rows, blk), F32)], axis=0).astype(BF16)

    q2 = q_ref[0]
    feat = lax.broadcasted_iota(jnp.int32, (LANES, tq), 0)
    blk_id = lax.broadcasted_iota(jnp.int32, (nb, tq), 0)
    col = lax.broadcasted_iota(jnp.int32, (nb, tq), 1)
    cur = c0 + (col >= blk).astype(jnp.int32)
    kmean = kmean_ref[...]
    bias_pad = jnp.zeros((LANES - nb, tq), F32)

    qas = []
    for h in range(HEAD_PAIR):
        qh = jnp.where((feat >= h * ATT_HEAD_DIM) & (feat < (h + 1) * ATT_HEAD_DIM), q2, 0.0)
        gate = jnp.dot(kmean, qh, precision=lax.Precision.HIGHEST,
                       preferred_element_type=F32)
        gate = jnp.where(blk_id < cur, gate, -jnp.inf)
        rank = jnp.zeros((nb, tq), F32)
        for n2 in range(nblk):
            row = gate[n2:n2 + 1, :]
            beats = (row > gate) | ((row == gate) & (blk_id > n2))
            rank = rank + jnp.where(beats, 1.0, 0.0)
        open_blk = ((blk_id < cur) & (rank < MOBA_TOPK)) | (blk_id == cur)
        bias = jnp.where(open_blk, 0.0, NEG_INF)
        qas.append(jnp.concatenate([qh * (ATT_HEAD_DIM ** -0.5 * LOG2_E), bias, bias_pad],
                                   axis=0).astype(BF16))

    def block_rows(n):
        return kb_ref[pl.ds(pl.multiple_of(n * blk, blk), blk), :]

    key_i = lax.broadcasted_iota(jnp.int32, (blk, tq), 0)
    qry_j = lax.broadcasted_iota(jnp.int32, (blk, tq), 1)
    s_refs = (s0_ref, s1_ref)

    def stage_a(first, slot, own, heads=range(HEAD_PAIR)):
        ks = [block_rows(first), block_rows(first + 1)]
        maxima = []
        for h in heads:
            tops = []
            for i in range(2):
                st = _dot(ks[i], qas[h])
                if own:
                    st = jnp.where(key_i <= qry_j - i * blk, st, NEG_INF)
                s_refs[slot][h, i] = st
                tops.append(jnp.max(st, axis=0, keepdims=True))
            maxima.append(jnp.maximum(tops[0], tops[1]))
        return maxima

    def stage_b(first, slot, m_run, maxima, heads=range(HEAD_PAIR)):
        m_out = []
        for h in heads:
            m_new = jnp.maximum(m_run[h], maxima[h])
            pv = (_dot(vt_ref[first, h], jnp.exp2(s_refs[slot][h, 0] - m_new).astype(BF16))
                  + _dot(vt_ref[first + 1, h], jnp.exp2(s_refs[slot][h, 1] - m_new).astype(BF16)))
            acc_ref[h] = jnp.exp2(m_run[h] - m_new) * acc_ref[h] + pv
            m_out.append(m_new)
        return m_out

    def pending_first(j):
        return jnp.where(j == 0, c0, 2 * j - 2)

    acc_ref[...] = jnp.zeros_like(acc_ref)
    m_start = [jnp.full((1, tq), NEG_INF, F32)] * HEAD_PAIR
    carry = tuple(m_start) + tuple(stage_a(c0, 0, own=True))

    def trip(j, slot, carry):
        m_run, maxima = carry[:HEAD_PAIR], carry[HEAD_PAIR:]
        new_maxima, m_new = [], []
        for h in range(HEAD_PAIR):
            new_maxima += stage_a(2 * j, 1 - slot, own=False, heads=(h,))
            m_new += stage_b(pending_first(j), slot, m_run, maxima, heads=(h,))
        return tuple(m_new) + tuple(new_maxima)

    carry = lax.fori_loop(0, lax.shift_right_logical(t, 1),
                          lambda i, cr: trip(2 * i + 1, 1, trip(2 * i, 0, cr)), carry)

    def odd_tail(carry):
        carry = trip(t - 1, 0, carry)
        stage_b(pending_first(t), 1, carry[:HEAD_PAIR], carry[HEAD_PAIR:])
        return 0

    def even_tail(carry):
        stage_b(pending_first(t), 0, carry[:HEAD_PAIR], carry[HEAD_PAIR:])
        return 0

    lax.cond((t & 1) == 1, odd_tail, even_tail, carry)
    outs = [acc_ref[h, 0:ATT_HEAD_DIM, :] / acc_ref[h, ATT_HEAD_DIM:ATT_HEAD_DIM + 1, :]
            for h in range(HEAD_PAIR)]
    o_ref[0] = jnp.concatenate(outs, axis=0).T.astype(o_ref.dtype)


def _attention_prompt(qt, kt, vt):
    B, _, L = qt.shape
    blk = MOBA_BLOCK
    tq = 2 * blk
    assert L % tq == 0
    nblk = L // blk
    assert nblk <= LANES
    nb = -(-nblk // 8) * 8
    ones_rows = 16
    groups = ATT_WIDTH // LANES
    return pl.pallas_call(
        functools.partial(_attn_kernel, nblk=nblk),
        grid=(B, groups, L // tq),
        in_specs=[pl.BlockSpec((1, LANES, tq), lambda b, g, i: (b, g, i)),
                  pl.BlockSpec((1, LANES, L), lambda b, g, i: (b, g, 0)),
                  pl.BlockSpec((1, LANES, L), lambda b, g, i: (b, g, 0))],
        out_specs=pl.BlockSpec((1, tq, LANES), lambda b, g, i: (b, i, g)),
        out_shape=jax.ShapeDtypeStruct((B, L, ATT_WIDTH), BF16),
        scratch_shapes=[pltpu.VMEM((L, 2 * LANES), BF16),
                        pltpu.VMEM((nblk, HEAD_PAIR, ATT_HEAD_DIM + ones_rows, blk), BF16),
                        pltpu.VMEM((nb, LANES), F32),
                        pltpu.VMEM((HEAD_PAIR, ATT_HEAD_DIM + ones_rows, tq), F32),
                        pltpu.VMEM((HEAD_PAIR, 2, blk, tq), F32),
                        pltpu.VMEM((HEAD_PAIR, 2, blk, tq), F32)],
        compiler_params=pltpu.CompilerParams(dimension_semantics=("parallel", "parallel", "arbitrary"),
                                             vmem_limit_bytes=VMEM_LIMIT),
        name="moba_prompt",
    )(qt, kt, vt)


def _mlstm_kernel(bi_ref, bf_ref, pt_ref, mq_ref, mk_ref, mv_ref, grow_ref, mo_ref, gout_ref, qbd_ref, *refs,
                  chunk, n_side_pages, steps_per_seq, nblk_seq):
    page_refs = refs[:n_side_pages]
    (ml_ref, c_out_ref, n_out_ref, m_out_ref, sel_ref, score_ref, cn_ref, m_ref,
     gate_ref) = refs[n_side_pages:]
    j = pl.program_id(1)
    dk, dv = ML_QK_DIM, ML_V_DIM
    part = lax.rem(pl.program_id(0) * pl.num_programs(1) + j, steps_per_seq)

    @pl.when(j == 0)
    def _():
        cn_ref[...] = jnp.zeros_like(cn_ref)
        m_ref[...] = jnp.zeros_like(m_ref)

    @pl.when(part == 0)
    def _():
        gate_ref[...] = jnp.zeros_like(gate_ref)

    t_i = lax.broadcasted_iota(jnp.int32, (chunk, chunk), 0)
    s_i = lax.broadcasted_iota(jnp.int32, (chunk, chunk), 1)
    tril = s_i <= t_i
    tril_b = jnp.where(tril, 1.0, 0.0).astype(BF16)
    triu_b = jnp.where(t_i <= s_i, 1.0, 0.0).astype(BF16)
    ones_cols = jnp.ones((chunk, LANES), BF16)
    grow = grow_ref[0]
    f_rows = _log_sigmoid(grow[N_ML_HEADS:2 * N_ML_HEADS, :]
                          + jnp.concatenate([jnp.full((1, 1), bf_ref[h], F32) for h in range(N_ML_HEADS)], axis=0))
    hi = f_rows.astype(BF16).astype(F32)
    mid = (f_rows - hi).astype(BF16).astype(F32)
    pieces = jnp.concatenate([hi, mid, f_rows - hi - mid], axis=1)
    pieces = jnp.concatenate([pieces, jnp.zeros((16 - N_ML_HEADS, 3 * chunk), F32)], axis=0).astype(BF16)
    rows = _dot(pieces, jnp.concatenate([triu_b] * 3, axis=0))
    cols = _dot_nt(jnp.concatenate([tril_b] * 3, axis=1), pieces)
    gates = []
    for h in range(N_ML_HEADS):
        i_row = grow[h:h + 1, :] + bi_ref[h]
        b_row = rows[h:h + 1]
        b_col = cols[:, h:h + 1]
        a_row = (i_row - b_row) * LOG2_E
        m_old = m_ref[h][:, 0:1]
        a_low = jnp.where(tril, a_row, NEG_INF)
        mm_col = jnp.maximum(jnp.max(a_low, axis=1, keepdims=True), m_old * LOG2_E)
        w_inter = jnp.exp2(m_old * LOG2_E - mm_col)
        floor_col = jnp.exp2(-(b_col * LOG2_E + mm_col))
        decay = jnp.exp2(a_low - mm_col)
        gates.append((b_row, a_row, m_old, mm_col, w_inter, floor_col, decay))

    _gate_pages(part, page_refs, qbd_ref, score_ref, gate_ref)

    hids = []
    for h in range(N_ML_HEADS):
        b_row, a_row, m_old, mm_col, w_inter, floor_col, decay = gates[h]
        q = mq_ref[0, :, h * dk:(h + 1) * dk]
        k = mk_ref[0, :, h * dk:(h + 1) * dk]
        v_ones = jnp.concatenate([mv_ref[0, :, h * dv:(h + 1) * dv], ones_cols], axis=1)
        cn = cn_ref[h]
        s = _dot_nt(q, k) * decay
        inter = _dot(q, cn.astype(BF16))
        intra = _dot(s.astype(BF16), v_ones)
        num = w_inter * inter[:, 0:dv] + intra[:, 0:dv]
        den = w_inter * inter[:, dv:dv + 1] + intra[:, dv:dv + 1]
        hids.append(num / jnp.maximum(jnp.abs(den), floor_col))

        mm_end = mm_col[chunk - 1:chunk, :]
        w_row = jnp.exp2(a_row - mm_end)
        kw = (k.astype(F32).T * w_row).astype(BF16)
        cn_ref[h] = jnp.exp2(m_old * LOG2_E - mm_end) * cn + _dot(kw, v_ones)
        m_ref[h] = jnp.broadcast_to(b_row[:, chunk - 1:chunk] + mm_end * (1.0 / LOG2_E), (1, LANES))

    for h in range(N_ML_HEADS):
        hn = _rms(hids[h], gout_ref[h:h + 1, :])
        mo = mo_ref[0, :, h * dv:(h + 1) * dv]
        ml_ref[0, :, h * dv:(h + 1) * dv] = (hn * jax.nn.sigmoid(mo)).astype(BF16)

    @pl.when(j == pl.num_programs(1) - 1)
    def _():
        for h in range(N_ML_HEADS):
            c_out_ref[0, h] = cn_ref[h, :, 0:dv].T
            n_out_ref[0, h:h + 1, :] = cn_ref[h, :, dv:dv + LANES].T[0:1, :]
            m_out_ref[0, h:h + 1, :] = m_ref[h]

    _gate_select(part, sel_ref, gate_ref, steps_per_seq=steps_per_seq, nblk_seq=nblk_seq)


def _mlstm_prompt(mq, mk, mv, grow, mo, pw, page_table, cache_kt, q_s):
    B, L, _ = mq.shape
    chunk = ML_CHUNK
    n_chunks = L // chunk
    side = _side_job(page_table, cache_kt, q_s, n_steps=B * n_chunks, n_grid=2, pt_pos=2,
                     linear_step=lambda b, j: b * n_chunks + j)
    tok = lambda w: pl.BlockSpec((1, chunk, w), lambda b, j, *_: (b, j, 0))
    per_b = lambda shp: pl.BlockSpec((1,) + shp, lambda b, j, *_: (b,) + (0,) * len(shp))
    grid_spec = pltpu.PrefetchScalarGridSpec(
        num_scalar_prefetch=3,
        grid=(B, n_chunks),
        in_specs=[tok(ML_QK_WIDTH), tok(ML_QK_WIDTH), tok(ML_V_WIDTH),
                  pl.BlockSpec((1, 16, chunk), lambda b, j, *_: (b, 0, j)), tok(ML_V_WIDTH),
                  pl.BlockSpec((N_ML_HEADS, ML_V_DIM), lambda b, j, *_: (0, 0))] + side["in_specs"],
        out_specs=(tok(ML_V_WIDTH), per_b((N_ML_HEADS, ML_V_DIM, ML_QK_DIM)),
                   per_b((N_ML_HEADS, ML_QK_DIM)), per_b((N_ML_HEADS, LANES))) + side["out_specs"],
        scratch_shapes=[pltpu.VMEM((N_ML_HEADS, ML_QK_DIM, ML_V_DIM + LANES), F32),
                        pltpu.VMEM((N_ML_HEADS, 1, LANES), F32)] + side["scratch"],
    )
    return pl.pallas_call(
        functools.partial(_mlstm_kernel, chunk=chunk, **side["static"]),
        grid_spec=grid_spec,
        out_shape=(jax.ShapeDtypeStruct((B, L, ML_V_WIDTH), BF16),
                   jax.ShapeDtypeStruct((B, N_ML_HEADS, ML_V_DIM, ML_QK_DIM), F32),
                   jax.ShapeDtypeStruct((B, N_ML_HEADS, ML_QK_DIM), F32),
                   jax.ShapeDtypeStruct((B, N_ML_HEADS, LANES), F32)) + side["out_shape"],
        compiler_params=pltpu.CompilerParams(dimension_semantics=("arbitrary", "arbitrary"),
                                             vmem_limit_bytes=VMEM_LIMIT),
        name="mlstm_prompt",
    )(pw["b_i"], pw["b_f"], page_table, mq, mk, mv, grow, mo, pw["g_ml_out"], *side["operands"])


def _post_kernel(x_ref, att_ref, ml_ref, ga_ref, gm_ref, pe_ref,
                 watt_ref, wml_ref, wout_ref, gffn_ref, wgate_ref, wup_ref, wdown_ref,
                 wple_ref, gple_ref, wpg_ref, y_ref):
    mix = (jax.nn.sigmoid(ga_ref[0]) * _dot(att_ref[0].astype(BF16), watt_ref[...])
           + jax.nn.sigmoid(gm_ref[0]) * _dot(ml_ref[0].astype(BF16), wml_ref[...]))
    h = x_ref[0] + _dot(mix.astype(BF16), wout_ref[...])
    hb = _rms(h, gffn_ref[...]).astype(BF16)
    act = jax.nn.silu(_dot(hb, wgate_ref[...])) * _dot(hb, wup_ref[...])
    h = h + _dot(act.astype(BF16), wdown_ref[...])
    gate = jax.nn.sigmoid(_dot(_rms(h, gple_ref[...]).astype(BF16), wpg_ref[...]))
    y_ref[0] = h + gate * _dot(pe_ref[0].astype(BF16), wple_ref[...])


def _post(x, att, ml, ga, gm, pe, pw, tm):
    B, L, _ = x.shape
    tok = lambda w: pl.BlockSpec((1, tm, w), lambda b, i: (b, i, 0))
    full = lambda a: pl.BlockSpec(a.shape, lambda b, i: (0,) * a.ndim, pipeline_mode=pl.Buffered(1))
    consts = (pw["w_att"], pw["w_ml"], pw["w_out"], pw["g_ffn"], pw["w_ffn_gate"], pw["w_ffn_up"],
              pw["w_ffn_down"], pw["w_ple"], pw["g_ple"], pw["w_ple_gate"])
    return pl.pallas_call(
        _post_kernel,
        grid=(B, L // tm),
        in_specs=[tok(D_MODEL), tok(ATT_WIDTH), tok(ML_V_WIDTH), tok(D_MODEL), tok(D_MODEL),
                  tok(pe.shape[-1])] + [full(a) for a in consts],
        out_specs=tok(D_MODEL),
        out_shape=jax.ShapeDtypeStruct((B, L, D_MODEL), F32),
        compiler_params=pltpu.CompilerParams(dimension_semantics=("parallel", "parallel"),
                                             vmem_limit_bytes=VMEM_LIMIT),
        name="merge_ffn",
    )(x, att, ml, ga, gm, pe, *consts)


def _row(ref, b):
    if ref.dtype == F32:
        return ref[0, pl.ds(b, 1), :]
    group = ref[0, pl.ds(pl.multiple_of(lax.shift_right_logical(b, 4) * 16, 16), 16), :].astype(F32)
    pick = lax.broadcasted_iota(jnp.int32, group.shape, 0) == (b & 15)
    return jnp.sum(jnp.where(pick, group, 0.0), axis=0, keepdims=True)


def _sample_attn_kernel(pool_ref, local_ref, *refs, n_pages):
    v_refs = refs[:N_ATT_HEADS * n_pages]
    score_ref, q_ref, kn_ref, vn_ref, o_ref = refs[N_ATT_HEADS * n_pages:]
    b = pl.program_id(0)
    scale = ATT_HEAD_DIM ** -0.5
    rows = 16
    q_row, kn_row, vn_row = _row(q_ref, b), _row(kn_ref, b), _row(vn_ref, b)
    outs = []
    for h in range(N_ATT_HEADS):
        lanes = slice(h * ATT_HEAD_DIM, (h + 1) * ATT_HEAD_DIM)
        s_own = jnp.sum(q_row[:, lanes] * kn_row[:, lanes], axis=1, keepdims=True) * scale
        pages = [local_ref[b, h * n_pages + j] for j in range(n_pages)]
        s = jnp.concatenate([score_ref[0, pg, h:h + 1, :] for pg in pages], axis=1) * scale
        v_all = jnp.concatenate([v_refs[h * n_pages + j][0].astype(BF16) for j in range(n_pages)], axis=1)
        m = jnp.maximum(s_own, jnp.max(s, axis=1, keepdims=True))
        p_own = jnp.exp(s_own - m)
        p = jnp.exp(s - m)
        l = p_own + jnp.sum(p, axis=1, keepdims=True)
        pv = _dot_nt(jnp.broadcast_to(p, (rows, n_pages * PAGE_SIZE)).astype(BF16), v_all)[0:1]
        outs.append((p_own * vn_row[:, lanes] + pv) / l)
    o_ref[0, pl.ds(b, 1), :] = jnp.concatenate(outs, axis=1)


def _sample_attention(page_table, sel, scores, cache_vt, q_s, k_s, v_s):
    Bs, n_seq_pages = page_table.shape
    ppb = MOBA_BLOCK // PAGE_SIZE
    n_pages = MOBA_TOPK * ppb
    local = (ppb * sel[..., None] + jnp.arange(ppb, dtype=jnp.int32)).reshape(Bs, N_ATT_HEADS * n_pages)
    pool = jnp.take_along_axis(page_table, local, axis=1)

    def tile_spec(h, j):
        return pl.BlockSpec((1, ATT_HEAD_DIM, PAGE_SIZE),
                            lambda b, pool_ref, local_ref: (pool_ref[b, h * n_pages + j], h, 0))

    tiles = [tile_spec(h, j) for h in range(N_ATT_HEADS) for j in range(n_pages)]
    rows = pl.BlockSpec((1, Bs, ATT_WIDTH), lambda b, *_: (0, 0, 0))
    grid_spec = pltpu.PrefetchScalarGridSpec(
        num_scalar_prefetch=2,
        grid=(Bs,),
        in_specs=tiles + [pl.BlockSpec((1, n_seq_pages, N_ATT_HEADS, PAGE_SIZE), lambda b, *_: (b, 0, 0, 0)),
                          rows, rows, rows],
        out_specs=rows,
    )
    return pl.pallas_call(
        functools.partial(_sample_attn_kernel, n_pages=n_pages),
        grid_spec=grid_spec,
        out_shape=jax.ShapeDtypeStruct((1, Bs, ATT_WIDTH), F32),
        compiler_params=pltpu.CompilerParams(dimension_semantics=("arbitrary",),
                                             vmem_limit_bytes=VMEM_LIMIT),
        name="moba_sample_attn",
    )(pool, local, *([cache_vt] * len(tiles)), scores, q_s, k_s, v_s)


def _mlstm_step_kernel(bi_ref, bf_ref, m0_ref, mq_ref, mk_ref, mv_ref, g_ref, mo_ref, gout_ref,
                       c0_ref, n0_ref, ml_ref, c_out_ref, n_out_ref, m_out_ref, *, seqs):
    dv = ML_V_DIM
    r_i = lax.broadcasted_iota(jnp.int32, (dv, dv), 0)
    c_i = lax.broadcasted_iota(jnp.int32, (dv, dv), 1)
    eye = r_i == c_i
    for i in range(seqs):
        _mlstm_step_one(i, pl.program_id(0) * seqs + i, eye, bi_ref, bf_ref, m0_ref, mq_ref, mk_ref, mv_ref,
                        g_ref, mo_ref, gout_ref, c0_ref, n0_ref, ml_ref, c_out_ref, n_out_ref, m_out_ref)


def _mlstm_step_one(i, b, eye, bi_ref, bf_ref, m0_ref, mq_ref, mk_ref, mv_ref, g_ref, mo_ref, gout_ref,
                    c0_ref, n0_ref, ml_ref, c_out_ref, n_out_ref, m_out_ref):
    dv = ML_V_DIM
    g = _row(g_ref, b)
    q_all, k_all, v_all, mo_all = _row(mq_ref, b), _row(mk_ref, b), _row(mv_ref, b), _row(mo_ref, b)
    outs = []
    for h in range(N_ML_HEADS):
        i_g = g[:, h:h + 1] + bi_ref[h]
        logf = _log_sigmoid(g[:, N_ML_HEADS + h:N_ML_HEADS + h + 1] + bf_ref[h])
        m_old = m0_ref[b, h]
        inter = logf + m_old
        m_t = jnp.maximum(inter, i_g)
        w_inter = jnp.exp(inter - m_t)
        w_in = jnp.exp(i_g - m_t)
        q = q_all[:, h * ML_QK_DIM:(h + 1) * ML_QK_DIM]
        k = k_all[:, h * ML_QK_DIM:(h + 1) * ML_QK_DIM]
        v = v_all[:, h * dv:(h + 1) * dv]
        c_old = c0_ref[0, i, h]
        n_old = n0_ref[0, i, h:h + 1, :]
        s = jnp.sum(q * k, axis=1, keepdims=True) * w_in
        v_col = jnp.sum(jnp.where(eye, v, 0.0), axis=1, keepdims=True)
        q_rows = jnp.broadcast_to(q, (16, ML_QK_DIM)).astype(BF16)
        cq = _dot_nt(q_rows, c_old.astype(BF16))[0:1]
        num = w_inter * cq + s * v
        den = w_inter * jnp.sum(n_old * q, axis=1, keepdims=True) + s
        hid = num / jnp.maximum(jnp.abs(den), jnp.exp(-m_t))
        c_out_ref[0, i, h] = w_inter * c_old + (w_in * v_col) * k
        n_out_ref[0, i, h:h + 1, :] = w_inter * n_old + w_in * k
        m_out_ref[i, h:h + 1, :] = jnp.broadcast_to(m_t, (1, LANES))
        hn = _rms(hid, gout_ref[h:h + 1, :])
        outs.append(hn * jax.nn.sigmoid(mo_all[:, h * dv:(h + 1) * dv]))
    ml_ref[0, pl.ds(b, 1), :] = jnp.concatenate(outs, axis=1)


def _mlstm_sample(mq, mk, mv, gcol, mo, state_c, state_n, state_m, pw):
    Bs = mq.shape[1]
    assert Bs % 16 == 0
    seqs = 2
    rows = lambda w: pl.BlockSpec((1, Bs, w), lambda b, *_: (0, 0, 0))
    grid_spec = pltpu.PrefetchScalarGridSpec(
        num_scalar_prefetch=3,
        grid=(Bs // seqs,),
        in_specs=[rows(ML_QK_WIDTH), rows(ML_QK_WIDTH), rows(ML_V_WIDTH), rows(LANES), rows(ML_V_WIDTH),
                  pl.BlockSpec((N_ML_HEADS, ML_V_DIM), lambda b, *_: (0, 0)),
                  pl.BlockSpec((1, seqs, N_ML_HEADS, ML_V_DIM, ML_QK_DIM), lambda b, *_: (0, b, 0, 0, 0)),
                  pl.BlockSpec((1, seqs, N_ML_HEADS, ML_QK_DIM), lambda b, *_: (0, b, 0, 0))],
        out_specs=(rows(ML_V_WIDTH),
                   pl.BlockSpec((1, seqs, N_ML_HEADS, ML_V_DIM, ML_QK_DIM), lambda b, *_: (0, b, 0, 0, 0)),
                   pl.BlockSpec((1, seqs, N_ML_HEADS, ML_QK_DIM), lambda b, *_: (0, b, 0, 0)),
                   pl.BlockSpec((seqs, N_ML_HEADS, LANES), lambda b, *_: (b, 0, 0))),
    )
    return pl.pallas_call(
        functools.partial(_mlstm_step_kernel, seqs=seqs),
        grid_spec=grid_spec,
        out_shape=(jax.ShapeDtypeStruct((1, Bs, ML_V_WIDTH), F32),
                   jax.ShapeDtypeStruct(state_c.shape, F32),
                   jax.ShapeDtypeStruct(state_n.shape, F32),
                   jax.ShapeDtypeStruct((Bs, N_ML_HEADS, LANES), F32)),
        compiler_params=pltpu.CompilerParams(dimension_semantics=("arbitrary",),
                                             vmem_limit_bytes=VMEM_LIMIT),
        name="mlstm_sample",
    )(pw["b_i"], pw["b_f"], state_m[0], mq, mk, mv, gcol, mo, pw["g_ml_out"], state_c, state_n)


def _rope_angles(pos):
    freqs = ROPE_THETA ** (-jnp.arange(0, ROPE_DIMS, 2, dtype=F32) / ROPE_DIMS)
    return pos.astype(F32)[:, None] * freqs[None, :]


def _rope_tables_cols(pos):
    ang = _rope_angles(pos).T
    return jnp.cos(ang), jnp.sin(ang)


def _rope_tables_rows(pos):
    half = ROPE_DIMS // 2
    n = pos.shape[0]
    ang = _rope_angles(pos)
    cos, sin = jnp.cos(ang), jnp.sin(ang)
    rest = ATT_HEAD_DIM - ROPE_DIMS
    zh = jnp.zeros((n, half), F32)
    cos_h = jnp.concatenate([cos, cos, jnp.ones((n, rest), F32)], axis=-1)
    sa_h = jnp.concatenate([-sin, zh, jnp.zeros((n, rest), F32)], axis=-1)
    sb_h = jnp.concatenate([zh, sin, jnp.zeros((n, rest), F32)], axis=-1)
    tile = lambda t: jnp.tile(t, (1, HEAD_PAIR))
    return tile(cos_h), tile(sa_h), tile(sb_h)


def _prepare_weights(w_in, b_igate, b_fgate, g_mix_norm, g_q_norm, g_k_norm, g_ml_out_norm,
                     w_att_branch, w_ml_branch, w_out, g_ffn_norm, w_ffn_gate, w_ffn_up, w_ffn_down,
                     w_ple, g_ple_norm, w_ple_gate):
    w_gates = w_in[:, _C_GATES:_C_MO]
    head_of = jnp.arange(ATT_WIDTH, dtype=jnp.int32) // ATT_HEAD_DIM
    w_qkv = w_in[:, _C_Q:_C_MQ].astype(BF16)
    return {
        "w_qkv": w_qkv,
        "w_qkv_t": w_qkv.T,
        "w_m": w_in[:, _C_MQ:_C_GATES].astype(BF16),
        "g_q_col": g_q_norm.reshape(1, ATT_HEAD_DIM, 1),
        "g_k_col": g_k_norm.reshape(1, ATT_HEAD_DIM, 1),
        "w_g": jnp.pad(w_gates, ((0, 0), (0, LANES - 2 * N_ML_HEADS))).astype(BF16),
        "w_gt": jnp.pad(w_gates.T, ((0, 16 - 2 * N_ML_HEADS), (0, 0))).astype(BF16),
        "w_b": w_in[:, _C_MO:_C_END].astype(BF16),
        "e_head": (head_of[:, None] == head_of[None, :]).astype(BF16),
        "g_mix": g_mix_norm.reshape(1, D_MODEL),
        "g_q": jnp.tile(g_q_norm, N_ATT_HEADS).reshape(1, ATT_WIDTH),
        "g_k": jnp.tile(g_k_norm, N_ATT_HEADS).reshape(1, ATT_WIDTH),
        "g_ml_out": g_ml_out_norm,
        "b_i": b_igate, "b_f": b_fgate,
        "w_att": w_att_branch.astype(BF16), "w_ml": w_ml_branch.astype(BF16), "w_out": w_out.astype(BF16),
        "g_ffn": g_ffn_norm.reshape(1, D_MODEL),
        "w_ffn_gate": w_ffn_gate.astype(BF16), "w_ffn_up": w_ffn_up.astype(BF16),
        "w_ffn_down": w_ffn_down.astype(BF16),
        "w_ple": w_ple.astype(BF16), "g_ple": g_ple_norm.reshape(1, D_MODEL),
        "w_ple_gate": w_ple_gate.astype(BF16),
    }


def _layer(x_p, pe_p, x_s, pe_s, cache_k, cache_v, page_table, state_c, state_n, state_m, pw):
    B, L, _ = x_p.shape
    Bs, Ls, _ = x_s.shape
    assert Ls == 1
    past_len = page_table.shape[1] * PAGE_SIZE
    n_pool = cache_k.shape[0]
    pages_t = lambda c: jnp.transpose(c, (0, 2, 3, 1)).reshape(n_pool, ATT_WIDTH, PAGE_SIZE)
    cache_kt, cache_vt = pages_t(cache_k), pages_t(cache_v)

    xt = x_s.reshape(1, Bs, D_MODEL)
    pos_s = jnp.full((Bs,), past_len, dtype=jnp.int32)
    q, k, v, mq_s, mk_s, mv_s, gcol_s, _, mo_s, ga_s, gm_s = _project(xt, pos_s, pw, tm=Bs, feature_major=False)

    pos_p = jnp.arange(L, dtype=jnp.int32)
    qt, kt, vt, mq, mk, mv, _, grow, mo, ga, gm = _project(x_p, pos_p, pw, tm=256, feature_major=True)
    att_p = _attention_prompt(qt, kt, vt)
    ml_p, c_p, n_p, m_p, sel, scores = _mlstm_prompt(mq, mk, mv, grow, mo, pw, page_table, cache_kt,
                                                     q.reshape(Bs, ATT_WIDTH))
    y_p = _post(x_p, att_p, ml_p, ga, gm, pe_p, pw, tm=256)
    tokens_major = lambda t: jnp.transpose(t.reshape(B, N_ATT_HEADS, ATT_HEAD_DIM, L), (0, 3, 1, 2))

    att_s = _sample_attention(page_table, sel[:, :, :MOBA_TOPK], scores, cache_vt, q, k, v)
    ml_s, c_s, n_s, m_s = _mlstm_sample(mq_s, mk_s, mv_s, gcol_s, mo_s, state_c, state_n, state_m, pw)
    y_s = _post(xt, att_s, ml_s, ga_s, gm_s, pe_s.reshape(1, Bs, pe_s.shape[-1]), pw, tm=Bs)
    prompt = (y_p, tokens_major(kt), tokens_major(vt), c_p, n_p, m_p[:, :, 0])
    sample = (y_s.reshape(Bs, 1, D_MODEL), k, v, c_s, n_s, m_s[:, :, 0])
    return prompt, sample


def kernel(x_prompt, x_sample, cache_k, cache_v, state_mlstm_C, state_mlstm_n, state_mlstm_m, page_table,
           p_prompt, p_sample, w_in, b_igate, b_fgate, g_mix_norm, g_q_norm, g_k_norm, g_ml_out_norm,
           w_att_branch, w_ml_branch, w_out, g_ffn_norm, w_ffn_gate, w_ffn_up, w_ffn_down,
           w_ple, g_ple_norm, w_ple_gate):
    depth = w_in.shape[0]
    assert depth == 1
    Bp, Lp, _ = x_prompt.shape
    Bs, Ls, _ = x_sample.shape
    pw = _prepare_weights(w_in[0], b_igate[0], b_fgate[0], g_mix_norm[0], g_q_norm[0], g_k_norm[0],
                          g_ml_out_norm[0], w_att_branch[0], w_ml_branch[0], w_out[0], g_ffn_norm[0],
                          w_ffn_gate[0], w_ffn_up[0], w_ffn_down[0], w_ple[0], g_ple_norm[0], w_ple_gate[0])
    (yp, kp, vp, cp, n_p, mp), (ys, ks, vs, cs, ns, ms) = _layer(
        x_prompt, p_prompt[0], x_sample, p_sample[0], cache_k[0], cache_v[0], page_table,
        state_mlstm_C, state_mlstm_n, state_mlstm_m, pw)
    heads = lambda t: t.reshape(1, Bs, Ls, N_ATT_HEADS, ATT_HEAD_DIM)
    return (yp, ys, kp[None], vp[None], heads(ks), heads(vs),
            cp[None], n_p[None], mp[None], cs, ns, ms[None])
```

```python
import functools

import jax
import jax.numpy as jnp
from jax import lax
from jax.experimental import pallas as pl
from jax.experimental.pallas import tpu as pltpu

F32 = jnp.float32
BF16 = jnp.bfloat16

D_MODEL = 1024
N_ATT_HEADS = 8
ATT_HEAD_DIM = 64
ATT_WIDTH = N_ATT_HEADS * ATT_HEAD_DIM
ROPE_DIMS = ATT_HEAD_DIM // 4
ROPE_THETA = 500000.0
MOBA_BLOCK = 256
MOBA_TOPK = 3
PAGE_SIZE = 128
N_ML_HEADS = 4
ML_QK_DIM = 128
ML_V_DIM = 256
ML_QK_WIDTH = N_ML_HEADS * ML_QK_DIM
ML_V_WIDTH = N_ML_HEADS * ML_V_DIM
NORM_EPS = 1e-6
NEG_INF = -1e30
LOG2_E = 1.4426950408889634

LANES = 128
HEAD_PAIR = LANES // ATT_HEAD_DIM
ML_CHUNK = 256
VMEM_LIMIT = 56 * 1024 * 1024

_C_Q, _C_K, _C_V = 0, ATT_WIDTH, 2 * ATT_WIDTH
_C_MQ = 3 * ATT_WIDTH
_C_MK = _C_MQ + ML_QK_WIDTH
_C_MV = _C_MK + ML_QK_WIDTH
_C_GATES = _C_MV + ML_V_WIDTH
_C_MO = _C_GATES + 2 * N_ML_HEADS
_C_GA = _C_MO + ML_V_WIDTH
_C_GM = _C_GA + D_MODEL
_C_END = _C_GM + D_MODEL


def _dot(a, b):
    return jnp.dot(a, b, preferred_element_type=F32)


def _dot_nt(a, b, precision=None):
    return lax.dot_general(a, b, (((1,), (1,)), ((), ())), precision=precision,
                           preferred_element_type=F32)


def _rms(x, g):
    return x * lax.rsqrt(jnp.mean(x * x, axis=-1, keepdims=True) + NORM_EPS) * g


def _log_sigmoid(x):
    return jnp.minimum(x, 0.0) - jnp.log1p(jnp.exp(-jnp.abs(x)))


def _proj_shared(xb, wm_ref, wg_ref, wgt_ref, wb_ref,
                 mq_ref, mk_ref, mv_ref, gcol_ref, grow_ref, mo_ref, ga_ref, gm_ref):
    mq_ref[0] = _dot(xb, wm_ref[:, 0:ML_QK_WIDTH]).astype(BF16)
    mk_ref[0] = (_dot(xb, wm_ref[:, ML_QK_WIDTH:2 * ML_QK_WIDTH]) * (ML_QK_DIM ** -0.5)).astype(BF16)
    mv_ref[0] = _dot(xb, wm_ref[:, 2 * ML_QK_WIDTH:2 * ML_QK_WIDTH + ML_V_WIDTH]).astype(BF16)
    gcol_ref[0] = _dot(xb, wg_ref[...])
    grow_ref[0] = _dot_nt(wgt_ref[...], xb)
    mo_ref[0] = _dot(xb, wb_ref[:, 0:ML_V_WIDTH])
    ga_ref[0] = _dot(xb, wb_ref[:, ML_V_WIDTH:ML_V_WIDTH + D_MODEL])
    gm_ref[0] = _dot(xb, wb_ref[:, ML_V_WIDTH + D_MODEL:ML_V_WIDTH + 2 * D_MODEL])


def _proj_kernel_rows(x_ref, cos_ref, sa_ref, sb_ref, gmix_ref, gq_ref, gk_ref, e_ref, wqkv_ref,
                      wm_ref, wg_ref, wgt_ref, wb_ref, q_ref, k_ref, v_ref, *rest_refs):
    xb = _rms(x_ref[0], gmix_ref[...]).astype(BF16)
    cos, sa, sb = cos_ref[...], sa_ref[...], sb_ref[...]
    e = e_ref[...]

    def head_norm_rope(z, g):
        zz = z * z
        hi = zz.astype(BF16)
        lo = (zz - hi.astype(F32)).astype(BF16)
        ss = _dot(hi, e) + _dot(lo, e)
        zn = z * lax.rsqrt(ss * (1.0 / ATT_HEAD_DIM) + NORM_EPS) * g
        outs = []
        for grp in range(ATT_WIDTH // LANES):
            zg = zn[:, grp * LANES:(grp + 1) * LANES]
            outs.append(zg * cos + pltpu.roll(zg, LANES - ROPE_DIMS // 2, 1) * sa
                        + pltpu.roll(zg, ROPE_DIMS // 2, 1) * sb)
        return jnp.concatenate(outs, axis=-1)

    q_ref[0] = head_norm_rope(_dot(xb, wqkv_ref[:, 0:ATT_WIDTH]), gq_ref[...])
    k_ref[0] = head_norm_rope(_dot(xb, wqkv_ref[:, ATT_WIDTH:2 * ATT_WIDTH]), gk_ref[...])
    v_ref[0] = _dot(xb, wqkv_ref[:, 2 * ATT_WIDTH:3 * ATT_WIDTH])
    _proj_shared(xb, wm_ref, wg_ref, wgt_ref, wb_ref, *rest_refs)


def _proj_kernel_cols(x_ref, cos_ref, sin_ref, gmix_ref, gq_ref, gk_ref, wqkvt_ref,
                      wm_ref, wg_ref, wgt_ref, wb_ref, qt_ref, kt_ref, vt_ref, *rest_refs):
    xb = _rms(x_ref[0], gmix_ref[...]).astype(BF16)
    tm = xb.shape[0]
    cos, sin = cos_ref[...][None], sin_ref[...][None]
    half = ROPE_DIMS // 2

    def head_norm_rope(z, g):
        z3 = z.reshape(N_ATT_HEADS, ATT_HEAD_DIM, tm)
        ss = jnp.sum(z3 * z3, axis=1, keepdims=True)
        zn = z3 * lax.rsqrt(ss * (1.0 / ATT_HEAD_DIM) + NORM_EPS) * g
        x1, x2 = zn[:, 0:half, :], zn[:, half:ROPE_DIMS, :]
        out = jnp.concatenate([x1 * cos - x2 * sin, x2 * cos + x1 * sin, zn[:, ROPE_DIMS:, :]], axis=1)
        return out.reshape(ATT_WIDTH, tm)

    zt = _dot_nt(wqkvt_ref[...], xb)
    qt_ref[0] = head_norm_rope(zt[0:ATT_WIDTH], gq_ref[...])
    kt_ref[0] = head_norm_rope(zt[ATT_WIDTH:2 * ATT_WIDTH], gk_ref[...])
    vt_ref[0] = zt[2 * ATT_WIDTH:3 * ATT_WIDTH]
    _proj_shared(xb, wm_ref, wg_ref, wgt_ref, wb_ref, *rest_refs)


def _project(x, pos, pw, tm, feature_major):
    B, L, _ = x.shape
    tok = lambda w: pl.BlockSpec((1, tm, w), lambda b, i: (b, i, 0))
    full = lambda a: pl.BlockSpec(a.shape, lambda b, i: (0,) * a.ndim, pipeline_mode=pl.Buffered(1))
    shared = (pw["w_m"], pw["w_g"], pw["w_gt"], pw["w_b"])
    rest_shape = (
        jax.ShapeDtypeStruct((B, L, ML_QK_WIDTH), BF16),
        jax.ShapeDtypeStruct((B, L, ML_QK_WIDTH), BF16),
        jax.ShapeDtypeStruct((B, L, ML_V_WIDTH), BF16),
        jax.ShapeDtypeStruct((B, L, LANES), F32),
        jax.ShapeDtypeStruct((B, 16, L), F32),
        jax.ShapeDtypeStruct((B, L, ML_V_WIDTH), F32),
        jax.ShapeDtypeStruct((B, L, D_MODEL), F32),
        jax.ShapeDtypeStruct((B, L, D_MODEL), F32),
    )
    rest_specs = (tok(ML_QK_WIDTH), tok(ML_QK_WIDTH), tok(ML_V_WIDTH), tok(LANES),
                  pl.BlockSpec((1, 16, tm), lambda b, i: (b, 0, i)),
                  tok(ML_V_WIDTH), tok(D_MODEL), tok(D_MODEL))
    if feature_major:
        cos, sin = _rope_tables_cols(pos)
        tab = pl.BlockSpec((ROPE_DIMS // 2, tm), lambda b, i: (0, i))
        consts = (pw["g_mix"], pw["g_q_col"], pw["g_k_col"], pw["w_qkv_t"]) + shared
        body, tables, tab_specs = _proj_kernel_cols, (cos, sin), [tab, tab]
        qkv_shape = (jax.ShapeDtypeStruct((B, ATT_WIDTH, L), F32),) * 3
        qkv_specs = (pl.BlockSpec((1, ATT_WIDTH, tm), lambda b, i: (b, 0, i)),) * 3
    else:
        tables = _rope_tables_rows(pos)
        tab = pl.BlockSpec((tm, LANES), lambda b, i: (i, 0))
        consts = (pw["g_mix"], pw["g_q"], pw["g_k"], pw["e_head"], pw["w_qkv"]) + shared
        body, tab_specs = _proj_kernel_rows, [tab, tab, tab]
        qkv_shape = (jax.ShapeDtypeStruct((B, L, ATT_WIDTH), F32),) * 3
        qkv_specs = (tok(ATT_WIDTH),) * 3
    return pl.pallas_call(
        body,
        grid=(B, L // tm),
        in_specs=[tok(D_MODEL)] + tab_specs + [full(a) for a in consts],
        out_specs=qkv_specs + rest_specs,
        out_shape=qkv_shape + rest_shape,
        compiler_params=pltpu.CompilerParams(dimension_semantics=("parallel", "parallel"),
                                             vmem_limit_bytes=VMEM_LIMIT),
        name="in_proj_cols" if feature_major else "in_proj_rows",
    )(x, *tables, *consts)


def _gate_pages(part, page_refs, qbd_ref, score_ref, gate_ref):
    blocks_per_step = len(page_refs) // 2
    lane = lax.broadcasted_iota(jnp.int32, (N_ATT_HEADS, LANES), 1)
    qbd = qbd_ref[0]
    gate = gate_ref[...]
    for i in range(blocks_per_step):
        keys = jnp.concatenate([page_refs[2 * i][0].astype(BF16), page_refs[2 * i + 1][0].astype(BF16)], axis=1)
        s = _dot(qbd, keys)
        s = s[0:N_ATT_HEADS] + s[N_ATT_HEADS:]
        score_ref[0, 2 * i] = s[:, 0:PAGE_SIZE]
        score_ref[0, 2 * i + 1] = s[:, PAGE_SIZE:]
        g = jnp.sum(s, axis=1, keepdims=True) * (1.0 / MOBA_BLOCK)
        gate = jnp.where(lane == part * blocks_per_step + i, g, gate)
    gate_ref[...] = gate


def _gate_select(part, sel_ref, gate_ref, *, steps_per_seq, nblk_seq):
    lane = lax.broadcasted_iota(jnp.int32, (N_ATT_HEADS, LANES), 1)

    @pl.when(part == steps_per_seq - 1)
    def _():
        gate = gate_ref[...]
        rank = jnp.zeros((N_ATT_HEADS, LANES), F32)
        for n2 in range(nblk_seq):
            col = gate[:, n2:n2 + 1]
            beats = (col > gate) | ((col == gate) & (lane > n2))
            rank = rank + jnp.where(beats, 1.0, 0.0)
        out = jnp.zeros((N_ATT_HEADS, LANES), jnp.int32)
        lane_f = lane.astype(F32)
        for r in range(MOBA_TOPK):
            pick = (rank == float(r)) & (lane < nblk_seq)
            idx = jnp.sum(jnp.where(pick, lane_f, 0.0), axis=1, keepdims=True)
            out = jnp.where(lane == r, idx.astype(jnp.int32), out)
        sel_ref[0] = out


def _side_job(page_table, cache_kt, q_s, n_steps, n_grid, pt_pos, linear_step):
    Bs, n_seq_pages = page_table.shape
    assert (Bs * n_seq_pages) % n_steps == 0
    pps = Bs * n_seq_pages // n_steps
    assert MOBA_BLOCK == 2 * PAGE_SIZE and pps % 2 == 0 and n_seq_pages % pps == 0
    steps_per_seq = n_seq_pages // pps
    nblk_seq = n_seq_pages * PAGE_SIZE // MOBA_BLOCK
    assert nblk_seq <= LANES
    head_of = jnp.arange(ATT_WIDTH, dtype=jnp.int32) // ATT_HEAD_DIM
    q_heads = jnp.where(head_of[None, None, :] == jnp.arange(N_ATT_HEADS, dtype=jnp.int32)[None, :, None],
                        q_s[:, None, :], 0.0)
    q_lead = q_heads.astype(BF16)
    q_rows = jnp.concatenate([q_lead, (q_heads - q_lead.astype(F32)).astype(BF16)], axis=1)

    def seq_part(args):
        step = linear_step(*args[:n_grid])
        return step // steps_per_seq, step % steps_per_seq

    def page_spec(p):
        def index(*args):
            seq, part = seq_part(args)
            return (args[n_grid + pt_pos][seq, part * pps + p], 0, 0)
        return pl.BlockSpec((1, ATT_WIDTH, PAGE_SIZE), index)

    return dict(
        in_specs=[pl.BlockSpec((1, 2 * N_ATT_HEADS, ATT_WIDTH), lambda *args: (seq_part(args)[0], 0, 0))]
                 + [page_spec(p) for p in range(pps)],
        operands=[q_rows] + [cache_kt] * pps,
        out_specs=(pl.BlockSpec((1, N_ATT_HEADS, LANES), lambda *args: (seq_part(args)[0], 0, 0)),
                   pl.BlockSpec((1, pps, N_ATT_HEADS, PAGE_SIZE), lambda *args: seq_part(args) + (0, 0))),
        out_shape=(jax.ShapeDtypeStruct((Bs, N_ATT_HEADS, LANES), jnp.int32),
                   jax.ShapeDtypeStruct((Bs, n_seq_pages, N_ATT_HEADS, PAGE_SIZE), F32)),
        scratch=[pltpu.VMEM((N_ATT_HEADS, LANES), F32)],
        static=dict(n_side_pages=pps, steps_per_seq=steps_per_seq, nblk_seq=nblk_seq),
    )


def _attn_kernel(q_ref, k_ref, v_ref, o_ref, kb_ref, vt_ref, kmean_ref, acc_ref, s0_ref, s1_ref, *, nblk):
    blk = MOBA_BLOCK
    tq = 2 * blk
    ones_rows = vt_ref.shape[2] - ATT_HEAD_DIM
    t = pl.program_id(2)
    c0 = 2 * t
    nb = kmean_ref.shape[0]

    @pl.when(t == 0)
    def _():
        lane = lax.broadcasted_iota(jnp.int32, (blk, LANES), 1)
        kmean_ref[...] = jnp.zeros_like(kmean_ref)
        for n in range(nblk):
            rows = slice(n * blk, (n + 1) * blk)
            kf = k_ref[0, :, rows].T
            kb_ref[rows, 0:LANES] = kf.astype(BF16)
            kb_ref[rows, LANES:2 * LANES] = jnp.where(lane == n, 1.0, 0.0).astype(BF16)
            kmean_ref[n:n + 1, :] = jnp.sum(kf, axis=0, keepdims=True) * (1.0 / blk)
            for h in range(HEAD_PAIR):
                vh = v_ref[0, h * ATT_HEAD_DIM:(h + 1) * ATT_HEAD_DIM, rows]
                vt_ref[n, h] = jnp.concatenate([vh, jnp.ones((ones_rows, blk), F32)], axis=0).astype(BF16)

    q2 = q_ref[0]
    feat = lax.broadcasted_iota(jnp.int32, (LANES, tq), 0)
    blk_id = lax.broadcasted_iota(jnp.int32, (nb, tq), 0)
    col = lax.broadcasted_iota(jnp.int32, (nb, tq), 1)
    cur = c0 + (col >= blk).astype(jnp.int32)
    kmean = kmean_ref[...]
    bias_pad = jnp.zeros((LANES - nb, tq), F32)

    qas = []
    for h in range(HEAD_PAIR):
        qh = jnp.where((feat >= h * ATT_HEAD_DIM) & (feat < (h + 1) * ATT_HEAD_DIM), q2, 0.0)
        gate = jnp.dot(kmean, qh, precision=lax.Precision.HIGHEST,
                       preferred_element_type=F32)
        gate = jnp.where(blk_id < cur, gate, -jnp.inf)
        rank = jnp.zeros((nb, tq), F32)
        for n2 in range(nblk):
            row = gate[n2:n2 + 1, :]
            beats = (row > gate) | ((row == gate) & (blk_id > n2))
            rank = rank + jnp.where(beats, 1.0, 0.0)
        open_blk = ((blk_id < cur) & (rank < MOBA_TOPK)) | (blk_id == cur)
        bias = jnp.where(open_blk, 0.0, NEG_INF)
        qas.append(jnp.concatenate([qh * (ATT_HEAD_DIM ** -0.5 * LOG2_E), bias, bias_pad],
                                   axis=0).astype(BF16))

    def block_rows(n):
        return kb_ref[pl.ds(pl.multiple_of(n * blk, blk), blk), :]

    key_i = lax.broadcasted_iota(jnp.int32, (blk, tq), 0)
    qry_j = lax.broadcasted_iota(jnp.int32, (blk, tq), 1)
    s_refs = (s0_ref, s1_ref)

    def stage_a(first, slot, own, heads=range(HEAD_PAIR)):
        ks = [block_rows(first), block_rows(first + 1)]
        maxima = []
        for h in heads:
            tops = []
            for i in range(2):
                st = _dot(ks[i], qas[h])
                if own:
                    st = jnp.where(key_i <= qry_j - i * blk, st, NEG_INF)
                s_refs[slot][h, i] = st
                tops.append(jnp.max(st, axis=0, keepdims=True))
            maxima.append(jnp.maximum(tops[0], tops[1]))
        return maxima

    def stage_b(first, slot, m_run, maxima, heads=range(HEAD_PAIR)):
        m_out = []
        for h in heads:
            m_new = jnp.maximum(m_run[h], maxima[h])
            pv = (_dot(vt_ref[first, h], jnp.exp2(s_refs[slot][h, 0] - m_new).astype(BF16))
                  + _dot(vt_ref[first + 1, h], jnp.exp2(s_refs[slot][h, 1] - m_new).astype(BF16)))
            acc_ref[h] = jnp.exp2(m_run[h] - m_new) * acc_ref[h] + pv
            m_out.append(m_new)
        return m_out

    def pending_first(j):
        return jnp.where(j == 0, c0, 2 * j - 2)

    acc_ref[...] = jnp.zeros_like(acc_ref)
    m_start = [jnp.full((1, tq), NEG_INF, F32)] * HEAD_PAIR
    carry = tuple(m_start) + tuple(stage_a(c0, 0, own=True))

    def trip(j, slot, carry):
        m_run, maxima = carry[:HEAD_PAIR], carry[HEAD_PAIR:]
        new_maxima, m_new = [], []
        for h in range(HEAD_PAIR):
            new_maxima += stage_a(2 * j, 1 - slot, own=False, heads=(h,))
            m_new += stage_b(pending_first(j), slot, m_run, maxima, heads=(h,))
        return tuple(m_new) + tuple(new_maxima)

    carry = lax.fori_loop(0, lax.shift_right_logical(t, 1),
                          lambda i, cr: trip(2 * i + 1, 1, trip(2 * i, 0, cr)), carry)

    def odd_tail(carry):
        carry = trip(t - 1, 0, carry)
        stage_b(pending_first(t), 1, carry[:HEAD_PAIR], carry[HEAD_PAIR:])
        return 0

    def even_tail(carry):
        stage_b(pending_first(t), 0, carry[:HEAD_PAIR], carry[HEAD_PAIR:])
        return 0

    lax.cond((t & 1) == 1, odd_tail, even_tail, carry)
    outs = [acc_ref[h, 0:ATT_HEAD_DIM, :] / acc_ref[h, ATT_HEAD_DIM:ATT_HEAD_DIM + 1, :]
            for h in range(HEAD_PAIR)]
    o_ref[0] = jnp.concatenate(outs, axis=0).T.astype(o_ref.dtype)


def _attention_prompt(qt, kt, vt):
    B, _, L = qt.shape
    blk = MOBA_BLOCK
    tq = 2 * blk
    assert L % tq == 0
    nblk = L // blk
    assert nblk <= LANES
    nb = -(-nblk // 8) * 8
    ones_rows = 16
    groups = ATT_WIDTH // LANES
    return pl.pallas_call(
        functools.partial(_attn_kernel, nblk=nblk),
        grid=(B, groups, L // tq),
        in_specs=[pl.BlockSpec((1, LANES, tq), lambda b, g, i: (b, g, i)),
                  pl.BlockSpec((1, LANES, L), lambda b, g, i: (b, g, 0)),
                  pl.BlockSpec((1, LANES, L), lambda b, g, i: (b, g, 0))],
        out_specs=pl.BlockSpec((1, tq, LANES), lambda b, g, i: (b, i, g)),
        out_shape=jax.ShapeDtypeStruct((B, L, ATT_WIDTH), BF16),
        scratch_shapes=[pltpu.VMEM((L, 2 * LANES), BF16),
                        pltpu.VMEM((nblk, HEAD_PAIR, ATT_HEAD_DIM + ones_rows, blk), BF16),
                        pltpu.VMEM((nb, LANES), F32),
                        pltpu.VMEM((HEAD_PAIR, ATT_HEAD_DIM + ones_rows, tq), F32),
                        pltpu.VMEM((HEAD_PAIR, 2, blk, tq), F32),
                        pltpu.VMEM((HEAD_PAIR, 2, blk, tq), F32)],
        compiler_params=pltpu.CompilerParams(dimension_semantics=("parallel", "parallel", "arbitrary"),
                                             vmem_limit_bytes=VMEM_LIMIT),
        name="moba_prompt",
    )(qt, kt, vt)


def _mlstm_kernel(bi_ref, bf_ref, pt_ref, mq_ref, mk_ref, mv_ref, grow_ref, mo_ref, gout_ref, qbd_ref, *refs,
                  chunk, n_side_pages, steps_per_seq, nblk_seq):
    page_refs = refs[:n_side_pages]
    (ml_ref, c_out_ref, n_out_ref, m_out_ref, sel_ref, score_ref, cn_ref, m_ref,
     gate_ref) = refs[n_side_pages:]
    j = pl.program_id(1)
    dk, dv = ML_QK_DIM, ML_V_DIM
    part = lax.rem(pl.program_id(0) * pl.num_programs(1) + j, steps_per_seq)

    @pl.when(j == 0)
    def _():
        cn_ref[...] = jnp.zeros_like(cn_ref)
        m_ref[...] = jnp.zeros_like(m_ref)

    @pl.when(part == 0)
    def _():
        gate_ref[...] = jnp.zeros_like(gate_ref)

    t_i = lax.broadcasted_iota(jnp.int32, (chunk, chunk), 0)
    s_i = lax.broadcasted_iota(jnp.int32, (chunk, chunk), 1)
    tril = s_i <= t_i
    tril_b = jnp.where(tril, 1.0, 0.0).astype(BF16)
    triu_b = jnp.where(t_i <= s_i, 1.0, 0.0).astype(BF16)
    ones_cols = jnp.ones((chunk, LANES), BF16)
    grow = grow_ref[0]
    f_rows = _log_sigmoid(grow[N_ML_HEADS:2 * N_ML_HEADS, :]
                          + jnp.concatenate([jnp.full((1, 1), bf_ref[h], F32) for h in range(N_ML_HEADS)], axis=0))
    hi = f_rows.astype(BF16).astype(F32)
    mid = (f_rows - hi).astype(BF16).astype(F32)
    pieces = jnp.concatenate([hi, mid, f_rows - hi - mid], axis=1)
    pieces = jnp.concatenate([pieces, jnp.zeros((16 - N_ML_HEADS, 3 * chunk), F32)], axis=0).astype(BF16)
    rows = _dot(pieces, jnp.concatenate([triu_b] * 3, axis=0))
    cols = _dot_nt(jnp.concatenate([tril_b] * 3, axis=1), pieces)
    gates = []
    for h in range(N_ML_HEADS):
        i_row = grow[h:h + 1, :] + bi_ref[h]
        b_row = rows[h:h + 1]
        b_col = cols[:, h:h + 1]
        a_row = (i_row - b_row) * LOG2_E
        m_old = m_ref[h][:, 0:1]
        a_low = jnp.where(tril, a_row, NEG_INF)
        mm_col = jnp.maximum(jnp.max(a_low, axis=1, keepdims=True), m_old * LOG2_E)
        w_inter = jnp.exp2(m_old * LOG2_E - mm_col)
        floor_col = jnp.exp2(-(b_col * LOG2_E + mm_col))
        decay = jnp.exp2(a_low - mm_col)
        gates.append((b_row, a_row, m_old, mm_col, w_inter, floor_col, decay))

    _gate_pages(part, page_refs, qbd_ref, score_ref, gate_ref)

    hids = []
    for h in range(N_ML_HEADS):
        b_row, a_row, m_old, mm_col, w_inter, floor_col, decay = gates[h]
        q = mq_ref[0, :, h * dk:(h + 1) * dk]
        k = mk_ref[0, :, h * dk:(h + 1) * dk]
        v_ones = jnp.concatenate([mv_ref[0, :, h * dv:(h + 1) * dv], ones_cols], axis=1)
        cn = cn_ref[h]
        s = _dot_nt(q, k) * decay
        inter = _dot(q, cn.astype(BF16))
        intra = _dot(s.astype(BF16), v_ones)
        num = w_inter * inter[:, 0:dv] + intra[:, 0:dv]
        den = w_inter * inter[:, dv:dv + 1] + intra[:, dv:dv + 1]
        hids.append(num / jnp.maximum(jnp.abs(den), floor_col))

        mm_end = mm_col[chunk - 1:chunk, :]
        w_row = jnp.exp2(a_row - mm_end)
        kw = (k.astype(F32).T * w_row).astype(BF16)
        cn_ref[h] = jnp.exp2(m_old * LOG2_E - mm_end) * cn + _dot(kw, v_ones)
        m_ref[h] = jnp.broadcast_to(b_row[:, chunk - 1:chunk] + mm_end * (1.0 / LOG2_E), (1, LANES))

    for h in range(N_ML_HEADS):
        hn = _rms(hids[h], gout_ref[h:h + 1, :])
        mo = mo_ref[0, :, h * dv:(h + 1) * dv]
        ml_ref[0, :, h * dv:(h + 1) * dv] = (hn * jax.nn.sigmoid(mo)).astype(BF16)

    @pl.when(j == pl.num_programs(1) - 1)
    def _():
        for h in range(N_ML_HEADS):
            c_out_ref[0, h] = cn_ref[h, :, 0:dv].T
            n_out_ref[0, h:h + 1, :] = cn_ref[h, :, dv:dv + LANES].T[0:1, :]
            m_out_ref[0, h:h + 1, :] = m_ref[h]

    _gate_select(part, sel_ref, gate_ref, steps_per_seq=steps_per_seq, nblk_seq=nblk_seq)


def _mlstm_prompt(mq, mk, mv, grow, mo, pw, page_table, cache_kt, q_s):
    B, L, _ = mq.shape
    chunk = ML_CHUNK
    n_chunks = L // chunk
    side = _side_job(page_table, cache_kt, q_s, n_steps=B * n_chunks, n_grid=2, pt_pos=2,
                     linear_step=lambda b, j: b * n_chunks + j)
    tok = lambda w: pl.BlockSpec((1, chunk, w), lambda b, j, *_: (b, j, 0))
    per_b = lambda shp: pl.BlockSpec((1,) + shp, lambda b, j, *_: (b,) + (0,) * len(shp))
    grid_spec = pltpu.PrefetchScalarGridSpec(
        num_scalar_prefetch=3,
        grid=(B, n_chunks),
        in_specs=[tok(ML_QK_WIDTH), tok(ML_QK_WIDTH), tok(ML_V_WIDTH),
                  pl.BlockSpec((1, 16, chunk), lambda b, j, *_: (b, 0, j)), tok(ML_V_WIDTH),
                  pl.BlockSpec((N_ML_HEADS, ML_V_DIM), lambda b, j, *_: (0, 0))] + side["in_specs"],
        out_specs=(tok(ML_V_WIDTH), per_b((N_ML_HEADS, ML_V_DIM, ML_QK_DIM)),
                   per_b((N_ML_HEADS, ML_QK_DIM)), per_b((N_ML_HEADS, LANES))) + side["out_specs"],
        scratch_shapes=[pltpu.VMEM((N_ML_HEADS, ML_QK_DIM, ML_V_DIM + LANES), F32),
                        pltpu.VMEM((N_ML_HEADS, 1, LANES), F32)] + side["scratch"],
    )
    return pl.pallas_call(
        functools.partial(_mlstm_kernel, chunk=chunk, **side["static"]),
        grid_spec=grid_spec,
        out_shape=(jax.ShapeDtypeStruct((B, L, ML_V_WIDTH), BF16),
                   jax.ShapeDtypeStruct((B, N_ML_HEADS, ML_V_DIM, ML_QK_DIM), F32),
                   jax.ShapeDtypeStruct((B, N_ML_HEADS, ML_QK_DIM), F32),
                   jax.ShapeDtypeStruct((B, N_ML_HEADS, LANES), F32)) + side["out_shape"],
        compiler_params=pltpu.CompilerParams(dimension_semantics=("arbitrary", "arbitrary"),
                                             vmem_limit_bytes=VMEM_LIMIT),
        name="mlstm_prompt",
    )(pw["b_i"], pw["b_f"], page_table, mq, mk, mv, grow, mo, pw["g_ml_out"], *side["operands"])


def _post_kernel(x_ref, att_ref, ml_ref, ga_ref, gm_ref, pe_ref,
                 watt_ref, wml_ref, wout_ref, gffn_ref, wgate_ref, wup_ref, wdown_ref,
                 wple_ref, gple_ref, wpg_ref, y_ref):
    mix = (jax.nn.sigmoid(ga_ref[0]) * _dot(att_ref[0].astype(BF16), watt_ref[...])
           + jax.nn.sigmoid(gm_ref[0]) * _dot(ml_ref[0].astype(BF16), wml_ref[...]))
    h = x_ref[0] + _dot(mix.astype(BF16), wout_ref[...])
    hb = _rms(h, gffn_ref[...]).astype(BF16)
    act = jax.nn.silu(_dot(hb, wgate_ref[...])) * _dot(hb, wup_ref[...])
    h = h + _dot(act.astype(BF16), wdown_ref[...])
    gate = jax.nn.sigmoid(_dot(_rms(h, gple_ref[...]).astype(BF16), wpg_ref[...]))
    y_ref[0] = h + gate * _dot(pe_ref[0].astype(BF16), wple_ref[...])


def _post(x, att, ml, ga, gm, pe, pw, tm):
    B, L, _ = x.shape
    tok = lambda w: pl.BlockSpec((1, tm, w), lambda b, i: (b, i, 0))
    full = lambda a: pl.BlockSpec(a.shape, lambda b, i: (0,) * a.ndim, pipeline_mode=pl.Buffered(1))
    consts = (pw["w_att"], pw["w_ml"], pw["w_out"], pw["g_ffn"], pw["w_ffn_gate"], pw["w_ffn_up"],
              pw["w_ffn_down"], pw["w_ple"], pw["g_ple"], pw["w_ple_gate"])
    return pl.pallas_call(
        _post_kernel,
        grid=(B, L // tm),
        in_specs=[tok(D_MODEL), tok(ATT_WIDTH), tok(ML_V_WIDTH), tok(D_MODEL), tok(D_MODEL),
                  tok(pe.shape[-1])] + [full(a) for a in consts],
        out_specs=tok(D_MODEL),
        out_shape=jax.ShapeDtypeStruct((B, L, D_MODEL), F32),
        compiler_params=pltpu.CompilerParams(dimension_semantics=("parallel", "parallel"),
                                             vmem_limit_bytes=VMEM_LIMIT),
        name="merge_ffn",
    )(x, att, ml, ga, gm, pe, *consts)


def _row(ref, b):
    if ref.dtype == F32:
        return ref[0, pl.ds(b, 1), :]
    group = ref[0, pl.ds(pl.multiple_of(lax.shift_right_logical(b, 4) * 16, 16), 16), :].astype(F32)
    pick = lax.broadcasted_iota(jnp.int32, group.shape, 0) == (b & 15)
    return jnp.sum(jnp.where(pick, group, 0.0), axis=0, keepdims=True)


def _sample_attn_kernel(pool_ref, local_ref, *refs, n_pages):
    v_refs = refs[:N_ATT_HEADS * n_pages]
    score_ref, q_ref, kn_ref, vn_ref, o_ref = refs[N_ATT_HEADS * n_pages:]
    b = pl.program_id(0)
    scale = ATT_HEAD_DIM ** -0.5
    rows = 16
    q_row, kn_row, vn_row = _row(q_ref, b), _row(kn_ref, b), _row(vn_ref, b)
    outs = []
    for h in range(N_ATT_HEADS):
        lanes = slice(h * ATT_HEAD_DIM, (h + 1) * ATT_HEAD_DIM)
        s_own = jnp.sum(q_row[:, lanes] * kn_row[:, lanes], axis=1, keepdims=True) * scale
        pages = [local_ref[b, h * n_pages + j] for j in range(n_pages)]
        s = jnp.concatenate([score_ref[0, pg, h:h + 1, :] for pg in pages], axis=1) * scale
        v_all = jnp.concatenate([v_refs[h * n_pages + j][0].astype(BF16) for j in range(n_pages)], axis=1)
        m = jnp.maximum(s_own, jnp.max(s, axis=1, keepdims=True))
        p_own = jnp.exp(s_own - m)
        p = jnp.exp(s - m)
        l = p_own + jnp.sum(p, axis=1, keepdims=True)
        pv = _dot_nt(jnp.broadcast_to(p, (rows, n_pages * PAGE_SIZE)).astype(BF16), v_all)[0:1]
        outs.append((p_own * vn_row[:, lanes] + pv) / l)
    o_ref[0, pl.ds(b, 1), :] = jnp.concatenate(outs, axis=1)


def _sample_attention(page_table, sel, scores, cache_vt, q_s, k_s, v_s):
    Bs, n_seq_pages = page_table.shape
    ppb = MOBA_BLOCK // PAGE_SIZE
    n_pages = MOBA_TOPK * ppb
    local = (ppb * sel[..., None] + jnp.arange(ppb, dtype=jnp.int32)).reshape(Bs, N_ATT_HEADS * n_pages)
    pool = jnp.take_along_axis(page_table, local, axis=1)

    def tile_spec(h, j):
        return pl.BlockSpec((1, ATT_HEAD_DIM, PAGE_SIZE),
                            lambda b, pool_ref, local_ref: (pool_ref[b, h * n_pages + j], h, 0))

    tiles = [tile_spec(h, j) for h in range(N_ATT_HEADS) for j in range(n_pages)]
    rows = pl.BlockSpec((1, Bs, ATT_WIDTH), lambda b, *_: (0, 0, 0))
    grid_spec = pltpu.PrefetchScalarGridSpec(
        num_scalar_prefetch=2,
        grid=(Bs,),
        in_specs=tiles + [pl.BlockSpec((1, n_seq_pages, N_ATT_HEADS, PAGE_SIZE), lambda b, *_: (b, 0, 0, 0)),
                          rows, rows, rows],
        out_specs=rows,
    )
    return pl.pallas_call(
        functools.partial(_sample_attn_kernel, n_pages=n_pages),
        grid_spec=grid_spec,
        out_shape=jax.ShapeDtypeStruct((1, Bs, ATT_WIDTH), F32),
        compiler_params=pltpu.CompilerParams(dimension_semantics=("arbitrary",),
                                             vmem_limit_bytes=VMEM_LIMIT),
        name="moba_sample_attn",
    )(pool, local, *([cache_vt] * len(tiles)), scores, q_s, k_s, v_s)


def _mlstm_step_kernel(bi_ref, bf_ref, m0_ref, mq_ref, mk_ref, mv_ref, g_ref, mo_ref, gout_ref,
                       c0_ref, n0_ref, ml_ref, c_out_ref, n_out_ref, m_out_ref, *, seqs):
    dv = ML_V_DIM
    r_i = lax.broadcasted_iota(jnp.int32, (dv, dv), 0)
    c_i = lax.broadcasted_iota(jnp.int32, (dv, dv), 1)
    eye = r_i == c_i
    for i in range(seqs):
        _mlstm_step_one(i, pl.program_id(0) * seqs + i, eye, bi_ref, bf_ref, m0_ref, mq_ref, mk_ref, mv_ref,
                        g_ref, mo_ref, gout_ref, c0_ref, n0_ref, ml_ref, c_out_ref, n_out_ref, m_out_ref)


def _mlstm_step_one(i, b, eye, bi_ref, bf_ref, m0_ref, mq_ref, mk_ref, mv_ref, g_ref, mo_ref, gout_ref,
                    c0_ref, n0_ref, ml_ref, c_out_ref, n_out_ref, m_out_ref):
    dv = ML_V_DIM
    g = _row(g_ref, b)
    q_all, k_all, v_all, mo_all = _row(mq_ref, b), _row(mk_ref, b), _row(mv_ref, b), _row(mo_ref, b)
    outs = []
    for h in range(N_ML_HEADS):
        i_g = g[:, h:h + 1] + bi_ref[h]
        logf = _log_sigmoid(g[:, N_ML_HEADS + h:N_ML_HEADS + h + 1] + bf_ref[h])
        m_old = m0_ref[b, h]
        inter = logf + m_old
        m_t = jnp.maximum(inter, i_g)
        w_inter = jnp.exp(inter - m_t)
        w_in = jnp.exp(i_g - m_t)
        q = q_all[:, h * ML_QK_DIM:(h + 1) * ML_QK_DIM]
        k = k_all[:, h * ML_QK_DIM:(h + 1) * ML_QK_DIM]
        v = v_all[:, h * dv:(h + 1) * dv]
        c_old = c0_ref[0, i, h]
        n_old = n0_ref[0, i, h:h + 1, :]
        s = jnp.sum(q * k, axis=1, keepdims=True) * w_in
        v_col = jnp.sum(jnp.where(eye, v, 0.0), axis=1, keepdims=True)
        q_rows = jnp.broadcast_to(q, (16, ML_QK_DIM)).astype(BF16)
        cq = _dot_nt(q_rows, c_old.astype(BF16))[0:1]
        num = w_inter * cq + s * v
        den = w_inter * jnp.sum(n_old * q, axis=1, keepdims=True) + s
        hid = num / jnp.maximum(jnp.abs(den), jnp.exp(-m_t))
        c_out_ref[0, i, h] = w_inter * c_old + (w_in * v_col) * k
        n_out_ref[0, i, h:h + 1, :] = w_inter * n_old + w_in * k
        m_out_ref[i, h:h + 1, :] = jnp.broadcast_to(m_t, (1, LANES))
        hn = _rms(hid, gout_ref[h:h + 1, :])
        outs.append(hn * jax.nn.sigmoid(mo_all[:, h * dv:(h + 1) * dv]))
    ml_ref[0, pl.ds(b, 1), :] = jnp.concatenate(outs, axis=1)


def _mlstm_sample(mq, mk, mv, gcol, mo, state_c, state_n, state_m, pw):
    Bs = mq.shape[1]
    assert Bs % 16 == 0
    seqs = 2
    rows = lambda w: pl.BlockSpec((1, Bs, w), lambda b, *_: (0, 0, 0))
    grid_spec = pltpu.PrefetchScalarGridSpec(
        num_scalar_prefetch=3,
        grid=(Bs // seqs,),
        in_specs=[rows(ML_QK_WIDTH), rows(ML_QK_WIDTH), rows(ML_V_WIDTH), rows(LANES), rows(ML_V_WIDTH),
                  pl.BlockSpec((N_ML_HEADS, ML_V_DIM), lambda b, *_: (0, 0)),
                  pl.BlockSpec((1, seqs, N_ML_HEADS, ML_V_DIM, ML_QK_DIM), lambda b, *_: (0, b, 0, 0, 0)),
                  pl.BlockSpec((1, seqs, N_ML_HEADS, ML_QK_DIM), lambda b, *_: (0, b, 0, 0))],
        out_specs=(rows(ML_V_WIDTH),
                   pl.BlockSpec((1, seqs, N_ML_HEADS, ML_V_DIM, ML_QK_DIM), lambda b, *_: (0, b, 0, 0, 0)),
                   pl.BlockSpec((1, seqs, N_ML_HEADS, ML_QK_DIM), lambda b, *_: (0, b, 0, 0)),
                   pl.BlockSpec((seqs, N_ML_HEADS, LANES), lambda b, *_: (b, 0, 0))),
    )
    return pl.pallas_call(
        functools.partial(_mlstm_step_kernel, seqs=seqs),
        grid_spec=grid_spec,
        out_shape=(jax.ShapeDtypeStruct((1, Bs, ML_V_WIDTH), F32),
                   jax.ShapeDtypeStruct(state_c.shape, F32),
                   jax.ShapeDtypeStruct(state_n.shape, F32),
                   jax.ShapeDtypeStruct((Bs, N_ML_HEADS, LANES), F32)),
        compiler_params=pltpu.CompilerParams(dimension_semantics=("arbitrary",),
                                             vmem_limit_bytes=VMEM_LIMIT),
        name="mlstm_sample",
    )(pw["b_i"], pw["b_f"], state_m[0], mq, mk, mv, gcol, mo, pw["g_ml_out"], state_c, state_n)


def _rope_angles(pos):
    freqs = ROPE_THETA ** (-jnp.arange(0, ROPE_DIMS, 2, dtype=F32) / ROPE_DIMS)
    return pos.astype(F32)[:, None] * freqs[None, :]


def _rope_tables_cols(pos):
    ang = _rope_angles(pos).T
    return jnp.cos(ang), jnp.sin(ang)


def _rope_tables_rows(pos):
    half = ROPE_DIMS // 2
    n = pos.shape[0]
    ang = _rope_angles(pos)
    cos, sin = jnp.cos(ang), jnp.sin(ang)
    rest = ATT_HEAD_DIM - ROPE_DIMS
    zh = jnp.zeros((n, half), F32)
    cos_h = jnp.concatenate([cos, cos, jnp.ones((n, rest), F32)], axis=-1)
    sa_h = jnp.concatenate([-sin, zh, jnp.zeros((n, rest), F32)], axis=-1)
    sb_h = jnp.concatenate([zh, sin, jnp.zeros((n, rest), F32)], axis=-1)
    tile = lambda t: jnp.tile(t, (1, HEAD_PAIR))
    return tile(cos_h), tile(sa_h), tile(sb_h)


def _prepare_weights(w_in, b_igate, b_fgate, g_mix_norm, g_q_norm, g_k_norm, g_ml_out_norm,
                     w_att_branch, w_ml_branch, w_out, g_ffn_norm, w_ffn_gate, w_ffn_up, w_ffn_down,
                     w_ple, g_ple_norm, w_ple_gate):
    w_gates = w_in[:, _C_GATES:_C_MO]
    head_of = jnp.arange(ATT_WIDTH, dtype=jnp.int32) // ATT_HEAD_DIM
    w_qkv = w_in[:, _C_Q:_C_MQ].astype(BF16)
    return {
        "w_qkv": w_qkv,
        "w_qkv_t": w_qkv.T,
        "w_m": w_in[:, _C_MQ:_C_GATES].astype(BF16),
        "g_q_col": g_q_norm.reshape(1, ATT_HEAD_DIM, 1),
        "g_k_col": g_k_norm.reshape(1, ATT_HEAD_DIM, 1),
        "w_g": jnp.pad(w_gates, ((0, 0), (0, LANES - 2 * N_ML_HEADS))).astype(BF16),
        "w_gt": jnp.pad(w_gates.T, ((0, 16 - 2 * N_ML_HEADS), (0, 0))).astype(BF16),
        "w_b": w_in[:, _C_MO:_C_END].astype(BF16),
        "e_head": (head_of[:, None] == head_of[None, :]).astype(BF16),
        "g_mix": g_mix_norm.reshape(1, D_MODEL),
        "g_q": jnp.tile(g_q_norm, N_ATT_HEADS).reshape(1, ATT_WIDTH),
        "g_k": jnp.tile(g_k_norm, N_ATT_HEADS).reshape(1, ATT_WIDTH),
        "g_ml_out": g_ml_out_norm,
        "b_i": b_igate, "b_f": b_fgate,
        "w_att": w_att_branch.astype(BF16), "w_ml": w_ml_branch.astype(BF16), "w_out": w_out.astype(BF16),
        "g_ffn": g_ffn_norm.reshape(1, D_MODEL),
        "w_ffn_gate": w_ffn_gate.astype(BF16), "w_ffn_up": w_ffn_up.astype(BF16),
        "w_ffn_down": w_ffn_down.astype(BF16),
        "w_ple": w_ple.astype(BF16), "g_ple": g_ple_norm.reshape(1, D_MODEL),
        "w_ple_gate": w_ple_gate.astype(BF16),
    }


def _layer(x_p, pe_p, x_s, pe_s, cache_k, cache_v, page_table, state_c, state_n, state_m, pw):
    B, L, _ = x_p.shape
    Bs, Ls, _ = x_s.shape
    assert Ls == 1
    past_len = page_table.shape[1] * PAGE_SIZE
    n_pool = cache_k.shape[0]
    pages_t = lambda c: jnp.transpose(c, (0, 2, 3, 1)).reshape(n_pool, ATT_WIDTH, PAGE_SIZE)
    cache_kt, cache_vt = pages_t(cache_k), pages_t(cache_v)

    xt = x_s.reshape(1, Bs, D_MODEL)
    pos_s = jnp.full((Bs,), past_len, dtype=jnp.int32)
    q, k, v, mq_s, mk_s, mv_s, gcol_s, _, mo_s, ga_s, gm_s = _project(xt, pos_s, pw, tm=Bs, feature_major=False)

    pos_p = jnp.arange(L, dtype=jnp.int32)
    qt, kt, vt, mq, mk, mv, _, grow, mo, ga, gm = _project(x_p, pos_p, pw, tm=256, feature_major=True)
    att_p = _attention_prompt(qt, kt, vt)
    ml_p, c_p, n_p, m_p, sel, scores = _mlstm_prompt(mq, mk, mv, grow, mo, pw, page_table, cache_kt,
                                                     q.reshape(Bs, ATT_WIDTH))
    y_p = _post(x_p, att_p, ml_p, ga, gm, pe_p, pw, tm=256)
    tokens_major = lambda t: jnp.transpose(t.reshape(B, N_ATT_HEADS, ATT_HEAD_DIM, L), (0, 3, 1, 2))

    att_s = _sample_attention(page_table, sel[:, :, :MOBA_TOPK], scores, cache_vt, q, k, v)
    ml_s, c_s, n_s, m_s = _mlstm_sample(mq_s, mk_s, mv_s, gcol_s, mo_s, state_c, state_n, state_m, pw)
    y_s = _post(xt, att_s, ml_s, ga_s, gm_s, pe_s.reshape(1, Bs, pe_s.shape[-1]), pw, tm=Bs)
    prompt = (y_p, tokens_major(kt), tokens_major(vt), c_p, n_p, m_p[:, :, 0])
    sample = (y_s.reshape(Bs, 1, D_MODEL), k, v, c_s, n_s, m_s[:, :, 0])
    return prompt, sample


def kernel(x_prompt, x_sample, cache_k, cache_v, state_mlstm_C, state_mlstm_n, state_mlstm_m, page_table,
           p_prompt, p_sample, w_in, b_igate, b_fgate, g_mix_norm, g_q_norm, g_k_norm, g_ml_out_norm,
           w_att_branch, w_ml_branch, w_out, g_ffn_norm, w_ffn_gate, w_ffn_up, w_ffn_down,
           w_ple, g_ple_norm, w_ple_gate):
    depth = w_in.shape[0]
    assert depth == 1
    Bp, Lp, _ = x_prompt.shape
    Bs, Ls, _ = x_sample.shape
    pw = _prepare_weights(w_in[0], b_igate[0], b_fgate[0], g_mix_norm[0], g_q_norm[0], g_k_norm[0],
                          g_ml_out_norm[0], w_att_branch[0], w_ml_branch[0], w_out[0], g_ffn_norm[0],
                          w_ffn_gate[0], w_ffn_up[0], w_ffn_down[0], w_ple[0], g_ple_norm[0], w_ple_gate[0])
    (yp, kp, vp, cp, n_p, mp), (ys, ks, vs, cs, ns, ms) = _layer(
        x_prompt, p_prompt[0], x_sample, p_sample[0], cache_k[0], cache_v[0], page_table,
        state_mlstm_C, state_mlstm_n, state_mlstm_m, pw)
    heads = lambda t: t.reshape(1, Bs, Ls, N_ATT_HEADS, ATT_HEAD_DIM)
    return (yp, ys, kp[None], vp[None], heads(ks), heads(vs),
            cp[None], n_p[None], mp[None], cs, ns, ms[None])
```
